```python
import jax, jax.numpy as jnp
from jax import lax
import numpy as np

D_MODEL = 2048
BATCH = 2
SEQ = 4096
DEPTH = 1

HEAD_DIM = 128
N_HEADS = D_MODEL // HEAD_DIM
N_HEADS_B = N_HEADS // 4
N_HEADS_A = N_HEADS - N_HEADS_B
DILATION_PATTERNS = ((128, 1), (512, 4), (2048, 16))
HEADS_PER_GROUP = N_HEADS_A // len(DILATION_PATTERNS)
WIDTH_A = HEADS_PER_GROUP * HEAD_DIM
WIDTH_B = N_HEADS_B * HEAD_DIM
GRID_W = 64
NA_ROWS = 8
NA_COLS = 16
D_FF = 4 * D_MODEL
N_BRANCHES = 2
BAND_BLOCK = 64
EPS = 1e-6
NEG = -1e30

kernel_name = "hybrid_dilated_neighbourhood_gated_encoder"


def _rmsnorm(x, g):
    xf = x.astype(jnp.float32)
    y = xf * lax.rsqrt(jnp.mean(xf * xf, axis=-1, keepdims=True) + EPS)
    return (y * g.astype(jnp.float32)).astype(x.dtype)


def _alibi_slopes(n):
    return jnp.asarray(2.0 ** (-8.0 * np.arange(1, n + 1) / n), dtype=jnp.float32)


def _banded_attention(q, k, v, slopes, half_window, stride):
    B, H, N, L, hd = q.shape
    Qb = BAND_BLOCK
    W = half_window
    nb = -(-L // Qb)
    Lp = nb * Qb
    Kb = Qb + 2 * W
    qp = jnp.pad(q, ((0, 0), (0, 0), (0, 0), (0, Lp - L), (0, 0)))
    kp = jnp.pad(k, ((0, 0), (0, 0), (0, 0), (W, Lp - L + W), (0, 0)))
    vp = jnp.pad(v, ((0, 0), (0, 0), (0, 0), (W, Lp - L + W), (0, 0)))
    key_idx = np.arange(nb)[:, None] * Qb + np.arange(Kb)[None, :]
    kb = jnp.take(kp, key_idx, axis=3)
    vb = jnp.take(vp, key_idx, axis=3)
    qb = qp.reshape(B, H, N, nb, Qb, hd)
    s = jnp.einsum('bhniqd,bhnikd->bhniqk', qb, kb).astype(jnp.float32) * (HEAD_DIM ** -0.5)
    qpos = np.arange(Lp).reshape(nb, Qb)
    kpos = key_idx - W
    rel = np.abs(kpos[:, None, :] - qpos[:, :, None])
    valid = (rel <= W) & (kpos >= 0)[:, None, :] & (kpos < L)[:, None, :]
    bias = -(slopes * stride)[:, None, None, None, None] * rel.astype(np.float32)[None, None]
    s = jnp.where(valid, s + bias, NEG)
    m = jnp.max(s, axis=-1, keepdims=True)
    p = jnp.exp(s - m)
    den = jnp.sum(p, axis=-1, keepdims=True)
    o = jnp.einsum('bhniqk,bhnikd->bhniqd', (p / den).astype(v.dtype), vb)
    lse = (m + jnp.log(den))[..., 0]
    o = o.reshape(B, H, N, Lp, hd)[:, :, :, :L]
    lse = lse.reshape(B, H, N, Lp)[:, :, :, :L]
    return o, lse


def _dilated_mixer(q, k, v):
    B, S = q.shape[0], q.shape[1]
    slopes = _alibi_slopes(N_HEADS_A)
    outs, lses = [], []
    for g, (window, d) in enumerate(DILATION_PATTERNS):
        hs = slice(g * HEADS_PER_GROUP, (g + 1) * HEADS_PER_GROUP)

        def to_residue(t):
            return t[:, :, hs].reshape(B, S // d, d, HEADS_PER_GROUP, HEAD_DIM).transpose(0, 3, 2, 1, 4)

        o, lse = _banded_attention(to_residue(q), to_residue(k), to_residue(v),
                                   slopes[hs], window // (2 * d), d)
        outs.append(o.transpose(0, 3, 2, 1, 4).reshape(B, S, HEADS_PER_GROUP, HEAD_DIM))
        lses.append(lse.transpose(0, 3, 2, 1).reshape(B, S, HEADS_PER_GROUP))
    o_all = jnp.stack(outs, axis=0).astype(jnp.float32)
    alpha = jax.nn.softmax(jnp.stack(lses, axis=0), axis=0)
    y = jnp.sum(alpha[..., None] * o_all, axis=0)
    return y.reshape(B, S, WIDTH_A).astype(q.dtype)


def _neighbourhood_mixer(q, k, v, rpb):
    B, S, H, hd = q.shape
    rows = S // GRID_W
    kh = min(NA_ROWS, rows)
    kw = NA_COLS

    def grid(t):
        return t.reshape(B, rows, GRID_W, H, hd).transpose(0, 3, 1, 2, 4)

    row_start = np.clip(np.arange(rows) - kh // 2, 0, rows - kh)
    row_idx = row_start[:, None] + np.arange(kh)[None, :]
    kg = jnp.take(grid(k), row_idx, axis=2).reshape(B, H, rows, kh * GRID_W, hd)
    vg = jnp.take(grid(v), row_idx, axis=2).reshape(B, H, rows, kh * GRID_W, hd)
    col = np.arange(GRID_W)
    col_start = np.clip(col - kw // 2, 0, GRID_W - kw)
    col_ok = (col[None, :] >= col_start[:, None]) & (col[None, :] < col_start[:, None] + kw)
    mask = np.tile(col_ok, (1, kh))
    dr = row_idx - np.arange(rows)[:, None]
    dc = np.clip(col[None, :] - col[:, None], -(kw - 1), kw - 1)
    bias = rpb.astype(jnp.float32)[:, dr + NA_ROWS - 1][..., dc + NA_COLS - 1]
    bias = bias.transpose(0, 1, 3, 2, 4).reshape(H, rows, GRID_W, kh * GRID_W)
    s = jnp.einsum('bhrqd,bhrkd->bhrqk', grid(q), kg).astype(jnp.float32) * (HEAD_DIM ** -0.5) + bias
    s = jnp.where(mask, s, NEG)
    p = jax.nn.softmax(s, axis=-1)
    o = jnp.einsum('bhrqk,bhrkd->bhrqd', p.astype(v.dtype), vg)
    return o.transpose(0, 2, 3, 1, 4).reshape(B, S, WIDTH_B)


def _mixer_block(h, w_qkv, w_gate, b_gate, rpb, w_proj_a, w_proj_b, w_out):
    B, S, _ = h.shape
    qkv = (h @ w_qkv).reshape(B, S, 3, N_HEADS, HEAD_DIM)
    q, k, v = qkv[:, :, 0], qkv[:, :, 1], qkv[:, :, 2]
    y_a = _dilated_mixer(q[:, :, :N_HEADS_A], k[:, :, :N_HEADS_A], v[:, :, :N_HEADS_A])
    y_b = _neighbourhood_mixer(q[:, :, N_HEADS_A:], k[:, :, N_HEADS_A:], v[:, :, N_HEADS_A:], rpb)
    gates = jax.nn.sigmoid(h @ w_gate + b_gate).reshape(B, S, N_BRANCHES, D_MODEL)
    merged = gates[:, :, 0] * (y_a @ w_proj_a) + gates[:, :, 1] * (y_b @ w_proj_b)
    return merged @ w_out


def _sqrelu_mlp(h, w_up, w_down):
    return jnp.square(jax.nn.relu(h @ w_up)) @ w_down


def setup_inputs(seed: int = 0) -> dict:
    key = jax.random.key(seed)
    ks = jax.random.split(key, 14)
    f32 = jnp.float32

    def nrm(k, shape, scale):
        return jax.random.normal(k, shape, f32) * scale

    return {
        "x": nrm(ks[0], (BATCH, SEQ, D_MODEL), 1.0),
        "norm_mix": 1.0 + nrm(ks[1], (DEPTH, D_MODEL), 0.02),
        "w_qkv": nrm(ks[2], (DEPTH, D_MODEL, 3 * N_HEADS * HEAD_DIM), D_MODEL ** -0.5),
        "w_gate": nrm(ks[3], (DEPTH, D_MODEL, N_BRANCHES * D_MODEL), D_MODEL ** -0.5),
        "b_gate": nrm(ks[4], (DEPTH, N_BRANCHES * D_MODEL), 0.02),
        "rpb": nrm(ks[5], (DEPTH, N_HEADS_B, 2 * NA_ROWS - 1, 2 * NA_COLS - 1), 0.1),
        "w_proj_a": nrm(ks[6], (DEPTH, WIDTH_A, D_MODEL), WIDTH_A ** -0.5),
        "w_proj_b": nrm(ks[7], (DEPTH, WIDTH_B, D_MODEL), WIDTH_B ** -0.5),
        "w_out": nrm(ks[8], (DEPTH, D_MODEL, D_MODEL), D_MODEL ** -0.5),
        "norm_mlp": 1.0 + nrm(ks[9], (DEPTH, D_MODEL), 0.02),
        "w_up": nrm(ks[10], (DEPTH, D_MODEL, D_FF), D_MODEL ** -0.5),
        "w_down": nrm(ks[11], (DEPTH, D_FF, D_MODEL), D_FF ** -0.5),
        "norm_final": 1.0 + nrm(ks[12], (D_MODEL,), 0.02),
    }


def reference(x, norm_mix, w_qkv, w_gate, b_gate, rpb, w_proj_a, w_proj_b, w_out,
              norm_mlp, w_up, w_down, norm_final):
    for l in range(DEPTH):
        h = _rmsnorm(x, norm_mix[l])
        x = x + _mixer_block(h, w_qkv[l], w_gate[l], b_gate[l], rpb[l],
                             w_proj_a[l], w_proj_b[l], w_out[l])
        h = _rmsnorm(x, norm_mlp[l])
        x = x + _sqrelu_mlp(h, w_up[l], w_down[l])
    return _rmsnorm(x, norm_final)
```

```python
import functools

import jax
import jax.numpy as jnp
import numpy as np
from jax import lax
from jax.experimental import pallas as pl
from jax.experimental.pallas import tpu as pltpu

D_MODEL = 2048
HEAD_DIM = 128
N_HEADS = D_MODEL // HEAD_DIM
N_HEADS_B = N_HEADS // 4
N_HEADS_A = N_HEADS - N_HEADS_B
DILATION_PATTERNS = ((128, 1), (512, 4), (2048, 16))
HEADS_PER_GROUP = N_HEADS_A // len(DILATION_PATTERNS)
GROUP_WIDTH = HEADS_PER_GROUP * HEAD_DIM
QKV_WIDTH = 3 * D_MODEL
GROUP_BLOCKS_PER_TOKEN = QKV_WIDTH // GROUP_WIDTH
GRID_W = 64
NA_ROWS = 8
NA_COLS = 16
D_FF = 4 * D_MODEL
EPS = 1e-6
NEG = -1e30
SCALE = HEAD_DIM ** -0.5

F32 = jnp.float32
BF16 = jnp.bfloat16

VMEM_LIMIT_BYTES = 56 * 1024 * 1024

PROJ_TM, PROJ_TN = 1024, 512
MERGE_TM = 256
MLP_TM, MLP_TF = 1024, 512
DIL_QB = 128
NA_ROWS_PER_STEP = 8


def _params(sem):
    return pltpu.CompilerParams(dimension_semantics=sem, vmem_limit_bytes=VMEM_LIMIT_BYTES)


def _rms(x, g):
    ms = jnp.mean(x * x, axis=-1, keepdims=True)
    return (x * lax.rsqrt(ms + EPS)) * g


def _proj_kernel(x_ref, g_ref, wq_ref, wg_ref, bg_ref, qkv_ref, gate_ref, h_ref, *, n_qkv_tiles):
    j = pl.program_id(1)

    @pl.when(j == 0)
    def _():
        h_ref[...] = _rms(x_ref[...], g_ref[...]).astype(BF16)

    @pl.when(j < n_qkv_tiles)
    def _():
        qkv_ref[...] = jnp.dot(h_ref[...], wq_ref[...], preferred_element_type=F32).astype(BF16)

    @pl.when(j >= n_qkv_tiles)
    def _():
        z = jnp.dot(h_ref[...], wg_ref[...], preferred_element_type=F32) + bg_ref[...]
        gate_ref[...] = jax.nn.sigmoid(z).astype(BF16)


def _project(x2d, g, w_qkv, w_gate, b_gate):
    m = x2d.shape[0]
    tm, tn = PROJ_TM, PROJ_TN
    nq = QKV_WIDTH // tn
    ng = (2 * D_MODEL) // tn
    kern = functools.partial(_proj_kernel, n_qkv_tiles=nq)
    return pl.pallas_call(
        kern,
        grid=(m // tm, nq + ng),
        in_specs=[
            pl.BlockSpec((tm, D_MODEL), lambda i, j: (i, 0)),
            pl.BlockSpec((1, D_MODEL), lambda i, j: (0, 0)),
            pl.BlockSpec((D_MODEL, tn), lambda i, j: (0, jnp.minimum(j, nq - 1))),
            pl.BlockSpec((D_MODEL, tn), lambda i, j: (0, jnp.maximum(j - nq, 0))),
            pl.BlockSpec((1, tn), lambda i, j: (0, jnp.maximum(j - nq, 0))),
        ],
        out_specs=[
            pl.BlockSpec((tm, tn), lambda i, j: (i, jnp.minimum(j, nq - 1))),
            pl.BlockSpec((tm, tn), lambda i, j: (i, jnp.maximum(j - nq, 0))),
        ],
        out_shape=[
            jax.ShapeDtypeStruct((m, QKV_WIDTH), BF16),
            jax.ShapeDtypeStruct((m, 2 * D_MODEL), BF16),
        ],
        scratch_shapes=[pltpu.VMEM((tm, D_MODEL), BF16)],
        compiler_params=_params(("arbitrary", "arbitrary")),
        name="proj_qkv_gate",
    )(x2d, g, w_qkv, w_gate, b_gate)


def _dilated_kernel(q_ref, k_ref, v_ref, o_ref, lse_ref, *, seq, half_window, coefs):
    qb = DIL_QB
    kw = min(seq, qb + 2 * half_window)
    n_blocks = seq // qb
    qrow = lax.broadcasted_iota(jnp.int32, (qb, kw), 0)
    kcol = lax.broadcasted_iota(jnp.int32, (qb, kw), 1)
    lane = lax.broadcasted_iota(jnp.int32, (qb, HEAD_DIM), 1)

    def block(qi, carry):
        q0 = pl.multiple_of(qi * qb, qb)
        ks = pl.multiple_of(jnp.clip(q0 - half_window, 0, seq - kw), half_window)
        dist = jnp.abs(kcol - qrow + (ks - q0))
        valid = dist <= half_window
        dist_f = dist.astype(F32)
        lse_all = jnp.zeros((qb, HEAD_DIM), F32)
        for h in range(HEADS_PER_GROUP):
            cols = slice(h * HEAD_DIM, (h + 1) * HEAD_DIM)
            q = q_ref[pl.ds(q0, qb), cols]
            k = k_ref[pl.ds(ks, kw), cols]
            v = v_ref[pl.ds(ks, kw), cols]
            s = lax.dot_general(q, k, (((1,), (1,)), ((), ())), preferred_element_type=F32)
            s = jnp.where(valid, s * SCALE - coefs[h] * dist_f, NEG)
            m = jnp.max(s, axis=-1, keepdims=True)
            p = jnp.exp(s - m)
            den = jnp.sum(p, axis=-1, keepdims=True)
            o = jnp.dot(p.astype(BF16), v, preferred_element_type=F32) / den
            o_ref[pl.ds(q0, qb), cols] = o.astype(BF16)
            lse_all = jnp.where(lane == h, m + jnp.log(den), lse_all)
        lse_ref[pl.ds(q0, qb), :] = lse_all
        return carry

    lax.fori_loop(0, n_blocks, block, 0)


def _dilated_group(qkv, batch, seq_total, group, window, dilation, slopes):
    d = dilation
    seq = seq_total // d
    half_window = window // (2 * d)
    coefs = tuple(float(slopes[group * HEADS_PER_GROUP + h]) * d for h in range(HEADS_PER_GROUP))
    qkv_view = qkv.reshape(batch, seq, d * QKV_WIDTH)
    nblk = GROUP_BLOCKS_PER_TOKEN
    kblk = D_MODEL // GROUP_WIDTH
    kern = functools.partial(_dilated_kernel, seq=seq, half_window=half_window, coefs=coefs)
    o, lse = pl.pallas_call(
        kern,
        grid=(batch, d),
        in_specs=[
            pl.BlockSpec((None, seq, GROUP_WIDTH), lambda b, r: (b, 0, r * nblk + group)),
            pl.BlockSpec((None, seq, GROUP_WIDTH), lambda b, r: (b, 0, r * nblk + kblk + group)),
            pl.BlockSpec((None, seq, GROUP_WIDTH), lambda b, r: (b, 0, r * nblk + 2 * kblk + group)),
        ],
        out_specs=[
            pl.BlockSpec((None, seq, GROUP_WIDTH), lambda b, r: (b, 0, r)),
            pl.BlockSpec((None, seq, HEAD_DIM), lambda b, r: (b, 0, r)),
        ],
        out_shape=[
            jax.ShapeDtypeStruct((batch, seq, d * GROUP_WIDTH), BF16),
            jax.ShapeDtypeStruct((batch, seq, d * HEAD_DIM), F32),
        ],
        compiler_params=_params(("arbitrary", "arbitrary")),
        name=f"dilated_attention_d{d}",
    )(qkv_view, qkv_view, qkv_view)
    return (o.reshape(batch * seq_total, GROUP_WIDTH), lse.reshape(batch * seq_total, HEAD_DIM))


N_BIAS_VARIANTS = NA_ROWS


def _na_bias_kernel(rpb_ref, out_ref, *, kh):
    var = pl.program_id(0)
    h = pl.program_id(1)
    nkeys = kh * GRID_W
    qc = lax.broadcasted_iota(jnp.int32, (GRID_W, nkeys), 0)
    kk = lax.broadcasted_iota(jnp.int32, (GRID_W, nkeys), 1)
    kc = kk % GRID_W
    col_idx = jnp.clip(kc - qc, -(NA_COLS - 1), NA_COLS - 1) + (NA_COLS - 1)
    kr_row = lax.broadcasted_iota(jnp.int32, (1, nkeys), 1) // GRID_W
    acc = jnp.zeros((GRID_W, nkeys), F32)
    for j in range(2 * NA_COLS - 1):
        row_vals = jnp.zeros((1, nkeys), F32)
        for kr in range(kh):
            row_vals = jnp.where(kr_row == kr, rpb_ref[h, var + kr, j], row_vals)
        acc = jnp.where(col_idx == j, row_vals, acc)
    out_ref[...] = acc


def _na_bias(rpb, kh):
    nkeys = kh * GRID_W
    return pl.pallas_call(
        functools.partial(_na_bias_kernel, kh=kh),
        grid=(N_BIAS_VARIANTS, N_HEADS_B),
        in_specs=[pl.BlockSpec(memory_space=pltpu.SMEM)],
        out_specs=pl.BlockSpec((None, None, GRID_W, nkeys), lambda v, h: (v, h, 0, 0)),
        out_shape=jax.ShapeDtypeStruct((N_BIAS_VARIANTS, N_HEADS_B, GRID_W, nkeys), F32),
        compiler_params=_params(("arbitrary", "arbitrary")),
        name="na_bias_expand",
    )(rpb)


def _na_kernel(q_ref, k_ref, v_ref, bias_ref, o_ref, *, rows, kh):
    rb = pl.program_id(1)
    nkeys = kh * GRID_W
    qc = lax.broadcasted_iota(jnp.int32, (GRID_W, nkeys), 0)
    kc = lax.broadcasted_iota(jnp.int32, (GRID_W, nkeys), 1) % GRID_W
    cs = jnp.clip(qc - NA_COLS // 2, 0, GRID_W - NA_COLS)
    col_ok = (kc >= cs) & (kc < cs + NA_COLS)

    def one_row(rl, carry):
        r = rb * NA_ROWS_PER_STEP + rl
        rs = jnp.clip(r - kh // 2, 0, rows - kh)
        var = rs - r + (NA_ROWS - 1)
        q0 = pl.multiple_of(rl * GRID_W, GRID_W)
        k0 = pl.multiple_of(rs * GRID_W, GRID_W)
        for h in range(N_HEADS_B):
            cols = slice(h * HEAD_DIM, (h + 1) * HEAD_DIM)
            q = q_ref[pl.ds(q0, GRID_W), cols]
            k = k_ref[pl.ds(k0, nkeys), cols]
            v = v_ref[pl.ds(k0, nkeys), cols]
            s = lax.dot_general(q, k, (((1,), (1,)), ((), ())), preferred_element_type=F32)
            s = jnp.where(col_ok, s * SCALE + bias_ref[var, h], NEG)
            m = jnp.max(s, axis=-1, keepdims=True)
            p = jnp.exp(s - m)
            den = jnp.sum(p, axis=-1, keepdims=True)
            o = jnp.dot(p.astype(BF16), v, preferred_element_type=F32) / den
            o_ref[pl.ds(q0, GRID_W), cols] = o.astype(BF16)
        return carry

    lax.fori_loop(0, NA_ROWS_PER_STEP, one_row, 0)


def _neighbourhood(qkv, rpb, batch, seq_total):
    rows = seq_total // GRID_W
    kh = min(NA_ROWS, rows)
    assert kh == NA_ROWS and rows % NA_ROWS_PER_STEP == 0
    bias = _na_bias(rpb, kh)
    qkv3 = qkv.reshape(batch, seq_total, QKV_WIDTH)
    tq = NA_ROWS_PER_STEP * GRID_W
    kblk = D_MODEL // GROUP_WIDTH
    qcol = N_HEADS_A // HEADS_PER_GROUP
    o = pl.pallas_call(
        functools.partial(_na_kernel, rows=rows, kh=kh),
        grid=(batch, rows // NA_ROWS_PER_STEP),
        in_specs=[
            pl.BlockSpec((None, tq, GROUP_WIDTH), lambda b, i: (b, i, qcol)),
            pl.BlockSpec((None, seq_total, GROUP_WIDTH), lambda b, i: (b, 0, kblk + qcol)),
            pl.BlockSpec((None, seq_total, GROUP_WIDTH), lambda b, i: (b, 0, 2 * kblk + qcol)),
            pl.BlockSpec((N_BIAS_VARIANTS, N_HEADS_B, GRID_W, kh * GRID_W), lambda b, i: (0, 0, 0, 0)),
        ],
        out_specs=pl.BlockSpec((None, tq, GROUP_WIDTH), lambda b, i: (b, i, 0)),
        out_shape=jax.ShapeDtypeStruct((batch, seq_total, GROUP_WIDTH), BF16),
        compiler_params=_params(("arbitrary", "arbitrary")),
        name="neighbourhood_attention",
    )(qkv3, qkv3, qkv3, bias)
    return o.reshape(batch * seq_total, GROUP_WIDTH)


def _merge_kernel(o0_ref, o1_ref, o2_ref, l0_ref, l1_ref, l2_ref, ob_ref, gate_ref, x_ref,
                  wpa_ref, wpb_ref, wout_ref, gn_ref, x2_ref, h2_ref):
    l0, l1, l2 = l0_ref[...], l1_ref[...], l2_ref[...]
    mx = jnp.maximum(jnp.maximum(l0, l1), l2)
    e0, e1, e2 = jnp.exp(l0 - mx), jnp.exp(l1 - mx), jnp.exp(l2 - mx)
    tot = e0 + e1 + e2
    a0, a1, a2 = e0 / tot, e1 / tot, e2 / tot
    parts = []
    for h in range(HEADS_PER_GROUP):
        cols = slice(h * HEAD_DIM, (h + 1) * HEAD_DIM)
        y = (a0[:, h:h + 1] * o0_ref[:, cols].astype(F32)
             + a1[:, h:h + 1] * o1_ref[:, cols].astype(F32)
             + a2[:, h:h + 1] * o2_ref[:, cols].astype(F32))
        parts.append(y.astype(BF16))
    ya = jnp.concatenate(parts, axis=1)
    ta = jnp.dot(ya, wpa_ref[...], preferred_element_type=F32)
    tb = jnp.dot(ob_ref[...], wpb_ref[...], preferred_element_type=F32)
    merged = (gate_ref[:, :D_MODEL].astype(F32) * ta + gate_ref[:, D_MODEL:].astype(F32) * tb)
    x2 = x_ref[...] + jnp.dot(merged.astype(BF16), wout_ref[...], preferred_element_type=F32)
    x2_ref[...] = x2
    h2_ref[...] = _rms(x2, gn_ref[...]).astype(BF16)


def _merge(o_groups, lse_groups, ob, gates, x2d, wpa, wpb, wout, gn):
    m = x2d.shape[0]
    tm = MERGE_TM
    row = lambda w: pl.BlockSpec((tm, w), lambda i: (i, 0))
    const = lambda a, b: pl.BlockSpec((a, b), lambda i: (0, 0), pipeline_mode=pl.Buffered(1))
    return pl.pallas_call(
        _merge_kernel,
        grid=(m // tm,),
        in_specs=[row(GROUP_WIDTH)] * 3 + [row(HEAD_DIM)] * 3
        + [row(GROUP_WIDTH), row(2 * D_MODEL), row(D_MODEL),
           const(GROUP_WIDTH, D_MODEL), const(GROUP_WIDTH, D_MODEL), const(D_MODEL, D_MODEL),
           const(1, D_MODEL)],
        out_specs=[row(D_MODEL), row(D_MODEL)],
        out_shape=[jax.ShapeDtypeStruct((m, D_MODEL), F32), jax.ShapeDtypeStruct((m, D_MODEL), BF16)],
        compiler_params=_params(("arbitrary",)),
        name="merge_out_proj",
    )(*o_groups, *lse_groups, ob, gates, x2d, wpa, wpb, wout, gn)


def _mlp_kernel(h2_ref, wup_ref, wdn_ref, x2_ref, gf_ref, out_ref, *, n_f):
    f = pl.program_id(1)
    hid = jnp.dot(h2_ref[...], wup_ref[...], preferred_element_type=F32)
    hid = jnp.square(jnp.maximum(hid, 0.0)).astype(BF16)
    part = jnp.dot(hid, wdn_ref[...], preferred_element_type=F32)

    @pl.when(f == 0)
    def _():
        out_ref[...] = x2_ref[...] + part

    @pl.when(f > 0)
    def _():
        out_ref[...] += part

    @pl.when(f == n_f - 1)
    def _():
        out_ref[...] = _rms(out_ref[...], gf_ref[...])


def _mlp(h2, wup, wdn, x2, gf):
    m = h2.shape[0]
    tm, tf = MLP_TM, MLP_TF
    n_f = D_FF // tf
    return pl.pallas_call(
        functools.partial(_mlp_kernel, n_f=n_f),
        grid=(m // tm, n_f),
        in_specs=[
            pl.BlockSpec((tm, D_MODEL), lambda i, f: (i, 0)),
            pl.BlockSpec((D_MODEL, tf), lambda i, f: (0, f)),
            pl.BlockSpec((tf, D_MODEL), lambda i, f: (f, 0)),
            pl.BlockSpec((tm, D_MODEL), lambda i, f: (i, 0), pipeline_mode=pl.Buffered(1)),
            pl.BlockSpec((1, D_MODEL), lambda i, f: (0, 0)),
        ],
        out_specs=pl.BlockSpec((tm, D_MODEL), lambda i, f: (i, 0)),
        out_shape=jax.ShapeDtypeStruct((m, D_MODEL), F32),
        compiler_params=_params(("arbitrary", "arbitrary")),
        name="mlp_residual_norm",
    )(h2, wup, wdn, x2, gf)


def _layer(x2d, batch, seq, norm_mix, w_qkv, w_gate, b_gate, rpb, w_proj_a, w_proj_b, w_out,
           norm_mlp, w_up, w_down):
    slopes = 2.0 ** (-8.0 * np.arange(1, N_HEADS_A + 1) / N_HEADS_A)
    row = lambda v: v.reshape(1, -1)
    qkv, gates = _project(x2d, row(norm_mix), w_qkv.astype(BF16), w_gate.astype(BF16), row(b_gate))
    o_groups, lse_groups = [], []
    for g, (window, d) in enumerate(DILATION_PATTERNS):
        o, lse = _dilated_group(qkv, batch, seq, g, window, d, slopes)
        o_groups.append(o)
        lse_groups.append(lse)
    ob = _neighbourhood(qkv, rpb, batch, seq)
    x2, h2 = _merge(o_groups, lse_groups, ob, gates, x2d, w_proj_a.astype(BF16),
                    w_proj_b.astype(BF16), w_out.astype(BF16), row(norm_mlp))
    return x2, h2


def kernel(x, norm_mix, w_qkv, w_gate, b_gate, rpb, w_proj_a, w_proj_b, w_out, norm_mlp, w_up,
           w_down, norm_final):
    batch, seq, _ = x.shape
    depth = norm_mix.shape[0]
    x2d = x.reshape(batch * seq, D_MODEL)
    for l in range(depth):
        last = l == depth - 1
        x2, h2 = _layer(x2d, batch, seq, norm_mix[l], w_qkv[l], w_gate[l], b_gate[l], rpb[l],
                        w_proj_a[l], w_proj_b[l], w_out[l], norm_mlp[l], w_up[l], w_down[l])
        assert last, "fused final norm assumes a single layer"
        x2d = _mlp(h2, w_up[l].astype(BF16), w_down[l].astype(BF16), x2, norm_final.reshape(1, -1))
    return x2d.reshape(batch, seq, D_MODEL)
```

```python
import functools

import jax
import jax.numpy as jnp
import numpy as np
from jax import lax
from jax.experimental import pallas as pl
from jax.experimental.pallas import tpu as pltpu

D_MODEL = 2048
HEAD_DIM = 128
N_HEADS = D_MODEL // HEAD_DIM
N_HEADS_B = N_HEADS // 4
N_HEADS_A = N_HEADS - N_HEADS_B
DILATION_PATTERNS = ((128, 1), (512, 4), (2048, 16))
N_GROUPS_A = len(DILATION_PATTERNS)
HEADS_PER_GROUP = N_HEADS_A // N_GROUPS_A
GROUP_WIDTH = HEADS_PER_GROUP * HEAD_DIM
N_HEAD_GROUPS = N_HEADS // HEADS_PER_GROUP
QKV_WIDTH = 3 * D_MODEL
GRID_W = 64
NA_ROWS = 8
NA_COLS = 16
D_FF = 4 * D_MODEL
EPS = 1e-6
NEG = -1e30
SCALE = HEAD_DIM ** -0.5
LANES = 128

F32 = jnp.float32
BF16 = jnp.bfloat16

VMEM_LIMIT_BYTES = 56 * 1024 * 1024

PROJ_TM, PROJ_TN = 1024, GROUP_WIDTH
PROJ_NORM_CHUNK = 512
MERGE_TM = 256
MLP_TM, MLP_TF = 1024, 512
DIL_QB = 128
NA_ROWS_PER_STEP = 8

GROUP_DILATIONS = tuple(d for _, d in DILATION_PATTERNS) + (1,)


def _params(sem):
    return pltpu.CompilerParams(dimension_semantics=sem, vmem_limit_bytes=VMEM_LIMIT_BYTES)


def _rms(x, g):
    ms = jnp.mean(x * x, axis=-1, keepdims=True)
    return (x * lax.rsqrt(ms + EPS)) * g


def _proj_kernel(x_ref, g_ref, wq_ref, wg_ref, bg_ref, *refs, dilations):
    n_groups = len(dilations)
    out_refs = refs[:n_groups]
    gate_ref = refs[n_groups]
    perm_ds = sorted(set(d for d in dilations if d > 1))
    h_refs = dict(zip([1] + perm_ds, refs[n_groups + 1:n_groups + 2 + len(perm_ds)]))
    slab_ref = refs[-1]
    tm = x_ref.shape[0]
    j = pl.program_id(1)

    @pl.when(j == 0)
    def _():
        x = x_ref[...]
        inv = lax.rsqrt(jnp.mean(x * x, axis=-1, keepdims=True) + EPS)
        n_slabs = PROJ_NORM_CHUNK // LANES
        for c0 in range(0, D_MODEL, PROJ_NORM_CHUNK):
            hc = (x_ref[:, c0:c0 + PROJ_NORM_CHUNK] * inv) * g_ref[:, c0:c0 + PROJ_NORM_CHUNK]
            h_refs[1][:, c0:c0 + PROJ_NORM_CHUNK] = hc.astype(BF16)
            for s in range(n_slabs):
                slab_ref[s] = hc[:, s * LANES:(s + 1) * LANES]
            for d in perm_ds:
                n = tm // d
                for s in range(n_slabs):
                    cols = slice(c0 + s * LANES, c0 + (s + 1) * LANES)
                    for r in range(d):
                        h_refs[d][r * n:(r + 1) * n, cols] = (
                            slab_ref[s, pl.ds(r, n, stride=d), :].astype(BF16))

    for grp, d in enumerate(dilations):
        @pl.when(j // 3 == grp)
        def _(grp=grp, d=d):
            y = jnp.dot(h_refs[d][...], wq_ref[...], preferred_element_type=F32)
            out_refs[grp][...] = y.astype(BF16).reshape(d, tm // d, y.shape[-1])

    @pl.when(j >= 3 * n_groups)
    def _():
        z = jnp.dot(h_refs[1][...], wg_ref[...], preferred_element_type=F32) + bg_ref[...]
        gate_ref[...] = jax.nn.sigmoid(z).astype(BF16)


def _project(x2d, batch, seq, g, w_qkv, w_gate, b_gate):
    m = x2d.shape[0]
    tm, tn = PROJ_TM, PROJ_TN
    dilations = GROUP_DILATIONS
    n_groups = len(dilations)
    nq = 3 * n_groups
    ng = (2 * D_MODEL) // tn
    tiles_per_batch = seq // tm
    perm_ds = sorted(set(d for d in dilations if d > 1))

    def w_qkv_map(i, j):
        jj = jnp.minimum(j, nq - 1)
        return (0, (jj % 3) * n_groups + jj // 3)

    def out_map(grp):
        return lambda i, j: (i // tiles_per_batch, 0, i % tiles_per_batch,
                             jnp.clip(j - 3 * grp, 0, 2))

    gate_col = lambda i, j: (0, jnp.maximum(j - nq, 0))
    return pl.pallas_call(
        functools.partial(_proj_kernel, dilations=dilations),
        grid=(m // tm, nq + ng),
        in_specs=[
            pl.BlockSpec((tm, D_MODEL), lambda i, j: (i, 0)),
            pl.BlockSpec((1, D_MODEL), lambda i, j: (0, 0)),
            pl.BlockSpec((D_MODEL, tn), w_qkv_map),
            pl.BlockSpec((D_MODEL, tn), gate_col),
            pl.BlockSpec((1, tn), gate_col),
        ],
        out_specs=[pl.BlockSpec((None, d, tm // d, tn), out_map(grp))
                   for grp, d in enumerate(dilations)]
        + [pl.BlockSpec((tm, tn), lambda i, j: (i, jnp.maximum(j - nq, 0)))],
        out_shape=[jax.ShapeDtypeStruct((batch, d, seq // d, 3 * GROUP_WIDTH), BF16)
                   for d in dilations]
        + [jax.ShapeDtypeStruct((m, 2 * D_MODEL), BF16)],
        scratch_shapes=[pltpu.VMEM((tm, D_MODEL), BF16) for _ in range(1 + len(perm_ds))]
        + [pltpu.VMEM((PROJ_NORM_CHUNK // LANES, tm, LANES), F32)],
        compiler_params=_params(("arbitrary", "arbitrary")),
        name="proj_qkv_gate",
    )(x2d, g, w_qkv, w_gate, b_gate)


def _dilated_kernel(q_ref, k_ref, v_ref, o_ref, lse_ref, *, seq, half_window, coefs):
    qb = DIL_QB
    kw = min(seq, qb + 2 * half_window)
    n_blocks = seq // qb
    qrow = lax.broadcasted_iota(jnp.int32, (qb, kw), 0)
    kcol = lax.broadcasted_iota(jnp.int32, (qb, kw), 1)
    lane = lax.broadcasted_iota(jnp.int32, (qb, HEAD_DIM), 1)

    def block(qi, carry):
        q0 = pl.multiple_of(qi * qb, qb)
        ks = pl.multiple_of(jnp.clip(q0 - half_window, 0, seq - kw), half_window)
        dist = jnp.abs(kcol - qrow + (ks - q0))
        valid = dist <= half_window
        dist_f = dist.astype(F32)
        lse_all = jnp.zeros((qb, HEAD_DIM), F32)
        for h in range(HEADS_PER_GROUP):
            cols = slice(h * HEAD_DIM, (h + 1) * HEAD_DIM)
            q = q_ref[pl.ds(q0, qb), cols]
            k = k_ref[pl.ds(ks, kw), cols]
            v = v_ref[pl.ds(ks, kw), cols]
            s = lax.dot_general(q, k, (((1,), (1,)), ((), ())), preferred_element_type=F32)
            s = jnp.where(valid, s * SCALE - coefs[h] * dist_f, NEG)
            m = jnp.max(s, axis=-1, keepdims=True)
            p = jnp.exp(s - m)
            den = jnp.sum(p, axis=-1, keepdims=True)
            o = jnp.dot(p.astype(BF16), v, preferred_element_type=F32) / den
            o_ref[pl.ds(q0, qb), cols] = o.astype(BF16)
            lse_all = jnp.where(lane == h, m + jnp.log(den), lse_all)
        lse_ref[pl.ds(q0, qb), :] = lse_all
        return carry

    lax.fori_loop(0, n_blocks, block, 0)


def _dilated_group(qkv_g, group, window, dilation, slopes):
    batch, d, seq, _ = qkv_g.shape
    half_window = window // (2 * d)
    coefs = tuple(float(slopes[group * HEADS_PER_GROUP + h]) * d for h in range(HEADS_PER_GROUP))
    part = lambda which: pl.BlockSpec((None, None, seq, GROUP_WIDTH),
                                      lambda b, r: (b, r, 0, which))
    return pl.pallas_call(
        functools.partial(_dilated_kernel, seq=seq, half_window=half_window, coefs=coefs),
        grid=(batch, d),
        in_specs=[part(0), part(1), part(2)],
        out_specs=[
            pl.BlockSpec((None, None, seq, GROUP_WIDTH), lambda b, r: (b, r, 0, 0)),
            pl.BlockSpec((None, None, seq, HEAD_DIM), lambda b, r: (b, r, 0, 0)),
        ],
        out_shape=[
            jax.ShapeDtypeStruct((batch, d, seq, GROUP_WIDTH), BF16),
            jax.ShapeDtypeStruct((batch, d, seq, HEAD_DIM), F32),
        ],
        compiler_params=_params(("arbitrary", "arbitrary")),
        name=f"dilated_attention_d{d}",
    )(qkv_g, qkv_g, qkv_g)


N_BIAS_VARIANTS = NA_ROWS


def _na_bias_kernel(rpb_ref, out_ref, *, kh):
    var = pl.program_id(0)
    h = pl.program_id(1)
    nkeys = kh * GRID_W
    qc = lax.broadcasted_iota(jnp.int32, (GRID_W, nkeys), 0)
    kk = lax.broadcasted_iota(jnp.int32, (GRID_W, nkeys), 1)
    kc = kk % GRID_W
    col_idx = jnp.clip(kc - qc, -(NA_COLS - 1), NA_COLS - 1) + (NA_COLS - 1)
    kr_row = lax.broadcasted_iota(jnp.int32, (1, nkeys), 1) // GRID_W
    acc = jnp.zeros((GRID_W, nkeys), F32)
    for j in range(2 * NA_COLS - 1):
        row_vals = jnp.zeros((1, nkeys), F32)
        for kr in range(kh):
            row_vals = jnp.where(kr_row == kr, rpb_ref[h, var + kr, j], row_vals)
        acc = jnp.where(col_idx == j, row_vals, acc)
    out_ref[...] = acc


def _na_bias(rpb, kh):
    nkeys = kh * GRID_W
    return pl.pallas_call(
        functools.partial(_na_bias_kernel, kh=kh),
        grid=(N_BIAS_VARIANTS, N_HEADS_B),
        in_specs=[pl.BlockSpec(memory_space=pltpu.SMEM)],
        out_specs=pl.BlockSpec((None, None, GRID_W, nkeys), lambda v, h: (v, h, 0, 0)),
        out_shape=jax.ShapeDtypeStruct((N_BIAS_VARIANTS, N_HEADS_B, GRID_W, nkeys), F32),
        compiler_params=_params(("arbitrary", "arbitrary")),
        name="na_bias_expand",
    )(rpb)


def _na_kernel(q_ref, k_ref, v_ref, bias_ref, o_ref, *, rows, kh):
    rb = pl.program_id(1)
    nkeys = kh * GRID_W
    qc = lax.broadcasted_iota(jnp.int32, (GRID_W, nkeys), 0)
    kc = lax.broadcasted_iota(jnp.int32, (GRID_W, nkeys), 1) % GRID_W
    cs = jnp.clip(qc - NA_COLS // 2, 0, GRID_W - NA_COLS)
    col_ok = (kc >= cs) & (kc < cs + NA_COLS)

    def one_row(rl, carry):
        r = rb * NA_ROWS_PER_STEP + rl
        rs = jnp.clip(r - kh // 2, 0, rows - kh)
        var = rs - r + (NA_ROWS - 1)
        q0 = pl.multiple_of(rl * GRID_W, GRID_W)
        k0 = pl.multiple_of(rs * GRID_W, GRID_W)
        for h in range(N_HEADS_B):
            cols = slice(h * HEAD_DIM, (h + 1) * HEAD_DIM)
            q = q_ref[pl.ds(q0, GRID_W), cols]
            k = k_ref[pl.ds(k0, nkeys), cols]
            v = v_ref[pl.ds(k0, nkeys), cols]
            s = lax.dot_general(q, k, (((1,), (1,)), ((), ())), preferred_element_type=F32)
            s = jnp.where(col_ok, s * SCALE + bias_ref[var, h], NEG)
            m = jnp.max(s, axis=-1, keepdims=True)
            p = jnp.exp(s - m)
            den = jnp.sum(p, axis=-1, keepdims=True)
            o = jnp.dot(p.astype(BF16), v, preferred_element_type=F32) / den
            o_ref[pl.ds(q0, GRID_W), cols] = o.astype(BF16)
        return carry

    lax.fori_loop(0, NA_ROWS_PER_STEP, one_row, 0)


def _neighbourhood(qkv_b, rpb):
    batch, _, seq_total, _ = qkv_b.shape
    rows = seq_total // GRID_W
    kh = min(NA_ROWS, rows)
    assert kh == NA_ROWS and rows % NA_ROWS_PER_STEP == 0
    bias = _na_bias(rpb, kh)
    tq = NA_ROWS_PER_STEP * GRID_W
    o = pl.pallas_call(
        functools.partial(_na_kernel, rows=rows, kh=kh),
        grid=(batch, rows // NA_ROWS_PER_STEP),
        in_specs=[
            pl.BlockSpec((None, None, tq, GROUP_WIDTH), lambda b, i: (b, 0, i, 0)),
            pl.BlockSpec((None, None, seq_total, GROUP_WIDTH), lambda b, i: (b, 0, 0, 1)),
            pl.BlockSpec((None, None, seq_total, GROUP_WIDTH), lambda b, i: (b, 0, 0, 2)),
            pl.BlockSpec((N_BIAS_VARIANTS, N_HEADS_B, GRID_W, kh * GRID_W), lambda b, i: (0, 0, 0, 0)),
        ],
        out_specs=pl.BlockSpec((None, tq, GROUP_WIDTH), lambda b, i: (b, i, 0)),
        out_shape=jax.ShapeDtypeStruct((batch, seq_total, GROUP_WIDTH), BF16),
        compiler_params=_params(("arbitrary", "arbitrary")),
        name="neighbourhood_attention",
    )(qkv_b, qkv_b, qkv_b, bias)
    return o.reshape(batch * seq_total, GROUP_WIDTH)


def _to_token_order_matrix(tm, d):
    t = lax.broadcasted_iota(jnp.int32, (tm, tm), 0)
    c = lax.broadcasted_iota(jnp.int32, (tm, tm), 1)
    return (c == (t % d) * (tm // d) + t // d).astype(BF16)


def _merge_kernel(o0_ref, o1_ref, o2_ref, l0_ref, l1_ref, l2_ref, ob_ref, gate_ref, x_ref,
                  wpa_ref, wpb_ref, wout_ref, gn_ref, x2_ref, h2_ref, lse_tok_ref, *, dilations):
    tm = x_ref.shape[0]
    o_tok, lse_tok = [], []
    for g, (o_ref, l_ref, d) in enumerate(zip((o0_ref, o1_ref, o2_ref),
                                              (l0_ref, l1_ref, l2_ref), dilations)):
        o = o_ref[...].reshape(tm, GROUP_WIDTH)
        if d == 1:
            o_tok.append(o.astype(F32))
            lse_tok.append(l_ref[...].reshape(tm, HEAD_DIM))
        else:
            o_tok.append(jnp.dot(_to_token_order_matrix(tm, d), o, preferred_element_type=F32))
            for r in range(d):
                lse_tok_ref[g, pl.ds(r, tm // d, stride=d), :] = l_ref[r]
            lse_tok.append(lse_tok_ref[g])
    l0, l1, l2 = lse_tok
    mx = jnp.maximum(jnp.maximum(l0, l1), l2)
    e0, e1, e2 = jnp.exp(l0 - mx), jnp.exp(l1 - mx), jnp.exp(l2 - mx)
    tot = e0 + e1 + e2
    a0, a1, a2 = e0 / tot, e1 / tot, e2 / tot
    parts = []
    for h in range(HEADS_PER_GROUP):
        cols = slice(h * HEAD_DIM, (h + 1) * HEAD_DIM)
        y = (a0[:, h:h + 1] * o_tok[0][:, cols] + a1[:, h:h + 1] * o_tok[1][:, cols]
             + a2[:, h:h + 1] * o_tok[2][:, cols])
        parts.append(y.astype(BF16))
    ya = jnp.concatenate(parts, axis=1)
    ta = jnp.dot(ya, wpa_ref[...], preferred_element_type=F32)
    tb = jnp.dot(ob_ref[...], wpb_ref[...], preferred_element_type=F32)
    merged = (gate_ref[:, :D_MODEL].astype(F32) * ta + gate_ref[:, D_MODEL:].astype(F32) * tb)
    x2 = x_ref[...] + jnp.dot(merged.astype(BF16), wout_ref[...], preferred_element_type=F32)
    x2_ref[...] = x2
    h2_ref[...] = _rms(x2, gn_ref[...]).astype(BF16)


def _merge(o_groups, lse_groups, ob, gates, x2d, seq, wpa, wpb, wout, gn):
    m = x2d.shape[0]
    tm = MERGE_TM
    tiles_per_batch = seq // tm
    dilations = tuple(o.shape[1] for o in o_groups)
    row = lambda w: pl.BlockSpec((tm, w), lambda i: (i, 0))
    grouped = lambda d, w: pl.BlockSpec(
        (None, d, tm // d, w), lambda i: (i // tiles_per_batch, 0, i % tiles_per_batch, 0))
    const = lambda a, b: pl.BlockSpec((a, b), lambda i: (0, 0), pipeline_mode=pl.Buffered(1))
    return pl.pallas_call(
        functools.partial(_merge_kernel, dilations=dilations),
        grid=(m // tm,),
        in_specs=[grouped(d, GROUP_WIDTH) for d in dilations]
        + [grouped(d, HEAD_DIM) for d in dilations]
        + [row(GROUP_WIDTH), row(2 * D_MODEL), row(D_MODEL),
           const(GROUP_WIDTH, D_MODEL), const(GROUP_WIDTH, D_MODEL), const(D_MODEL, D_MODEL),
           const(1, D_MODEL)],
        out_specs=[row(D_MODEL), row(D_MODEL)],
        out_shape=[jax.ShapeDtypeStruct((m, D_MODEL), F32), jax.ShapeDtypeStruct((m, D_MODEL), BF16)],
        scratch_shapes=[pltpu.VMEM((len(dilations), tm, HEAD_DIM), F32)],
        compiler_params=_params(("arbitrary",)),
        name="merge_out_proj",
    )(*o_groups, *lse_groups, ob, gates, x2d, wpa, wpb, wout, gn)


def _mlp_kernel(h2_ref, wup_ref, wdn_ref, x2_ref, gf_ref, out_ref, *, n_f):
    f = pl.program_id(1)

    @pl.when(f == 0)
    def _():
        out_ref[...] = x2_ref[...]

    hid = jnp.dot(h2_ref[...], wup_ref[...], preferred_element_type=F32)
    hid = jnp.square(jnp.maximum(hid, 0.0)).astype(BF16)
    out_ref[...] += jnp.dot(hid, wdn_ref[...], preferred_element_type=F32)

    @pl.when(f == n_f - 1)
    def _():
        out_ref[...] = _rms(out_ref[...], gf_ref[...])


def _mlp(h2, wup, wdn, x2, gf):
    m = h2.shape[0]
    tm, tf = MLP_TM, MLP_TF
    n_f = D_FF // tf
    return pl.pallas_call(
        functools.partial(_mlp_kernel, n_f=n_f),
        grid=(m // tm, n_f),
        in_specs=[
            pl.BlockSpec((tm, D_MODEL), lambda i, f: (i, 0)),
            pl.BlockSpec((D_MODEL, tf), lambda i, f: (0, f)),
            pl.BlockSpec((tf, D_MODEL), lambda i, f: (f, 0)),
            pl.BlockSpec((tm, D_MODEL), lambda i, f: (i, 0), pipeline_mode=pl.Buffered(1)),
            pl.BlockSpec((1, D_MODEL), lambda i, f: (0, 0)),
        ],
        out_specs=pl.BlockSpec((tm, D_MODEL), lambda i, f: (i, 0)),
        out_shape=jax.ShapeDtypeStruct((m, D_MODEL), F32),
        compiler_params=_params(("arbitrary", "arbitrary")),
        name="mlp_residual_norm",
    )(h2, wup, wdn, x2, gf)


def _layer(x2d, batch, seq, norm_mix, w_qkv, w_gate, b_gate, rpb, w_proj_a, w_proj_b, w_out,
           norm_mlp):
    slopes = 2.0 ** (-8.0 * np.arange(1, N_HEADS_A + 1) / N_HEADS_A)
    row = lambda v: v.reshape(1, -1)
    *qkv_groups, gates = _project(x2d, batch, seq, row(norm_mix), w_qkv.astype(BF16),
                                  w_gate.astype(BF16), row(b_gate))
    o_groups, lse_groups = [], []
    for g, (window, d) in enumerate(DILATION_PATTERNS):
        o, lse = _dilated_group(qkv_groups[g], g, window, d, slopes)
        o_groups.append(o)
        lse_groups.append(lse)
    ob = _neighbourhood(qkv_groups[N_GROUPS_A], rpb)
    return _merge(o_groups, lse_groups, ob, gates, x2d, seq, w_proj_a.astype(BF16),
                  w_proj_b.astype(BF16), w_out.astype(BF16), row(norm_mlp))


def kernel(x, norm_mix, w_qkv, w_gate, b_gate, rpb, w_proj_a, w_proj_b, w_out, norm_mlp, w_up,
           w_down, norm_final):
    batch, seq, _ = x.shape
    depth = norm_mix.shape[0]
    assert depth == 1
    x2d = x.reshape(batch * seq, D_MODEL)
    x2, h2 = _layer(x2d, batch, seq, norm_mix[0], w_qkv[0], w_gate[0], b_gate[0], rpb[0],
                    w_proj_a[0], w_proj_b[0], w_out[0], norm_mlp[0])
    out = _mlp(h2, w_up[0].astype(BF16), w_down[0].astype(BF16), x2, norm_final.reshape(1, -1))
    return out.reshape(batch, seq, D_MODEL)
```

```python
import functools

import jax
import jax.numpy as jnp
import numpy as np
from jax import lax
from jax.experimental import pallas as pl
from jax.experimental.pallas import tpu as pltpu

D_MODEL = 2048
HEAD_DIM = 128
N_HEADS = D_MODEL // HEAD_DIM
N_HEADS_B = N_HEADS // 4
N_HEADS_A = N_HEADS - N_HEADS_B
DILATION_PATTERNS = ((128, 1), (512, 4), (2048, 16))
N_GROUPS_A = len(DILATION_PATTERNS)
HEADS_PER_GROUP = N_HEADS_A // N_GROUPS_A
GROUP_WIDTH = HEADS_PER_GROUP * HEAD_DIM
N_HEAD_GROUPS = N_HEADS // HEADS_PER_GROUP
QKV_WIDTH = 3 * D_MODEL
GRID_W = 64
NA_ROWS = 8
NA_COLS = 16
D_FF = 4 * D_MODEL
EPS = 1e-6
NEG = -1e30
SCALE = HEAD_DIM ** -0.5
LANES = 128

F32 = jnp.float32
BF16 = jnp.bfloat16

VMEM_LIMIT_BYTES = 56 * 1024 * 1024

PROJ_TM, PROJ_TN = 1024, GROUP_WIDTH
PROJ_NORM_CHUNK = 512
MERGE_TM = 256
MLP_TM, MLP_TF = 1024, 1024
DIL_QB = 128
DIL_UNROLL = 2
NA_ROWS_PER_STEP = 8
NA_UNROLL = 4

GROUP_DILATIONS = tuple(d for _, d in DILATION_PATTERNS) + (1,)


def _params(sem):
    return pltpu.CompilerParams(dimension_semantics=sem, vmem_limit_bytes=VMEM_LIMIT_BYTES)


def _rms(x, g):
    ms = jnp.mean(x * x, axis=-1, keepdims=True)
    return (x * lax.rsqrt(ms + EPS)) * g


def _proj_kernel(x_ref, g_ref, wq_ref, wg_ref, bg_ref, *refs, dilations):
    n_groups = len(dilations)
    out_refs = refs[:n_groups]
    gate_ref = refs[n_groups]
    perm_ds = sorted(set(d for d in dilations if d > 1))
    h_refs = dict(zip([1] + perm_ds, refs[n_groups + 1:n_groups + 2 + len(perm_ds)]))
    slab_ref = refs[-1]
    tm = x_ref.shape[0]
    j = pl.program_id(1)

    @pl.when(j == 0)
    def _():
        x = x_ref[...]
        inv = lax.rsqrt(jnp.mean(x * x, axis=-1, keepdims=True) + EPS)
        n_slabs = PROJ_NORM_CHUNK // LANES
        for c0 in range(0, D_MODEL, PROJ_NORM_CHUNK):
            hc = (x_ref[:, c0:c0 + PROJ_NORM_CHUNK] * inv) * g_ref[:, c0:c0 + PROJ_NORM_CHUNK]
            h_refs[1][:, c0:c0 + PROJ_NORM_CHUNK] = hc.astype(BF16)
            for s in range(n_slabs):
                slab_ref[s] = hc[:, s * LANES:(s + 1) * LANES]
            for d in perm_ds:
                n = tm // d
                for s in range(n_slabs):
                    cols = slice(c0 + s * LANES, c0 + (s + 1) * LANES)
                    for r in range(d):
                        h_refs[d][r * n:(r + 1) * n, cols] = (
                            slab_ref[s, pl.ds(r, n, stride=d), :].astype(BF16))

    for grp, d in enumerate(dilations):
        @pl.when(j // 3 == grp)
        def _(grp=grp, d=d):
            y = jnp.dot(h_refs[d][...], wq_ref[...], preferred_element_type=F32)
            out_refs[grp][...] = y.astype(BF16).reshape(d, tm // d, y.shape[-1])

    @pl.when(j >= 3 * n_groups)
    def _():
        z = jnp.dot(h_refs[1][...], wg_ref[...], preferred_element_type=F32) + bg_ref[...]
        gate_ref[...] = jax.nn.sigmoid(z).astype(BF16)


def _project(x2d, batch, seq, g, w_qkv, w_gate, b_gate):
    m = x2d.shape[0]
    tm, tn = PROJ_TM, PROJ_TN
    dilations = GROUP_DILATIONS
    n_groups = len(dilations)
    nq = 3 * n_groups
    ng = (2 * D_MODEL) // tn
    tiles_per_batch = seq // tm
    perm_ds = sorted(set(d for d in dilations if d > 1))

    def w_qkv_map(i, j):
        jj = jnp.minimum(j, nq - 1)
        return (0, (jj % 3) * n_groups + jj // 3)

    def out_map(grp):
        return lambda i, j: (i // tiles_per_batch, 0, i % tiles_per_batch,
                             jnp.clip(j - 3 * grp, 0, 2))

    gate_col = lambda i, j: (0, jnp.maximum(j - nq, 0))
    return pl.pallas_call(
        functools.partial(_proj_kernel, dilations=dilations),
        grid=(m // tm, nq + ng),
        in_specs=[
            pl.BlockSpec((tm, D_MODEL), lambda i, j: (i, 0)),
            pl.BlockSpec((1, D_MODEL), lambda i, j: (0, 0)),
            pl.BlockSpec((D_MODEL, tn), w_qkv_map),
            pl.BlockSpec((D_MODEL, tn), gate_col),
            pl.BlockSpec((1, tn), gate_col),
        ],
        out_specs=[pl.BlockSpec((None, d, tm // d, tn), out_map(grp))
                   for grp, d in enumerate(dilations)]
        + [pl.BlockSpec((tm, tn), lambda i, j: (i, jnp.maximum(j - nq, 0)))],
        out_shape=[jax.ShapeDtypeStruct((batch, d, seq // d, 3 * GROUP_WIDTH), BF16)
                   for d in dilations]
        + [jax.ShapeDtypeStruct((m, 2 * D_MODEL), BF16)],
        scratch_shapes=[pltpu.VMEM((tm, D_MODEL), BF16) for _ in range(1 + len(perm_ds))]
        + [pltpu.VMEM((PROJ_NORM_CHUNK // LANES, tm, LANES), F32)],
        compiler_params=_params(("arbitrary", "arbitrary")),
        name="proj_qkv_gate",
    )(x2d, g, w_qkv, w_gate, b_gate)


def _dilated_kernel(q_ref, k_ref, v_ref, o_ref, lse_ref, *, seq, half_window, coefs):
    qb = DIL_QB
    kw = min(seq, qb + 2 * half_window)
    n_blocks = seq // qb
    qrow = lax.broadcasted_iota(jnp.int32, (qb, kw), 0)
    kcol = lax.broadcasted_iota(jnp.int32, (qb, kw), 1)
    lane = lax.broadcasted_iota(jnp.int32, (qb, HEAD_DIM), 1)

    def block(qi, carry):
        q0 = pl.multiple_of(qi * qb, qb)
        ks = pl.multiple_of(jnp.clip(q0 - half_window, 0, seq - kw), half_window)
        dist = jnp.abs(kcol - qrow + (ks - q0))
        valid = dist <= half_window
        dist_f = dist.astype(F32)
        heads = range(HEADS_PER_GROUP)
        head_cols = [slice(h * HEAD_DIM, (h + 1) * HEAD_DIM) for h in heads]
        scores = [lax.dot_general(q_ref[pl.ds(q0, qb), c], k_ref[pl.ds(ks, kw), c],
                                  (((1,), (1,)), ((), ())), preferred_element_type=F32)
                  for c in head_cols]
        probs, dens = [], []
        lse_all = jnp.zeros((qb, HEAD_DIM), F32)
        for h in heads:
            s = jnp.where(valid, scores[h] * SCALE - coefs[h] * dist_f, NEG)
            m = jnp.max(s, axis=-1, keepdims=True)
            p = jnp.exp(s - m)
            den = jnp.sum(p, axis=-1, keepdims=True)
            probs.append(p.astype(BF16))
            dens.append(den)
            lse_all = jnp.where(lane == h, m + jnp.log(den), lse_all)
        for h in heads:
            o = jnp.dot(probs[h], v_ref[pl.ds(ks, kw), head_cols[h]],
                        preferred_element_type=F32) / dens[h]
            o_ref[pl.ds(q0, qb), head_cols[h]] = o.astype(BF16)
        lse_ref[pl.ds(q0, qb), :] = lse_all
        return carry

    lax.fori_loop(0, n_blocks, block, 0, unroll=DIL_UNROLL)


def _dilated_group(qkv_g, group, window, dilation, slopes):
    batch, d, seq, _ = qkv_g.shape
    half_window = window // (2 * d)
    coefs = tuple(float(slopes[group * HEADS_PER_GROUP + h]) * d for h in range(HEADS_PER_GROUP))
    part = lambda which: pl.BlockSpec((None, None, seq, GROUP_WIDTH),
                                      lambda b, r: (b, r, 0, which))
    return pl.pallas_call(
        functools.partial(_dilated_kernel, seq=seq, half_window=half_window, coefs=coefs),
        grid=(batch, d),
        in_specs=[part(0), part(1), part(2)],
        out_specs=[
            pl.BlockSpec((None, None, seq, GROUP_WIDTH), lambda b, r: (b, r, 0, 0)),
            pl.BlockSpec((None, None, seq, HEAD_DIM), lambda b, r: (b, r, 0, 0)),
        ],
        out_shape=[
            jax.ShapeDtypeStruct((batch, d, seq, GROUP_WIDTH), BF16),
            jax.ShapeDtypeStruct((batch, d, seq, HEAD_DIM), F32),
        ],
        compiler_params=_params(("arbitrary", "arbitrary")),
        name=f"dilated_attention_d{d}",
    )(qkv_g, qkv_g, qkv_g)


N_BIAS_VARIANTS = NA_ROWS


def _na_bias_kernel(rpb_ref, out_ref, *, kh):
    var = pl.program_id(0)
    h = pl.program_id(1)
    nkeys = kh * GRID_W
    qc = lax.broadcasted_iota(jnp.int32, (GRID_W, nkeys), 0)
    kk = lax.broadcasted_iota(jnp.int32, (GRID_W, nkeys), 1)
    kc = kk % GRID_W
    col_idx = jnp.clip(kc - qc, -(NA_COLS - 1), NA_COLS - 1) + (NA_COLS - 1)
    kr_row = lax.broadcasted_iota(jnp.int32, (1, nkeys), 1) // GRID_W
    acc = jnp.zeros((GRID_W, nkeys), F32)
    for j in range(2 * NA_COLS - 1):
        row_vals = jnp.zeros((1, nkeys), F32)
        for kr in range(kh):
            row_vals = jnp.where(kr_row == kr, rpb_ref[h, var + kr, j], row_vals)
        acc = jnp.where(col_idx == j, row_vals, acc)
    out_ref[...] = acc


def _na_bias(rpb, kh):
    nkeys = kh * GRID_W
    return pl.pallas_call(
        functools.partial(_na_bias_kernel, kh=kh),
        grid=(N_BIAS_VARIANTS, N_HEADS_B),
        in_specs=[pl.BlockSpec(memory_space=pltpu.SMEM)],
        out_specs=pl.BlockSpec((None, None, GRID_W, nkeys), lambda v, h: (v, h, 0, 0)),
        out_shape=jax.ShapeDtypeStruct((N_BIAS_VARIANTS, N_HEADS_B, GRID_W, nkeys), F32),
        compiler_params=_params(("arbitrary", "arbitrary")),
        name="na_bias_expand",
    )(rpb)


def _na_kernel(q_ref, k_ref, v_ref, bias_ref, o_ref, *, rows, kh):
    rb = pl.program_id(1)
    nkeys = kh * GRID_W
    qc = lax.broadcasted_iota(jnp.int32, (GRID_W, nkeys), 0)
    kc = lax.broadcasted_iota(jnp.int32, (GRID_W, nkeys), 1) % GRID_W
    cs = jnp.clip(qc - NA_COLS // 2, 0, GRID_W - NA_COLS)
    col_ok = (kc >= cs) & (kc < cs + NA_COLS)

    def one_row(rl, carry):
        r = rb * NA_ROWS_PER_STEP + rl
        rs = jnp.clip(r - kh // 2, 0, rows - kh)
        var = rs - r + (NA_ROWS - 1)
        q0 = pl.multiple_of(rl * GRID_W, GRID_W)
        k0 = pl.multiple_of(rs * GRID_W, GRID_W)
        heads = range(N_HEADS_B)
        head_cols = [slice(h * HEAD_DIM, (h + 1) * HEAD_DIM) for h in heads]
        scores = [lax.dot_general(q_ref[pl.ds(q0, GRID_W), c], k_ref[pl.ds(k0, nkeys), c],
                                  (((1,), (1,)), ((), ())), preferred_element_type=F32)
                  for c in head_cols]
        probs, dens = [], []
        for h in heads:
            s = jnp.where(col_ok, scores[h] * SCALE + bias_ref[var, h], NEG)
            m = jnp.max(s, axis=-1, keepdims=True)
            p = jnp.exp(s - m)
            dens.append(jnp.sum(p, axis=-1, keepdims=True))
            probs.append(p.astype(BF16))
        for h in heads:
            o = jnp.dot(probs[h], v_ref[pl.ds(k0, nkeys), head_cols[h]],
                        preferred_element_type=F32) / dens[h]
            o_ref[pl.ds(q0, GRID_W), head_cols[h]] = o.astype(BF16)
        return carry

    lax.fori_loop(0, NA_ROWS_PER_STEP, one_row, 0, unroll=NA_UNROLL)


def _neighbourhood(qkv_b, rpb):
    batch, _, seq_total, _ = qkv_b.shape
    rows = seq_total // GRID_W
    kh = min(NA_ROWS, rows)
    assert kh == NA_ROWS and rows % NA_ROWS_PER_STEP == 0
    bias = _na_bias(rpb, kh)
    tq = NA_ROWS_PER_STEP * GRID_W
    o = pl.pallas_call(
        functools.partial(_na_kernel, rows=rows, kh=kh),
        grid=(batch, rows // NA_ROWS_PER_STEP),
        in_specs=[
            pl.BlockSpec((None, None, tq, GROUP_WIDTH), lambda b, i: (b, 0, i, 0)),
            pl.BlockSpec((None, None, seq_total, GROUP_WIDTH), lambda b, i: (b, 0, 0, 1)),
            pl.BlockSpec((None, None, seq_total, GROUP_WIDTH), lambda b, i: (b, 0, 0, 2)),
            pl.BlockSpec((N_BIAS_VARIANTS, N_HEADS_B, GRID_W, kh * GRID_W), lambda b, i: (0, 0, 0, 0)),
        ],
        out_specs=pl.BlockSpec((None, tq, GROUP_WIDTH), lambda b, i: (b, i, 0)),
        out_shape=jax.ShapeDtypeStruct((batch, seq_total, GROUP_WIDTH), BF16),
        compiler_params=_params(("arbitrary", "arbitrary")),
        name="neighbourhood_attention",
    )(qkv_b, qkv_b, qkv_b, bias)
    return o.reshape(batch * seq_total, GROUP_WIDTH)


def _to_token_order_matrix(tm, d):
    t = lax.broadcasted_iota(jnp.int32, (tm, tm), 0)
    c = lax.broadcasted_iota(jnp.int32, (tm, tm), 1)
    return (c == (t % d) * (tm // d) + t // d).astype(BF16)


def _merge_kernel(o0_ref, o1_ref, o2_ref, l0_ref, l1_ref, l2_ref, ob_ref, gate_ref, x_ref,
                  wpa_ref, wpb_ref, wout_ref, gn_ref, x2_ref, h2_ref, lse_tok_ref, *, dilations):
    tm = x_ref.shape[0]
    o_tok, lse_tok = [], []
    for g, (o_ref, l_ref, d) in enumerate(zip((o0_ref, o1_ref, o2_ref),
                                              (l0_ref, l1_ref, l2_ref), dilations)):
        o = o_ref[...].reshape(tm, GROUP_WIDTH)
        if d == 1:
            o_tok.append(o.astype(F32))
            lse_tok.append(l_ref[...].reshape(tm, HEAD_DIM))
        else:
            o_tok.append(jnp.dot(_to_token_order_matrix(tm, d), o, preferred_element_type=F32))
            for r in range(d):
                lse_tok_ref[g, pl.ds(r, tm // d, stride=d), :] = l_ref[r]
            lse_tok.append(lse_tok_ref[g])
    l0, l1, l2 = lse_tok
    mx = jnp.maximum(jnp.maximum(l0, l1), l2)
    e0, e1, e2 = jnp.exp(l0 - mx), jnp.exp(l1 - mx), jnp.exp(l2 - mx)
    tot = e0 + e1 + e2
    a0, a1, a2 = e0 / tot, e1 / tot, e2 / tot
    parts = []
    for h in range(HEADS_PER_GROUP):
        cols = slice(h * HEAD_DIM, (h + 1) * HEAD_DIM)
        y = (a0[:, h:h + 1] * o_tok[0][:, cols] + a1[:, h:h + 1] * o_tok[1][:, cols]
             + a2[:, h:h + 1] * o_tok[2][:, cols])
        parts.append(y.astype(BF16))
    ya = jnp.concatenate(parts, axis=1)
    ta = jnp.dot(ya, wpa_ref[...], preferred_element_type=F32)
    tb = jnp.dot(ob_ref[...], wpb_ref[...], preferred_element_type=F32)
    merged = (gate_ref[:, :D_MODEL].astype(F32) * ta + gate_ref[:, D_MODEL:].astype(F32) * tb)
    x2 = x_ref[...] + jnp.dot(merged.astype(BF16), wout_ref[...], preferred_element_type=F32)
    x2_ref[...] = x2
    h2_ref[...] = _rms(x2, gn_ref[...]).astype(BF16)


def _merge(o_groups, lse_groups, ob, gates, x2d, seq, wpa, wpb, wout, gn):
    m = x2d.shape[0]
    tm = MERGE_TM
    tiles_per_batch = seq // tm
    dilations = tuple(o.shape[1] for o in o_groups)
    row = lambda w: pl.BlockSpec((tm, w), lambda i: (i, 0))
    grouped = lambda d, w: pl.BlockSpec(
        (None, d, tm // d, w), lambda i: (i // tiles_per_batch, 0, i % tiles_per_batch, 0))
    const = lambda a, b: pl.BlockSpec((a, b), lambda i: (0, 0), pipeline_mode=pl.Buffered(1))
    return pl.pallas_call(
        functools.partial(_merge_kernel, dilations=dilations),
        grid=(m // tm,),
        in_specs=[grouped(d, GROUP_WIDTH) for d in dilations]
        + [grouped(d, HEAD_DIM) for d in dilations]
        + [row(GROUP_WIDTH), row(2 * D_MODEL), row(D_MODEL),
           const(GROUP_WIDTH, D_MODEL), const(GROUP_WIDTH, D_MODEL), const(D_MODEL, D_MODEL),
           const(1, D_MODEL)],
        out_specs=[row(D_MODEL), row(D_MODEL)],
        out_shape=[jax.ShapeDtypeStruct((m, D_MODEL), F32), jax.ShapeDtypeStruct((m, D_MODEL), BF16)],
        scratch_shapes=[pltpu.VMEM((len(dilations), tm, HEAD_DIM), F32)],
        compiler_params=_params(("arbitrary",)),
        name="merge_out_proj",
    )(*o_groups, *lse_groups, ob, gates, x2d, wpa, wpb, wout, gn)


def _mlp_kernel(h2_ref, wup_ref, wdn_ref, x2_hbm, gf_ref, out_ref, x2_sem, *, n_f):
    i = pl.program_id(0)
    f = pl.program_id(1)
    tm = out_ref.shape[0]

    def residual_copy():
        return pltpu.make_async_copy(x2_hbm.at[pl.ds(i * tm, tm), :], out_ref, x2_sem)

    @pl.when(f == 0)
    def _():
        residual_copy().start()

    hid = jnp.dot(h2_ref[...], wup_ref[...], preferred_element_type=F32)
    hid = jnp.square(jnp.maximum(hid, 0.0)).astype(BF16)

    @pl.when(f == 0)
    def _():
        residual_copy().wait()

    out_ref[...] += jnp.dot(hid, wdn_ref[...], preferred_element_type=F32)

    @pl.when(f == n_f - 1)
    def _():
        out_ref[...] = _rms(out_ref[...], gf_ref[...])


def _mlp(h2, wup, wdn, x2, gf):
    m = h2.shape[0]
    tm, tf = MLP_TM, MLP_TF
    n_f = D_FF // tf
    return pl.pallas_call(
        functools.partial(_mlp_kernel, n_f=n_f),
        grid=(m // tm, n_f),
        in_specs=[
            pl.BlockSpec((tm, D_MODEL), lambda i, f: (i, 0)),
            pl.BlockSpec((D_MODEL, tf), lambda i, f: (0, f)),
            pl.BlockSpec((tf, D_MODEL), lambda i, f: (f, 0)),
            pl.BlockSpec(memory_space=pl.ANY),
            pl.BlockSpec((1, D_MODEL), lambda i, f: (0, 0)),
        ],
        out_specs=pl.BlockSpec((tm, D_MODEL), lambda i, f: (i, 0)),
        out_shape=jax.ShapeDtypeStruct((m, D_MODEL), F32),
        scratch_shapes=[pltpu.SemaphoreType.DMA],
        compiler_params=_params(("arbitrary", "arbitrary")),
        name="mlp_residual_norm",
    )(h2, wup, wdn, x2, gf)


def _layer(x2d, batch, seq, norm_mix, w_qkv, w_gate, b_gate, rpb, w_proj_a, w_proj_b, w_out,
           norm_mlp):
    slopes = 2.0 ** (-8.0 * np.arange(1, N_HEADS_A + 1) / N_HEADS_A)
    row = lambda v: v.reshape(1, -1)
    *qkv_groups, gates = _project(x2d, batch, seq, row(norm_mix), w_qkv.astype(BF16),
                                  w_gate.astype(BF16), row(b_gate))
    o_groups, lse_groups = [], []
    for g, (window, d) in enumerate(DILATION_PATTERNS):
        o, lse = _dilated_group(qkv_groups[g], g, window, d, slopes)
        o_groups.append(o)
        lse_groups.append(lse)
    ob = _neighbourhood(qkv_groups[N_GROUPS_A], rpb)
    return _merge(o_groups, lse_groups, ob, gates, x2d, seq, w_proj_a.astype(BF16),
                  w_proj_b.astype(BF16), w_out.astype(BF16), row(norm_mlp))


def kernel(x, norm_mix, w_qkv, w_gate, b_gate, rpb, w_proj_a, w_proj_b, w_out, norm_mlp, w_up,
           w_down, norm_final):
    batch, seq, _ = x.shape
    depth = norm_mix.shape[0]
    assert depth == 1
    x2d = x.reshape(batch * seq, D_MODEL)
    x2, h2 = _layer(x2d, batch, seq, norm_mix[0], w_qkv[0], w_gate[0], b_gate[0], rpb[0],
                    w_proj_a[0], w_proj_b[0], w_out[0], norm_mlp[0])
    out = _mlp(h2, w_up[0].astype(BF16), w_down[0].astype(BF16), x2, norm_final.reshape(1, -1))
    return out.reshape(batch, seq, D_MODEL)
```

```python
import functools

import jax
import jax.numpy as jnp
import numpy as np
from jax import lax
from jax.experimental import pallas as pl
from jax.experimental.pallas import tpu as pltpu

D_MODEL = 2048
HEAD_DIM = 128
N_HEADS = D_MODEL // HEAD_DIM
N_HEADS_B = N_HEADS // 4
N_HEADS_A = N_HEADS - N_HEADS_B
DILATION_PATTERNS = ((128, 1), (512, 4), (2048, 16))
N_GROUPS_A = len(DILATION_PATTERNS)
HEADS_PER_GROUP = N_HEADS_A // N_GROUPS_A
GROUP_WIDTH = HEADS_PER_GROUP * HEAD_DIM
N_HEAD_GROUPS = N_HEADS // HEADS_PER_GROUP
QKV_WIDTH = 3 * D_MODEL
GRID_W = 64
NA_ROWS = 8
NA_COLS = 16
D_FF = 4 * D_MODEL
EPS = 1e-6
NEG = -1e30
SCALE = HEAD_DIM ** -0.5
LANES = 128

F32 = jnp.float32
BF16 = jnp.bfloat16

VMEM_LIMIT_BYTES = 56 * 1024 * 1024

PROJ_TM, PROJ_TN = 1024, GROUP_WIDTH
PROJ_NORM_CHUNK = 512
MERGE_TM = 256
MLP_TM, MLP_TF = 1024, 1024
DIL_QB = 128
DIL_UNROLL = 2
NA_ROWS_PER_STEP = 8
NA_UNROLL = 4

GROUP_DILATIONS = tuple(d for _, d in DILATION_PATTERNS) + (1,)


def _params(sem):
    return pltpu.CompilerParams(dimension_semantics=sem, vmem_limit_bytes=VMEM_LIMIT_BYTES)


def _rms(x, g):
    ms = jnp.mean(x * x, axis=-1, keepdims=True)
    return (x * lax.rsqrt(ms + EPS)) * g


def _proj_kernel(x_ref, g_ref, wq_ref, wg_ref, bg_ref, *refs, dilations):
    n_groups = len(dilations)
    out_refs = refs[:n_groups]
    gate_ref = refs[n_groups]
    perm_ds = sorted(set(d for d in dilations if d > 1))
    h_refs = dict(zip([1] + perm_ds, refs[n_groups + 1:n_groups + 2 + len(perm_ds)]))
    slab_ref = refs[-1]
    tm = x_ref.shape[0]
    j = pl.program_id(1)

    @pl.when(j == 0)
    def _():
        x = x_ref[...]
        inv = lax.rsqrt(jnp.mean(x * x, axis=-1, keepdims=True) + EPS)
        n_slabs = PROJ_NORM_CHUNK // LANES
        for c0 in range(0, D_MODEL, PROJ_NORM_CHUNK):
            hc = (x_ref[:, c0:c0 + PROJ_NORM_CHUNK] * inv) * g_ref[:, c0:c0 + PROJ_NORM_CHUNK]
            h_refs[1][:, c0:c0 + PROJ_NORM_CHUNK] = hc.astype(BF16)
            for s in range(n_slabs):
                slab_ref[s] = hc[:, s * LANES:(s + 1) * LANES]
            for d in perm_ds:
                n = tm // d
                for s in range(n_slabs):
                    cols = slice(c0 + s * LANES, c0 + (s + 1) * LANES)
                    for r in range(d):
                        h_refs[d][r * n:(r + 1) * n, cols] = (
                            slab_ref[s, pl.ds(r, n, stride=d), :].astype(BF16))

    for grp, d in enumerate(dilations):
        @pl.when(j // 3 == grp)
        def _(grp=grp, d=d):
            y = jnp.dot(h_refs[d][...], wq_ref[...], preferred_element_type=F32)
            out_refs[grp][...] = y.astype(BF16).reshape(d, tm // d, y.shape[-1])

    @pl.when(j >= 3 * n_groups)
    def _():
        z = jnp.dot(h_refs[1][...], wg_ref[...], preferred_element_type=F32) + bg_ref[...]
        gate_ref[...] = jax.nn.sigmoid(z).astype(BF16)


def _project(x2d, batch, seq, g, w_qkv, w_gate, b_gate):
    m = x2d.shape[0]
    tm, tn = PROJ_TM, PROJ_TN
    dilations = GROUP_DILATIONS
    n_groups = len(dilations)
    nq = 3 * n_groups
    ng = (2 * D_MODEL) // tn
    tiles_per_batch = seq // tm
    perm_ds = sorted(set(d for d in dilations if d > 1))

    def w_qkv_map(i, j):
        jj = jnp.minimum(j, nq - 1)
        return (0, (jj % 3) * n_groups + jj // 3)

    def out_map(grp):
        return lambda i, j: (i // tiles_per_batch, 0, i % tiles_per_batch,
                             jnp.clip(j - 3 * grp, 0, 2))

    gate_col = lambda i, j: (0, jnp.maximum(j - nq, 0))
    return pl.pallas_call(
        functools.partial(_proj_kernel, dilations=dilations),
        grid=(m // tm, nq + ng),
        in_specs=[
            pl.BlockSpec((tm, D_MODEL), lambda i, j: (i, 0)),
            pl.BlockSpec((1, D_MODEL), lambda i, j: (0, 0)),
            pl.BlockSpec((D_MODEL, tn), w_qkv_map),
            pl.BlockSpec((D_MODEL, tn), gate_col),
            pl.BlockSpec((1, tn), gate_col),
        ],
        out_specs=[pl.BlockSpec((None, d, tm // d, tn), out_map(grp))
                   for grp, d in enumerate(dilations)]
        + [pl.BlockSpec((tm, tn), lambda i, j: (i, jnp.maximum(j - nq, 0)))],
        out_shape=[jax.ShapeDtypeStruct((batch, d, seq // d, 3 * GROUP_WIDTH), BF16)
                   for d in dilations]
        + [jax.ShapeDtypeStruct((m, 2 * D_MODEL), BF16)],
        scratch_shapes=[pltpu.VMEM((tm, D_MODEL), BF16) for _ in range(1 + len(perm_ds))]
        + [pltpu.VMEM((PROJ_NORM_CHUNK // LANES, tm, LANES), F32)],
        compiler_params=_params(("arbitrary", "arbitrary")),
        name="proj_qkv_gate",
    )(x2d, g, w_qkv, w_gate, b_gate)


def _dilated_kernel(q_ref, k_ref, v_ref, o_ref, lse_ref, *, seq, half_window, coefs):
    qb = DIL_QB
    kw = min(seq, qb + 2 * half_window)
    n_blocks = seq // qb
    qrow = lax.broadcasted_iota(jnp.int32, (qb, kw), 0)
    kcol = lax.broadcasted_iota(jnp.int32, (qb, kw), 1)
    lane = lax.broadcasted_iota(jnp.int32, (qb, HEAD_DIM), 1)

    def block(qi, carry):
        q0 = pl.multiple_of(qi * qb, qb)
        ks = pl.multiple_of(jnp.clip(q0 - half_window, 0, seq - kw), half_window)
        dist = jnp.abs(kcol - qrow + (ks - q0))
        valid = dist <= half_window
        dist_f = dist.astype(F32)
        heads = range(HEADS_PER_GROUP)
        head_cols = [slice(h * HEAD_DIM, (h + 1) * HEAD_DIM) for h in heads]
        scores = [lax.dot_general(q_ref[pl.ds(q0, qb), c], k_ref[pl.ds(ks, kw), c],
                                  (((1,), (1,)), ((), ())), preferred_element_type=F32)
                  for c in head_cols]
        probs, dens = [], []
        lse_all = jnp.zeros((qb, HEAD_DIM), F32)
        for h in heads:
            s = jnp.where(valid, scores[h] * SCALE - coefs[h] * dist_f, NEG)
            m = jnp.max(s, axis=-1, keepdims=True)
            p = jnp.exp(s - m)
            den = jnp.sum(p, axis=-1, keepdims=True)
            probs.append(p.astype(BF16))
            dens.append(den)
            lse_all = jnp.where(lane == h, m + jnp.log(den), lse_all)
        for h in heads:
            o = jnp.dot(probs[h], v_ref[pl.ds(ks, kw), head_cols[h]],
                        preferred_element_type=F32) / dens[h]
            o_ref[pl.ds(q0, qb), head_cols[h]] = o.astype(BF16)
        lse_ref[pl.ds(q0, qb), :] = lse_all
        return carry

    lax.fori_loop(0, n_blocks, block, 0, unroll=DIL_UNROLL)


def _dilated_group(qkv_g, group, window, dilation, slopes):
    batch, d, seq, _ = qkv_g.shape
    half_window = window // (2 * d)
    coefs = tuple(float(slopes[group * HEADS_PER_GROUP + h]) * d for h in range(HEADS_PER_GROUP))
    part = lambda which: pl.BlockSpec((None, None, seq, GROUP_WIDTH),
                                      lambda b, r: (b, r, 0, which))
    return pl.pallas_call(
        functools.partial(_dilated_kernel, seq=seq, half_window=half_window, coefs=coefs),
        grid=(batch, d),
        in_specs=[part(0), part(1), part(2)],
        out_specs=[
            pl.BlockSpec((None, None, seq, GROUP_WIDTH), lambda b, r: (b, r, 0, 0)),
            pl.BlockSpec((None, None, seq, HEAD_DIM), lambda b, r: (b, r, 0, 0)),
        ],
        out_shape=[
            jax.ShapeDtypeStruct((batch, d, seq, GROUP_WIDTH), BF16),
            jax.ShapeDtypeStruct((batch, d, seq, HEAD_DIM), F32),
        ],
        compiler_params=_params(("arbitrary", "arbitrary")),
        name=f"dilated_attention_d{d}",
    )(qkv_g, qkv_g, qkv_g)


N_BIAS_VARIANTS = NA_ROWS


def _na_bias_kernel(rpb_ref, out_ref, *, kh):
    h = pl.program_id(0)
    n_off = 2 * NA_ROWS - 1
    width = (n_off + 1) * GRID_W
    qc = lax.broadcasted_iota(jnp.int32, (GRID_W, width), 0)
    kc = lax.broadcasted_iota(jnp.int32, (GRID_W, width), 1) % GRID_W
    col_idx = jnp.clip(kc - qc, -(NA_COLS - 1), NA_COLS - 1) + (NA_COLS - 1)
    off_row = lax.broadcasted_iota(jnp.int32, (1, width), 1) // GRID_W
    table = jnp.zeros((GRID_W, width), F32)
    for j in range(2 * NA_COLS - 1):
        row_vals = jnp.zeros((1, width), F32)
        for a in range(n_off):
            row_vals = jnp.where(off_row == a, rpb_ref[h, a, j], row_vals)
        table = jnp.where(col_idx == j, row_vals, table)
    for var in range(N_BIAS_VARIANTS):
        out_ref[var] = table[:, var * GRID_W:(var + kh) * GRID_W]


def _na_bias(rpb, kh):
    nkeys = kh * GRID_W
    return pl.pallas_call(
        functools.partial(_na_bias_kernel, kh=kh),
        grid=(N_HEADS_B,),
        in_specs=[pl.BlockSpec(memory_space=pltpu.SMEM)],
        out_specs=pl.BlockSpec((N_BIAS_VARIANTS, None, GRID_W, nkeys), lambda h: (0, h, 0, 0)),
        out_shape=jax.ShapeDtypeStruct((N_BIAS_VARIANTS, N_HEADS_B, GRID_W, nkeys), F32),
        compiler_params=_params(("arbitrary",)),
        name="na_bias_expand",
    )(rpb)


def _na_kernel(q_ref, k_ref, v_ref, bias_ref, o_ref, *, rows, kh):
    rb = pl.program_id(1)
    nkeys = kh * GRID_W
    qc = lax.broadcasted_iota(jnp.int32, (GRID_W, nkeys), 0)
    kc = lax.broadcasted_iota(jnp.int32, (GRID_W, nkeys), 1) % GRID_W
    cs = jnp.clip(qc - NA_COLS // 2, 0, GRID_W - NA_COLS)
    col_ok = (kc >= cs) & (kc < cs + NA_COLS)

    def one_row(rl, carry):
        r = rb * NA_ROWS_PER_STEP + rl
        rs = jnp.clip(r - kh // 2, 0, rows - kh)
        var = rs - r + (NA_ROWS - 1)
        q0 = pl.multiple_of(rl * GRID_W, GRID_W)
        k0 = pl.multiple_of(rs * GRID_W, GRID_W)
        heads = range(N_HEADS_B)
        head_cols = [slice(h * HEAD_DIM, (h + 1) * HEAD_DIM) for h in heads]
        scores = [lax.dot_general(q_ref[pl.ds(q0, GRID_W), c], k_ref[pl.ds(k0, nkeys), c],
                                  (((1,), (1,)), ((), ())), preferred_element_type=F32)
                  for c in head_cols]
        probs, dens = [], []
        for h in heads:
            s = jnp.where(col_ok, scores[h] * SCALE + bias_ref[var, h], NEG)
            m = jnp.max(s, axis=-1, keepdims=True)
            p = jnp.exp(s - m)
            dens.append(jnp.sum(p, axis=-1, keepdims=True))
            probs.append(p.astype(BF16))
        for h in heads:
            o = jnp.dot(probs[h], v_ref[pl.ds(k0, nkeys), head_cols[h]],
                        preferred_element_type=F32) / dens[h]
            o_ref[pl.ds(q0, GRID_W), head_cols[h]] = o.astype(BF16)
        return carry

    lax.fori_loop(0, NA_ROWS_PER_STEP, one_row, 0, unroll=NA_UNROLL)


def _neighbourhood(qkv_b, rpb):
    batch, _, seq_total, _ = qkv_b.shape
    rows = seq_total // GRID_W
    kh = min(NA_ROWS, rows)
    assert kh == NA_ROWS and rows % NA_ROWS_PER_STEP == 0
    bias = _na_bias(rpb, kh)
    tq = NA_ROWS_PER_STEP * GRID_W
    o = pl.pallas_call(
        functools.partial(_na_kernel, rows=rows, kh=kh),
        grid=(batch, rows // NA_ROWS_PER_STEP),
        in_specs=[
            pl.BlockSpec((None, None, tq, GROUP_WIDTH), lambda b, i: (b, 0, i, 0)),
            pl.BlockSpec((None, None, seq_total, GROUP_WIDTH), lambda b, i: (b, 0, 0, 1)),
            pl.BlockSpec((None, None, seq_total, GROUP_WIDTH), lambda b, i: (b, 0, 0, 2)),
            pl.BlockSpec((N_BIAS_VARIANTS, N_HEADS_B, GRID_W, kh * GRID_W), lambda b, i: (0, 0, 0, 0)),
        ],
        out_specs=pl.BlockSpec((None, tq, GROUP_WIDTH), lambda b, i: (b, i, 0)),
        out_shape=jax.ShapeDtypeStruct((batch, seq_total, GROUP_WIDTH), BF16),
        compiler_params=_params(("arbitrary", "arbitrary")),
        name="neighbourhood_attention",
    )(qkv_b, qkv_b, qkv_b, bias)
    return o.reshape(batch * seq_total, GROUP_WIDTH)


def _to_token_order_matrix(tm, d):
    t = lax.broadcasted_iota(jnp.int32, (tm, tm), 0)
    c = lax.broadcasted_iota(jnp.int32, (tm, tm), 1)
    return (c == (t % d) * (tm // d) + t // d).astype(BF16)


def _merge_kernel(o0_ref, o1_ref, o2_ref, l0_ref, l1_ref, l2_ref, ob_ref, gate_ref, x_ref,
                  wpa_ref, wpb_ref, wout_ref, gn_ref, wup_ref, wdn_ref,
                  x2_ref, h2_ref, wup_bf_ref, wdn_bf_ref, lse_tok_ref, *, dilations):
    tm = x_ref.shape[0]
    wup_bf_ref[...] = wup_ref[...].astype(BF16)
    wdn_bf_ref[...] = wdn_ref[...].astype(BF16)
    o_tok, lse_tok = [], []
    for g, (o_ref, l_ref, d) in enumerate(zip((o0_ref, o1_ref, o2_ref),
                                              (l0_ref, l1_ref, l2_ref), dilations)):
        o = o_ref[...].reshape(tm, GROUP_WIDTH)
        if d == 1:
            o_tok.append(o.astype(F32))
            lse_tok.append(l_ref[...].reshape(tm, HEAD_DIM))
        else:
            o_tok.append(jnp.dot(_to_token_order_matrix(tm, d), o, preferred_element_type=F32))
            for r in range(d):
                lse_tok_ref[g, pl.ds(r, tm // d, stride=d), :] = l_ref[r]
            lse_tok.append(lse_tok_ref[g])
    l0, l1, l2 = lse_tok
    mx = jnp.maximum(jnp.maximum(l0, l1), l2)
    e0, e1, e2 = jnp.exp(l0 - mx), jnp.exp(l1 - mx), jnp.exp(l2 - mx)
    tot = e0 + e1 + e2
    a0, a1, a2 = e0 / tot, e1 / tot, e2 / tot
    parts = []
    for h in range(HEADS_PER_GROUP):
        cols = slice(h * HEAD_DIM, (h + 1) * HEAD_DIM)
        y = (a0[:, h:h + 1] * o_tok[0][:, cols] + a1[:, h:h + 1] * o_tok[1][:, cols]
             + a2[:, h:h + 1] * o_tok[2][:, cols])
        parts.append(y.astype(BF16))
    ya = jnp.concatenate(parts, axis=1)
    ta = jnp.dot(ya, wpa_ref[...], preferred_element_type=F32)
    tb = jnp.dot(ob_ref[...], wpb_ref[...], preferred_element_type=F32)
    merged = (gate_ref[:, :D_MODEL].astype(F32) * ta + gate_ref[:, D_MODEL:].astype(F32) * tb)
    x2 = x_ref[...] + jnp.dot(merged.astype(BF16), wout_ref[...], preferred_element_type=F32)
    x2_ref[...] = x2
    h2_ref[...] = _rms(x2, gn_ref[...]).astype(BF16)


def _merge(o_groups, lse_groups, ob, gates, x2d, seq, wpa, wpb, wout, gn, w_up, w_down):
    m = x2d.shape[0]
    tm = MERGE_TM
    n_steps = m // tm
    tiles_per_batch = seq // tm
    dilations = tuple(o.shape[1] for o in o_groups)
    row = lambda w: pl.BlockSpec((tm, w), lambda i: (i, 0))
    wup_rows = pl.BlockSpec((w_up.shape[0] // n_steps, w_up.shape[1]), lambda i: (i, 0))
    wdn_rows = pl.BlockSpec((w_down.shape[0] // n_steps, w_down.shape[1]), lambda i: (i, 0))
    grouped = lambda d, w: pl.BlockSpec(
        (None, d, tm // d, w), lambda i: (i // tiles_per_batch, 0, i % tiles_per_batch, 0))
    const = lambda a, b: pl.BlockSpec((a, b), lambda i: (0, 0), pipeline_mode=pl.Buffered(1))
    return pl.pallas_call(
        functools.partial(_merge_kernel, dilations=dilations),
        grid=(m // tm,),
        in_specs=[grouped(d, GROUP_WIDTH) for d in dilations]
        + [grouped(d, HEAD_DIM) for d in dilations]
        + [row(GROUP_WIDTH), row(2 * D_MODEL), row(D_MODEL),
           const(GROUP_WIDTH, D_MODEL), const(GROUP_WIDTH, D_MODEL), const(D_MODEL, D_MODEL),
           const(1, D_MODEL), wup_rows, wdn_rows],
        out_specs=[row(D_MODEL), row(D_MODEL), wup_rows, wdn_rows],
        out_shape=[jax.ShapeDtypeStruct((m, D_MODEL), F32), jax.ShapeDtypeStruct((m, D_MODEL), BF16),
                   jax.ShapeDtypeStruct(w_up.shape, BF16), jax.ShapeDtypeStruct(w_down.shape, BF16)],
        scratch_shapes=[pltpu.VMEM((len(dilations), tm, HEAD_DIM), F32)],
        compiler_params=_params(("arbitrary",)),
        name="merge_out_proj",
    )(*o_groups, *lse_groups, ob, gates, x2d, wpa, wpb, wout, gn, w_up, w_down)


def _mlp_kernel(h2_ref, wup_ref, wdn_ref, x2_hbm, gf_ref, out_ref, x2_sem, *, n_f):
    i = pl.program_id(0)
    f = pl.program_id(1)
    tm = out_ref.shape[0]

    def residual_copy():
        return pltpu.make_async_copy(x2_hbm.at[pl.ds(i * tm, tm), :], out_ref, x2_sem)

    @pl.when(f == 0)
    def _():
        residual_copy().start()

    hid = jnp.dot(h2_ref[...], wup_ref[...], preferred_element_type=F32)
    hid = jnp.square(jnp.maximum(hid, 0.0)).astype(BF16)

    @pl.when(f == 0)
    def _():
        residual_copy().wait()

    out_ref[...] += jnp.dot(hid, wdn_ref[...], preferred_element_type=F32)

    @pl.when(f == n_f - 1)
    def _():
        out_ref[...] = _rms(out_ref[...], gf_ref[...])


def _mlp(h2, wup, wdn, x2, gf):
    m = h2.shape[0]
    tm, tf = MLP_TM, MLP_TF
    n_f = D_FF // tf
    return pl.pallas_call(
        functools.partial(_mlp_kernel, n_f=n_f),
        grid=(m // tm, n_f),
        in_specs=[
            pl.BlockSpec((tm, D_MODEL), lambda i, f: (i, 0)),
            pl.BlockSpec((D_MODEL, tf), lambda i, f: (0, f)),
            pl.BlockSpec((tf, D_MODEL), lambda i, f: (f, 0)),
            pl.BlockSpec(memory_space=pl.ANY),
            pl.BlockSpec((1, D_MODEL), lambda i, f: (0, 0)),
        ],
        out_specs=pl.BlockSpec((tm, D_MODEL), lambda i, f: (i, 0)),
        out_shape=jax.ShapeDtypeStruct((m, D_MODEL), F32),
        scratch_shapes=[pltpu.SemaphoreType.DMA],
        compiler_params=_params(("arbitrary", "arbitrary")),
        name="mlp_residual_norm",
    )(h2, wup, wdn, x2, gf)


def _layer(x2d, batch, seq, norm_mix, w_qkv, w_gate, b_gate, rpb, w_proj_a, w_proj_b, w_out,
           norm_mlp, w_up, w_down):
    slopes = 2.0 ** (-8.0 * np.arange(1, N_HEADS_A + 1) / N_HEADS_A)
    row = lambda v: v.reshape(1, -1)
    *qkv_groups, gates = _project(x2d, batch, seq, row(norm_mix), w_qkv.astype(BF16),
                                  w_gate.astype(BF16), row(b_gate))
    o_groups, lse_groups = [], []
    for g, (window, d) in enumerate(DILATION_PATTERNS):
        o, lse = _dilated_group(qkv_groups[g], g, window, d, slopes)
        o_groups.append(o)
        lse_groups.append(lse)
    ob = _neighbourhood(qkv_groups[N_GROUPS_A], rpb)
    return _merge(o_groups, lse_groups, ob, gates, x2d, seq, w_proj_a.astype(BF16),
                  w_proj_b.astype(BF16), w_out.astype(BF16), row(norm_mlp), w_up, w_down)


def kernel(x, norm_mix, w_qkv, w_gate, b_gate, rpb, w_proj_a, w_proj_b, w_out, norm_mlp, w_up,
           w_down, norm_final):
    batch, seq, _ = x.shape
    depth = norm_mix.shape[0]
    assert depth == 1
    x2d = x.reshape(batch * seq, D_MODEL)
    x2, h2, w_up_bf, w_down_bf = _layer(
        x2d, batch, seq, norm_mix[0], w_qkv[0], w_gate[0], b_gate[0], rpb[0],
        w_proj_a[0], w_proj_b[0], w_out[0], norm_mlp[0], w_up[0], w_down[0])
    out = _mlp(h2, w_up_bf, w_down_bf, x2, norm_final.reshape(1, -1))
    return out.reshape(batch, seq, D_MODEL)
```

```python
import functools

import jax
import jax.numpy as jnp
import numpy as np
from jax import lax
from jax.experimental import pallas as pl
from jax.experimental.pallas import tpu as pltpu

D_MODEL = 2048
HEAD_DIM = 128
N_HEADS = D_MODEL // HEAD_DIM
N_HEADS_B = N_HEADS // 4
N_HEADS_A = N_HEADS - N_HEADS_B
DILATION_PATTERNS = ((128, 1), (512, 4), (2048, 16))
N_GROUPS_A = len(DILATION_PATTERNS)
HEADS_PER_GROUP = N_HEADS_A // N_GROUPS_A
GROUP_WIDTH = HEADS_PER_GROUP * HEAD_DIM
N_HEAD_GROUPS = N_HEADS // HEADS_PER_GROUP
QKV_WIDTH = 3 * D_MODEL
GRID_W = 64
NA_ROWS = 8
NA_COLS = 16
D_FF = 4 * D_MODEL
EPS = 1e-6
NEG = -1e30
SCALE = HEAD_DIM ** -0.5
LANES = 128

F32 = jnp.float32
BF16 = jnp.bfloat16

VMEM_LIMIT_BYTES = 56 * 1024 * 1024

PERM_TILE = 1024
NORM_CHUNK = 512
PROJ_TM, PROJ_TN = 2048, GROUP_WIDTH
SIDE_CAST_ROWS = 32
MERGE_TM = 256
MLP_TM, MLP_TF = 1024, 1024
DIL_QB = 128
DIL_UNROLL = 2
NA_ROWS_PER_STEP = 8
NA_UNROLL = 4

GROUP_DILATIONS = tuple(d for _, d in DILATION_PATTERNS) + (1,)
ROW_ORDERS = tuple(sorted(set(GROUP_DILATIONS)))
GROUP_ROW_ORDER = tuple(ROW_ORDERS.index(d) for d in GROUP_DILATIONS)


def _params(sem):
    return pltpu.CompilerParams(dimension_semantics=sem, vmem_limit_bytes=VMEM_LIMIT_BYTES)


def _rms(x, g):
    ms = jnp.mean(x * x, axis=-1, keepdims=True)
    return (x * lax.rsqrt(ms + EPS)) * g


def _norm_kernel(x_ref, g_ref, h_ref, slab_ref):
    tm = x_ref.shape[0]
    x = x_ref[...]
    inv = lax.rsqrt(jnp.mean(x * x, axis=-1, keepdims=True) + EPS)
    n_slabs = NORM_CHUNK // LANES
    for c0 in range(0, D_MODEL, NORM_CHUNK):
        hc = (x_ref[:, c0:c0 + NORM_CHUNK] * inv) * g_ref[:, c0:c0 + NORM_CHUNK]
        h_ref[0, :, c0:c0 + NORM_CHUNK] = hc.astype(BF16)
        for s in range(n_slabs):
            slab_ref[s] = hc[:, s * LANES:(s + 1) * LANES]
        for v, d in enumerate(ROW_ORDERS):
            if d == 1:
                continue
            n = tm // d
            for s in range(n_slabs):
                cols = slice(c0 + s * LANES, c0 + (s + 1) * LANES)
                for r in range(d):
                    h_ref[v, r * n:(r + 1) * n, cols] = (
                        slab_ref[s, pl.ds(r, n, stride=d), :].astype(BF16))


def _norm(x2d, g):
    m = x2d.shape[0]
    tm = PERM_TILE
    nv = len(ROW_ORDERS)
    assert ROW_ORDERS[0] == 1
    return pl.pallas_call(
        _norm_kernel,
        grid=(m // tm,),
        in_specs=[pl.BlockSpec((tm, D_MODEL), lambda i: (i, 0)),
                  pl.BlockSpec((1, D_MODEL), lambda i: (0, 0))],
        out_specs=pl.BlockSpec((nv, tm, D_MODEL), lambda i: (0, i, 0)),
        out_shape=jax.ShapeDtypeStruct((nv, m, D_MODEL), BF16),
        scratch_shapes=[pltpu.VMEM((NORM_CHUNK // LANES, tm, LANES), F32)],
        compiler_params=_params(("arbitrary",)),
        name="rmsnorm_row_orders",
    )(x2d, g)


def _proj_kernel(h_ref, wq_ref, wg_ref, bg_ref, wout_ref, wpa_ref, wpb_ref,
                 qkv_ref, gate_ref, wout_bf_ref, wpa_bf_ref, wpb_bf_ref, w_bf_ref, *, n_qkv_steps):
    j = pl.program_id(0)
    i = pl.program_id(1)
    is_qkv = j < n_qkv_steps

    @pl.when((i == 0) & is_qkv)
    def _():
        w_bf_ref[...] = wq_ref[...].astype(BF16)

    @pl.when((i == 0) & jnp.logical_not(is_qkv))
    def _():
        w_bf_ref[...] = wg_ref[...].astype(BF16)

    def cast_merge_weights():
        wout_bf_ref[...] = wout_ref[...].astype(BF16)
        wpa_bf_ref[...] = wpa_ref[...].astype(BF16)
        wpb_bf_ref[...] = wpb_ref[...].astype(BF16)

    @pl.when(is_qkv)
    def _():
        cast_merge_weights()
        y = jnp.dot(h_ref[...], w_bf_ref[...], preferred_element_type=F32)
        qkv_ref[...] = y.astype(BF16)

    @pl.when(jnp.logical_not(is_qkv))
    def _():
        cast_merge_weights()
        z = jnp.dot(h_ref[...], w_bf_ref[...], preferred_element_type=F32) + bg_ref[...]
        gate_ref[...] = jax.nn.sigmoid(z).astype(BF16)


def _project(h_orders, w_qkv, w_gate, b_gate, w_out, w_proj_a, w_proj_b):
    m = h_orders.shape[1]
    tm, tn = PROJ_TM, PROJ_TN
    n_groups = N_HEAD_GROUPS
    nq = 3 * n_groups
    ng = (2 * D_MODEL) // tn
    n_i = m // tm
    row_order = GROUP_ROW_ORDER

    def lhs_map(j, i):
        v = jnp.int32(0)
        for grp in range(n_groups):
            v = jnp.where(j // 3 == grp, row_order[grp], v)
        return (v, i, 0)

    def w_qkv_map(j, i):
        jj = jnp.minimum(j, nq - 1)
        return (0, (jj % 3) * n_groups + jj // 3)

    gate_col = lambda j, i: (0, jnp.maximum(j - nq, 0))

    def qkv_out_map(j, i):
        jj = jnp.minimum(j, nq - 1)
        return (jj // 3, jnp.where(j < nq, i, n_i - 1), jj % 3)

    gate_out_map = lambda j, i: (jnp.where(j >= nq, i, 0), jnp.maximum(j - nq, 0))

    def side_rows(w):
        n_blocks = w.shape[0] // SIDE_CAST_ROWS
        return pl.BlockSpec((SIDE_CAST_ROWS, w.shape[1]),
                            lambda j, i: (jnp.minimum(j * n_i + i, n_blocks - 1), 0))

    side = [w_out, w_proj_a, w_proj_b]
    assert all(w.shape[0] // SIDE_CAST_ROWS <= (nq + ng) * n_i for w in side)
    return pl.pallas_call(
        functools.partial(_proj_kernel, n_qkv_steps=nq),
        grid=(nq + ng, n_i),
        in_specs=[
            pl.BlockSpec((None, tm, D_MODEL), lhs_map),
            pl.BlockSpec((D_MODEL, tn), w_qkv_map),
            pl.BlockSpec((D_MODEL, tn), gate_col),
            pl.BlockSpec((1, tn), gate_col),
        ] + [side_rows(w) for w in side],
        out_specs=[
            pl.BlockSpec((None, tm, tn), qkv_out_map),
            pl.BlockSpec((tm, tn), gate_out_map),
        ] + [side_rows(w) for w in side],
        out_shape=[
            jax.ShapeDtypeStruct((n_groups, m, 3 * GROUP_WIDTH), BF16),
            jax.ShapeDtypeStruct((m, 2 * D_MODEL), BF16),
        ] + [jax.ShapeDtypeStruct(w.shape, BF16) for w in side],
        scratch_shapes=[pltpu.VMEM((D_MODEL, tn), BF16)],
        compiler_params=_params(("arbitrary", "arbitrary")),
        name="proj_qkv_gate",
    )(h_orders, w_qkv, w_gate, b_gate, *side)


def _dilated_kernel(q_ref, k_ref, v_ref, o_ref, lse_ref, *, seq, half_window, coefs):
    q_ref, k_ref, v_ref = (ref.reshape(seq, GROUP_WIDTH) for ref in (q_ref, k_ref, v_ref))
    qb = DIL_QB
    kw = min(seq, qb + 2 * half_window)
    n_blocks = seq // qb
    qrow = lax.broadcasted_iota(jnp.int32, (qb, kw), 0)
    kcol = lax.broadcasted_iota(jnp.int32, (qb, kw), 1)
    lane = lax.broadcasted_iota(jnp.int32, (qb, HEAD_DIM), 1)

    def block(qi, carry):
        q0 = pl.multiple_of(qi * qb, qb)
        ks = pl.multiple_of(jnp.clip(q0 - half_window, 0, seq - kw), half_window)
        dist = jnp.abs(kcol - qrow + (ks - q0))
        valid = dist <= half_window
        dist_f = dist.astype(F32)
        heads = range(HEADS_PER_GROUP)
        head_cols = [slice(h * HEAD_DIM, (h + 1) * HEAD_DIM) for h in heads]
        scores = [lax.dot_general(q_ref[pl.ds(q0, qb), c], k_ref[pl.ds(ks, kw), c],
                                  (((1,), (1,)), ((), ())), preferred_element_type=F32)
                  for c in head_cols]
        probs, dens = [], []
        lse_all = jnp.zeros((qb, HEAD_DIM), F32)
        for h in heads:
            s = jnp.where(valid, scores[h] * SCALE - coefs[h] * dist_f, NEG)
            m = jnp.max(s, axis=-1, keepdims=True)
            p = jnp.exp(s - m)
            den = jnp.sum(p, axis=-1, keepdims=True)
            probs.append(p.astype(BF16))
            dens.append(den)
            lse_all = jnp.where(lane == h, m + jnp.log(den), lse_all)
        for h in heads:
            o = jnp.dot(probs[h], v_ref[pl.ds(ks, kw), head_cols[h]],
                        preferred_element_type=F32) / dens[h]
            o_ref[pl.ds(q0, qb), head_cols[h]] = o.astype(BF16)
        lse_ref[pl.ds(q0, qb), :] = lse_all
        return carry

    lax.fori_loop(0, n_blocks, block, 0, unroll=DIL_UNROLL)


def _dilated_group(qkv, batch, seq_total, group, window, dilation, slopes):
    d = dilation
    seq = seq_total // d
    half_window = window // (2 * d)
    coefs = tuple(float(slopes[group * HEADS_PER_GROUP + h]) * d for h in range(HEADS_PER_GROUP))
    tiles = seq_total // PERM_TILE
    rows = PERM_TILE // d
    view = qkv.reshape(qkv.shape[0], batch, tiles, d, rows, qkv.shape[-1])
    part = lambda which: pl.BlockSpec((None, None, tiles, None, rows, GROUP_WIDTH),
                                      lambda b, r: (group, b, 0, r, 0, which))
    return pl.pallas_call(
        functools.partial(_dilated_kernel, seq=seq, half_window=half_window, coefs=coefs),
        grid=(batch, d),
        in_specs=[part(0), part(1), part(2)],
        out_specs=[
            pl.BlockSpec((None, None, seq, GROUP_WIDTH), lambda b, r: (b, r, 0, 0)),
            pl.BlockSpec((None, None, seq, HEAD_DIM), lambda b, r: (b, r, 0, 0)),
        ],
        out_shape=[
            jax.ShapeDtypeStruct((batch, d, seq, GROUP_WIDTH), BF16),
            jax.ShapeDtypeStruct((batch, d, seq, HEAD_DIM), F32),
        ],
        compiler_params=_params(("arbitrary", "arbitrary")),
        name=f"dilated_attention_d{d}",
    )(view, view, view)


N_BIAS_VARIANTS = NA_ROWS


def _na_bias_kernel(rpb_ref, out_ref, *, kh):
    h = pl.program_id(0)
    n_off = 2 * NA_ROWS - 1
    width = (n_off + 1) * GRID_W
    qc = lax.broadcasted_iota(jnp.int32, (GRID_W, width), 0)
    kc = lax.broadcasted_iota(jnp.int32, (GRID_W, width), 1) % GRID_W
    col_idx = jnp.clip(kc - qc, -(NA_COLS - 1), NA_COLS - 1) + (NA_COLS - 1)
    off_row = lax.broadcasted_iota(jnp.int32, (1, width), 1) // GRID_W
    table = jnp.zeros((GRID_W, width), F32)
    for j in range(2 * NA_COLS - 1):
        row_vals = jnp.zeros((1, width), F32)
        for a in range(n_off):
            row_vals = jnp.where(off_row == a, rpb_ref[h, a, j], row_vals)
        table = jnp.where(col_idx == j, row_vals, table)
    for var in range(N_BIAS_VARIANTS):
        out_ref[var] = table[:, var * GRID_W:(var + kh) * GRID_W]


def _na_bias(rpb, kh):
    nkeys = kh * GRID_W
    return pl.pallas_call(
        functools.partial(_na_bias_kernel, kh=kh),
        grid=(N_HEADS_B,),
        in_specs=[pl.BlockSpec(memory_space=pltpu.SMEM)],
        out_specs=pl.BlockSpec((N_BIAS_VARIANTS, None, GRID_W, nkeys), lambda h: (0, h, 0, 0)),
        out_shape=jax.ShapeDtypeStruct((N_BIAS_VARIANTS, N_HEADS_B, GRID_W, nkeys), F32),
        compiler_params=_params(("arbitrary",)),
        name="na_bias_expand",
    )(rpb)


def _na_kernel(q_ref, k_ref, v_ref, bias_ref, o_ref, *, rows, kh):
    rb = pl.program_id(1)
    nkeys = kh * GRID_W
    qc = lax.broadcasted_iota(jnp.int32, (GRID_W, nkeys), 0)
    kc = lax.broadcasted_iota(jnp.int32, (GRID_W, nkeys), 1) % GRID_W
    cs = jnp.clip(qc - NA_COLS // 2, 0, GRID_W - NA_COLS)
    col_ok = (kc >= cs) & (kc < cs + NA_COLS)

    def one_row(rl, carry):
        r = rb * NA_ROWS_PER_STEP + rl
        rs = jnp.clip(r - kh // 2, 0, rows - kh)
        var = rs - r + (NA_ROWS - 1)
        q0 = pl.multiple_of(rl * GRID_W, GRID_W)
        k0 = pl.multiple_of(rs * GRID_W, GRID_W)
        heads = range(N_HEADS_B)
        head_cols = [slice(h * HEAD_DIM, (h + 1) * HEAD_DIM) for h in heads]
        scores = [lax.dot_general(q_ref[pl.ds(q0, GRID_W), c], k_ref[pl.ds(k0, nkeys), c],
                                  (((1,), (1,)), ((), ())), preferred_element_type=F32)
                  for c in head_cols]
        probs, dens = [], []
        for h in heads:
            s = jnp.where(col_ok, scores[h] * SCALE + bias_ref[var, h], NEG)
            m = jnp.max(s, axis=-1, keepdims=True)
            p = jnp.exp(s - m)
            dens.append(jnp.sum(p, axis=-1, keepdims=True))
            probs.append(p.astype(BF16))
        for h in heads:
            o = jnp.dot(probs[h], v_ref[pl.ds(k0, nkeys), head_cols[h]],
                        preferred_element_type=F32) / dens[h]
            o_ref[pl.ds(q0, GRID_W), head_cols[h]] = o.astype(BF16)
        return carry

    lax.fori_loop(0, NA_ROWS_PER_STEP, one_row, 0, unroll=NA_UNROLL)


def _neighbourhood(qkv, rpb, batch, seq_total):
    group = N_GROUPS_A
    assert GROUP_DILATIONS[group] == 1
    rows = seq_total // GRID_W
    kh = min(NA_ROWS, rows)
    assert kh == NA_ROWS and rows % NA_ROWS_PER_STEP == 0
    bias = _na_bias(rpb, kh)
    tq = NA_ROWS_PER_STEP * GRID_W
    view = qkv.reshape(qkv.shape[0], batch, seq_total, qkv.shape[-1])
    o = pl.pallas_call(
        functools.partial(_na_kernel, rows=rows, kh=kh),
        grid=(batch, rows // NA_ROWS_PER_STEP),
        in_specs=[
            pl.BlockSpec((None, None, tq, GROUP_WIDTH), lambda b, i: (group, b, i, 0)),
            pl.BlockSpec((None, None, seq_total, GROUP_WIDTH), lambda b, i: (group, b, 0, 1)),
            pl.BlockSpec((None, None, seq_total, GROUP_WIDTH), lambda b, i: (group, b, 0, 2)),
            pl.BlockSpec((N_BIAS_VARIANTS, N_HEADS_B, GRID_W, kh * GRID_W), lambda b, i: (0, 0, 0, 0)),
        ],
        out_specs=pl.BlockSpec((None, tq, GROUP_WIDTH), lambda b, i: (b, i, 0)),
        out_shape=jax.ShapeDtypeStruct((batch, seq_total, GROUP_WIDTH), BF16),
        compiler_params=_params(("arbitrary", "arbitrary")),
        name="neighbourhood_attention",
    )(view, view, view, bias)
    return o.reshape(batch * seq_total, GROUP_WIDTH)


def _to_token_order_matrix(tm, d):
    t = lax.broadcasted_iota(jnp.int32, (tm, tm), 0)
    c = lax.broadcasted_iota(jnp.int32, (tm, tm), 1)
    return (c == (t % d) * (tm // d) + t // d).astype(BF16)


def _merge_kernel(o0_ref, o1_ref, o2_ref, l0_ref, l1_ref, l2_ref, ob_ref, gate_ref, x_ref,
                  wpa_ref, wpb_ref, wout_ref, gn_ref, wup_ref, wdn_ref,
                  x2_ref, h2_ref, wup_bf_ref, wdn_bf_ref, lse_tok_ref, *, dilations):
    tm = x_ref.shape[0]
    wup_bf_ref[...] = wup_ref[...].astype(BF16)
    wdn_bf_ref[...] = wdn_ref[...].astype(BF16)
    o_tok, lse_tok = [], []
    for g, (o_ref, l_ref, d) in enumerate(zip((o0_ref, o1_ref, o2_ref),
                                              (l0_ref, l1_ref, l2_ref), dilations)):
        o = o_ref[...].reshape(tm, GROUP_WIDTH)
        if d == 1:
            o_tok.append(o.astype(F32))
            lse_tok.append(l_ref[...].reshape(tm, HEAD_DIM))
        else:
            o_tok.append(jnp.dot(_to_token_order_matrix(tm, d), o, preferred_element_type=F32))
            for r in range(d):
                lse_tok_ref[g, pl.ds(r, tm // d, stride=d), :] = l_ref[r]
            lse_tok.append(lse_tok_ref[g])
    l0, l1, l2 = lse_tok
    mx = jnp.maximum(jnp.maximum(l0, l1), l2)
    e0, e1, e2 = jnp.exp(l0 - mx), jnp.exp(l1 - mx), jnp.exp(l2 - mx)
    tot = e0 + e1 + e2
    a0, a1, a2 = e0 / tot, e1 / tot, e2 / tot
    parts = []
    for h in range(HEADS_PER_GROUP):
        cols = slice(h * HEAD_DIM, (h + 1) * HEAD_DIM)
        y = (a0[:, h:h + 1] * o_tok[0][:, cols] + a1[:, h:h + 1] * o_tok[1][:, cols]
             + a2[:, h:h + 1] * o_tok[2][:, cols])
        parts.append(y.astype(BF16))
    ya = jnp.concatenate(parts, axis=1)
    ta = jnp.dot(ya, wpa_ref[...], preferred_element_type=F32)
    tb = jnp.dot(ob_ref[...], wpb_ref[...], preferred_element_type=F32)
    merged = (gate_ref[:, :D_MODEL].astype(F32) * ta + gate_ref[:, D_MODEL:].astype(F32) * tb)
    x2 = x_ref[...] + jnp.dot(merged.astype(BF16), wout_ref[...], preferred_element_type=F32)
    x2_ref[...] = x2
    h2_ref[...] = _rms(x2, gn_ref[...]).astype(BF16)


def _merge(o_groups, lse_groups, ob, gates, x2d, seq, wpa, wpb, wout, gn, w_up, w_down):
    m = x2d.shape[0]
    tm = MERGE_TM
    n_steps = m // tm
    tiles_per_batch = seq // tm
    dilations = tuple(o.shape[1] for o in o_groups)
    row = lambda w: pl.BlockSpec((tm, w), lambda i: (i, 0))
    wup_rows = pl.BlockSpec((w_up.shape[0] // n_steps, w_up.shape[1]), lambda i: (i, 0))
    wdn_rows = pl.BlockSpec((w_down.shape[0] // n_steps, w_down.shape[1]), lambda i: (i, 0))
    grouped = lambda d, w: pl.BlockSpec(
        (None, d, tm // d, w), lambda i: (i // tiles_per_batch, 0, i % tiles_per_batch, 0))
    const = lambda a, b: pl.BlockSpec((a, b), lambda i: (0, 0), pipeline_mode=pl.Buffered(1))
    return pl.pallas_call(
        functools.partial(_merge_kernel, dilations=dilations),
        grid=(n_steps,),
        in_specs=[grouped(d, GROUP_WIDTH) for d in dilations]
        + [grouped(d, HEAD_DIM) for d in dilations]
        + [row(GROUP_WIDTH), row(2 * D_MODEL), row(D_MODEL),
           const(GROUP_WIDTH, D_MODEL), const(GROUP_WIDTH, D_MODEL), const(D_MODEL, D_MODEL),
           const(1, D_MODEL), wup_rows, wdn_rows],
        out_specs=[row(D_MODEL), row(D_MODEL), wup_rows, wdn_rows],
        out_shape=[jax.ShapeDtypeStruct((m, D_MODEL), F32), jax.ShapeDtypeStruct((m, D_MODEL), BF16),
                   jax.ShapeDtypeStruct(w_up.shape, BF16), jax.ShapeDtypeStruct(w_down.shape, BF16)],
        scratch_shapes=[pltpu.VMEM((len(dilations), tm, HEAD_DIM), F32)],
        compiler_params=_params(("arbitrary",)),
        name="merge_out_proj",
    )(*o_groups, *lse_groups, ob, gates, x2d, wpa, wpb, wout, gn, w_up, w_down)


def _mlp_kernel(h2_ref, wup_ref, wdn_ref, x2_hbm, gf_ref, out_ref, x2_sem, *, n_f):
    i = pl.program_id(0)
    f = pl.program_id(1)
    tm = out_ref.shape[0]

    def residual_copy():
        return pltpu.make_async_copy(x2_hbm.at[pl.ds(i * tm, tm), :], out_ref, x2_sem)

    @pl.when(f == 0)
    def _():
        residual_copy().start()

    hid = jnp.dot(h2_ref[...], wup_ref[...], preferred_element_type=F32)
    hid = jnp.square(jnp.maximum(hid, 0.0)).astype(BF16)

    @pl.when(f == 0)
    def _():
        residual_copy().wait()

    out_ref[...] += jnp.dot(hid, wdn_ref[...], preferred_element_type=F32)

    @pl.when(f == n_f - 1)
    def _():
        out_ref[...] = _rms(out_ref[...], gf_ref[...])


def _mlp(h2, wup, wdn, x2, gf):
    m = h2.shape[0]
    tm, tf = MLP_TM, MLP_TF
    n_f = D_FF // tf
    return pl.pallas_call(
        functools.partial(_mlp_kernel, n_f=n_f),
        grid=(m // tm, n_f),
        in_specs=[
            pl.BlockSpec((tm, D_MODEL), lambda i, f: (i, 0)),
            pl.BlockSpec((D_MODEL, tf), lambda i, f: (0, f)),
            pl.BlockSpec((tf, D_MODEL), lambda i, f: (f, 0)),
            pl.BlockSpec(memory_space=pl.ANY),
            pl.BlockSpec((1, D_MODEL), lambda i, f: (0, 0)),
        ],
        out_specs=pl.BlockSpec((tm, D_MODEL), lambda i, f: (i, 0)),
        out_shape=jax.ShapeDtypeStruct((m, D_MODEL), F32),
        scratch_shapes=[pltpu.SemaphoreType.DMA],
        compiler_params=_params(("arbitrary", "arbitrary")),
        name="mlp_residual_norm",
    )(h2, wup, wdn, x2, gf)


def _layer(x2d, batch, seq, norm_mix, w_qkv, w_gate, b_gate, rpb, w_proj_a, w_proj_b, w_out,
           norm_mlp, w_up, w_down):
    slopes = 2.0 ** (-8.0 * np.arange(1, N_HEADS_A + 1) / N_HEADS_A)
    row = lambda v: v.reshape(1, -1)
    h_orders = _norm(x2d, row(norm_mix))
    qkv, gates, w_out_bf, w_pa_bf, w_pb_bf = _project(
        h_orders, w_qkv, w_gate, row(b_gate), w_out, w_proj_a, w_proj_b)
    o_groups, lse_groups = [], []
    for g, (window, d) in enumerate(DILATION_PATTERNS):
        o, lse = _dilated_group(qkv, batch, seq, g, window, d, slopes)
        o_groups.append(o)
        lse_groups.append(lse)
    ob = _neighbourhood(qkv, rpb, batch, seq)
    return _merge(o_groups, lse_groups, ob, gates, x2d, seq, w_pa_bf, w_pb_bf, w_out_bf,
                  row(norm_mlp), w_up, w_down)


def kernel(x, norm_mix, w_qkv, w_gate, b_gate, rpb, w_proj_a, w_proj_b, w_out, norm_mlp, w_up,
           w_down, norm_final):
    batch, seq, _ = x.shape
    depth = norm_mix.shape[0]
    assert depth == 1 and seq % PERM_TILE == 0
    x2d = x.reshape(batch * seq, D_MODEL)
    x2, h2, w_up_bf, w_down_bf = _layer(
        x2d, batch, seq, norm_mix[0], w_qkv[0], w_gate[0], b_gate[0], rpb[0],
        w_proj_a[0], w_proj_b[0], w_out[0], norm_mlp[0], w_up[0], w_down[0])
    out = _mlp(h2, w_up_bf, w_down_bf, x2, norm_final.reshape(1, -1))
    return out.reshape(batch, seq, D_MODEL)
```

```python
import functools

import jax
import jax.numpy as jnp
import numpy as np
from jax import lax
from jax.experimental import pallas as pl
from jax.experimental.pallas import tpu as pltpu

D_MODEL = 2048
HEAD_DIM = 128
N_HEADS = D_MODEL // HEAD_DIM
N_HEADS_B = N_HEADS // 4
N_HEADS_A = N_HEADS - N_HEADS_B
DILATION_PATTERNS = ((128, 1), (512, 4), (2048, 16))
N_GROUPS_A = len(DILATION_PATTERNS)
HEADS_PER_GROUP = N_HEADS_A // N_GROUPS_A
GROUP_WIDTH = HEADS_PER_GROUP * HEAD_DIM
N_HEAD_GROUPS = N_HEADS // HEADS_PER_GROUP
QKV_WIDTH = 3 * D_MODEL
GRID_W = 64
NA_ROWS = 8
NA_COLS = 16
D_FF = 4 * D_MODEL
EPS = 1e-6
NEG = -1e30
SCALE = HEAD_DIM ** -0.5
LOG2E = float(np.log2(np.e))
LN2 = float(np.log(2.0))
LANES = 128

F32 = jnp.float32
BF16 = jnp.bfloat16

VMEM_LIMIT_BYTES = 56 * 1024 * 1024

PERM_TILE = 1024
NORM_CHUNK = 512
PROJ_TM, PROJ_TN = 2048, GROUP_WIDTH
SIDE_CAST_ROWS = 32
MERGE_TM = 256
MLP_TM, MLP_TF = 1024, 1024
DIL_QB = 128
DIL_UNROLL = 2
N_WINDOW_CASES = 3
NA_ROWS_PER_STEP = 8
NA_UNROLL = 4

GROUP_DILATIONS = tuple(d for _, d in DILATION_PATTERNS) + (1,)
ROW_ORDERS = tuple(sorted(set(GROUP_DILATIONS)))
GROUP_ROW_ORDER = tuple(ROW_ORDERS.index(d) for d in GROUP_DILATIONS)


def _params(sem):
    return pltpu.CompilerParams(dimension_semantics=sem, vmem_limit_bytes=VMEM_LIMIT_BYTES)


def _rms(x, g):
    ms = jnp.mean(x * x, axis=-1, keepdims=True)
    return (x * lax.rsqrt(ms + EPS)) * g


def _norm_kernel(x_ref, g_ref, h_ref, slab_ref):
    tm = x_ref.shape[0]
    x = x_ref[...]
    inv = lax.rsqrt(jnp.mean(x * x, axis=-1, keepdims=True) + EPS)
    n_slabs = NORM_CHUNK // LANES
    for c0 in range(0, D_MODEL, NORM_CHUNK):
        hc = (x_ref[:, c0:c0 + NORM_CHUNK] * inv) * g_ref[:, c0:c0 + NORM_CHUNK]
        h_ref[0, :, c0:c0 + NORM_CHUNK] = hc.astype(BF16)
        for s in range(n_slabs):
            slab_ref[s] = hc[:, s * LANES:(s + 1) * LANES]
        for v, d in enumerate(ROW_ORDERS):
            if d == 1:
                continue
            n = tm // d
            for s in range(n_slabs):
                cols = slice(c0 + s * LANES, c0 + (s + 1) * LANES)
                for r in range(d):
                    h_ref[v, r * n:(r + 1) * n, cols] = (
                        slab_ref[s, pl.ds(r, n, stride=d), :].astype(BF16))


def _norm(x2d, g):
    m = x2d.shape[0]
    tm = PERM_TILE
    nv = len(ROW_ORDERS)
    assert ROW_ORDERS[0] == 1
    return pl.pallas_call(
        _norm_kernel,
        grid=(m // tm,),
        in_specs=[pl.BlockSpec((tm, D_MODEL), lambda i: (i, 0)),
                  pl.BlockSpec((1, D_MODEL), lambda i: (0, 0))],
        out_specs=pl.BlockSpec((nv, tm, D_MODEL), lambda i: (0, i, 0)),
        out_shape=jax.ShapeDtypeStruct((nv, m, D_MODEL), BF16),
        scratch_shapes=[pltpu.VMEM((NORM_CHUNK // LANES, tm, LANES), F32)],
        compiler_params=_params(("arbitrary",)),
        name="rmsnorm_row_orders",
    )(x2d, g)


def _proj_kernel(h_ref, wq_ref, wg_ref, bg_ref, wout_ref, wpa_ref, wpb_ref,
                 qkv_ref, gate_ref, wout_bf_ref, wpa_bf_ref, wpb_bf_ref, w_bf_ref, *, n_qkv_steps):
    j = pl.program_id(0)
    i = pl.program_id(1)
    is_qkv = j < n_qkv_steps

    @pl.when((i == 0) & is_qkv)
    def _():
        w_bf_ref[...] = wq_ref[...].astype(BF16)

    @pl.when((i == 0) & jnp.logical_not(is_qkv))
    def _():
        w_bf_ref[...] = wg_ref[...].astype(BF16)

    def cast_merge_weights():
        wout_bf_ref[...] = wout_ref[...].astype(BF16)
        wpa_bf_ref[...] = wpa_ref[...].astype(BF16)
        wpb_bf_ref[...] = wpb_ref[...].astype(BF16)

    @pl.when(is_qkv)
    def _():
        cast_merge_weights()
        y = jnp.dot(h_ref[...], w_bf_ref[...], preferred_element_type=F32)
        qkv_ref[...] = y.astype(BF16)

    @pl.when(jnp.logical_not(is_qkv))
    def _():
        cast_merge_weights()
        z = jnp.dot(h_ref[...], w_bf_ref[...], preferred_element_type=F32) + bg_ref[...]
        gate_ref[...] = (0.5 * jnp.tanh(0.5 * z) + 0.5).astype(BF16)


def _project(h_orders, w_qkv, w_gate, b_gate, w_out, w_proj_a, w_proj_b):
    m = h_orders.shape[1]
    tm, tn = PROJ_TM, PROJ_TN
    n_groups = N_HEAD_GROUPS
    nq = 3 * n_groups
    ng = (2 * D_MODEL) // tn
    n_i = m // tm
    row_order = GROUP_ROW_ORDER

    def lhs_map(j, i):
        v = jnp.int32(0)
        for grp in range(n_groups):
            v = jnp.where(j // 3 == grp, row_order[grp], v)
        return (v, i, 0)

    def w_qkv_map(j, i):
        jj = jnp.minimum(j, nq - 1)
        return (0, (jj % 3) * n_groups + jj // 3)

    gate_col = lambda j, i: (0, jnp.maximum(j - nq, 0))

    def qkv_out_map(j, i):
        jj = jnp.minimum(j, nq - 1)
        return (jj // 3, jnp.where(j < nq, i, n_i - 1), jj % 3)

    gate_out_map = lambda j, i: (jnp.where(j >= nq, i, 0), jnp.maximum(j - nq, 0))

    def side_rows(w):
        n_blocks = w.shape[0] // SIDE_CAST_ROWS
        return pl.BlockSpec((SIDE_CAST_ROWS, w.shape[1]),
                            lambda j, i: (jnp.minimum(j * n_i + i, n_blocks - 1), 0))

    side = [w_out, w_proj_a, w_proj_b]
    assert all(w.shape[0] // SIDE_CAST_ROWS <= (nq + ng) * n_i for w in side)
    return pl.pallas_call(
        functools.partial(_proj_kernel, n_qkv_steps=nq),
        grid=(nq + ng, n_i),
        in_specs=[
            pl.BlockSpec((None, tm, D_MODEL), lhs_map),
            pl.BlockSpec((D_MODEL, tn), w_qkv_map),
            pl.BlockSpec((D_MODEL, tn), gate_col),
            pl.BlockSpec((1, tn), gate_col),
        ] + [side_rows(w) for w in side],
        out_specs=[
            pl.BlockSpec((None, tm, tn), qkv_out_map),
            pl.BlockSpec((tm, tn), gate_out_map),
        ] + [side_rows(w) for w in side],
        out_shape=[
            jax.ShapeDtypeStruct((n_groups, m, 3 * GROUP_WIDTH), BF16),
            jax.ShapeDtypeStruct((m, 2 * D_MODEL), BF16),
        ] + [jax.ShapeDtypeStruct(w.shape, BF16) for w in side],
        scratch_shapes=[pltpu.VMEM((D_MODEL, tn), BF16)],
        compiler_params=_params(("arbitrary", "arbitrary")),
        name="proj_qkv_gate",
    )(h_orders, w_qkv, w_gate, b_gate, *side)


def _dilated_kernel(q_ref, k_ref, v_ref, o_ref, lse_ref, bias_ref, *, seq, half_window, coefs):
    q_ref, k_ref, v_ref = (ref.reshape(seq, GROUP_WIDTH) for ref in (q_ref, k_ref, v_ref))
    qb = DIL_QB
    kw = min(seq, qb + 2 * half_window)
    n_blocks = seq // qb
    lane = lax.broadcasted_iota(jnp.int32, (qb, HEAD_DIM), 1)
    heads = range(HEADS_PER_GROUP)
    head_cols = [slice(h * HEAD_DIM, (h + 1) * HEAD_DIM) for h in heads]

    @pl.when((pl.program_id(0) == 0) & (pl.program_id(1) == 0))
    def _():
        qrow = lax.broadcasted_iota(jnp.int32, (qb, kw), 0)
        kcol = lax.broadcasted_iota(jnp.int32, (qb, kw), 1)
        for c in range(bias_ref.shape[0]):
            dist = jnp.abs(kcol - qrow - c * half_window)
            dist_f = dist.astype(F32)
            for h in heads:
                bias_ref[c, h] = jnp.where(dist <= half_window, (-coefs[h] * LOG2E) * dist_f, NEG)

    def block(qi, carry):
        q0 = pl.multiple_of(qi * qb, qb)
        ks = pl.multiple_of(jnp.clip(q0 - half_window, 0, seq - kw), half_window)
        window_case = (q0 - ks) // half_window
        scores = [lax.dot_general(q_ref[pl.ds(q0, qb), c], k_ref[pl.ds(ks, kw), c],
                                  (((1,), (1,)), ((), ())), preferred_element_type=F32)
                  for c in head_cols]
        probs, dens = [], []
        lse_all = jnp.zeros((qb, HEAD_DIM), F32)
        for h in heads:
            t = scores[h] * (SCALE * LOG2E) + bias_ref[window_case, h]
            m = jnp.max(t, axis=-1, keepdims=True)
            p = jnp.exp2(t - m)
            den = jnp.sum(p, axis=-1, keepdims=True)
            probs.append(p.astype(BF16))
            dens.append(den)
            lse_all = jnp.where(lane == h, m * LN2 + jnp.log(den), lse_all)
        for h in heads:
            o = jnp.dot(probs[h], v_ref[pl.ds(ks, kw), head_cols[h]],
                        preferred_element_type=F32) / dens[h]
            o_ref[pl.ds(q0, qb), head_cols[h]] = o.astype(BF16)
        lse_ref[pl.ds(q0, qb), :] = lse_all
        return carry

    lax.fori_loop(0, n_blocks, block, 0, unroll=DIL_UNROLL)


def _dilated_group(qkv, batch, seq_total, group, window, dilation, slopes):
    d = dilation
    seq = seq_total // d
    half_window = window // (2 * d)
    kw = min(seq, DIL_QB + 2 * half_window)
    coefs = tuple(float(slopes[group * HEADS_PER_GROUP + h]) * d for h in range(HEADS_PER_GROUP))
    tiles = seq_total // PERM_TILE
    rows = PERM_TILE // d
    view = qkv.reshape(qkv.shape[0], batch, tiles, d, rows, qkv.shape[-1])
    part = lambda which: pl.BlockSpec((None, None, tiles, None, rows, GROUP_WIDTH),
                                      lambda b, r: (group, b, 0, r, 0, which))
    return pl.pallas_call(
        functools.partial(_dilated_kernel, seq=seq, half_window=half_window, coefs=coefs),
        grid=(batch, d),
        in_specs=[part(0), part(1), part(2)],
        out_specs=[
            pl.BlockSpec((None, None, seq, GROUP_WIDTH), lambda b, r: (b, r, 0, 0)),
            pl.BlockSpec((None, None, seq, HEAD_DIM), lambda b, r: (b, r, 0, 0)),
        ],
        out_shape=[
            jax.ShapeDtypeStruct((batch, d, seq, GROUP_WIDTH), BF16),
            jax.ShapeDtypeStruct((batch, d, seq, HEAD_DIM), F32),
        ],
        scratch_shapes=[pltpu.VMEM((N_WINDOW_CASES, HEADS_PER_GROUP, DIL_QB, kw), F32)],
        compiler_params=_params(("arbitrary", "arbitrary")),
        name=f"dilated_attention_d{d}",
    )(view, view, view)


N_BIAS_VARIANTS = NA_ROWS


def _na_bias_kernel(rpb_ref, out_ref, *, kh):
    h = pl.program_id(0)
    n_off = 2 * NA_ROWS - 1
    width = (n_off + 1) * GRID_W
    qc = lax.broadcasted_iota(jnp.int32, (GRID_W, width), 0)
    kc = lax.broadcasted_iota(jnp.int32, (GRID_W, width), 1) % GRID_W
    col_idx = jnp.clip(kc - qc, -(NA_COLS - 1), NA_COLS - 1) + (NA_COLS - 1)
    off_row = lax.broadcasted_iota(jnp.int32, (1, width), 1) // GRID_W
    table = jnp.zeros((GRID_W, width), F32)
    for j in range(2 * NA_COLS - 1):
        row_vals = jnp.zeros((1, width), F32)
        for a in range(n_off):
            row_vals = jnp.where(off_row == a, rpb_ref[h, a, j], row_vals)
        table = jnp.where(col_idx == j, row_vals, table)
    cs = jnp.clip(qc - NA_COLS // 2, 0, GRID_W - NA_COLS)
    table = jnp.where((kc >= cs) & (kc < cs + NA_COLS), table * LOG2E, NEG)
    for var in range(N_BIAS_VARIANTS):
        out_ref[var] = table[:, var * GRID_W:(var + kh) * GRID_W]


def _na_bias(rpb, kh):
    nkeys = kh * GRID_W
    return pl.pallas_call(
        functools.partial(_na_bias_kernel, kh=kh),
        grid=(N_HEADS_B,),
        in_specs=[pl.BlockSpec(memory_space=pltpu.SMEM)],
        out_specs=pl.BlockSpec((N_BIAS_VARIANTS, None, GRID_W, nkeys), lambda h: (0, h, 0, 0)),
        out_shape=jax.ShapeDtypeStruct((N_BIAS_VARIANTS, N_HEADS_B, GRID_W, nkeys), F32),
        compiler_params=_params(("arbitrary",)),
        name="na_bias_expand",
    )(rpb)


def _na_kernel(q_ref, k_ref, v_ref, bias_ref, o_ref, *, rows, kh):
    rb = pl.program_id(1)
    nkeys = kh * GRID_W

    def one_row(rl, carry):
        r = rb * NA_ROWS_PER_STEP + rl
        rs = jnp.clip(r - kh // 2, 0, rows - kh)
        var = rs - r + (NA_ROWS - 1)
        q0 = pl.multiple_of(rl * GRID_W, GRID_W)
        k0 = pl.multiple_of(rs * GRID_W, GRID_W)
        heads = range(N_HEADS_B)
        head_cols = [slice(h * HEAD_DIM, (h + 1) * HEAD_DIM) for h in heads]
        scores = [lax.dot_general(q_ref[pl.ds(q0, GRID_W), c], k_ref[pl.ds(k0, nkeys), c],
                                  (((1,), (1,)), ((), ())), preferred_element_type=F32)
                  for c in head_cols]
        probs, dens = [], []
        for h in heads:
            t = scores[h] * (SCALE * LOG2E) + bias_ref[var, h]
            m = jnp.max(t, axis=-1, keepdims=True)
            p = jnp.exp2(t - m)
            dens.append(jnp.sum(p, axis=-1, keepdims=True))
            probs.append(p.astype(BF16))
        for h in heads:
            o = jnp.dot(probs[h], v_ref[pl.ds(k0, nkeys), head_cols[h]],
                        preferred_element_type=F32) / dens[h]
            o_ref[pl.ds(q0, GRID_W), head_cols[h]] = o.astype(BF16)
        return carry

    lax.fori_loop(0, NA_ROWS_PER_STEP, one_row, 0, unroll=NA_UNROLL)


def _neighbourhood(qkv, rpb, batch, seq_total):
    group = N_GROUPS_A
    assert GROUP_DILATIONS[group] == 1
    rows = seq_total // GRID_W
    kh = min(NA_ROWS, rows)
    assert kh == NA_ROWS and rows % NA_ROWS_PER_STEP == 0
    bias = _na_bias(rpb, kh)
    tq = NA_ROWS_PER_STEP * GRID_W
    view = qkv.reshape(qkv.shape[0], batch, seq_total, qkv.shape[-1])
    o = pl.pallas_call(
        functools.partial(_na_kernel, rows=rows, kh=kh),
        grid=(batch, rows // NA_ROWS_PER_STEP),
        in_specs=[
            pl.BlockSpec((None, None, tq, GROUP_WIDTH), lambda b, i: (group, b, i, 0)),
            pl.BlockSpec((None, None, seq_total, GROUP_WIDTH), lambda b, i: (group, b, 0, 1)),
            pl.BlockSpec((None, None, seq_total, GROUP_WIDTH), lambda b, i: (group, b, 0, 2)),
            pl.BlockSpec((N_BIAS_VARIANTS, N_HEADS_B, GRID_W, kh * GRID_W), lambda b, i: (0, 0, 0, 0)),
        ],
        out_specs=pl.BlockSpec((None, tq, GROUP_WIDTH), lambda b, i: (b, i, 0)),
        out_shape=jax.ShapeDtypeStruct((batch, seq_total, GROUP_WIDTH), BF16),
        compiler_params=_params(("arbitrary", "arbitrary")),
        name="neighbourhood_attention",
    )(view, view, view, bias)
    return o.reshape(batch * seq_total, GROUP_WIDTH)


def _to_token_order_matrix(tm, d):
    t = lax.broadcasted_iota(jnp.int32, (tm, tm), 0)
    c = lax.broadcasted_iota(jnp.int32, (tm, tm), 1)
    return (c == (t % d) * (tm // d) + t // d).astype(BF16)


def _merge_kernel(o0_ref, o1_ref, o2_ref, l0_ref, l1_ref, l2_ref, ob_ref, gate_ref, x_ref,
                  wpa_ref, wpb_ref, wout_ref, gn_ref, wup_ref, wdn_ref,
                  x2_ref, h2_ref, wup_bf_ref, wdn_bf_ref, lse_tok_ref, *, dilations):
    tm = x_ref.shape[0]
    wup_bf_ref[...] = wup_ref[...].astype(BF16)
    wdn_bf_ref[...] = wdn_ref[...].astype(BF16)
    o_tok, lse_tok = [], []
    for g, (o_ref, l_ref, d) in enumerate(zip((o0_ref, o1_ref, o2_ref),
                                              (l0_ref, l1_ref, l2_ref), dilations)):
        o = o_ref[...].reshape(tm, GROUP_WIDTH)
        if d == 1:
            o_tok.append(o.astype(F32))
            lse_tok.append(l_ref[...].reshape(tm, HEAD_DIM))
        else:
            o_tok.append(jnp.dot(_to_token_order_matrix(tm, d), o, preferred_element_type=F32))
            for r in range(d):
                lse_tok_ref[g, pl.ds(r, tm // d, stride=d), :] = l_ref[r]
            lse_tok.append(lse_tok_ref[g])
    l0, l1, l2 = lse_tok
    mx = jnp.maximum(jnp.maximum(l0, l1), l2)
    e0, e1, e2 = jnp.exp(l0 - mx), jnp.exp(l1 - mx), jnp.exp(l2 - mx)
    tot = e0 + e1 + e2
    a0, a1, a2 = e0 / tot, e1 / tot, e2 / tot
    parts = []
    for h in range(HEADS_PER_GROUP):
        cols = slice(h * HEAD_DIM, (h + 1) * HEAD_DIM)
        y = (a0[:, h:h + 1] * o_tok[0][:, cols] + a1[:, h:h + 1] * o_tok[1][:, cols]
             + a2[:, h:h + 1] * o_tok[2][:, cols])
        parts.append(y.astype(BF16))
    ya = jnp.concatenate(parts, axis=1)
    ta = jnp.dot(ya, wpa_ref[...], preferred_element_type=F32)
    tb = jnp.dot(ob_ref[...], wpb_ref[...], preferred_element_type=F32)
    merged = (gate_ref[:, :D_MODEL].astype(F32) * ta + gate_ref[:, D_MODEL:].astype(F32) * tb)
    x2 = x_ref[...] + jnp.dot(merged.astype(BF16), wout_ref[...], preferred_element_type=F32)
    x2_ref[...] = x2
    h2_ref[...] = _rms(x2, gn_ref[...]).astype(BF16)


def _merge(o_groups, lse_groups, ob, gates, x2d, seq, wpa, wpb, wout, gn, w_up, w_down):
    m = x2d.shape[0]
    tm = MERGE_TM
    n_steps = m // tm
    tiles_per_batch = seq // tm
    dilations = tuple(o.shape[1] for o in o_groups)
    row = lambda w: pl.BlockSpec((tm, w), lambda i: (i, 0))
    wup_rows = pl.BlockSpec((w_up.shape[0] // n_steps, w_up.shape[1]), lambda i: (i, 0))
    wdn_rows = pl.BlockSpec((w_down.shape[0] // n_steps, w_down.shape[1]), lambda i: (i, 0))
    grouped = lambda d, w: pl.BlockSpec(
        (None, d, tm // d, w), lambda i: (i // tiles_per_batch, 0, i % tiles_per_batch, 0))
    const = lambda a, b: pl.BlockSpec((a, b), lambda i: (0, 0), pipeline_mode=pl.Buffered(1))
    return pl.pallas_call(
        functools.partial(_merge_kernel, dilations=dilations),
        grid=(n_steps,),
        in_specs=[grouped(d, GROUP_WIDTH) for d in dilations]
        + [grouped(d, HEAD_DIM) for d in dilations]
        + [row(GROUP_WIDTH), row(2 * D_MODEL), row(D_MODEL),
           const(GROUP_WIDTH, D_MODEL), const(GROUP_WIDTH, D_MODEL), const(D_MODEL, D_MODEL),
           const(1, D_MODEL), wup_rows, wdn_rows],
        out_specs=[row(D_MODEL), row(D_MODEL), wup_rows, wdn_rows],
        out_shape=[jax.ShapeDtypeStruct((m, D_MODEL), F32), jax.ShapeDtypeStruct((m, D_MODEL), BF16),
                   jax.ShapeDtypeStruct(w_up.shape, BF16), jax.ShapeDtypeStruct(w_down.shape, BF16)],
        scratch_shapes=[pltpu.VMEM((len(dilations), tm, HEAD_DIM), F32)],
        compiler_params=_params(("arbitrary",)),
        name="merge_out_proj",
    )(*o_groups, *lse_groups, ob, gates, x2d, wpa, wpb, wout, gn, w_up, w_down)


def _mlp_kernel(h2_ref, wup_ref, wdn_ref, x2_hbm, gf_ref, out_ref, x2_sem, *, n_f):
    i = pl.program_id(0)
    f = pl.program_id(1)
    tm = out_ref.shape[0]

    def residual_copy():
        return pltpu.make_async_copy(x2_hbm.at[pl.ds(i * tm, tm), :], out_ref, x2_sem)

    @pl.when(f == 0)
    def _():
        residual_copy().start()

    hid = jnp.dot(h2_ref[...], wup_ref[...], preferred_element_type=F32)
    hid = jnp.square(jnp.maximum(hid, 0.0)).astype(BF16)

    @pl.when(f == 0)
    def _():
        residual_copy().wait()

    out_ref[...] += jnp.dot(hid, wdn_ref[...], preferred_element_type=F32)

    @pl.when(f == n_f - 1)
    def _():
        out_ref[...] = _rms(out_ref[...], gf_ref[...])


def _mlp(h2, wup, wdn, x2, gf):
    m = h2.shape[0]
    tm, tf = MLP_TM, MLP_TF
    n_f = D_FF // tf
    return pl.pallas_call(
        functools.partial(_mlp_kernel, n_f=n_f),
        grid=(m // tm, n_f),
        in_specs=[
            pl.BlockSpec((tm, D_MODEL), lambda i, f: (i, 0)),
            pl.BlockSpec((D_MODEL, tf), lambda i, f: (0, f)),
            pl.BlockSpec((tf, D_MODEL), lambda i, f: (f, 0)),
            pl.BlockSpec(memory_space=pl.ANY),
            pl.BlockSpec((1, D_MODEL), lambda i, f: (0, 0)),
        ],
        out_specs=pl.BlockSpec((tm, D_MODEL), lambda i, f: (i, 0)),
        out_shape=jax.ShapeDtypeStruct((m, D_MODEL), F32),
        scratch_shapes=[pltpu.SemaphoreType.DMA],
        compiler_params=_params(("arbitrary", "arbitrary")),
        name="mlp_residual_norm",
    )(h2, wup, wdn, x2, gf)


def _layer(x2d, batch, seq, norm_mix, w_qkv, w_gate, b_gate, rpb, w_proj_a, w_proj_b, w_out,
           norm_mlp, w_up, w_down):
    slopes = 2.0 ** (-8.0 * np.arange(1, N_HEADS_A + 1) / N_HEADS_A)
    row = lambda v: v.reshape(1, -1)
    h_orders = _norm(x2d, row(norm_mix))
    qkv, gates, w_out_bf, w_pa_bf, w_pb_bf = _project(
        h_orders, w_qkv, w_gate, row(b_gate), w_out, w_proj_a, w_proj_b)
    o_groups, lse_groups = [], []
    for g, (window, d) in enumerate(DILATION_PATTERNS):
        o, lse = _dilated_group(qkv, batch, seq, g, window, d, slopes)
        o_groups.append(o)
        lse_groups.append(lse)
    ob = _neighbourhood(qkv, rpb, batch, seq)
    return _merge(o_groups, lse_groups, ob, gates, x2d, seq, w_pa_bf, w_pb_bf, w_out_bf,
                  row(norm_mlp), w_up, w_down)


def kernel(x, norm_mix, w_qkv, w_gate, b_gate, rpb, w_proj_a, w_proj_b, w_out, norm_mlp, w_up,
           w_down, norm_final):
    batch, seq, _ = x.shape
    depth = norm_mix.shape[0]
    assert depth == 1 and seq % PERM_TILE == 0
    x2d = x.reshape(batch * seq, D_MODEL)
    x2, h2, w_up_bf, w_down_bf = _layer(
        x2d, batch, seq, norm_mix[0], w_qkv[0], w_gate[0], b_gate[0], rpb[0],
        w_proj_a[0], w_proj_b[0], w_out[0], norm_mlp[0], w_up[0], w_down[0])
    out = _mlp(h2, w_up_bf, w_down_bf, x2, norm_final.reshape(1, -1))
    return out.reshape(batch, seq, D_MODEL)
```

```python
import functools

import jax
import jax.numpy as jnp
import numpy as np
from jax import lax
from jax.experimental import pallas as pl
from jax.experimental.pallas import tpu as pltpu

D_MODEL = 2048
HEAD_DIM = 128
N_HEADS = D_MODEL // HEAD_DIM
N_HEADS_B = N_HEADS // 4
N_HEADS_A = N_HEADS - N_HEADS_B
DILATION_PATTERNS = ((128, 1), (512, 4), (2048, 16))
N_GROUPS_A = len(DILATION_PATTERNS)
HEADS_PER_GROUP = N_HEADS_A // N_GROUPS_A
GROUP_WIDTH = HEADS_PER_GROUP * HEAD_DIM
N_HEAD_GROUPS = N_HEADS // HEADS_PER_GROUP
QKV_WIDTH = 3 * D_MODEL
GRID_W = 64
NA_ROWS = 8
NA_COLS = 16
D_FF = 4 * D_MODEL
EPS = 1e-6
NEG = -1e30
SCALE = HEAD_DIM ** -0.5
LOG2E = float(np.log2(np.e))
LN2 = float(np.log(2.0))
LANES = 128

F32 = jnp.float32
BF16 = jnp.bfloat16

VMEM_LIMIT_BYTES = 56 * 1024 * 1024

PERM_TILE = 1024
NORM_CHUNK = 512
PROJ_TM, PROJ_TN = 2048, GROUP_WIDTH
SIDE_CAST_ROWS = 32
MERGE_TM = 256
MLP_TM, MLP_TF = 1024, 1024
DIL_QB = 128
DIL_UNROLL = 4
DIL_ROWS_PER_STEP = 1024
N_WINDOW_CASES = 3
NA_ROWS_PER_STEP = 16
NA_UNROLL = 4

GROUP_DILATIONS = tuple(d for _, d in DILATION_PATTERNS) + (1,)
ROW_ORDERS = tuple(sorted(set(GROUP_DILATIONS)))
GROUP_ROW_ORDER = tuple(ROW_ORDERS.index(d) for d in GROUP_DILATIONS)


def _params(sem):
    return pltpu.CompilerParams(dimension_semantics=sem, vmem_limit_bytes=VMEM_LIMIT_BYTES)


def _rms(x, g):
    ms = jnp.mean(x * x, axis=-1, keepdims=True)
    return (x * lax.rsqrt(ms + EPS)) * g


def _norm_kernel(x_ref, g_ref, h_ref, slab_ref):
    tm = x_ref.shape[0]
    x = x_ref[...]
    inv = lax.rsqrt(jnp.mean(x * x, axis=-1, keepdims=True) + EPS)
    n_slabs = NORM_CHUNK // LANES
    for c0 in range(0, D_MODEL, NORM_CHUNK):
        hc = (x_ref[:, c0:c0 + NORM_CHUNK] * inv) * g_ref[:, c0:c0 + NORM_CHUNK]
        h_ref[0, :, c0:c0 + NORM_CHUNK] = hc.astype(BF16)
        for s in range(n_slabs):
            slab_ref[0, s] = hc[:, s * LANES:(s + 1) * LANES]
        for v in range(1, len(ROW_ORDERS)):
            d_prev, d = ROW_ORDERS[v - 1], ROW_ORDERS[v]
            q = d // d_prev
            n_prev, n = tm // d_prev, tm // d
            last = v == len(ROW_ORDERS) - 1
            for s in range(n_slabs):
                cols = slice(c0 + s * LANES, c0 + (s + 1) * LANES)
                for r in range(d_prev):
                    for r2 in range(q):
                        rows = slab_ref[(v - 1) % 2, s, pl.ds(r * n_prev + r2, n, stride=q), :]
                        dst = (d_prev * r2 + r) * n
                        h_ref[v, dst:dst + n, cols] = rows.astype(BF16)
                        if not last:
                            slab_ref[v % 2, s, dst:dst + n, :] = rows


def _norm(x2d, g):
    m = x2d.shape[0]
    tm = PERM_TILE
    nv = len(ROW_ORDERS)
    assert ROW_ORDERS[0] == 1
    return pl.pallas_call(
        _norm_kernel,
        grid=(m // tm,),
        in_specs=[pl.BlockSpec((tm, D_MODEL), lambda i: (i, 0)),
                  pl.BlockSpec((1, D_MODEL), lambda i: (0, 0))],
        out_specs=pl.BlockSpec((nv, tm, D_MODEL), lambda i: (0, i, 0)),
        out_shape=jax.ShapeDtypeStruct((nv, m, D_MODEL), BF16),
        scratch_shapes=[pltpu.VMEM((2, NORM_CHUNK // LANES, tm, LANES), F32)],
        compiler_params=_params(("arbitrary",)),
        name="rmsnorm_row_orders",
    )(x2d, g)


def _proj_kernel(h_ref, wq_ref, wg_ref, bg_ref, wout_ref, wpa_ref, wpb_ref,
                 qkv_ref, gate_ref, wout_bf_ref, wpa_bf_ref, wpb_bf_ref, w_bf_ref, *, n_qkv_steps):
    j = pl.program_id(0)
    i = pl.program_id(1)
    is_qkv = j < n_qkv_steps

    @pl.when((i == 0) & is_qkv)
    def _():
        w_bf_ref[...] = wq_ref[...].astype(BF16)

    @pl.when((i == 0) & jnp.logical_not(is_qkv))
    def _():
        w_bf_ref[...] = wg_ref[...].astype(BF16)

    def cast_merge_weights():
        wout_bf_ref[...] = wout_ref[...].astype(BF16)
        wpa_bf_ref[...] = wpa_ref[...].astype(BF16)
        wpb_bf_ref[...] = wpb_ref[...].astype(BF16)

    @pl.when(is_qkv)
    def _():
        cast_merge_weights()
        y = jnp.dot(h_ref[...], w_bf_ref[...], preferred_element_type=F32)
        qkv_ref[...] = y.astype(BF16)

    @pl.when(jnp.logical_not(is_qkv))
    def _():
        cast_merge_weights()
        z = jnp.dot(h_ref[...], w_bf_ref[...], preferred_element_type=F32) + bg_ref[...]
        gate_ref[...] = (0.5 * jnp.tanh(0.5 * z) + 0.5).astype(BF16)


def _project(h_orders, w_qkv, w_gate, b_gate, w_out, w_proj_a, w_proj_b):
    m = h_orders.shape[1]
    tm, tn = PROJ_TM, PROJ_TN
    n_groups = N_HEAD_GROUPS
    nq = 3 * n_groups
    ng = (2 * D_MODEL) // tn
    n_i = m // tm
    row_order = GROUP_ROW_ORDER

    def lhs_map(j, i):
        v = jnp.int32(0)
        for grp in range(n_groups):
            v = jnp.where(j // 3 == grp, row_order[grp], v)
        return (v, i, 0)

    def w_qkv_map(j, i):
        jj = jnp.minimum(j, nq - 1)
        return (0, (jj % 3) * n_groups + jj // 3)

    gate_col = lambda j, i: (0, jnp.maximum(j - nq, 0))

    def qkv_out_map(j, i):
        jj = jnp.minimum(j, nq - 1)
        return (jj // 3, jnp.where(j < nq, i, n_i - 1), jj % 3)

    gate_out_map = lambda j, i: (jnp.where(j >= nq, i, 0), jnp.maximum(j - nq, 0))

    def side_rows(w):
        n_blocks = w.shape[0] // SIDE_CAST_ROWS
        return pl.BlockSpec((SIDE_CAST_ROWS, w.shape[1]),
                            lambda j, i: (jnp.minimum(j * n_i + i, n_blocks - 1), 0))

    side = [w_out, w_proj_a, w_proj_b]
    assert all(w.shape[0] // SIDE_CAST_ROWS <= (nq + ng) * n_i for w in side)
    return pl.pallas_call(
        functools.partial(_proj_kernel, n_qkv_steps=nq),
        grid=(nq + ng, n_i),
        in_specs=[
            pl.BlockSpec((None, tm, D_MODEL), lhs_map),
            pl.BlockSpec((D_MODEL, tn), w_qkv_map),
            pl.BlockSpec((D_MODEL, tn), gate_col),
            pl.BlockSpec((1, tn), gate_col),
        ] + [side_rows(w) for w in side],
        out_specs=[
            pl.BlockSpec((None, tm, tn), qkv_out_map),
            pl.BlockSpec((tm, tn), gate_out_map),
        ] + [side_rows(w) for w in side],
        out_shape=[
            jax.ShapeDtypeStruct((n_groups, m, 3 * GROUP_WIDTH), BF16),
            jax.ShapeDtypeStruct((m, 2 * D_MODEL), BF16),
        ] + [jax.ShapeDtypeStruct(w.shape, BF16) for w in side],
        scratch_shapes=[pltpu.VMEM((D_MODEL, tn), BF16)],
        compiler_params=_params(("arbitrary", "arbitrary")),
        name="proj_qkv_gate",
    )(h_orders, w_qkv, w_gate, b_gate, *side)


def _dilated_block_shape(seq, half_window):
    kw = DIL_QB + 2 * half_window
    return (seq, seq) if seq <= kw else (DIL_QB, kw)


def _dilated_kernel(q_ref, k_ref, v_ref, o_ref, lse_ref, bias_ref, *, seq, half_window, coefs):
    qb, kw = _dilated_block_shape(seq, half_window)
    n_blocks = seq // qb
    n_tiles, n_res = q_ref.shape[:2]

    def rows_loader(ref):
        if n_res == 1:
            flat = ref.at[:, 0].reshape(seq, GROUP_WIDTH)
            return lambda rr, start, size, cols: flat[pl.ds(start, size), cols]
        assert n_blocks == 1
        return lambda rr, start, size, cols: jnp.concatenate(
            [ref[t, rr, :, cols] for t in range(n_tiles)], axis=0)

    load_q, load_k, load_v = rows_loader(q_ref), rows_loader(k_ref), rows_loader(v_ref)
    lane = lax.broadcasted_iota(jnp.int32, (qb, HEAD_DIM), 1)
    heads = range(HEADS_PER_GROUP)
    head_cols = [slice(h * HEAD_DIM, (h + 1) * HEAD_DIM) for h in heads]

    @pl.when((pl.program_id(0) == 0) & (pl.program_id(1) == 0))
    def _():
        qrow = lax.broadcasted_iota(jnp.int32, (qb, kw), 0)
        kcol = lax.broadcasted_iota(jnp.int32, (qb, kw), 1)
        for c in range(bias_ref.shape[0]):
            dist = jnp.abs(kcol - qrow - c * half_window)
            dist_f = dist.astype(F32)
            for h in heads:
                bias_ref[c, h] = jnp.where(dist <= half_window, (-coefs[h] * LOG2E) * dist_f, NEG)

    def block(qi, carry):
        q0 = pl.multiple_of(qi * qb, qb)
        ks = pl.multiple_of(jnp.clip(q0 - half_window, 0, seq - kw), half_window)
        window_case = (q0 - ks) // half_window
        pairs = [(rr, h) for rr in range(n_res) for h in heads]
        scores = [lax.dot_general(load_q(rr, q0, qb, head_cols[h]), load_k(rr, ks, kw, head_cols[h]),
                                  (((1,), (1,)), ((), ())), preferred_element_type=F32)
                  for rr, h in pairs]
        probs, dens = [], []
        lse_all = [jnp.zeros((qb, HEAD_DIM), F32) for _ in range(n_res)]
        for (rr, h), s in zip(pairs, scores):
            t = s * (SCALE * LOG2E) + bias_ref[window_case, h]
            m = jnp.max(t, axis=-1, keepdims=True)
            p = jnp.exp2(t - m)
            den = jnp.sum(p, axis=-1, keepdims=True)
            probs.append(p.astype(BF16))
            dens.append(den)
            lse_all[rr] = jnp.where(lane == h, m * LN2 + jnp.log(den), lse_all[rr])
        for (rr, h), p, den in zip(pairs, probs, dens):
            o = jnp.dot(p, load_v(rr, ks, kw, head_cols[h]), preferred_element_type=F32) / den
            o_ref[rr, pl.ds(q0, qb), head_cols[h]] = o.astype(BF16)
        for rr in range(n_res):
            lse_ref[rr, pl.ds(q0, qb), :] = lse_all[rr]
        return carry

    lax.fori_loop(0, n_blocks, block, 0, unroll=min(DIL_UNROLL, n_blocks))


def _dilated_group(qkv, batch, seq_total, group, window, dilation, slopes):
    d = dilation
    seq = seq_total // d
    half_window = window // (2 * d)
    qb, kw = _dilated_block_shape(seq, half_window)
    coefs = tuple(float(slopes[group * HEADS_PER_GROUP + h]) * d for h in range(HEADS_PER_GROUP))
    tiles = seq_total // PERM_TILE
    rows = PERM_TILE // d
    view = qkv.reshape(qkv.shape[0], batch, tiles, d, rows, qkv.shape[-1])
    n_res = min(d, max(1, DIL_ROWS_PER_STEP // seq))
    part = lambda which: pl.BlockSpec((None, None, tiles, n_res, rows, GROUP_WIDTH),
                                      lambda b, r: (group, b, 0, r, 0, which))
    return pl.pallas_call(
        functools.partial(_dilated_kernel, seq=seq, half_window=half_window, coefs=coefs),
        grid=(batch, d // n_res),
        in_specs=[part(0), part(1), part(2)],
        out_specs=[
            pl.BlockSpec((None, n_res, seq, GROUP_WIDTH), lambda b, r: (b, r, 0, 0)),
            pl.BlockSpec((None, n_res, seq, HEAD_DIM), lambda b, r: (b, r, 0, 0)),
        ],
        out_shape=[
            jax.ShapeDtypeStruct((batch, d, seq, GROUP_WIDTH), BF16),
            jax.ShapeDtypeStruct((batch, d, seq, HEAD_DIM), F32),
        ],
        scratch_shapes=[pltpu.VMEM((N_WINDOW_CASES, HEADS_PER_GROUP, qb, kw), F32)],
        compiler_params=_params(("arbitrary", "arbitrary")),
        name=f"dilated_attention_d{d}",
    )(view, view, view)


N_BIAS_VARIANTS = NA_ROWS


def _na_bias_kernel(rpb_ref, out_ref, *, kh):
    h = pl.program_id(0)
    n_off = 2 * NA_ROWS - 1
    width = (n_off + 1) * GRID_W
    qc = lax.broadcasted_iota(jnp.int32, (GRID_W, width), 0)
    kc = lax.broadcasted_iota(jnp.int32, (GRID_W, width), 1) % GRID_W
    col_idx = jnp.clip(kc - qc, -(NA_COLS - 1), NA_COLS - 1) + (NA_COLS - 1)
    off_row = lax.broadcasted_iota(jnp.int32, (1, width), 1) // GRID_W
    table = jnp.zeros((GRID_W, width), F32)
    for j in range(2 * NA_COLS - 1):
        row_vals = jnp.zeros((1, width), F32)
        for a in range(n_off):
            row_vals = jnp.where(off_row == a, rpb_ref[h, a, j], row_vals)
        table = jnp.where(col_idx == j, row_vals, table)
    cs = jnp.clip(qc - NA_COLS // 2, 0, GRID_W - NA_COLS)
    table = jnp.where((kc >= cs) & (kc < cs + NA_COLS), table * LOG2E, NEG)
    for var in range(N_BIAS_VARIANTS):
        out_ref[var] = table[:, var * GRID_W:(var + kh) * GRID_W]


def _na_bias(rpb, kh):
    nkeys = kh * GRID_W
    return pl.pallas_call(
        functools.partial(_na_bias_kernel, kh=kh),
        grid=(N_HEADS_B,),
        in_specs=[pl.BlockSpec(memory_space=pltpu.SMEM)],
        out_specs=pl.BlockSpec((N_BIAS_VARIANTS, None, GRID_W, nkeys), lambda h: (0, h, 0, 0)),
        out_shape=jax.ShapeDtypeStruct((N_BIAS_VARIANTS, N_HEADS_B, GRID_W, nkeys), F32),
        compiler_params=_params(("arbitrary",)),
        name="na_bias_expand",
    )(rpb)


def _na_kernel(q_ref, k_ref, v_ref, bias_ref, o_ref, *, rows, kh):
    rb = pl.program_id(1)
    nkeys = kh * GRID_W

    def one_row(rl, carry):
        r = rb * NA_ROWS_PER_STEP + rl
        rs = jnp.clip(r - kh // 2, 0, rows - kh)
        var = rs - r + (NA_ROWS - 1)
        q0 = pl.multiple_of(rl * GRID_W, GRID_W)
        k0 = pl.multiple_of(rs * GRID_W, GRID_W)
        heads = range(N_HEADS_B)
        head_cols = [slice(h * HEAD_DIM, (h + 1) * HEAD_DIM) for h in heads]
        scores = [lax.dot_general(q_ref[pl.ds(q0, GRID_W), c], k_ref[pl.ds(k0, nkeys), c],
                                  (((1,), (1,)), ((), ())), preferred_element_type=F32)
                  for c in head_cols]
        probs, dens = [], []
        for h in heads:
            t = scores[h] * (SCALE * LOG2E) + bias_ref[var, h]
            m = jnp.max(t, axis=-1, keepdims=True)
            p = jnp.exp2(t - m)
            dens.append(jnp.sum(p, axis=-1, keepdims=True))
            probs.append(p.astype(BF16))
        for h in heads:
            o = jnp.dot(probs[h], v_ref[pl.ds(k0, nkeys), head_cols[h]],
                        preferred_element_type=F32) / dens[h]
            o_ref[pl.ds(q0, GRID_W), head_cols[h]] = o.astype(BF16)
        return carry

    lax.fori_loop(0, NA_ROWS_PER_STEP, one_row, 0, unroll=NA_UNROLL)


def _neighbourhood(qkv, rpb, batch, seq_total):
    group = N_GROUPS_A
    assert GROUP_DILATIONS[group] == 1
    rows = seq_total // GRID_W
    kh = min(NA_ROWS, rows)
    assert kh == NA_ROWS and rows % NA_ROWS_PER_STEP == 0
    bias = _na_bias(rpb, kh)
    tq = NA_ROWS_PER_STEP * GRID_W
    view = qkv.reshape(qkv.shape[0], batch, seq_total, qkv.shape[-1])
    o = pl.pallas_call(
        functools.partial(_na_kernel, rows=rows, kh=kh),
        grid=(batch, rows // NA_ROWS_PER_STEP),
        in_specs=[
            pl.BlockSpec((None, None, tq, GROUP_WIDTH), lambda b, i: (group, b, i, 0)),
            pl.BlockSpec((None, None, seq_total, GROUP_WIDTH), lambda b, i: (group, b, 0, 1)),
            pl.BlockSpec((None, None, seq_total, GROUP_WIDTH), lambda b, i: (group, b, 0, 2)),
            pl.BlockSpec((N_BIAS_VARIANTS, N_HEADS_B, GRID_W, kh * GRID_W), lambda b, i: (0, 0, 0, 0)),
        ],
        out_specs=pl.BlockSpec((None, tq, GROUP_WIDTH), lambda b, i: (b, i, 0)),
        out_shape=jax.ShapeDtypeStruct((batch, seq_total, GROUP_WIDTH), BF16),
        compiler_params=_params(("arbitrary", "arbitrary")),
        name="neighbourhood_attention",
    )(view, view, view, bias)
    return o.reshape(batch * seq_total, GROUP_WIDTH)


def _to_token_order_matrix(tm, d):
    t = lax.broadcasted_iota(jnp.int32, (tm, tm), 0)
    c = lax.broadcasted_iota(jnp.int32, (tm, tm), 1)
    return (c == (t % d) * (tm // d) + t // d).astype(BF16)


def _merge_kernel(o0_ref, o1_ref, o2_ref, l0_ref, l1_ref, l2_ref, ob_ref, gate_ref, x_ref,
                  wpa_ref, wpb_ref, wout_ref, gn_ref, wup_ref, wdn_ref,
                  x2_ref, h2_ref, wup_bf_ref, wdn_bf_ref, lse_tok_ref, *, dilations):
    tm = x_ref.shape[0]
    wup_bf_ref[...] = wup_ref[...].astype(BF16)
    wdn_bf_ref[...] = wdn_ref[...].astype(BF16)
    o_tok, lse_tok = [], []
    for g, (o_ref, l_ref, d) in enumerate(zip((o0_ref, o1_ref, o2_ref),
                                              (l0_ref, l1_ref, l2_ref), dilations)):
        o = o_ref[...].reshape(tm, GROUP_WIDTH)
        if d == 1:
            o_tok.append(o.astype(F32))
            lse_tok.append(l_ref[...].reshape(tm, HEAD_DIM))
        else:
            o_tok.append(jnp.dot(_to_token_order_matrix(tm, d), o, preferred_element_type=F32))
            for r in range(d):
                lse_tok_ref[g, pl.ds(r, tm // d, stride=d), :] = l_ref[r]
            lse_tok.append(lse_tok_ref[g])
    l0, l1, l2 = lse_tok
    mx = jnp.maximum(jnp.maximum(l0, l1), l2)
    e0, e1, e2 = jnp.exp(l0 - mx), jnp.exp(l1 - mx), jnp.exp(l2 - mx)
    tot = e0 + e1 + e2
    a0, a1, a2 = e0 / tot, e1 / tot, e2 / tot
    parts = []
    for h in range(HEADS_PER_GROUP):
        cols = slice(h * HEAD_DIM, (h + 1) * HEAD_DIM)
        y = (a0[:, h:h + 1] * o_tok[0][:, cols] + a1[:, h:h + 1] * o_tok[1][:, cols]
             + a2[:, h:h + 1] * o_tok[2][:, cols])
        parts.append(y.astype(BF16))
    ya = jnp.concatenate(parts, axis=1)
    ta = jnp.dot(ya, wpa_ref[...], preferred_element_type=F32)
    tb = jnp.dot(ob_ref[...], wpb_ref[...], preferred_element_type=F32)
    merged = (gate_ref[:, :D_MODEL].astype(F32) * ta + gate_ref[:, D_MODEL:].astype(F32) * tb)
    x2 = x_ref[...] + jnp.dot(merged.astype(BF16), wout_ref[...], preferred_element_type=F32)
    x2_ref[...] = x2
    h2_ref[...] = _rms(x2, gn_ref[...]).astype(BF16)


def _merge(o_groups, lse_groups, ob, gates, x2d, seq, wpa, wpb, wout, gn, w_up, w_down):
    m = x2d.shape[0]
    tm = MERGE_TM
    n_steps = m // tm
    tiles_per_batch = seq // tm
    dilations = tuple(o.shape[1] for o in o_groups)
    row = lambda w: pl.BlockSpec((tm, w), lambda i: (i, 0))
    wup_rows = pl.BlockSpec((w_up.shape[0] // n_steps, w_up.shape[1]), lambda i: (i, 0))
    wdn_rows = pl.BlockSpec((w_down.shape[0] // n_steps, w_down.shape[1]), lambda i: (i, 0))
    grouped = lambda d, w: pl.BlockSpec(
        (None, d, tm // d, w), lambda i: (i // tiles_per_batch, 0, i % tiles_per_batch, 0))
    const = lambda a, b: pl.BlockSpec((a, b), lambda i: (0, 0), pipeline_mode=pl.Buffered(1))
    return pl.pallas_call(
        functools.partial(_merge_kernel, dilations=dilations),
        grid=(n_steps,),
        in_specs=[grouped(d, GROUP_WIDTH) for d in dilations]
        + [grouped(d, HEAD_DIM) for d in dilations]
        + [row(GROUP_WIDTH), row(2 * D_MODEL), row(D_MODEL),
           const(GROUP_WIDTH, D_MODEL), const(GROUP_WIDTH, D_MODEL), const(D_MODEL, D_MODEL),
           const(1, D_MODEL), wup_rows, wdn_rows],
        out_specs=[row(D_MODEL), row(D_MODEL), wup_rows, wdn_rows],
        out_shape=[jax.ShapeDtypeStruct((m, D_MODEL), F32), jax.ShapeDtypeStruct((m, D_MODEL), BF16),
                   jax.ShapeDtypeStruct(w_up.shape, BF16), jax.ShapeDtypeStruct(w_down.shape, BF16)],
        scratch_shapes=[pltpu.VMEM((len(dilations), tm, HEAD_DIM), F32)],
        compiler_params=_params(("arbitrary",)),
        name="merge_out_proj",
    )(*o_groups, *lse_groups, ob, gates, x2d, wpa, wpb, wout, gn, w_up, w_down)


def _mlp_kernel(h2_ref, wup_ref, wdn_ref, x2_hbm, gf_ref, out_ref, x2_sem, *, n_f):
    i = pl.program_id(0)
    f = pl.program_id(1)
    tm = out_ref.shape[0]

    def residual_copy():
        return pltpu.make_async_copy(x2_hbm.at[pl.ds(i * tm, tm), :], out_ref, x2_sem)

    @pl.when(f == 0)
    def _():
        residual_copy().start()

    hid = jnp.dot(h2_ref[...], wup_ref[...], preferred_element_type=F32)
    hid = jnp.square(jnp.maximum(hid, 0.0)).astype(BF16)

    @pl.when(f == 0)
    def _():
        residual_copy().wait()

    out_ref[...] += jnp.dot(hid, wdn_ref[...], preferred_element_type=F32)

    @pl.when(f == n_f - 1)
    def _():
        out_ref[...] = _rms(out_ref[...], gf_ref[...])


def _mlp(h2, wup, wdn, x2, gf):
    m = h2.shape[0]
    tm, tf = MLP_TM, MLP_TF
    n_f = D_FF // tf
    return pl.pallas_call(
        functools.partial(_mlp_kernel, n_f=n_f),
        grid=(m // tm, n_f),
        in_specs=[
            pl.BlockSpec((tm, D_MODEL), lambda i, f: (i, 0)),
            pl.BlockSpec((D_MODEL, tf), lambda i, f: (0, f)),
            pl.BlockSpec((tf, D_MODEL), lambda i, f: (f, 0)),
            pl.BlockSpec(memory_space=pl.ANY),
            pl.BlockSpec((1, D_MODEL), lambda i, f: (0, 0)),
        ],
        out_specs=pl.BlockSpec((tm, D_MODEL), lambda i, f: (i, 0)),
        out_shape=jax.ShapeDtypeStruct((m, D_MODEL), F32),
        scratch_shapes=[pltpu.SemaphoreType.DMA],
        compiler_params=_params(("arbitrary", "arbitrary")),
        name="mlp_residual_norm",
    )(h2, wup, wdn, x2, gf)


def _layer(x2d, batch, seq, norm_mix, w_qkv, w_gate, b_gate, rpb, w_proj_a, w_proj_b, w_out,
           norm_mlp, w_up, w_down):
    slopes = 2.0 ** (-8.0 * np.arange(1, N_HEADS_A + 1) / N_HEADS_A)
    row = lambda v: v.reshape(1, -1)
    h_orders = _norm(x2d, row(norm_mix))
    qkv, gates, w_out_bf, w_pa_bf, w_pb_bf = _project(
        h_orders, w_qkv, w_gate, row(b_gate), w_out, w_proj_a, w_proj_b)
    o_groups, lse_groups = [], []
    for g, (window, d) in enumerate(DILATION_PATTERNS):
        o, lse = _dilated_group(qkv, batch, seq, g, window, d, slopes)
        o_groups.append(o)
        lse_groups.append(lse)
    ob = _neighbourhood(qkv, rpb, batch, seq)
    return _merge(o_groups, lse_groups, ob, gates, x2d, seq, w_pa_bf, w_pb_bf, w_out_bf,
                  row(norm_mlp), w_up, w_down)


def kernel(x, norm_mix, w_qkv, w_gate, b_gate, rpb, w_proj_a, w_proj_b, w_out, norm_mlp, w_up,
           w_down, norm_final):
    batch, seq, _ = x.shape
    depth = norm_mix.shape[0]
    assert depth == 1 and seq % PERM_TILE == 0
    x2d = x.reshape(batch * seq, D_MODEL)
    x2, h2, w_up_bf, w_down_bf = _layer(
        x2d, batch, seq, norm_mix[0], w_qkv[0], w_gate[0], b_gate[0], rpb[0],
        w_proj_a[0], w_proj_b[0], w_out[0], norm_mlp[0], w_up[0], w_down[0])
    out = _mlp(h2, w_up_bf, w_down_bf, x2, norm_final.reshape(1, -1))
    return out.reshape(batch, seq, D_MODEL)
```

```python
import functools

import jax
import jax.numpy as jnp
import numpy as np
from jax import lax
from jax.experimental import pallas as pl
from jax.experimental.pallas import tpu as pltpu

D_MODEL = 2048
HEAD_DIM = 128
N_HEADS = D_MODEL // HEAD_DIM
N_HEADS_B = N_HEADS // 4
N_HEADS_A = N_HEADS - N_HEADS_B
DILATION_PATTERNS = ((128, 1), (512, 4), (2048, 16))
N_GROUPS_A = len(DILATION_PATTERNS)
HEADS_PER_GROUP = N_HEADS_A // N_GROUPS_A
GROUP_WIDTH = HEADS_PER_GROUP * HEAD_DIM
N_HEAD_GROUPS = N_HEADS // HEADS_PER_GROUP
QKV_WIDTH = 3 * D_MODEL
GRID_W = 64
NA_ROWS = 8
NA_COLS = 16
D_FF = 4 * D_MODEL
EPS = 1e-6
NEG = -1e30
SCALE = HEAD_DIM ** -0.5
LOG2E = float(np.log2(np.e))
LN2 = float(np.log(2.0))
LANES = 128

F32 = jnp.float32
BF16 = jnp.bfloat16

VMEM_LIMIT_BYTES = 56 * 1024 * 1024

PERM_TILE = 1024
NORM_CHUNK = 512
PROJ_TM, PROJ_TN = 2048, GROUP_WIDTH
SIDE_CAST_MIN_ROWS = 16
MERGE_TM = 256
MLP_TM, MLP_TF = 1024, 1024
DIL_QB = 128
DIL_UNROLL = 4
DIL_ROWS_PER_STEP = 1024
N_WINDOW_CASES = 3
NA_ROWS_PER_STEP = 16
NA_UNROLL = 4

GROUP_DILATIONS = tuple(d for _, d in DILATION_PATTERNS) + (1,)
ROW_ORDERS = tuple(sorted(set(GROUP_DILATIONS)))
GROUP_ROW_ORDER = tuple(ROW_ORDERS.index(d) for d in GROUP_DILATIONS)


def _params(sem):
    return pltpu.CompilerParams(dimension_semantics=sem, vmem_limit_bytes=VMEM_LIMIT_BYTES)


def _rms(x, g):
    ms = jnp.mean(x * x, axis=-1, keepdims=True)
    return (x * lax.rsqrt(ms + EPS)) * g


def _norm_kernel(x_ref, g_ref, h_ref, slab_ref):
    tm = x_ref.shape[0]
    x = x_ref[...]
    inv = lax.rsqrt(jnp.mean(x * x, axis=-1, keepdims=True) + EPS)
    n_slabs = NORM_CHUNK // LANES
    for c0 in range(0, D_MODEL, NORM_CHUNK):
        hc = (x_ref[:, c0:c0 + NORM_CHUNK] * inv) * g_ref[:, c0:c0 + NORM_CHUNK]
        h_ref[0, :, c0:c0 + NORM_CHUNK] = hc.astype(BF16)
        for s in range(n_slabs):
            slab_ref[0, s] = hc[:, s * LANES:(s + 1) * LANES]
        for v in range(1, len(ROW_ORDERS)):
            d_prev, d = ROW_ORDERS[v - 1], ROW_ORDERS[v]
            q = d // d_prev
            n_prev, n = tm // d_prev, tm // d
            last = v == len(ROW_ORDERS) - 1
            for s in range(n_slabs):
                cols = slice(c0 + s * LANES, c0 + (s + 1) * LANES)
                for r in range(d_prev):
                    for r2 in range(q):
                        rows = slab_ref[(v - 1) % 2, s, pl.ds(r * n_prev + r2, n, stride=q), :]
                        dst = (d_prev * r2 + r) * n
                        h_ref[v, dst:dst + n, cols] = rows.astype(BF16)
                        if not last:
                            slab_ref[v % 2, s, dst:dst + n, :] = rows


def _norm(x2d, g):
    m = x2d.shape[0]
    tm = PERM_TILE
    nv = len(ROW_ORDERS)
    assert ROW_ORDERS[0] == 1
    return pl.pallas_call(
        _norm_kernel,
        grid=(m // tm,),
        in_specs=[pl.BlockSpec((tm, D_MODEL), lambda i: (i, 0)),
                  pl.BlockSpec((1, D_MODEL), lambda i: (0, 0))],
        out_specs=pl.BlockSpec((nv, tm, D_MODEL), lambda i: (0, i, 0)),
        out_shape=jax.ShapeDtypeStruct((nv, m, D_MODEL), BF16),
        scratch_shapes=[pltpu.VMEM((2, NORM_CHUNK // LANES, tm, LANES), F32)],
        compiler_params=_params(("arbitrary",)),
        name="rmsnorm_row_orders",
    )(x2d, g)


def _proj_kernel(h_ref, w_first_ref, wq_next_ref, wg_next_ref, bg_ref, *refs, n_qkv_steps, n_side):
    side_in = refs[:n_side]
    qkv_ref, gate_ref = refs[n_side:n_side + 2]
    side_out = refs[n_side + 2:2 * n_side + 2]
    w_bf_ref = refs[-1]
    j = pl.program_id(0)
    i = pl.program_id(1)

    @pl.when((j == 0) & (i == 0))
    def _():
        w_bf_ref[0] = w_first_ref[...].astype(BF16)

    def side_casts(next_ref):
        part = next_ref.shape[0]
        rows = pl.ds(pl.multiple_of(i * part, part), part)
        w_bf_ref[(j + 1) % 2, rows, :] = next_ref[...].astype(BF16)
        for src, dst in zip(side_in, side_out):
            dst[...] = src[...].astype(BF16)

    def qkv_step(next_ref):
        y = jnp.dot(h_ref[...], w_bf_ref[j % 2], preferred_element_type=F32)
        qkv_ref[...] = y.astype(BF16)
        side_casts(next_ref)

    @pl.when(j < n_qkv_steps - 1)
    def _():
        qkv_step(wq_next_ref)

    @pl.when(j == n_qkv_steps - 1)
    def _():
        qkv_step(wg_next_ref)

    @pl.when(j >= n_qkv_steps)
    def _():
        z = jnp.dot(h_ref[...], w_bf_ref[j % 2], preferred_element_type=F32) + bg_ref[...]
        gate_ref[...] = (0.5 * jnp.tanh(0.5 * z) + 0.5).astype(BF16)
        side_casts(wg_next_ref)


def _project(h_orders, w_qkv, w_gate, b_gate, later_weights):
    m = h_orders.shape[1]
    tm, tn = PROJ_TM, PROJ_TN
    n_groups = N_HEAD_GROUPS
    nq = 3 * n_groups
    ng = (2 * D_MODEL) // tn
    n_i = m // tm
    row_order = GROUP_ROW_ORDER

    def lhs_map(j, i):
        v = jnp.int32(0)
        for grp in range(n_groups):
            v = jnp.where(j // 3 == grp, row_order[grp], v)
        return (v, i, 0)

    def w_qkv_col(j):
        jj = jnp.minimum(j, nq - 1)
        return (jj % 3) * n_groups + jj // 3

    gate_col = lambda j, i: (0, jnp.maximum(j - nq, 0))
    wq_next_map = lambda j, i: (i, w_qkv_col(j + 1))
    wg_next_map = lambda j, i: (i, jnp.clip(j + 1 - nq, 0, ng - 1))

    def qkv_out_map(j, i):
        jj = jnp.minimum(j, nq - 1)
        return (jj // 3, jnp.where(j < nq, i, n_i - 1), jj % 3)

    gate_out_map = lambda j, i: (jnp.where(j >= nq, i, 0), jnp.maximum(j - nq, 0))

    n_steps = (nq + ng) * n_i

    def side_rows(w):
        rows = SIDE_CAST_MIN_ROWS
        while rows * n_steps < w.shape[0] or w.shape[0] % rows:
            rows *= 2
        n_blocks = w.shape[0] // rows
        return pl.BlockSpec((rows, w.shape[1]),
                            lambda j, i: (jnp.minimum(j * n_i + i, n_blocks - 1), 0))

    side = list(later_weights)
    return pl.pallas_call(
        functools.partial(_proj_kernel, n_qkv_steps=nq, n_side=len(side)),
        grid=(nq + ng, n_i),
        in_specs=[
            pl.BlockSpec((None, tm, D_MODEL), lhs_map),
            pl.BlockSpec((D_MODEL, tn), lambda j, i: (0, 0), pipeline_mode=pl.Buffered(1)),
            pl.BlockSpec((D_MODEL // n_i, tn), wq_next_map),
            pl.BlockSpec((D_MODEL // n_i, tn), wg_next_map),
            pl.BlockSpec((1, tn), gate_col),
        ] + [side_rows(w) for w in side],
        out_specs=[
            pl.BlockSpec((None, tm, tn), qkv_out_map),
            pl.BlockSpec((tm, tn), gate_out_map),
        ] + [side_rows(w) for w in side],
        out_shape=[
            jax.ShapeDtypeStruct((n_groups, m, 3 * GROUP_WIDTH), BF16),
            jax.ShapeDtypeStruct((m, 2 * D_MODEL), BF16),
        ] + [jax.ShapeDtypeStruct(w.shape, BF16) for w in side],
        scratch_shapes=[pltpu.VMEM((2, D_MODEL, tn), BF16)],
        compiler_params=_params(("arbitrary", "arbitrary")),
        name="proj_qkv_gate",
    )(h_orders, w_qkv, w_qkv, w_gate, b_gate, *side)


def _dilated_block_shape(seq, half_window):
    kw = DIL_QB + 2 * half_window
    return (seq, seq) if seq <= kw else (DIL_QB, kw)


def _dilated_kernel(q_ref, k_ref, v_ref, o_ref, lse_ref, bias_ref, *, seq, half_window, coefs):
    qb, kw = _dilated_block_shape(seq, half_window)
    n_blocks = seq // qb
    n_tiles, n_res = q_ref.shape[:2]

    def rows_loader(ref):
        if n_res == 1:
            flat = ref.at[:, 0].reshape(seq, GROUP_WIDTH)
            return lambda rr, start, size, cols: flat[pl.ds(start, size), cols]
        assert n_blocks == 1
        return lambda rr, start, size, cols: jnp.concatenate(
            [ref[t, rr, :, cols] for t in range(n_tiles)], axis=0)

    load_q, load_k, load_v = rows_loader(q_ref), rows_loader(k_ref), rows_loader(v_ref)
    lane = lax.broadcasted_iota(jnp.int32, (qb, HEAD_DIM), 1)
    heads = range(HEADS_PER_GROUP)
    head_cols = [slice(h * HEAD_DIM, (h + 1) * HEAD_DIM) for h in heads]

    @pl.when((pl.program_id(0) == 0) & (pl.program_id(1) == 0))
    def _():
        qrow = lax.broadcasted_iota(jnp.int32, (qb, kw), 0)
        kcol = lax.broadcasted_iota(jnp.int32, (qb, kw), 1)
        for c in range(bias_ref.shape[0]):
            dist = jnp.abs(kcol - qrow - c * half_window)
            dist_f = dist.astype(F32)
            for h in heads:
                bias_ref[c, h] = jnp.where(dist <= half_window, (-coefs[h] * LOG2E) * dist_f, NEG)

    def block(qi, carry):
        q0 = pl.multiple_of(qi * qb, qb)
        ks = pl.multiple_of(jnp.clip(q0 - half_window, 0, seq - kw), half_window)
        window_case = (q0 - ks) // half_window
        pairs = [(rr, h) for rr in range(n_res) for h in heads]
        scores = [lax.dot_general(load_q(rr, q0, qb, head_cols[h]), load_k(rr, ks, kw, head_cols[h]),
                                  (((1,), (1,)), ((), ())), preferred_element_type=F32)
                  for rr, h in pairs]
        probs, dens = [], []
        lse_all = [jnp.zeros((qb, HEAD_DIM), F32) for _ in range(n_res)]
        for (rr, h), s in zip(pairs, scores):
            t = s * (SCALE * LOG2E) + bias_ref[window_case, h]
            m = jnp.max(t, axis=-1, keepdims=True)
            p = jnp.exp2(t - m)
            den = jnp.sum(p, axis=-1, keepdims=True)
            probs.append(p.astype(BF16))
            dens.append(den)
            lse_all[rr] = jnp.where(lane == h, m * LN2 + jnp.log(den), lse_all[rr])
        for (rr, h), p, den in zip(pairs, probs, dens):
            o = jnp.dot(p, load_v(rr, ks, kw, head_cols[h]), preferred_element_type=F32) / den
            o_ref[rr, pl.ds(q0, qb), head_cols[h]] = o.astype(BF16)
        for rr in range(n_res):
            lse_ref[rr, pl.ds(q0, qb), :] = lse_all[rr]
        return carry

    lax.fori_loop(0, n_blocks, block, 0, unroll=min(DIL_UNROLL, n_blocks))


def _dilated_group(qkv, batch, seq_total, group, window, dilation, slopes):
    d = dilation
    seq = seq_total // d
    half_window = window // (2 * d)
    qb, kw = _dilated_block_shape(seq, half_window)
    coefs = tuple(float(slopes[group * HEADS_PER_GROUP + h]) * d for h in range(HEADS_PER_GROUP))
    tiles = seq_total // PERM_TILE
    rows = PERM_TILE // d
    view = qkv.reshape(qkv.shape[0], batch, tiles, d, rows, qkv.shape[-1])
    n_res = min(d, max(1, DIL_ROWS_PER_STEP // seq))
    part = lambda which: pl.BlockSpec((None, None, tiles, n_res, rows, GROUP_WIDTH),
                                      lambda b, r: (group, b, 0, r, 0, which))
    return pl.pallas_call(
        functools.partial(_dilated_kernel, seq=seq, half_window=half_window, coefs=coefs),
        grid=(batch, d // n_res),
        in_specs=[part(0), part(1), part(2)],
        out_specs=[
            pl.BlockSpec((None, n_res, seq, GROUP_WIDTH), lambda b, r: (b, r, 0, 0)),
            pl.BlockSpec((None, n_res, seq, HEAD_DIM), lambda b, r: (b, r, 0, 0)),
        ],
        out_shape=[
            jax.ShapeDtypeStruct((batch, d, seq, GROUP_WIDTH), BF16),
            jax.ShapeDtypeStruct((batch, d, seq, HEAD_DIM), F32),
        ],
        scratch_shapes=[pltpu.VMEM((N_WINDOW_CASES, HEADS_PER_GROUP, qb, kw), F32)],
        compiler_params=_params(("arbitrary", "arbitrary")),
        name=f"dilated_attention_d{d}",
    )(view, view, view)


N_BIAS_VARIANTS = NA_ROWS


def _na_bias_kernel(rpb_ref, out_ref, *, kh):
    h = pl.program_id(0)
    n_off = 2 * NA_ROWS - 1
    width = (n_off + 1) * GRID_W
    qc = lax.broadcasted_iota(jnp.int32, (GRID_W, width), 0)
    kc = lax.broadcasted_iota(jnp.int32, (GRID_W, width), 1) % GRID_W
    col_idx = jnp.clip(kc - qc, -(NA_COLS - 1), NA_COLS - 1) + (NA_COLS - 1)
    off_row = lax.broadcasted_iota(jnp.int32, (1, width), 1) // GRID_W
    table = jnp.zeros((GRID_W, width), F32)
    for j in range(2 * NA_COLS - 1):
        row_vals = jnp.zeros((1, width), F32)
        for a in range(n_off):
            row_vals = jnp.where(off_row == a, rpb_ref[h, a, j], row_vals)
        table = jnp.where(col_idx == j, row_vals, table)
    cs = jnp.clip(qc - NA_COLS // 2, 0, GRID_W - NA_COLS)
    table = jnp.where((kc >= cs) & (kc < cs + NA_COLS), table * LOG2E, NEG)
    for var in range(N_BIAS_VARIANTS):
        out_ref[var] = table[:, var * GRID_W:(var + kh) * GRID_W]


def _na_bias(rpb, kh):
    nkeys = kh * GRID_W
    return pl.pallas_call(
        functools.partial(_na_bias_kernel, kh=kh),
        grid=(N_HEADS_B,),
        in_specs=[pl.BlockSpec(memory_space=pltpu.SMEM)],
        out_specs=pl.BlockSpec((N_BIAS_VARIANTS, None, GRID_W, nkeys), lambda h: (0, h, 0, 0)),
        out_shape=jax.ShapeDtypeStruct((N_BIAS_VARIANTS, N_HEADS_B, GRID_W, nkeys), F32),
        compiler_params=_params(("arbitrary",)),
        name="na_bias_expand",
    )(rpb)


def _na_kernel(q_ref, k_ref, v_ref, bias_ref, o_ref, *, rows, kh):
    rb = pl.program_id(1)
    nkeys = kh * GRID_W

    def one_row(rl, carry):
        r = rb * NA_ROWS_PER_STEP + rl
        rs = jnp.clip(r - kh // 2, 0, rows - kh)
        var = rs - r + (NA_ROWS - 1)
        q0 = pl.multiple_of(rl * GRID_W, GRID_W)
        k0 = pl.multiple_of(rs * GRID_W, GRID_W)
        heads = range(N_HEADS_B)
        head_cols = [slice(h * HEAD_DIM, (h + 1) * HEAD_DIM) for h in heads]
        scores = [lax.dot_general(q_ref[pl.ds(q0, GRID_W), c], k_ref[pl.ds(k0, nkeys), c],
                                  (((1,), (1,)), ((), ())), preferred_element_type=F32)
                  for c in head_cols]
        probs, dens = [], []
        for h in heads:
            t = scores[h] * (SCALE * LOG2E) + bias_ref[var, h]
            m = jnp.max(t, axis=-1, keepdims=True)
            p = jnp.exp2(t - m)
            dens.append(jnp.sum(p, axis=-1, keepdims=True))
            probs.append(p.astype(BF16))
        for h in heads:
            o = jnp.dot(probs[h], v_ref[pl.ds(k0, nkeys), head_cols[h]],
                        preferred_element_type=F32) / dens[h]
            o_ref[pl.ds(q0, GRID_W), head_cols[h]] = o.astype(BF16)
        return carry

    lax.fori_loop(0, NA_ROWS_PER_STEP, one_row, 0, unroll=NA_UNROLL)


def _neighbourhood(qkv, rpb, batch, seq_total):
    group = N_GROUPS_A
    assert GROUP_DILATIONS[group] == 1
    rows = seq_total // GRID_W
    kh = min(NA_ROWS, rows)
    assert kh == NA_ROWS and rows % NA_ROWS_PER_STEP == 0
    bias = _na_bias(rpb, kh)
    tq = NA_ROWS_PER_STEP * GRID_W
    view = qkv.reshape(qkv.shape[0], batch, seq_total, qkv.shape[-1])
    o = pl.pallas_call(
        functools.partial(_na_kernel, rows=rows, kh=kh),
        grid=(batch, rows // NA_ROWS_PER_STEP),
        in_specs=[
            pl.BlockSpec((None, None, tq, GROUP_WIDTH), lambda b, i: (group, b, i, 0)),
            pl.BlockSpec((None, None, seq_total, GROUP_WIDTH), lambda b, i: (group, b, 0, 1)),
            pl.BlockSpec((None, None, seq_total, GROUP_WIDTH), lambda b, i: (group, b, 0, 2)),
            pl.BlockSpec((N_BIAS_VARIANTS, N_HEADS_B, GRID_W, kh * GRID_W), lambda b, i: (0, 0, 0, 0)),
        ],
        out_specs=pl.BlockSpec((None, tq, GROUP_WIDTH), lambda b, i: (b, i, 0)),
        out_shape=jax.ShapeDtypeStruct((batch, seq_total, GROUP_WIDTH), BF16),
        compiler_params=_params(("arbitrary", "arbitrary")),
        name="neighbourhood_attention",
    )(view, view, view, bias)
    return o.reshape(batch * seq_total, GROUP_WIDTH)


def _to_token_order_matrix(tm, d):
    t = lax.broadcasted_iota(jnp.int32, (tm, tm), 0)
    c = lax.broadcasted_iota(jnp.int32, (tm, tm), 1)
    return (c == (t % d) * (tm // d) + t // d).astype(BF16)


def _merge_kernel(o0_ref, o1_ref, o2_ref, l0_ref, l1_ref, l2_ref, ob_ref, gate_ref, x_ref,
                  wpa_ref, wpb_ref, wout_ref, gn_ref, x2_ref, h2_ref, lse_tok_ref, *, dilations):
    tm = x_ref.shape[0]
    o_tok, lse_tok = [], []
    for g, (o_ref, l_ref, d) in enumerate(zip((o0_ref, o1_ref, o2_ref),
                                              (l0_ref, l1_ref, l2_ref), dilations)):
        o = o_ref[...].reshape(tm, GROUP_WIDTH)
        if d == 1:
            o_tok.append(o.astype(F32))
            lse_tok.append(l_ref[...].reshape(tm, HEAD_DIM))
        else:
            o_tok.append(jnp.dot(_to_token_order_matrix(tm, d), o, preferred_element_type=F32))
            for r in range(d):
                lse_tok_ref[g, pl.ds(r, tm // d, stride=d), :] = l_ref[r]
            lse_tok.append(lse_tok_ref[g])
    l0, l1, l2 = lse_tok
    mx = jnp.maximum(jnp.maximum(l0, l1), l2)
    e0, e1, e2 = jnp.exp(l0 - mx), jnp.exp(l1 - mx), jnp.exp(l2 - mx)
    tot = e0 + e1 + e2
    a0, a1, a2 = e0 / tot, e1 / tot, e2 / tot
    parts = []
    for h in range(HEADS_PER_GROUP):
        cols = slice(h * HEAD_DIM, (h + 1) * HEAD_DIM)
        y = (a0[:, h:h + 1] * o_tok[0][:, cols] + a1[:, h:h + 1] * o_tok[1][:, cols]
             + a2[:, h:h + 1] * o_tok[2][:, cols])
        parts.append(y.astype(BF16))
    ya = jnp.concatenate(parts, axis=1)
    ta = jnp.dot(ya, wpa_ref[...], preferred_element_type=F32)
    tb = jnp.dot(ob_ref[...], wpb_ref[...], preferred_element_type=F32)
    merged = (gate_ref[:, :D_MODEL].astype(F32) * ta + gate_ref[:, D_MODEL:].astype(F32) * tb)
    x2 = x_ref[...] + jnp.dot(merged.astype(BF16), wout_ref[...], preferred_element_type=F32)
    x2_ref[...] = x2
    h2_ref[...] = _rms(x2, gn_ref[...]).astype(BF16)


def _merge(o_groups, lse_groups, ob, gates, x2d, seq, wpa, wpb, wout, gn):
    m = x2d.shape[0]
    tm = MERGE_TM
    n_steps = m // tm
    tiles_per_batch = seq // tm
    dilations = tuple(o.shape[1] for o in o_groups)
    row = lambda w: pl.BlockSpec((tm, w), lambda i: (i, 0))
    grouped = lambda d, w: pl.BlockSpec(
        (None, d, tm // d, w), lambda i: (i // tiles_per_batch, 0, i % tiles_per_batch, 0))
    const = lambda a, b: pl.BlockSpec((a, b), lambda i: (0, 0), pipeline_mode=pl.Buffered(1))
    return pl.pallas_call(
        functools.partial(_merge_kernel, dilations=dilations),
        grid=(n_steps,),
        in_specs=[grouped(d, GROUP_WIDTH) for d in dilations]
        + [grouped(d, HEAD_DIM) for d in dilations]
        + [row(GROUP_WIDTH), row(2 * D_MODEL), row(D_MODEL),
           const(GROUP_WIDTH, D_MODEL), const(GROUP_WIDTH, D_MODEL), const(D_MODEL, D_MODEL),
           const(1, D_MODEL)],
        out_specs=[row(D_MODEL), row(D_MODEL)],
        out_shape=[jax.ShapeDtypeStruct((m, D_MODEL), F32), jax.ShapeDtypeStruct((m, D_MODEL), BF16)],
        scratch_shapes=[pltpu.VMEM((len(dilations), tm, HEAD_DIM), F32)],
        compiler_params=_params(("arbitrary",)),
        name="merge_out_proj",
    )(*o_groups, *lse_groups, ob, gates, x2d, wpa, wpb, wout, gn)


def _mlp_kernel(h2_ref, wup_ref, wdn_ref, x2_hbm, gf_ref, out_ref, x2_sem, *, n_f):
    i = pl.program_id(0)
    f = pl.program_id(1)
    tm = out_ref.shape[0]

    def residual_copy():
        return pltpu.make_async_copy(x2_hbm.at[pl.ds(i * tm, tm), :], out_ref, x2_sem)

    @pl.when(f == 0)
    def _():
        residual_copy().start()

    hid = jnp.dot(h2_ref[...], wup_ref[...], preferred_element_type=F32)
    hid = jnp.square(jnp.maximum(hid, 0.0)).astype(BF16)

    @pl.when(f == 0)
    def _():
        residual_copy().wait()

    out_ref[...] += jnp.dot(hid, wdn_ref[...], preferred_element_type=F32)

    @pl.when(f == n_f - 1)
    def _():
        out_ref[...] = _rms(out_ref[...], gf_ref[...])


def _mlp(h2, wup, wdn, x2, gf):
    m = h2.shape[0]
    tm, tf = MLP_TM, MLP_TF
    n_f = D_FF // tf
    return pl.pallas_call(
        functools.partial(_mlp_kernel, n_f=n_f),
        grid=(m // tm, n_f),
        in_specs=[
            pl.BlockSpec((tm, D_MODEL), lambda i, f: (i, 0)),
            pl.BlockSpec((D_MODEL, tf), lambda i, f: (0, f)),
            pl.BlockSpec((tf, D_MODEL), lambda i, f: (f, 0)),
            pl.BlockSpec(memory_space=pl.ANY),
            pl.BlockSpec((1, D_MODEL), lambda i, f: (0, 0)),
        ],
        out_specs=pl.BlockSpec((tm, D_MODEL), lambda i, f: (i, 0)),
        out_shape=jax.ShapeDtypeStruct((m, D_MODEL), F32),
        scratch_shapes=[pltpu.SemaphoreType.DMA],
        compiler_params=_params(("arbitrary", "arbitrary")),
        name="mlp_residual_norm",
    )(h2, wup, wdn, x2, gf)


def _layer(x2d, batch, seq, norm_mix, w_qkv, w_gate, b_gate, rpb, w_proj_a, w_proj_b, w_out,
           norm_mlp, w_up, w_down):
    slopes = 2.0 ** (-8.0 * np.arange(1, N_HEADS_A + 1) / N_HEADS_A)
    row = lambda v: v.reshape(1, -1)
    h_orders = _norm(x2d, row(norm_mix))
    qkv, gates, w_out_bf, w_pa_bf, w_pb_bf, w_up_bf, w_down_bf = _project(
        h_orders, w_qkv, w_gate, row(b_gate), (w_out, w_proj_a, w_proj_b, w_up, w_down))
    o_groups, lse_groups = [], []
    for g, (window, d) in enumerate(DILATION_PATTERNS):
        o, lse = _dilated_group(qkv, batch, seq, g, window, d, slopes)
        o_groups.append(o)
        lse_groups.append(lse)
    ob = _neighbourhood(qkv, rpb, batch, seq)
    x2, h2 = _merge(o_groups, lse_groups, ob, gates, x2d, seq, w_pa_bf, w_pb_bf, w_out_bf,
                    row(norm_mlp))
    return x2, h2, w_up_bf, w_down_bf


def kernel(x, norm_mix, w_qkv, w_gate, b_gate, rpb, w_proj_a, w_proj_b, w_out, norm_mlp, w_up,
           w_down, norm_final):
    batch, seq, _ = x.shape
    depth = norm_mix.shape[0]
    assert depth == 1 and seq % PERM_TILE == 0
    x2d = x.reshape(batch * seq, D_MODEL)
    x2, h2, w_up_bf, w_down_bf = _layer(
        x2d, batch, seq, norm_mix[0], w_qkv[0], w_gate[0], b_gate[0], rpb[0],
        w_proj_a[0], w_proj_b[0], w_out[0], norm_mlp[0], w_up[0], w_down[0])
    out = _mlp(h2, w_up_bf, w_down_bf, x2, norm_final.reshape(1, -1))
    return out.reshape(batch, seq, D_MODEL)
```

```python
import functools

import jax
import jax.numpy as jnp
import numpy as np
from jax import lax
from jax.experimental import pallas as pl
from jax.experimental.pallas import tpu as pltpu

D_MODEL = 2048
HEAD_DIM = 128
N_HEADS = D_MODEL // HEAD_DIM
N_HEADS_B = N_HEADS // 4
N_HEADS_A = N_HEADS - N_HEADS_B
DILATION_PATTERNS = ((128, 1), (512, 4), (2048, 16))
N_GROUPS_A = len(DILATION_PATTERNS)
HEADS_PER_GROUP = N_HEADS_A // N_GROUPS_A
GROUP_WIDTH = HEADS_PER_GROUP * HEAD_DIM
N_HEAD_GROUPS = N_HEADS // HEADS_PER_GROUP
QKV_WIDTH = 3 * D_MODEL
GRID_W = 64
NA_ROWS = 8
NA_COLS = 16
D_FF = 4 * D_MODEL
EPS = 1e-6
NEG = -1e30
SCALE = HEAD_DIM ** -0.5
LOG2E = float(np.log2(np.e))
LN2 = float(np.log(2.0))
LANES = 128

F32 = jnp.float32
BF16 = jnp.bfloat16

VMEM_LIMIT_BYTES = 56 * 1024 * 1024

PERM_TILE = 1024
NORM_CHUNK = 512
PROJ_TM, PROJ_TN = 2048, GROUP_WIDTH
SIDE_CAST_ROWS = 32
MERGE_TM = 256
MLP_TM, MLP_TF = 1024, 1024
DIL_QB = 128
DIL_UNROLL = 4
DIL_ROWS_PER_STEP = 1024
N_WINDOW_CASES = 3
NA_ROWS_PER_STEP = 16
NA_UNROLL = 4

GROUP_DILATIONS = tuple(d for _, d in DILATION_PATTERNS) + (1,)
ROW_ORDERS = tuple(sorted(set(GROUP_DILATIONS)))
GROUP_ROW_ORDER = tuple(ROW_ORDERS.index(d) for d in GROUP_DILATIONS)


def _params(sem):
    return pltpu.CompilerParams(dimension_semantics=sem, vmem_limit_bytes=VMEM_LIMIT_BYTES)


def _rms(x, g):
    ms = jnp.mean(x * x, axis=-1, keepdims=True)
    return (x * lax.rsqrt(ms + EPS)) * g


def _norm_kernel(x_ref, g_ref, h_ref, slab_ref):
    tm = x_ref.shape[0]
    x = x_ref[...]
    inv = lax.rsqrt(jnp.mean(x * x, axis=-1, keepdims=True) + EPS)
    n_slabs = NORM_CHUNK // LANES
    for c0 in range(0, D_MODEL, NORM_CHUNK):
        hc = (x_ref[:, c0:c0 + NORM_CHUNK] * inv) * g_ref[:, c0:c0 + NORM_CHUNK]
        h_ref[0, :, c0:c0 + NORM_CHUNK] = hc.astype(BF16)
        for s in range(n_slabs):
            slab_ref[0, s] = hc[:, s * LANES:(s + 1) * LANES]
        for v in range(1, len(ROW_ORDERS)):
            d_prev, d = ROW_ORDERS[v - 1], ROW_ORDERS[v]
            q = d // d_prev
            n_prev, n = tm // d_prev, tm // d
            last = v == len(ROW_ORDERS) - 1
            for s in range(n_slabs):
                cols = slice(c0 + s * LANES, c0 + (s + 1) * LANES)
                for r in range(d_prev):
                    for r2 in range(q):
                        rows = slab_ref[(v - 1) % 2, s, pl.ds(r * n_prev + r2, n, stride=q), :]
                        dst = (d_prev * r2 + r) * n
                        h_ref[v, dst:dst + n, cols] = rows.astype(BF16)
                        if not last:
                            slab_ref[v % 2, s, dst:dst + n, :] = rows


def _norm(x2d, g):
    m = x2d.shape[0]
    tm = PERM_TILE
    nv = len(ROW_ORDERS)
    assert ROW_ORDERS[0] == 1
    return pl.pallas_call(
        _norm_kernel,
        grid=(m // tm,),
        in_specs=[pl.BlockSpec((tm, D_MODEL), lambda i: (i, 0)),
                  pl.BlockSpec((1, D_MODEL), lambda i: (0, 0))],
        out_specs=pl.BlockSpec((nv, tm, D_MODEL), lambda i: (0, i, 0)),
        out_shape=jax.ShapeDtypeStruct((nv, m, D_MODEL), BF16),
        scratch_shapes=[pltpu.VMEM((2, NORM_CHUNK // LANES, tm, LANES), F32)],
        compiler_params=_params(("arbitrary",)),
        name="rmsnorm_row_orders",
    )(x2d, g)


def _proj_kernel(h_ref, w_first_ref, wq_next_ref, wg_next_ref, bg_ref, wout_ref, wpa_ref, wpb_ref,
                 qkv_ref, gate_ref, wout_bf_ref, wpa_bf_ref, wpb_bf_ref, w_bf_ref, *, n_qkv_steps):
    j = pl.program_id(0)
    i = pl.program_id(1)

    @pl.when((j == 0) & (i == 0))
    def _():
        w_bf_ref[0] = w_first_ref[...].astype(BF16)

    def side_casts(next_ref):
        part = next_ref.shape[0]
        rows = pl.ds(pl.multiple_of(i * part, part), part)
        w_bf_ref[(j + 1) % 2, rows, :] = next_ref[...].astype(BF16)
        wout_bf_ref[...] = wout_ref[...].astype(BF16)
        wpa_bf_ref[...] = wpa_ref[...].astype(BF16)
        wpb_bf_ref[...] = wpb_ref[...].astype(BF16)

    def qkv_step(next_ref):
        y = jnp.dot(h_ref[...], w_bf_ref[j % 2], preferred_element_type=F32)
        qkv_ref[...] = y.astype(BF16)
        side_casts(next_ref)

    @pl.when(j < n_qkv_steps - 1)
    def _():
        qkv_step(wq_next_ref)

    @pl.when(j == n_qkv_steps - 1)
    def _():
        qkv_step(wg_next_ref)

    @pl.when(j >= n_qkv_steps)
    def _():
        z = jnp.dot(h_ref[...], w_bf_ref[j % 2], preferred_element_type=F32) + bg_ref[...]
        gate_ref[...] = (0.5 * jnp.tanh(0.5 * z) + 0.5).astype(BF16)
        side_casts(wg_next_ref)


def _project(h_orders, w_qkv, w_gate, b_gate, w_out, w_proj_a, w_proj_b):
    m = h_orders.shape[1]
    tm, tn = PROJ_TM, PROJ_TN
    n_groups = N_HEAD_GROUPS
    nq = 3 * n_groups
    ng = (2 * D_MODEL) // tn
    n_i = m // tm
    row_order = GROUP_ROW_ORDER

    def lhs_map(j, i):
        v = jnp.int32(0)
        for grp in range(n_groups):
            v = jnp.where(j // 3 == grp, row_order[grp], v)
        return (v, i, 0)

    def w_qkv_col(j):
        jj = jnp.minimum(j, nq - 1)
        return (jj % 3) * n_groups + jj // 3

    gate_col = lambda j, i: (0, jnp.maximum(j - nq, 0))
    wq_next_map = lambda j, i: (i, w_qkv_col(j + 1))
    wg_next_map = lambda j, i: (i, jnp.clip(j + 1 - nq, 0, ng - 1))

    def qkv_out_map(j, i):
        jj = jnp.minimum(j, nq - 1)
        return (jj // 3, jnp.where(j < nq, i, n_i - 1), jj % 3)

    gate_out_map = lambda j, i: (jnp.where(j >= nq, i, 0), jnp.maximum(j - nq, 0))

    def side_rows(w):
        n_blocks = w.shape[0] // SIDE_CAST_ROWS
        return pl.BlockSpec((SIDE_CAST_ROWS, w.shape[1]),
                            lambda j, i: (jnp.minimum(j * n_i + i, n_blocks - 1), 0))

    side = [w_out, w_proj_a, w_proj_b]
    assert all(w.shape[0] // SIDE_CAST_ROWS <= (nq + ng) * n_i for w in side)
    return pl.pallas_call(
        functools.partial(_proj_kernel, n_qkv_steps=nq),
        grid=(nq + ng, n_i),
        in_specs=[
            pl.BlockSpec((None, tm, D_MODEL), lhs_map),
            pl.BlockSpec((D_MODEL, tn), lambda j, i: (0, 0), pipeline_mode=pl.Buffered(1)),
            pl.BlockSpec((D_MODEL // n_i, tn), wq_next_map),
            pl.BlockSpec((D_MODEL // n_i, tn), wg_next_map),
            pl.BlockSpec((1, tn), gate_col),
        ] + [side_rows(w) for w in side],
        out_specs=[
            pl.BlockSpec((None, tm, tn), qkv_out_map),
            pl.BlockSpec((tm, tn), gate_out_map),
        ] + [side_rows(w) for w in side],
        out_shape=[
            jax.ShapeDtypeStruct((n_groups, m, 3 * GROUP_WIDTH), BF16),
            jax.ShapeDtypeStruct((m, 2 * D_MODEL), BF16),
        ] + [jax.ShapeDtypeStruct(w.shape, BF16) for w in side],
        scratch_shapes=[pltpu.VMEM((2, D_MODEL, tn), BF16)],
        compiler_params=_params(("arbitrary", "arbitrary")),
        name="proj_qkv_gate",
    )(h_orders, w_qkv, w_qkv, w_gate, b_gate, *side)


def _dilated_block_shape(seq, half_window):
    kw = DIL_QB + 2 * half_window
    return (seq, seq) if seq <= kw else (DIL_QB, kw)


def _dilated_kernel(q_ref, k_ref, v_ref, o_ref, lse_ref, bias_ref, *, seq, half_window, coefs):
    qb, kw = _dilated_block_shape(seq, half_window)
    n_blocks = seq // qb
    n_tiles, n_res = q_ref.shape[:2]

    def rows_loader(ref):
        if n_res == 1:
            flat = ref.at[:, 0].reshape(seq, GROUP_WIDTH)
            return lambda rr, start, size, cols: flat[pl.ds(start, size), cols]
        assert n_blocks == 1
        return lambda rr, start, size, cols: jnp.concatenate(
            [ref[t, rr, :, cols] for t in range(n_tiles)], axis=0)

    load_q, load_k, load_v = rows_loader(q_ref), rows_loader(k_ref), rows_loader(v_ref)
    lane = lax.broadcasted_iota(jnp.int32, (qb, HEAD_DIM), 1)
    heads = range(HEADS_PER_GROUP)
    head_cols = [slice(h * HEAD_DIM, (h + 1) * HEAD_DIM) for h in heads]

    @pl.when((pl.program_id(0) == 0) & (pl.program_id(1) == 0))
    def _():
        qrow = lax.broadcasted_iota(jnp.int32, (qb, kw), 0)
        kcol = lax.broadcasted_iota(jnp.int32, (qb, kw), 1)
        for c in range(bias_ref.shape[0]):
            dist = jnp.abs(kcol - qrow - c * half_window)
            dist_f = dist.astype(F32)
            for h in heads:
                bias_ref[c, h] = jnp.where(dist <= half_window, (-coefs[h] * LOG2E) * dist_f, NEG)

    def block(qi, carry):
        q0 = pl.multiple_of(qi * qb, qb)
        ks = pl.multiple_of(jnp.clip(q0 - half_window, 0, seq - kw), half_window)
        window_case = (q0 - ks) // half_window
        pairs = [(rr, h) for rr in range(n_res) for h in heads]
        scores = [lax.dot_general(load_q(rr, q0, qb, head_cols[h]), load_k(rr, ks, kw, head_cols[h]),
                                  (((1,), (1,)), ((), ())), preferred_element_type=F32)
                  for rr, h in pairs]
        probs, dens = [], []
        lse_all = [jnp.zeros((qb, HEAD_DIM), F32) for _ in range(n_res)]
        for (rr, h), s in zip(pairs, scores):
            t = s * (SCALE * LOG2E) + bias_ref[window_case, h]
            m = jnp.max(t, axis=-1, keepdims=True)
            p = jnp.exp2(t - m)
            den = jnp.sum(p, axis=-1, keepdims=True)
            probs.append(p.astype(BF16))
            dens.append(den)
            lse_all[rr] = jnp.where(lane == h, m * LN2 + jnp.log(den), lse_all[rr])
        for (rr, h), p, den in zip(pairs, probs, dens):
            o = jnp.dot(p, load_v(rr, ks, kw, head_cols[h]), preferred_element_type=F32) / den
            o_ref[rr, pl.ds(q0, qb), head_cols[h]] = o.astype(BF16)
        for rr in range(n_res):
            lse_ref[rr, pl.ds(q0, qb), :] = lse_all[rr]
        return carry

    lax.fori_loop(0, n_blocks, block, 0, unroll=min(DIL_UNROLL, n_blocks))


def _dilated_group(qkv, batch, seq_total, group, window, dilation, slopes):
    d = dilation
    seq = seq_total // d
    half_window = window // (2 * d)
    qb, kw = _dilated_block_shape(seq, half_window)
    coefs = tuple(float(slopes[group * HEADS_PER_GROUP + h]) * d for h in range(HEADS_PER_GROUP))
    tiles = seq_total // PERM_TILE
    rows = PERM_TILE // d
    view = qkv.reshape(qkv.shape[0], batch, tiles, d, rows, qkv.shape[-1])
    n_res = min(d, max(1, DIL_ROWS_PER_STEP // seq))
    part = lambda which: pl.BlockSpec((None, None, tiles, n_res, rows, GROUP_WIDTH),
                                      lambda b, r: (group, b, 0, r, 0, which))
    return pl.pallas_call(
        functools.partial(_dilated_kernel, seq=seq, half_window=half_window, coefs=coefs),
        grid=(batch, d // n_res),
        in_specs=[part(0), part(1), part(2)],
        out_specs=[
            pl.BlockSpec((None, n_res, seq, GROUP_WIDTH), lambda b, r: (b, r, 0, 0)),
            pl.BlockSpec((None, n_res, seq, HEAD_DIM), lambda b, r: (b, r, 0, 0)),
        ],
        out_shape=[
            jax.ShapeDtypeStruct((batch, d, seq, GROUP_WIDTH), BF16),
            jax.ShapeDtypeStruct((batch, d, seq, HEAD_DIM), F32),
        ],
        scratch_shapes=[pltpu.VMEM((N_WINDOW_CASES, HEADS_PER_GROUP, qb, kw), F32)],
        compiler_params=_params(("arbitrary", "arbitrary")),
        name=f"dilated_attention_d{d}",
    )(view, view, view)


N_BIAS_VARIANTS = NA_ROWS


def _na_bias_kernel(rpb_ref, out_ref, *, kh):
    h = pl.program_id(0)
    n_off = 2 * NA_ROWS - 1
    width = (n_off + 1) * GRID_W
    qc = lax.broadcasted_iota(jnp.int32, (GRID_W, width), 0)
    kc = lax.broadcasted_iota(jnp.int32, (GRID_W, width), 1) % GRID_W
    col_idx = jnp.clip(kc - qc, -(NA_COLS - 1), NA_COLS - 1) + (NA_COLS - 1)
    off_row = lax.broadcasted_iota(jnp.int32, (1, width), 1) // GRID_W
    table = jnp.zeros((GRID_W, width), F32)
    for j in range(2 * NA_COLS - 1):
        row_vals = jnp.zeros((1, width), F32)
        for a in range(n_off):
            row_vals = jnp.where(off_row == a, rpb_ref[h, a, j], row_vals)
        table = jnp.where(col_idx == j, row_vals, table)
    cs = jnp.clip(qc - NA_COLS // 2, 0, GRID_W - NA_COLS)
    table = jnp.where((kc >= cs) & (kc < cs + NA_COLS), table * LOG2E, NEG)
    for var in range(N_BIAS_VARIANTS):
        out_ref[var] = table[:, var * GRID_W:(var + kh) * GRID_W]


def _na_bias(rpb, kh):
    nkeys = kh * GRID_W
    return pl.pallas_call(
        functools.partial(_na_bias_kernel, kh=kh),
        grid=(N_HEADS_B,),
        in_specs=[pl.BlockSpec(memory_space=pltpu.SMEM)],
        out_specs=pl.BlockSpec((N_BIAS_VARIANTS, None, GRID_W, nkeys), lambda h: (0, h, 0, 0)),
        out_shape=jax.ShapeDtypeStruct((N_BIAS_VARIANTS, N_HEADS_B, GRID_W, nkeys), F32),
        compiler_params=_params(("arbitrary",)),
        name="na_bias_expand",
    )(rpb)


def _na_kernel(q_ref, k_ref, v_ref, bias_ref, o_ref, *, rows, kh):
    rb = pl.program_id(1)
    nkeys = kh * GRID_W

    def one_row(rl, carry):
        r = rb * NA_ROWS_PER_STEP + rl
        rs = jnp.clip(r - kh // 2, 0, rows - kh)
        var = rs - r + (NA_ROWS - 1)
        q0 = pl.multiple_of(rl * GRID_W, GRID_W)
        k0 = pl.multiple_of(rs * GRID_W, GRID_W)
        heads = range(N_HEADS_B)
        head_cols = [slice(h * HEAD_DIM, (h + 1) * HEAD_DIM) for h in heads]
        scores = [lax.dot_general(q_ref[pl.ds(q0, GRID_W), c], k_ref[pl.ds(k0, nkeys), c],
                                  (((1,), (1,)), ((), ())), preferred_element_type=F32)
                  for c in head_cols]
        probs, dens = [], []
        for h in heads:
            t = scores[h] * (SCALE * LOG2E) + bias_ref[var, h]
            m = jnp.max(t, axis=-1, keepdims=True)
            p = jnp.exp2(t - m)
            dens.append(jnp.sum(p, axis=-1, keepdims=True))
            probs.append(p.astype(BF16))
        for h in heads:
            o = jnp.dot(probs[h], v_ref[pl.ds(k0, nkeys), head_cols[h]],
                        preferred_element_type=F32) / dens[h]
            o_ref[pl.ds(q0, GRID_W), head_cols[h]] = o.astype(BF16)
        return carry

    lax.fori_loop(0, NA_ROWS_PER_STEP, one_row, 0, unroll=NA_UNROLL)


def _neighbourhood(qkv, rpb, batch, seq_total):
    group = N_GROUPS_A
    assert GROUP_DILATIONS[group] == 1
    rows = seq_total // GRID_W
    kh = min(NA_ROWS, rows)
    assert kh == NA_ROWS and rows % NA_ROWS_PER_STEP == 0
    bias = _na_bias(rpb, kh)
    tq = NA_ROWS_PER_STEP * GRID_W
    view = qkv.reshape(qkv.shape[0], batch, seq_total, qkv.shape[-1])
    o = pl.pallas_call(
        functools.partial(_na_kernel, rows=rows, kh=kh),
        grid=(batch, rows // NA_ROWS_PER_STEP),
        in_specs=[
            pl.BlockSpec((None, None, tq, GROUP_WIDTH), lambda b, i: (group, b, i, 0)),
            pl.BlockSpec((None, None, seq_total, GROUP_WIDTH), lambda b, i: (group, b, 0, 1)),
            pl.BlockSpec((None, None, seq_total, GROUP_WIDTH), lambda b, i: (group, b, 0, 2)),
            pl.BlockSpec((N_BIAS_VARIANTS, N_HEADS_B, GRID_W, kh * GRID_W), lambda b, i: (0, 0, 0, 0)),
        ],
        out_specs=pl.BlockSpec((None, tq, GROUP_WIDTH), lambda b, i: (b, i, 0)),
        out_shape=jax.ShapeDtypeStruct((batch, seq_total, GROUP_WIDTH), BF16),
        compiler_params=_params(("arbitrary", "arbitrary")),
        name="neighbourhood_attention",
    )(view, view, view, bias)
    return o.reshape(batch * seq_total, GROUP_WIDTH)


def _to_token_order_matrix(tm, d):
    t = lax.broadcasted_iota(jnp.int32, (tm, tm), 0)
    c = lax.broadcasted_iota(jnp.int32, (tm, tm), 1)
    return (c == (t % d) * (tm // d) + t // d).astype(BF16)


def _merge_kernel(o0_ref, o1_ref, o2_ref, l0_ref, l1_ref, l2_ref, ob_ref, gate_ref, x_ref,
                  wpa_ref, wpb_ref, wout_ref, gn_ref, wup_ref, wdn_ref,
                  x2_ref, h2_ref, wup_bf_ref, wdn_bf_ref, lse_tok_ref, *, dilations):
    tm = x_ref.shape[0]
    o_tok, lse_tok = [], []
    for g, (o_ref, l_ref, d) in enumerate(zip((o0_ref, o1_ref, o2_ref),
                                              (l0_ref, l1_ref, l2_ref), dilations)):
        o = o_ref[...].reshape(tm, GROUP_WIDTH)
        if d == 1:
            o_tok.append(o.astype(F32))
            lse_tok.append(l_ref[...].reshape(tm, HEAD_DIM))
        else:
            o_tok.append(jnp.dot(_to_token_order_matrix(tm, d), o, preferred_element_type=F32))
            for r in range(d):
                lse_tok_ref[g, pl.ds(r, tm // d, stride=d), :] = l_ref[r]
            lse_tok.append(lse_tok_ref[g])
    l0, l1, l2 = lse_tok
    mx = jnp.maximum(jnp.maximum(l0, l1), l2)
    e0, e1, e2 = jnp.exp(l0 - mx), jnp.exp(l1 - mx), jnp.exp(l2 - mx)
    tot = e0 + e1 + e2
    a0, a1, a2 = e0 / tot, e1 / tot, e2 / tot
    parts = []
    for h in range(HEADS_PER_GROUP):
        cols = slice(h * HEAD_DIM, (h + 1) * HEAD_DIM)
        y = (a0[:, h:h + 1] * o_tok[0][:, cols] + a1[:, h:h + 1] * o_tok[1][:, cols]
             + a2[:, h:h + 1] * o_tok[2][:, cols])
        parts.append(y.astype(BF16))
    ya = jnp.concatenate(parts, axis=1)
    ta = jnp.dot(ya, wpa_ref[...], preferred_element_type=F32)
    tb = jnp.dot(ob_ref[...], wpb_ref[...], preferred_element_type=F32)
    merged = (gate_ref[:, :D_MODEL].astype(F32) * ta + gate_ref[:, D_MODEL:].astype(F32) * tb)
    tf = wup_bf_ref.shape[-1]
    for f in range(wup_bf_ref.shape[0]):
        wup_bf_ref[f] = wup_ref[:, f * tf:(f + 1) * tf].astype(BF16)
    x2 = x_ref[...] + jnp.dot(merged.astype(BF16), wout_ref[...], preferred_element_type=F32)
    x2_ref[...] = x2
    wdn_bf_ref[...] = wdn_ref[...].astype(BF16)
    h2_ref[...] = _rms(x2, gn_ref[...]).astype(BF16)


def _merge(o_groups, lse_groups, ob, gates, x2d, seq, wpa, wpb, wout, gn, w_up, w_down):
    m = x2d.shape[0]
    tm = MERGE_TM
    n_steps = m // tm
    tiles_per_batch = seq // tm
    dilations = tuple(o.shape[1] for o in o_groups)
    row = lambda w: pl.BlockSpec((tm, w), lambda i: (i, 0))
    wup_rows = pl.BlockSpec((w_up.shape[0] // n_steps, w_up.shape[1]), lambda i: (i, 0))
    wdn_rows = pl.BlockSpec((w_down.shape[0] // n_steps, w_down.shape[1]), lambda i: (i, 0))
    n_f = w_up.shape[1] // MLP_TF
    wup_tiles = pl.BlockSpec((n_f, w_up.shape[0] // n_steps, MLP_TF), lambda i: (0, i, 0))
    grouped = lambda d, w: pl.BlockSpec(
        (None, d, tm // d, w), lambda i: (i // tiles_per_batch, 0, i % tiles_per_batch, 0))
    const = lambda a, b: pl.BlockSpec((a, b), lambda i: (0, 0), pipeline_mode=pl.Buffered(1))
    return pl.pallas_call(
        functools.partial(_merge_kernel, dilations=dilations),
        grid=(n_steps,),
        in_specs=[grouped(d, GROUP_WIDTH) for d in dilations]
        + [grouped(d, HEAD_DIM) for d in dilations]
        + [row(GROUP_WIDTH), row(2 * D_MODEL), row(D_MODEL),
           const(GROUP_WIDTH, D_MODEL), const(GROUP_WIDTH, D_MODEL), const(D_MODEL, D_MODEL),
           const(1, D_MODEL), wup_rows, wdn_rows],
        out_specs=[row(D_MODEL), row(D_MODEL), wup_tiles, wdn_rows],
        out_shape=[jax.ShapeDtypeStruct((m, D_MODEL), F32), jax.ShapeDtypeStruct((m, D_MODEL), BF16),
                   jax.ShapeDtypeStruct((n_f, w_up.shape[0], MLP_TF), BF16),
                   jax.ShapeDtypeStruct(w_down.shape, BF16)],
        scratch_shapes=[pltpu.VMEM((len(dilations), tm, HEAD_DIM), F32)],
        compiler_params=_params(("arbitrary",)),
        name="merge_out_proj",
    )(*o_groups, *lse_groups, ob, gates, x2d, wpa, wpb, wout, gn, w_up, w_down)


def _mlp_kernel(h2_ref, wup_ref, wdn_ref, x2_hbm, gf_ref, out_ref, x2_sem, *, n_f):
    i = pl.program_id(0)
    f = pl.program_id(1)
    tm = out_ref.shape[0]

    def residual_copy():
        return pltpu.make_async_copy(x2_hbm.at[pl.ds(i * tm, tm), :], out_ref, x2_sem)

    @pl.when(f == 0)
    def _():
        residual_copy().start()

    hid = jnp.dot(h2_ref[...], wup_ref[...], preferred_element_type=F32)
    hid = jnp.square(jnp.maximum(hid, 0.0)).astype(BF16)

    @pl.when(f == 0)
    def _():
        residual_copy().wait()

    out_ref[...] += jnp.dot(hid, wdn_ref[...], preferred_element_type=F32)

    @pl.when(f == n_f - 1)
    def _():
        out_ref[...] = _rms(out_ref[...], gf_ref[...])


def _mlp(h2, wup, wdn, x2, gf):
    m = h2.shape[0]
    tm = MLP_TM
    n_f, _, tf = wup.shape
    return pl.pallas_call(
        functools.partial(_mlp_kernel, n_f=n_f),
        grid=(m // tm, n_f),
        in_specs=[
            pl.BlockSpec((tm, D_MODEL), lambda i, f: (i, 0)),
            pl.BlockSpec((None, D_MODEL, tf), lambda i, f: (f, 0, 0)),
            pl.BlockSpec((tf, D_MODEL), lambda i, f: (f, 0)),
            pl.BlockSpec(memory_space=pl.ANY),
            pl.BlockSpec((1, D_MODEL), lambda i, f: (0, 0)),
        ],
        out_specs=pl.BlockSpec((tm, D_MODEL), lambda i, f: (i, 0)),
        out_shape=jax.ShapeDtypeStruct((m, D_MODEL), F32),
        scratch_shapes=[pltpu.SemaphoreType.DMA],
        compiler_params=_params(("arbitrary", "arbitrary")),
        name="mlp_residual_norm",
    )(h2, wup, wdn, x2, gf)


def _layer(x2d, batch, seq, norm_mix, w_qkv, w_gate, b_gate, rpb, w_proj_a, w_proj_b, w_out,
           norm_mlp, w_up, w_down):
    slopes = 2.0 ** (-8.0 * np.arange(1, N_HEADS_A + 1) / N_HEADS_A)
    row = lambda v: v.reshape(1, -1)
    h_orders = _norm(x2d, row(norm_mix))
    qkv, gates, w_out_bf, w_pa_bf, w_pb_bf = _project(
        h_orders, w_qkv, w_gate, row(b_gate), w_out, w_proj_a, w_proj_b)
    o_groups, lse_groups = [], []
    for g, (window, d) in enumerate(DILATION_PATTERNS):
        o, lse = _dilated_group(qkv, batch, seq, g, window, d, slopes)
        o_groups.append(o)
        lse_groups.append(lse)
    ob = _neighbourhood(qkv, rpb, batch, seq)
    return _merge(o_groups, lse_groups, ob, gates, x2d, seq, w_pa_bf, w_pb_bf, w_out_bf,
                  row(norm_mlp), w_up, w_down)


def kernel(x, norm_mix, w_qkv, w_gate, b_gate, rpb, w_proj_a, w_proj_b, w_out, norm_mlp, w_up,
           w_down, norm_final):
    batch, seq, _ = x.shape
    depth = norm_mix.shape[0]
    assert depth == 1 and seq % PERM_TILE == 0
    x2d = x.reshape(batch * seq, D_MODEL)
    x2, h2, w_up_bf, w_down_bf = _layer(
        x2d, batch, seq, norm_mix[0], w_qkv[0], w_gate[0], b_gate[0], rpb[0],
        w_proj_a[0], w_proj_b[0], w_out[0], norm_mlp[0], w_up[0], w_down[0])
    out = _mlp(h2, w_up_bf, w_down_bf, x2, norm_final.reshape(1, -1))
    return out.reshape(batch, seq, D_MODEL)
```

```python
import functools

import jax
import jax.numpy as jnp
import numpy as np
from jax import lax
from jax.experimental import pallas as pl
from jax.experimental.pallas import tpu as pltpu

D_MODEL = 2048
HEAD_DIM = 128
N_HEADS = D_MODEL // HEAD_DIM
N_HEADS_B = N_HEADS // 4
N_HEADS_A = N_HEADS - N_HEADS_B
DILATION_PATTERNS = ((128, 1), (512, 4), (2048, 16))
N_GROUPS_A = len(DILATION_PATTERNS)
HEADS_PER_GROUP = N_HEADS_A // N_GROUPS_A
GROUP_WIDTH = HEADS_PER_GROUP * HEAD_DIM
N_HEAD_GROUPS = N_HEADS // HEADS_PER_GROUP
QKV_WIDTH = 3 * D_MODEL
GRID_W = 64
NA_ROWS = 8
NA_COLS = 16
D_FF = 4 * D_MODEL
EPS = 1e-6
NEG = -1e30
SCALE = HEAD_DIM ** -0.5
LOG2E = float(np.log2(np.e))
LN2 = float(np.log(2.0))
LANES = 128

F32 = jnp.float32
BF16 = jnp.bfloat16

VMEM_LIMIT_BYTES = 56 * 1024 * 1024

PERM_TILE = 1024
NORM_CHUNK = 512
PROJ_TM, PROJ_TN = 2048, GROUP_WIDTH
SIDE_CAST_ROWS = 32
MERGE_TM = 256
MLP_TM, MLP_TF = 1024, 1024
DIL_QB = 128
DIL_UNROLL = 4
DIL_ROWS_PER_STEP = 1024
N_WINDOW_CASES = 3
NA_ROWS_PER_STEP = 16
NA_UNROLL = 4

GROUP_DILATIONS = tuple(d for _, d in DILATION_PATTERNS) + (1,)
ROW_ORDERS = tuple(sorted(set(GROUP_DILATIONS)))
GROUP_ROW_ORDER = tuple(ROW_ORDERS.index(d) for d in GROUP_DILATIONS)


def _params(sem):
    return pltpu.CompilerParams(dimension_semantics=sem, vmem_limit_bytes=VMEM_LIMIT_BYTES)


def _rms(x, g):
    ms = jnp.mean(x * x, axis=-1, keepdims=True)
    return (x * lax.rsqrt(ms + EPS)) * g


def _norm_kernel(x_ref, g_ref, h_ref, slab_ref):
    tm = x_ref.shape[0]
    x = x_ref[...]
    inv = lax.rsqrt(jnp.mean(x * x, axis=-1, keepdims=True) + EPS)
    n_slabs = NORM_CHUNK // LANES
    for c0 in range(0, D_MODEL, NORM_CHUNK):
        hc = (x_ref[:, c0:c0 + NORM_CHUNK] * inv) * g_ref[:, c0:c0 + NORM_CHUNK]
        h_ref[0, :, c0:c0 + NORM_CHUNK] = hc.astype(BF16)
        for s in range(n_slabs):
            slab_ref[0, s] = hc[:, s * LANES:(s + 1) * LANES]
        for v in range(1, len(ROW_ORDERS)):
            d_prev, d = ROW_ORDERS[v - 1], ROW_ORDERS[v]
            q = d // d_prev
            n_prev, n = tm // d_prev, tm // d
            last = v == len(ROW_ORDERS) - 1
            for s in range(n_slabs):
                cols = slice(c0 + s * LANES, c0 + (s + 1) * LANES)
                for r in range(d_prev):
                    for r2 in range(q):
                        rows = slab_ref[(v - 1) % 2, s, pl.ds(r * n_prev + r2, n, stride=q), :]
                        dst = (d_prev * r2 + r) * n
                        h_ref[v, dst:dst + n, cols] = rows.astype(BF16)
                        if not last:
                            slab_ref[v % 2, s, dst:dst + n, :] = rows


def _norm(x2d, g):
    m = x2d.shape[0]
    tm = PERM_TILE
    nv = len(ROW_ORDERS)
    assert ROW_ORDERS[0] == 1
    return pl.pallas_call(
        _norm_kernel,
        grid=(m // tm,),
        in_specs=[pl.BlockSpec((tm, D_MODEL), lambda i: (i, 0)),
                  pl.BlockSpec((1, D_MODEL), lambda i: (0, 0))],
        out_specs=pl.BlockSpec((nv, tm, D_MODEL), lambda i: (0, i, 0)),
        out_shape=jax.ShapeDtypeStruct((nv, m, D_MODEL), BF16),
        scratch_shapes=[pltpu.VMEM((2, NORM_CHUNK // LANES, tm, LANES), F32)],
        compiler_params=_params(("arbitrary",)),
        name="rmsnorm_row_orders",
    )(x2d, g)


def _proj_kernel(h_ref, w_first_ref, wq_next_ref, wg_next_ref, bg_ref, wout_ref, wpa_ref, wpb_ref,
                 qkv_ref, gate_ref, wout_bf_ref, wpa_bf_ref, wpb_bf_ref, w_bf_ref, *, n_qkv_steps):
    j = pl.program_id(0)
    i = pl.program_id(1)

    @pl.when((j == 0) & (i == 0))
    def _():
        w_bf_ref[0] = w_first_ref[...].astype(BF16)

    def side_casts(next_ref):
        part = next_ref.shape[0]
        rows = pl.ds(pl.multiple_of(i * part, part), part)
        w_bf_ref[(j + 1) % 2, rows, :] = next_ref[...].astype(BF16)
        wout_bf_ref[...] = wout_ref[...].astype(BF16)
        wpa_bf_ref[...] = wpa_ref[...].astype(BF16)
        wpb_bf_ref[...] = wpb_ref[...].astype(BF16)

    def qkv_step(next_ref):
        y = jnp.dot(h_ref[...], w_bf_ref[j % 2], preferred_element_type=F32)
        qkv_ref[...] = y.astype(BF16)
        side_casts(next_ref)

    @pl.when(j < n_qkv_steps - 1)
    def _():
        qkv_step(wq_next_ref)

    @pl.when(j == n_qkv_steps - 1)
    def _():
        qkv_step(wg_next_ref)

    @pl.when(j >= n_qkv_steps)
    def _():
        z = jnp.dot(h_ref[...], w_bf_ref[j % 2], preferred_element_type=F32) + bg_ref[...]
        gate_ref[...] = (0.5 * jnp.tanh(0.5 * z) + 0.5).astype(BF16)
        side_casts(wg_next_ref)


def _project(h_orders, w_qkv, w_gate, b_gate, w_out, w_proj_a, w_proj_b):
    m = h_orders.shape[1]
    tm, tn = PROJ_TM, PROJ_TN
    n_groups = N_HEAD_GROUPS
    nq = 3 * n_groups
    ng = (2 * D_MODEL) // tn
    n_i = m // tm
    row_order = GROUP_ROW_ORDER

    def lhs_map(j, i):
        v = jnp.int32(0)
        for grp in range(n_groups):
            v = jnp.where(j // 3 == grp, row_order[grp], v)
        return (v, i, 0)

    def w_qkv_col(j):
        jj = jnp.minimum(j, nq - 1)
        return (jj % 3) * n_groups + jj // 3

    gate_col = lambda j, i: (0, jnp.maximum(j - nq, 0))
    wq_next_map = lambda j, i: (i, w_qkv_col(j + 1))
    wg_next_map = lambda j, i: (i, jnp.clip(j + 1 - nq, 0, ng - 1))

    def qkv_out_map(j, i):
        jj = jnp.minimum(j, nq - 1)
        return (jj // 3, jnp.where(j < nq, i, n_i - 1), jj % 3)

    gate_out_map = lambda j, i: (jnp.where(j >= nq, i, 0), jnp.maximum(j - nq, 0))

    def side_rows(w):
        n_blocks = w.shape[0] // SIDE_CAST_ROWS
        return pl.BlockSpec((SIDE_CAST_ROWS, w.shape[1]),
                            lambda j, i: (jnp.minimum(j * n_i + i, n_blocks - 1), 0))

    side = [w_out, w_proj_a, w_proj_b]
    assert all(w.shape[0] // SIDE_CAST_ROWS <= (nq + ng) * n_i for w in side)
    return pl.pallas_call(
        functools.partial(_proj_kernel, n_qkv_steps=nq),
        grid=(nq + ng, n_i),
        in_specs=[
            pl.BlockSpec((None, tm, D_MODEL), lhs_map),
            pl.BlockSpec((D_MODEL, tn), lambda j, i: (0, 0), pipeline_mode=pl.Buffered(1)),
            pl.BlockSpec((D_MODEL // n_i, tn), wq_next_map),
            pl.BlockSpec((D_MODEL // n_i, tn), wg_next_map),
            pl.BlockSpec((1, tn), gate_col),
        ] + [side_rows(w) for w in side],
        out_specs=[
            pl.BlockSpec((None, tm, tn), qkv_out_map),
            pl.BlockSpec((tm, tn), gate_out_map),
        ] + [side_rows(w) for w in side],
        out_shape=[
            jax.ShapeDtypeStruct((n_groups, m, 3 * GROUP_WIDTH), BF16),
            jax.ShapeDtypeStruct((m, 2 * D_MODEL), BF16),
        ] + [jax.ShapeDtypeStruct(w.shape, BF16) for w in side],
        scratch_shapes=[pltpu.VMEM((2, D_MODEL, tn), BF16)],
        compiler_params=_params(("arbitrary", "arbitrary")),
        name="proj_qkv_gate",
    )(h_orders, w_qkv, w_qkv, w_gate, b_gate, *side)


def _dilated_block_shape(seq, half_window):
    kw = DIL_QB + 2 * half_window
    return (seq, seq) if seq <= kw else (DIL_QB, kw)


def _dilated_kernel(q_ref, k_ref, v_ref, o_ref, lse_ref, bias_ref, *, seq, half_window, coefs):
    qb, kw = _dilated_block_shape(seq, half_window)
    n_blocks = seq // qb
    n_tiles, n_res = q_ref.shape[:2]

    def rows_loader(ref):
        if n_res == 1:
            flat = ref.at[:, 0].reshape(seq, GROUP_WIDTH)
            return lambda rr, start, size, cols: flat[pl.ds(start, size), cols]
        assert n_blocks == 1
        return lambda rr, start, size, cols: jnp.concatenate(
            [ref[t, rr, :, cols] for t in range(n_tiles)], axis=0)

    load_q, load_k, load_v = rows_loader(q_ref), rows_loader(k_ref), rows_loader(v_ref)
    lane = lax.broadcasted_iota(jnp.int32, (qb, HEAD_DIM), 1)
    heads = range(HEADS_PER_GROUP)
    head_cols = [slice(h * HEAD_DIM, (h + 1) * HEAD_DIM) for h in heads]

    @pl.when((pl.program_id(0) == 0) & (pl.program_id(1) == 0))
    def _():
        qrow = lax.broadcasted_iota(jnp.int32, (qb, kw), 0)
        kcol = lax.broadcasted_iota(jnp.int32, (qb, kw), 1)
        for c in range(bias_ref.shape[0]):
            dist = jnp.abs(kcol - qrow - c * half_window)
            dist_f = dist.astype(F32)
            for h in heads:
                bias_ref[c, h] = jnp.where(dist <= half_window, (-coefs[h] * LOG2E) * dist_f, NEG)

    def block(qi, carry):
        q0 = pl.multiple_of(qi * qb, qb)
        ks = pl.multiple_of(jnp.clip(q0 - half_window, 0, seq - kw), half_window)
        window_case = (q0 - ks) // half_window
        pairs = [(rr, h) for rr in range(n_res) for h in heads]
        scores = [lax.dot_general(load_q(rr, q0, qb, head_cols[h]), load_k(rr, ks, kw, head_cols[h]),
                                  (((1,), (1,)), ((), ())), preferred_element_type=F32)
                  for rr, h in pairs]
        probs, dens = [], []
        lse_all = [jnp.zeros((qb, HEAD_DIM), F32) for _ in range(n_res)]
        for (rr, h), s in zip(pairs, scores):
            t = s * (SCALE * LOG2E) + bias_ref[window_case, h]
            m = jnp.max(t, axis=-1, keepdims=True)
            p = jnp.exp2(t - m)
            den = jnp.sum(p, axis=-1, keepdims=True)
            probs.append(p.astype(BF16))
            dens.append(den)
            lse_all[rr] = jnp.where(lane == h, m * LN2 + jnp.log(den), lse_all[rr])
        for (rr, h), p, den in zip(pairs, probs, dens):
            o = jnp.dot(p, load_v(rr, ks, kw, head_cols[h]), preferred_element_type=F32) / den
            o_ref[rr, pl.ds(q0, qb), head_cols[h]] = o.astype(BF16)
        for rr in range(n_res):
            lse_ref[rr, pl.ds(q0, qb), :] = lse_all[rr]
        return carry

    lax.fori_loop(0, n_blocks, block, 0, unroll=min(DIL_UNROLL, n_blocks))


def _dilated_group(qkv, batch, seq_total, group, window, dilation, slopes):
    d = dilation
    seq = seq_total // d
    half_window = window // (2 * d)
    qb, kw = _dilated_block_shape(seq, half_window)
    coefs = tuple(float(slopes[group * HEADS_PER_GROUP + h]) * d for h in range(HEADS_PER_GROUP))
    tiles = seq_total // PERM_TILE
    rows = PERM_TILE // d
    view = qkv.reshape(qkv.shape[0], batch, tiles, d, rows, qkv.shape[-1])
    n_res = min(d, max(1, DIL_ROWS_PER_STEP // seq))
    part = lambda which: pl.BlockSpec((None, None, tiles, n_res, rows, GROUP_WIDTH),
                                      lambda b, r: (group, b, 0, r, 0, which))
    return pl.pallas_call(
        functools.partial(_dilated_kernel, seq=seq, half_window=half_window, coefs=coefs),
        grid=(batch, d // n_res),
        in_specs=[part(0), part(1), part(2)],
        out_specs=[
            pl.BlockSpec((None, n_res, seq, GROUP_WIDTH), lambda b, r: (b, r, 0, 0)),
            pl.BlockSpec((None, n_res, seq, HEAD_DIM), lambda b, r: (b, r, 0, 0)),
        ],
        out_shape=[
            jax.ShapeDtypeStruct((batch, d, seq, GROUP_WIDTH), BF16),
            jax.ShapeDtypeStruct((batch, d, seq, HEAD_DIM), F32),
        ],
        scratch_shapes=[pltpu.VMEM((N_WINDOW_CASES, HEADS_PER_GROUP, qb, kw), F32)],
        compiler_params=_params(("arbitrary", "arbitrary")),
        name=f"dilated_attention_d{d}",
    )(view, view, view)


N_BIAS_VARIANTS = NA_ROWS


def _expand_na_bias(rpb_ref, out_ref, h, kh):
    n_off = 2 * NA_ROWS - 1
    width = (n_off + 1) * GRID_W
    qc = lax.broadcasted_iota(jnp.int32, (GRID_W, width), 0)
    kc = lax.broadcasted_iota(jnp.int32, (GRID_W, width), 1) % GRID_W
    col_idx = jnp.clip(kc - qc, -(NA_COLS - 1), NA_COLS - 1) + (NA_COLS - 1)
    off_row = lax.broadcasted_iota(jnp.int32, (1, width), 1) // GRID_W
    table = jnp.zeros((GRID_W, width), F32)
    for j in range(2 * NA_COLS - 1):
        row_vals = jnp.zeros((1, width), F32)
        for a in range(n_off):
            row_vals = jnp.where(off_row == a, rpb_ref[h, a, j], row_vals)
        table = jnp.where(col_idx == j, row_vals, table)
    cs = jnp.clip(qc - NA_COLS // 2, 0, GRID_W - NA_COLS)
    table = jnp.where((kc >= cs) & (kc < cs + NA_COLS), table * LOG2E, NEG)
    for var in range(N_BIAS_VARIANTS):
        out_ref[var, h] = table[:, var * GRID_W:(var + kh) * GRID_W]


def _na_kernel(rpb_ref, q_ref, k_ref, v_ref, o_ref, bias_ref, *, rows, kh):
    rb = pl.program_id(1)
    nkeys = kh * GRID_W

    @pl.when((pl.program_id(0) == 0) & (rb == 0))
    def _():
        for h in range(N_HEADS_B):
            _expand_na_bias(rpb_ref, bias_ref, h, kh)

    def one_row(rl, carry):
        r = rb * NA_ROWS_PER_STEP + rl
        rs = jnp.clip(r - kh // 2, 0, rows - kh)
        var = rs - r + (NA_ROWS - 1)
        q0 = pl.multiple_of(rl * GRID_W, GRID_W)
        k0 = pl.multiple_of(rs * GRID_W, GRID_W)
        heads = range(N_HEADS_B)
        head_cols = [slice(h * HEAD_DIM, (h + 1) * HEAD_DIM) for h in heads]
        scores = [lax.dot_general(q_ref[pl.ds(q0, GRID_W), c], k_ref[pl.ds(k0, nkeys), c],
                                  (((1,), (1,)), ((), ())), preferred_element_type=F32)
                  for c in head_cols]
        probs, dens = [], []
        for h in heads:
            t = scores[h] * (SCALE * LOG2E) + bias_ref[var, h]
            m = jnp.max(t, axis=-1, keepdims=True)
            p = jnp.exp2(t - m)
            dens.append(jnp.sum(p, axis=-1, keepdims=True))
            probs.append(p.astype(BF16))
        for h in heads:
            o = jnp.dot(probs[h], v_ref[pl.ds(k0, nkeys), head_cols[h]],
                        preferred_element_type=F32) / dens[h]
            o_ref[pl.ds(q0, GRID_W), head_cols[h]] = o.astype(BF16)
        return carry

    lax.fori_loop(0, NA_ROWS_PER_STEP, one_row, 0, unroll=NA_UNROLL)


def _neighbourhood(qkv, rpb, batch, seq_total):
    group = N_GROUPS_A
    assert GROUP_DILATIONS[group] == 1
    rows = seq_total // GRID_W
    kh = min(NA_ROWS, rows)
    assert kh == NA_ROWS and rows % NA_ROWS_PER_STEP == 0
    tq = NA_ROWS_PER_STEP * GRID_W
    view = qkv.reshape(qkv.shape[0], batch, seq_total, qkv.shape[-1])
    o = pl.pallas_call(
        functools.partial(_na_kernel, rows=rows, kh=kh),
        grid=(batch, rows // NA_ROWS_PER_STEP),
        in_specs=[
            pl.BlockSpec(memory_space=pltpu.SMEM),
            pl.BlockSpec((None, None, tq, GROUP_WIDTH), lambda b, i: (group, b, i, 0)),
            pl.BlockSpec((None, None, seq_total, GROUP_WIDTH), lambda b, i: (group, b, 0, 1)),
            pl.BlockSpec((None, None, seq_total, GROUP_WIDTH), lambda b, i: (group, b, 0, 2)),
        ],
        out_specs=pl.BlockSpec((None, tq, GROUP_WIDTH), lambda b, i: (b, i, 0)),
        out_shape=jax.ShapeDtypeStruct((batch, seq_total, GROUP_WIDTH), BF16),
        scratch_shapes=[pltpu.VMEM((N_BIAS_VARIANTS, N_HEADS_B, GRID_W, kh * GRID_W), F32)],
        compiler_params=_params(("arbitrary", "arbitrary")),
        name="neighbourhood_attention",
    )(rpb, view, view, view)
    return o.reshape(batch * seq_total, GROUP_WIDTH)


def _to_token_order_matrix(tm, d):
    t = lax.broadcasted_iota(jnp.int32, (tm, tm), 0)
    c = lax.broadcasted_iota(jnp.int32, (tm, tm), 1)
    return (c == (t % d) * (tm // d) + t // d).astype(BF16)


def _merge_kernel(o0_ref, o1_ref, o2_ref, l0_ref, l1_ref, l2_ref, ob_ref, gate_ref, x_ref,
                  wpa_ref, wpb_ref, wout_ref, gn_ref, wup_ref, wdn_ref,
                  x2_ref, h2_ref, wup_bf_ref, wdn_bf_ref, lse_tok_ref, *, dilations):
    tm = x_ref.shape[0]
    o_tok, lse_tok = [], []
    for g, (o_ref, l_ref, d) in enumerate(zip((o0_ref, o1_ref, o2_ref),
                                              (l0_ref, l1_ref, l2_ref), dilations)):
        o = o_ref[...].reshape(tm, GROUP_WIDTH)
        if d == 1:
            o_tok.append(o.astype(F32))
            lse_tok.append(l_ref[...].reshape(tm, HEAD_DIM))
        else:
            o_tok.append(jnp.dot(_to_token_order_matrix(tm, d), o, preferred_element_type=F32))
            for r in range(d):
                lse_tok_ref[g, pl.ds(r, tm // d, stride=d), :] = l_ref[r]
            lse_tok.append(lse_tok_ref[g])
    l0, l1, l2 = lse_tok
    mx = jnp.maximum(jnp.maximum(l0, l1), l2)
    e0, e1, e2 = jnp.exp(l0 - mx), jnp.exp(l1 - mx), jnp.exp(l2 - mx)
    tot = e0 + e1 + e2
    a0, a1, a2 = e0 / tot, e1 / tot, e2 / tot
    parts = []
    for h in range(HEADS_PER_GROUP):
        cols = slice(h * HEAD_DIM, (h + 1) * HEAD_DIM)
        y = (a0[:, h:h + 1] * o_tok[0][:, cols] + a1[:, h:h + 1] * o_tok[1][:, cols]
             + a2[:, h:h + 1] * o_tok[2][:, cols])
        parts.append(y.astype(BF16))
    ya = jnp.concatenate(parts, axis=1)
    ta = jnp.dot(ya, wpa_ref[...], preferred_element_type=F32)
    tb = jnp.dot(ob_ref[...], wpb_ref[...], preferred_element_type=F32)
    merged = (gate_ref[:, :D_MODEL].astype(F32) * ta + gate_ref[:, D_MODEL:].astype(F32) * tb)
    wup_bf_ref[...] = wup_ref[...].astype(BF16)
    x2 = x_ref[...] + jnp.dot(merged.astype(BF16), wout_ref[...], preferred_element_type=F32)
    x2_ref[...] = x2
    wdn_bf_ref[...] = wdn_ref[...].astype(BF16)
    h2_ref[...] = _rms(x2, gn_ref[...]).astype(BF16)


def _merge(o_groups, lse_groups, ob, gates, x2d, seq, wpa, wpb, wout, gn, w_up, w_down):
    m = x2d.shape[0]
    tm = MERGE_TM
    n_steps = m // tm
    tiles_per_batch = seq // tm
    dilations = tuple(o.shape[1] for o in o_groups)
    row = lambda w: pl.BlockSpec((tm, w), lambda i: (i, 0))
    wup_rows = pl.BlockSpec((w_up.shape[0] // n_steps, w_up.shape[1]), lambda i: (i, 0))
    wdn_rows = pl.BlockSpec((w_down.shape[0] // n_steps, w_down.shape[1]), lambda i: (i, 0))
    grouped = lambda d, w: pl.BlockSpec(
        (None, d, tm // d, w), lambda i: (i // tiles_per_batch, 0, i % tiles_per_batch, 0))
    const = lambda a, b: pl.BlockSpec((a, b), lambda i: (0, 0), pipeline_mode=pl.Buffered(1))
    return pl.pallas_call(
        functools.partial(_merge_kernel, dilations=dilations),
        grid=(n_steps,),
        in_specs=[grouped(d, GROUP_WIDTH) for d in dilations]
        + [grouped(d, HEAD_DIM) for d in dilations]
        + [row(GROUP_WIDTH), row(2 * D_MODEL), row(D_MODEL),
           const(GROUP_WIDTH, D_MODEL), const(GROUP_WIDTH, D_MODEL), const(D_MODEL, D_MODEL),
           const(1, D_MODEL), wup_rows, wdn_rows],
        out_specs=[row(D_MODEL), row(D_MODEL), wup_rows, wdn_rows],
        out_shape=[jax.ShapeDtypeStruct((m, D_MODEL), F32), jax.ShapeDtypeStruct((m, D_MODEL), BF16),
                   jax.ShapeDtypeStruct(w_up.shape, BF16), jax.ShapeDtypeStruct(w_down.shape, BF16)],
        scratch_shapes=[pltpu.VMEM((len(dilations), tm, HEAD_DIM), F32)],
        compiler_params=_params(("arbitrary",)),
        name="merge_out_proj",
    )(*o_groups, *lse_groups, ob, gates, x2d, wpa, wpb, wout, gn, w_up, w_down)


def _mlp_kernel(h2_ref, wup_ref, wdn_ref, x2_hbm, gf_ref, out_ref, x2_sem, *, n_f):
    i = pl.program_id(0)
    f = pl.program_id(1)
    tm = out_ref.shape[0]

    def residual_copy():
        return pltpu.make_async_copy(x2_hbm.at[pl.ds(i * tm, tm), :], out_ref, x2_sem)

    @pl.when(f == 0)
    def _():
        residual_copy().start()

    hid = jnp.dot(h2_ref[...], wup_ref[...], preferred_element_type=F32)
    hid = jnp.square(jnp.maximum(hid, 0.0)).astype(BF16)

    @pl.when(f == 0)
    def _():
        residual_copy().wait()

    out_ref[...] += jnp.dot(hid, wdn_ref[...], preferred_element_type=F32)

    @pl.when(f == n_f - 1)
    def _():
        out_ref[...] = _rms(out_ref[...], gf_ref[...])


def _mlp(h2, wup, wdn, x2, gf):
    m = h2.shape[0]
    tm, tf = MLP_TM, MLP_TF
    n_f = D_FF // tf
    return pl.pallas_call(
        functools.partial(_mlp_kernel, n_f=n_f),
        grid=(m // tm, n_f),
        in_specs=[
            pl.BlockSpec((tm, D_MODEL), lambda i, f: (i, 0)),
            pl.BlockSpec((D_MODEL, tf), lambda i, f: (0, f)),
            pl.BlockSpec((tf, D_MODEL), lambda i, f: (f, 0)),
            pl.BlockSpec(memory_space=pl.ANY),
            pl.BlockSpec((1, D_MODEL), lambda i, f: (0, 0)),
        ],
        out_specs=pl.BlockSpec((tm, D_MODEL), lambda i, f: (i, 0)),
        out_shape=jax.ShapeDtypeStruct((m, D_MODEL), F32),
        scratch_shapes=[pltpu.SemaphoreType.DMA],
        compiler_params=_params(("arbitrary", "arbitrary")),
        name="mlp_residual_norm",
    )(h2, wup, wdn, x2, gf)


def _layer(x2d, batch, seq, norm_mix, w_qkv, w_gate, b_gate, rpb, w_proj_a, w_proj_b, w_out,
           norm_mlp, w_up, w_down):
    slopes = 2.0 ** (-8.0 * np.arange(1, N_HEADS_A + 1) / N_HEADS_A)
    row = lambda v: v.reshape(1, -1)
    h_orders = _norm(x2d, row(norm_mix))
    qkv, gates, w_out_bf, w_pa_bf, w_pb_bf = _project(
        h_orders, w_qkv, w_gate, row(b_gate), w_out, w_proj_a, w_proj_b)
    o_groups, lse_groups = [], []
    for g, (window, d) in enumerate(DILATION_PATTERNS):
        o, lse = _dilated_group(qkv, batch, seq, g, window, d, slopes)
        o_groups.append(o)
        lse_groups.append(lse)
    ob = _neighbourhood(qkv, rpb, batch, seq)
    return _merge(o_groups, lse_groups, ob, gates, x2d, seq, w_pa_bf, w_pb_bf, w_out_bf,
                  row(norm_mlp), w_up, w_down)


def kernel(x, norm_mix, w_qkv, w_gate, b_gate, rpb, w_proj_a, w_proj_b, w_out, norm_mlp, w_up,
           w_down, norm_final):
    batch, seq, _ = x.shape
    depth = norm_mix.shape[0]
    assert depth == 1 and seq % PERM_TILE == 0
    x2d = x.reshape(batch * seq, D_MODEL)
    x2, h2, w_up_bf, w_down_bf = _layer(
        x2d, batch, seq, norm_mix[0], w_qkv[0], w_gate[0], b_gate[0], rpb[0],
        w_proj_a[0], w_proj_b[0], w_out[0], norm_mlp[0], w_up[0], w_down[0])
    out = _mlp(h2, w_up_bf, w_down_bf, x2, norm_final.reshape(1, -1))
    return out.reshape(batch, seq, D_MODEL)
```

```python
import functools

import jax
import jax.numpy as jnp
import numpy as np
from jax import lax
from jax.experimental import pallas as pl
from jax.experimental.pallas import tpu as pltpu

D_MODEL = 2048
HEAD_DIM = 128
N_HEADS = D_MODEL // HEAD_DIM
N_HEADS_B = N_HEADS // 4
N_HEADS_A = N_HEADS - N_HEADS_B
DILATION_PATTERNS = ((128, 1), (512, 4), (2048, 16))
N_GROUPS_A = len(DILATION_PATTERNS)
HEADS_PER_GROUP = N_HEADS_A // N_GROUPS_A
GROUP_WIDTH = HEADS_PER_GROUP * HEAD_DIM
N_HEAD_GROUPS = N_HEADS // HEADS_PER_GROUP
QKV_WIDTH = 3 * D_MODEL
GRID_W = 64
NA_ROWS = 8
NA_COLS = 16
D_FF = 4 * D_MODEL
EPS = 1e-6
NEG = -1e30
SCALE = HEAD_DIM ** -0.5
LOG2E = float(np.log2(np.e))
LN2 = float(np.log(2.0))
LANES = 128

F32 = jnp.float32
BF16 = jnp.bfloat16

VMEM_LIMIT_BYTES = 56 * 1024 * 1024

PERM_TILE = 1024
NORM_CHUNK = 512
PROJ_TM, PROJ_TN = 2048, GROUP_WIDTH
SIDE_CAST_ROWS = 32
MERGE_TM = 256
MERGE_ROW_CHUNKS = 2
MLP_TM, MLP_TF = 1024, 1024
DIL_QB = 128
DIL_UNROLL = 4
DIL_ROWS_PER_STEP = 1024
N_WINDOW_CASES = 3
NA_ROWS_PER_STEP = 16
NA_UNROLL = 4

GROUP_DILATIONS = tuple(d for _, d in DILATION_PATTERNS) + (1,)
ROW_ORDERS = tuple(sorted(set(GROUP_DILATIONS)))
GROUP_ROW_ORDER = tuple(ROW_ORDERS.index(d) for d in GROUP_DILATIONS)


def _params(sem):
    return pltpu.CompilerParams(dimension_semantics=sem, vmem_limit_bytes=VMEM_LIMIT_BYTES)


def _rms(x, g):
    ms = jnp.mean(x * x, axis=-1, keepdims=True)
    return (x * lax.rsqrt(ms + EPS)) * g


def _norm_kernel(x_ref, g_ref, h_ref, slab_ref):
    tm = x_ref.shape[0]
    x = x_ref[...]
    inv = lax.rsqrt(jnp.mean(x * x, axis=-1, keepdims=True) + EPS)
    n_slabs = NORM_CHUNK // LANES
    for c0 in range(0, D_MODEL, NORM_CHUNK):
        hc = (x_ref[:, c0:c0 + NORM_CHUNK] * inv) * g_ref[:, c0:c0 + NORM_CHUNK]
        h_ref[0, :, c0:c0 + NORM_CHUNK] = hc.astype(BF16)
        for s in range(n_slabs):
            slab_ref[0, s] = hc[:, s * LANES:(s + 1) * LANES]
        for v in range(1, len(ROW_ORDERS)):
            d_prev, d = ROW_ORDERS[v - 1], ROW_ORDERS[v]
            q = d // d_prev
            n_prev, n = tm // d_prev, tm // d
            last = v == len(ROW_ORDERS) - 1
            for s in range(n_slabs):
                cols = slice(c0 + s * LANES, c0 + (s + 1) * LANES)
                for r in range(d_prev):
                    for r2 in range(q):
                        rows = slab_ref[(v - 1) % 2, s, pl.ds(r * n_prev + r2, n, stride=q), :]
                        dst = (d_prev * r2 + r) * n
                        h_ref[v, dst:dst + n, cols] = rows.astype(BF16)
                        if not last:
                            slab_ref[v % 2, s, dst:dst + n, :] = rows


def _norm(x2d, g):
    m = x2d.shape[0]
    tm = PERM_TILE
    nv = len(ROW_ORDERS)
    assert ROW_ORDERS[0] == 1
    return pl.pallas_call(
        _norm_kernel,
        grid=(m // tm,),
        in_specs=[pl.BlockSpec((tm, D_MODEL), lambda i: (i, 0)),
                  pl.BlockSpec((1, D_MODEL), lambda i: (0, 0))],
        out_specs=pl.BlockSpec((nv, tm, D_MODEL), lambda i: (0, i, 0)),
        out_shape=jax.ShapeDtypeStruct((nv, m, D_MODEL), BF16),
        scratch_shapes=[pltpu.VMEM((2, NORM_CHUNK // LANES, tm, LANES), F32)],
        compiler_params=_params(("arbitrary",)),
        name="rmsnorm_row_orders",
    )(x2d, g)


def _proj_kernel(h_ref, w_first_ref, wq_next_ref, wg_next_ref, bg_ref, wout_ref, wpa_ref, wpb_ref,
                 qkv_ref, gate_ref, wout_bf_ref, wpa_bf_ref, wpb_bf_ref, w_bf_ref, *, n_qkv_steps):
    j = pl.program_id(0)
    i = pl.program_id(1)

    @pl.when((j == 0) & (i == 0))
    def _():
        w_bf_ref[0] = w_first_ref[...].astype(BF16)

    def side_casts(next_ref):
        part = next_ref.shape[0]
        rows = pl.ds(pl.multiple_of(i * part, part), part)
        w_bf_ref[(j + 1) % 2, rows, :] = next_ref[...].astype(BF16)
        wout_bf_ref[...] = wout_ref[...].astype(BF16)
        wpa_bf_ref[...] = wpa_ref[...].astype(BF16)
        wpb_bf_ref[...] = wpb_ref[...].astype(BF16)

    def qkv_step(next_ref):
        y = jnp.dot(h_ref[...], w_bf_ref[j % 2], preferred_element_type=F32)
        qkv_ref[...] = y.astype(BF16)
        side_casts(next_ref)

    @pl.when(j < n_qkv_steps - 1)
    def _():
        qkv_step(wq_next_ref)

    @pl.when(j == n_qkv_steps - 1)
    def _():
        qkv_step(wg_next_ref)

    @pl.when(j >= n_qkv_steps)
    def _():
        z = jnp.dot(h_ref[...], w_bf_ref[j % 2], preferred_element_type=F32) + bg_ref[...]
        gate_ref[...] = (0.5 * jnp.tanh(0.5 * z) + 0.5).astype(BF16)
        side_casts(wg_next_ref)


def _project(h_orders, w_qkv, w_gate, b_gate, w_out, w_proj_a, w_proj_b):
    m = h_orders.shape[1]
    tm, tn = PROJ_TM, PROJ_TN
    n_groups = N_HEAD_GROUPS
    nq = 3 * n_groups
    ng = (2 * D_MODEL) // tn
    n_i = m // tm
    row_order = GROUP_ROW_ORDER

    def lhs_map(j, i):
        v = jnp.int32(0)
        for grp in range(n_groups):
            v = jnp.where(j // 3 == grp, row_order[grp], v)
        return (v, i, 0)

    def w_qkv_col(j):
        jj = jnp.minimum(j, nq - 1)
        return (jj % 3) * n_groups + jj // 3

    gate_col = lambda j, i: (0, jnp.maximum(j - nq, 0))
    wq_next_map = lambda j, i: (i, w_qkv_col(j + 1))
    wg_next_map = lambda j, i: (i, jnp.clip(j + 1 - nq, 0, ng - 1))

    def qkv_out_map(j, i):
        jj = jnp.minimum(j, nq - 1)
        return (jj // 3, jnp.where(j < nq, i, n_i - 1), jj % 3)

    gate_out_map = lambda j, i: (jnp.where(j >= nq, i, 0), jnp.maximum(j - nq, 0))

    def side_rows(w):
        n_blocks = w.shape[0] // SIDE_CAST_ROWS
        return pl.BlockSpec((SIDE_CAST_ROWS, w.shape[1]),
                            lambda j, i: (jnp.minimum(j * n_i + i, n_blocks - 1), 0))

    side = [w_out, w_proj_a, w_proj_b]
    assert all(w.shape[0] // SIDE_CAST_ROWS <= (nq + ng) * n_i for w in side)
    return pl.pallas_call(
        functools.partial(_proj_kernel, n_qkv_steps=nq),
        grid=(nq + ng, n_i),
        in_specs=[
            pl.BlockSpec((None, tm, D_MODEL), lhs_map),
            pl.BlockSpec((D_MODEL, tn), lambda j, i: (0, 0), pipeline_mode=pl.Buffered(1)),
            pl.BlockSpec((D_MODEL // n_i, tn), wq_next_map),
            pl.BlockSpec((D_MODEL // n_i, tn), wg_next_map),
            pl.BlockSpec((1, tn), gate_col),
        ] + [side_rows(w) for w in side],
        out_specs=[
            pl.BlockSpec((None, tm, tn), qkv_out_map),
            pl.BlockSpec((tm, tn), gate_out_map),
        ] + [side_rows(w) for w in side],
        out_shape=[
            jax.ShapeDtypeStruct((n_groups, m, 3 * GROUP_WIDTH), BF16),
            jax.ShapeDtypeStruct((m, 2 * D_MODEL), BF16),
        ] + [jax.ShapeDtypeStruct(w.shape, BF16) for w in side],
        scratch_shapes=[pltpu.VMEM((2, D_MODEL, tn), BF16)],
        compiler_params=_params(("arbitrary", "arbitrary")),
        name="proj_qkv_gate",
    )(h_orders, w_qkv, w_qkv, w_gate, b_gate, *side)


def _dilated_block_shape(seq, half_window):
    kw = DIL_QB + 2 * half_window
    return (seq, seq) if seq <= kw else (DIL_QB, kw)


def _dilated_kernel(q_ref, k_ref, v_ref, o_ref, lse_ref, bias_ref, *, seq, half_window, coefs):
    qb, kw = _dilated_block_shape(seq, half_window)
    n_blocks = seq // qb
    n_tiles, n_res = q_ref.shape[:2]

    def rows_loader(ref):
        if n_res == 1:
            flat = ref.at[:, 0].reshape(seq, GROUP_WIDTH)
            return lambda rr, start, size, cols: flat[pl.ds(start, size), cols]
        assert n_blocks == 1
        return lambda rr, start, size, cols: jnp.concatenate(
            [ref[t, rr, :, cols] for t in range(n_tiles)], axis=0)

    load_q, load_k, load_v = rows_loader(q_ref), rows_loader(k_ref), rows_loader(v_ref)
    lane = lax.broadcasted_iota(jnp.int32, (qb, HEAD_DIM), 1)
    heads = range(HEADS_PER_GROUP)
    head_cols = [slice(h * HEAD_DIM, (h + 1) * HEAD_DIM) for h in heads]

    @pl.when((pl.program_id(0) == 0) & (pl.program_id(1) == 0))
    def _():
        qrow = lax.broadcasted_iota(jnp.int32, (qb, kw), 0)
        kcol = lax.broadcasted_iota(jnp.int32, (qb, kw), 1)
        for c in range(bias_ref.shape[0]):
            dist = jnp.abs(kcol - qrow - c * half_window)
            dist_f = dist.astype(F32)
            for h in heads:
                bias_ref[c, h] = jnp.where(dist <= half_window, (-coefs[h] * LOG2E) * dist_f, NEG)

    def block(qi, carry):
        q0 = pl.multiple_of(qi * qb, qb)
        ks = pl.multiple_of(jnp.clip(q0 - half_window, 0, seq - kw), half_window)
        window_case = (q0 - ks) // half_window
        pairs = [(rr, h) for rr in range(n_res) for h in heads]
        scores = [lax.dot_general(load_q(rr, q0, qb, head_cols[h]), load_k(rr, ks, kw, head_cols[h]),
                                  (((1,), (1,)), ((), ())), preferred_element_type=F32)
                  for rr, h in pairs]
        probs, dens = [], []
        lse_all = [jnp.zeros((qb, HEAD_DIM), F32) for _ in range(n_res)]
        for (rr, h), s in zip(pairs, scores):
            t = s * (SCALE * LOG2E) + bias_ref[window_case, h]
            m = jnp.max(t, axis=-1, keepdims=True)
            p = jnp.exp2(t - m)
            den = jnp.sum(p, axis=-1, keepdims=True)
            probs.append(p.astype(BF16))
            dens.append(den)
            lse_all[rr] = jnp.where(lane == h, m * LN2 + jnp.log(den), lse_all[rr])
        for (rr, h), p, den in zip(pairs, probs, dens):
            o = jnp.dot(p, load_v(rr, ks, kw, head_cols[h]), preferred_element_type=F32) / den
            o_ref[rr, pl.ds(q0, qb), head_cols[h]] = o.astype(BF16)
        for rr in range(n_res):
            lse_ref[rr, pl.ds(q0, qb), :] = lse_all[rr]
        return carry

    lax.fori_loop(0, n_blocks, block, 0, unroll=min(DIL_UNROLL, n_blocks))


def _dilated_group(qkv, batch, seq_total, group, window, dilation, slopes):
    d = dilation
    seq = seq_total // d
    half_window = window // (2 * d)
    qb, kw = _dilated_block_shape(seq, half_window)
    coefs = tuple(float(slopes[group * HEADS_PER_GROUP + h]) * d for h in range(HEADS_PER_GROUP))
    tiles = seq_total // PERM_TILE
    rows = PERM_TILE // d
    view = qkv.reshape(qkv.shape[0], batch, tiles, d, rows, qkv.shape[-1])
    n_res = min(d, max(1, DIL_ROWS_PER_STEP // seq))
    part = lambda which: pl.BlockSpec((None, None, tiles, n_res, rows, GROUP_WIDTH),
                                      lambda b, r: (group, b, 0, r, 0, which))
    return pl.pallas_call(
        functools.partial(_dilated_kernel, seq=seq, half_window=half_window, coefs=coefs),
        grid=(batch, d // n_res),
        in_specs=[part(0), part(1), part(2)],
        out_specs=[
            pl.BlockSpec((None, n_res, seq, GROUP_WIDTH), lambda b, r: (b, r, 0, 0)),
            pl.BlockSpec((None, n_res, seq, HEAD_DIM), lambda b, r: (b, r, 0, 0)),
        ],
        out_shape=[
            jax.ShapeDtypeStruct((batch, d, seq, GROUP_WIDTH), BF16),
            jax.ShapeDtypeStruct((batch, d, seq, HEAD_DIM), F32),
        ],
        scratch_shapes=[pltpu.VMEM((N_WINDOW_CASES, HEADS_PER_GROUP, qb, kw), F32)],
        compiler_params=_params(("arbitrary", "arbitrary")),
        name=f"dilated_attention_d{d}",
    )(view, view, view)


N_BIAS_VARIANTS = NA_ROWS


def _expand_na_bias(rpb_ref, out_ref, h, kh):
    n_off = 2 * NA_ROWS - 1
    width = (n_off + 1) * GRID_W
    qc = lax.broadcasted_iota(jnp.int32, (GRID_W, width), 0)
    kc = lax.broadcasted_iota(jnp.int32, (GRID_W, width), 1) % GRID_W
    col_idx = jnp.clip(kc - qc, -(NA_COLS - 1), NA_COLS - 1) + (NA_COLS - 1)
    off_row = lax.broadcasted_iota(jnp.int32, (1, width), 1) // GRID_W
    table = jnp.zeros((GRID_W, width), F32)
    for j in range(2 * NA_COLS - 1):
        row_vals = jnp.zeros((1, width), F32)
        for a in range(n_off):
            row_vals = jnp.where(off_row == a, rpb_ref[h, a, j], row_vals)
        table = jnp.where(col_idx == j, row_vals, table)
    cs = jnp.clip(qc - NA_COLS // 2, 0, GRID_W - NA_COLS)
    table = jnp.where((kc >= cs) & (kc < cs + NA_COLS), table * LOG2E, NEG)
    for var in range(N_BIAS_VARIANTS):
        out_ref[var, h] = table[:, var * GRID_W:(var + kh) * GRID_W]


def _na_kernel(rpb_ref, q_ref, k_ref, v_ref, o_ref, bias_ref, *, rows, kh):
    rb = pl.program_id(1)
    nkeys = kh * GRID_W

    @pl.when((pl.program_id(0) == 0) & (rb == 0))
    def _():
        for h in range(N_HEADS_B):
            _expand_na_bias(rpb_ref, bias_ref, h, kh)

    def one_row(rl, carry):
        r = rb * NA_ROWS_PER_STEP + rl
        rs = jnp.clip(r - kh // 2, 0, rows - kh)
        var = rs - r + (NA_ROWS - 1)
        q0 = pl.multiple_of(rl * GRID_W, GRID_W)
        k0 = pl.multiple_of(rs * GRID_W, GRID_W)
        heads = range(N_HEADS_B)
        head_cols = [slice(h * HEAD_DIM, (h + 1) * HEAD_DIM) for h in heads]
        scores = [lax.dot_general(q_ref[pl.ds(q0, GRID_W), c], k_ref[pl.ds(k0, nkeys), c],
                                  (((1,), (1,)), ((), ())), preferred_element_type=F32)
                  for c in head_cols]
        probs, dens = [], []
        for h in heads:
            t = scores[h] * (SCALE * LOG2E) + bias_ref[var, h]
            m = jnp.max(t, axis=-1, keepdims=True)
            p = jnp.exp2(t - m)
            dens.append(jnp.sum(p, axis=-1, keepdims=True))
            probs.append(p.astype(BF16))
        for h in heads:
            o = jnp.dot(probs[h], v_ref[pl.ds(k0, nkeys), head_cols[h]],
                        preferred_element_type=F32) / dens[h]
            o_ref[pl.ds(q0, GRID_W), head_cols[h]] = o.astype(BF16)
        return carry

    lax.fori_loop(0, NA_ROWS_PER_STEP, one_row, 0, unroll=NA_UNROLL)


def _neighbourhood(qkv, rpb, batch, seq_total):
    group = N_GROUPS_A
    assert GROUP_DILATIONS[group] == 1
    rows = seq_total // GRID_W
    kh = min(NA_ROWS, rows)
    assert kh == NA_ROWS and rows % NA_ROWS_PER_STEP == 0
    tq = NA_ROWS_PER_STEP * GRID_W
    view = qkv.reshape(qkv.shape[0], batch, seq_total, qkv.shape[-1])
    o = pl.pallas_call(
        functools.partial(_na_kernel, rows=rows, kh=kh),
        grid=(batch, rows // NA_ROWS_PER_STEP),
        in_specs=[
            pl.BlockSpec(memory_space=pltpu.SMEM),
            pl.BlockSpec((None, None, tq, GROUP_WIDTH), lambda b, i: (group, b, i, 0)),
            pl.BlockSpec((None, None, seq_total, GROUP_WIDTH), lambda b, i: (group, b, 0, 1)),
            pl.BlockSpec((None, None, seq_total, GROUP_WIDTH), lambda b, i: (group, b, 0, 2)),
        ],
        out_specs=pl.BlockSpec((None, tq, GROUP_WIDTH), lambda b, i: (b, i, 0)),
        out_shape=jax.ShapeDtypeStruct((batch, seq_total, GROUP_WIDTH), BF16),
        scratch_shapes=[pltpu.VMEM((N_BIAS_VARIANTS, N_HEADS_B, GRID_W, kh * GRID_W), F32)],
        compiler_params=_params(("arbitrary", "arbitrary")),
        name="neighbourhood_attention",
    )(rpb, view, view, view)
    return o.reshape(batch * seq_total, GROUP_WIDTH)


def _to_token_order_matrix(tm, d):
    t = lax.broadcasted_iota(jnp.int32, (tm, tm), 0)
    c = lax.broadcasted_iota(jnp.int32, (tm, tm), 1)
    return (c == (t % d) * (tm // d) + t // d).astype(BF16)


def _merge_kernel(o0_ref, o1_ref, o2_ref, l0_ref, l1_ref, l2_ref, ob_ref, gate_ref, x_ref,
                  wpa_ref, wpb_ref, wout_ref, gn_ref, wup_ref, wdn_ref,
                  x2_ref, h2_ref, wup_bf_ref, wdn_bf_ref, lse_tok_ref, *, dilations):
    tm = x_ref.shape[0]
    o_tok, lse_tok = [], []
    for g, (o_ref, l_ref, d) in enumerate(zip((o0_ref, o1_ref, o2_ref),
                                              (l0_ref, l1_ref, l2_ref), dilations)):
        o = o_ref[...].reshape(tm, GROUP_WIDTH)
        if d == 1:
            o_tok.append(o.astype(F32))
            lse_tok.append(l_ref[...].reshape(tm, HEAD_DIM))
        else:
            o_tok.append(jnp.dot(_to_token_order_matrix(tm, d), o, preferred_element_type=F32))
            for r in range(d):
                lse_tok_ref[g, pl.ds(r, tm // d, stride=d), :] = l_ref[r]
            lse_tok.append(lse_tok_ref[g])
    l0, l1, l2 = lse_tok
    mx = jnp.maximum(jnp.maximum(l0, l1), l2)
    e0, e1, e2 = jnp.exp(l0 - mx), jnp.exp(l1 - mx), jnp.exp(l2 - mx)
    tot = e0 + e1 + e2
    a0, a1, a2 = e0 / tot, e1 / tot, e2 / tot
    parts = []
    for h in range(HEADS_PER_GROUP):
        cols = slice(h * HEAD_DIM, (h + 1) * HEAD_DIM)
        y = (a0[:, h:h + 1] * o_tok[0][:, cols] + a1[:, h:h + 1] * o_tok[1][:, cols]
             + a2[:, h:h + 1] * o_tok[2][:, cols])
        parts.append(y.astype(BF16))
    ya = jnp.concatenate(parts, axis=1)
    half = tm // MERGE_ROW_CHUNKS
    x2s = []
    for r0 in range(0, tm, half):
        rows = slice(r0, r0 + half)
        ta = jnp.dot(ya[rows], wpa_ref[...], preferred_element_type=F32)
        tb = jnp.dot(ob_ref[rows, :], wpb_ref[...], preferred_element_type=F32)
        merged = (gate_ref[rows, :D_MODEL].astype(F32) * ta
                  + gate_ref[rows, D_MODEL:].astype(F32) * tb)
        x2 = x_ref[rows, :] + jnp.dot(merged.astype(BF16), wout_ref[...],
                                      preferred_element_type=F32)
        x2_ref[rows, :] = x2
        x2s.append(x2)
        if r0 == 0:
            wup_bf_ref[...] = wup_ref[...].astype(BF16)
        elif r0 == half:
            wdn_bf_ref[...] = wdn_ref[...].astype(BF16)
    for r0, x2 in zip(range(0, tm, half), x2s):
        h2_ref[r0:r0 + half, :] = _rms(x2, gn_ref[...]).astype(BF16)


def _merge(o_groups, lse_groups, ob, gates, x2d, seq, wpa, wpb, wout, gn, w_up, w_down):
    m = x2d.shape[0]
    tm = MERGE_TM
    n_steps = m // tm
    tiles_per_batch = seq // tm
    dilations = tuple(o.shape[1] for o in o_groups)
    row = lambda w: pl.BlockSpec((tm, w), lambda i: (i, 0))
    wup_rows = pl.BlockSpec((w_up.shape[0] // n_steps, w_up.shape[1]), lambda i: (i, 0))
    wdn_rows = pl.BlockSpec((w_down.shape[0] // n_steps, w_down.shape[1]), lambda i: (i, 0))
    grouped = lambda d, w: pl.BlockSpec(
        (None, d, tm // d, w), lambda i: (i // tiles_per_batch, 0, i % tiles_per_batch, 0))
    const = lambda a, b: pl.BlockSpec((a, b), lambda i: (0, 0), pipeline_mode=pl.Buffered(1))
    return pl.pallas_call(
        functools.partial(_merge_kernel, dilations=dilations),
        grid=(n_steps,),
        in_specs=[grouped(d, GROUP_WIDTH) for d in dilations]
        + [grouped(d, HEAD_DIM) for d in dilations]
        + [row(GROUP_WIDTH), row(2 * D_MODEL), row(D_MODEL),
           const(GROUP_WIDTH, D_MODEL), const(GROUP_WIDTH, D_MODEL), const(D_MODEL, D_MODEL),
           const(1, D_MODEL), wup_rows, wdn_rows],
        out_specs=[row(D_MODEL), row(D_MODEL), wup_rows, wdn_rows],
        out_shape=[jax.ShapeDtypeStruct((m, D_MODEL), F32), jax.ShapeDtypeStruct((m, D_MODEL), BF16),
                   jax.ShapeDtypeStruct(w_up.shape, BF16), jax.ShapeDtypeStruct(w_down.shape, BF16)],
        scratch_shapes=[pltpu.VMEM((len(dilations), tm, HEAD_DIM), F32)],
        compiler_params=_params(("arbitrary",)),
        name="merge_out_proj",
    )(*o_groups, *lse_groups, ob, gates, x2d, wpa, wpb, wout, gn, w_up, w_down)


def _mlp_kernel(h2_ref, wup_ref, wdn_ref, x2_hbm, gf_ref, out_ref, x2_sem, *, n_f):
    i = pl.program_id(0)
    f = pl.program_id(1)
    tm = out_ref.shape[0]

    def residual_copy():
        return pltpu.make_async_copy(x2_hbm.at[pl.ds(i * tm, tm), :], out_ref, x2_sem)

    @pl.when(f == 0)
    def _():
        residual_copy().start()

    hid = jnp.dot(h2_ref[...], wup_ref[...], preferred_element_type=F32)
    hid = jnp.square(jnp.maximum(hid, 0.0)).astype(BF16)

    @pl.when(f == 0)
    def _():
        residual_copy().wait()

    out_ref[...] += jnp.dot(hid, wdn_ref[...], preferred_element_type=F32)

    @pl.when(f == n_f - 1)
    def _():
        out_ref[...] = _rms(out_ref[...], gf_ref[...])


def _mlp(h2, wup, wdn, x2, gf):
    m = h2.shape[0]
    tm, tf = MLP_TM, MLP_TF
    n_f = D_FF // tf
    return pl.pallas_call(
        functools.partial(_mlp_kernel, n_f=n_f),
        grid=(m // tm, n_f),
        in_specs=[
            pl.BlockSpec((tm, D_MODEL), lambda i, f: (i, 0)),
            pl.BlockSpec((D_MODEL, tf), lambda i, f: (0, f)),
            pl.BlockSpec((tf, D_MODEL), lambda i, f: (f, 0)),
            pl.BlockSpec(memory_space=pl.ANY),
            pl.BlockSpec((1, D_MODEL), lambda i, f: (0, 0)),
        ],
        out_specs=pl.BlockSpec((tm, D_MODEL), lambda i, f: (i, 0)),
        out_shape=jax.ShapeDtypeStruct((m, D_MODEL), F32),
        scratch_shapes=[pltpu.SemaphoreType.DMA],
        compiler_params=_params(("arbitrary", "arbitrary")),
        name="mlp_residual_norm",
    )(h2, wup, wdn, x2, gf)


def _layer(x2d, batch, seq, norm_mix, w_qkv, w_gate, b_gate, rpb, w_proj_a, w_proj_b, w_out,
           norm_mlp, w_up, w_down):
    slopes = 2.0 ** (-8.0 * np.arange(1, N_HEADS_A + 1) / N_HEADS_A)
    row = lambda v: v.reshape(1, -1)
    h_orders = _norm(x2d, row(norm_mix))
    qkv, gates, w_out_bf, w_pa_bf, w_pb_bf = _project(
        h_orders, w_qkv, w_gate, row(b_gate), w_out, w_proj_a, w_proj_b)
    o_groups, lse_groups = [], []
    for g, (window, d) in enumerate(DILATION_PATTERNS):
        o, lse = _dilated_group(qkv, batch, seq, g, window, d, slopes)
        o_groups.append(o)
        lse_groups.append(lse)
    ob = _neighbourhood(qkv, rpb, batch, seq)
    return _merge(o_groups, lse_groups, ob, gates, x2d, seq, w_pa_bf, w_pb_bf, w_out_bf,
                  row(norm_mlp), w_up, w_down)


def kernel(x, norm_mix, w_qkv, w_gate, b_gate, rpb, w_proj_a, w_proj_b, w_out, norm_mlp, w_up,
           w_down, norm_final):
    batch, seq, _ = x.shape
    depth = norm_mix.shape[0]
    assert depth == 1 and seq % PERM_TILE == 0
    x2d = x.reshape(batch * seq, D_MODEL)
    x2, h2, w_up_bf, w_down_bf = _layer(
        x2d, batch, seq, norm_mix[0], w_qkv[0], w_gate[0], b_gate[0], rpb[0],
        w_proj_a[0], w_proj_b[0], w_out[0], norm_mlp[0], w_up[0], w_down[0])
    out = _mlp(h2, w_up_bf, w_down_bf, x2, norm_final.reshape(1, -1))
    return out.reshape(batch, seq, D_MODEL)
```

```python
import functools

import jax
import jax.numpy as jnp
import numpy as np
from jax import lax
from jax.experimental import pallas as pl
from jax.experimental.pallas import tpu as pltpu

D_MODEL = 2048
HEAD_DIM = 128
N_HEADS = D_MODEL // HEAD_DIM
N_HEADS_B = N_HEADS // 4
N_HEADS_A = N_HEADS - N_HEADS_B
DILATION_PATTERNS = ((128, 1), (512, 4), (2048, 16))
N_GROUPS_A = len(DILATION_PATTERNS)
HEADS_PER_GROUP = N_HEADS_A // N_GROUPS_A
GROUP_WIDTH = HEADS_PER_GROUP * HEAD_DIM
N_HEAD_GROUPS = N_HEADS // HEADS_PER_GROUP
QKV_WIDTH = 3 * D_MODEL
GRID_W = 64
NA_ROWS = 8
NA_COLS = 16
D_FF = 4 * D_MODEL
EPS = 1e-6
NEG = -1e30
SCALE = HEAD_DIM ** -0.5
LOG2E = float(np.log2(np.e))
LN2 = float(np.log(2.0))
LANES = 128

F32 = jnp.float32
BF16 = jnp.bfloat16

VMEM_LIMIT_BYTES = 56 * 1024 * 1024

PERM_TILE = 1024
NORM_CHUNK = 512
PROJ_TM, PROJ_TN = 2048, GROUP_WIDTH
SIDE_CAST_ROWS = 32
MERGE_TM = 256
MERGE_ROW_CHUNKS = 2
MLP_TM, MLP_TF = 1024, 1024
DIL_QB = 128
DIL_UNROLL = 4
DIL_ROWS_PER_STEP = 1024
N_WINDOW_CASES = 3
NA_ROWS_PER_STEP = 8
NA_UNROLL = 4

GROUP_DILATIONS = tuple(d for _, d in DILATION_PATTERNS) + (1,)
ROW_ORDERS = tuple(sorted(set(GROUP_DILATIONS)))
GROUP_ROW_ORDER = tuple(ROW_ORDERS.index(d) for d in GROUP_DILATIONS)


def _params(sem):
    return pltpu.CompilerParams(dimension_semantics=sem, vmem_limit_bytes=VMEM_LIMIT_BYTES)


def _rms(x, g):
    ms = jnp.mean(x * x, axis=-1, keepdims=True)
    return (x * lax.rsqrt(ms + EPS)) * g


def _norm_kernel(x_ref, g_ref, h_ref, slab_ref):
    tm = x_ref.shape[0]
    x = x_ref[...]
    inv = lax.rsqrt(jnp.mean(x * x, axis=-1, keepdims=True) + EPS)
    n_slabs = NORM_CHUNK // LANES
    for c0 in range(0, D_MODEL, NORM_CHUNK):
        hc = (x_ref[:, c0:c0 + NORM_CHUNK] * inv) * g_ref[:, c0:c0 + NORM_CHUNK]
        h_ref[0, :, c0:c0 + NORM_CHUNK] = hc.astype(BF16)
        for s in range(n_slabs):
            slab_ref[0, s] = hc[:, s * LANES:(s + 1) * LANES]
        for v in range(1, len(ROW_ORDERS)):
            d_prev, d = ROW_ORDERS[v - 1], ROW_ORDERS[v]
            q = d // d_prev
            n_prev, n = tm // d_prev, tm // d
            last = v == len(ROW_ORDERS) - 1
            for s in range(n_slabs):
                cols = slice(c0 + s * LANES, c0 + (s + 1) * LANES)
                for r in range(d_prev):
                    for r2 in range(q):
                        rows = slab_ref[(v - 1) % 2, s, pl.ds(r * n_prev + r2, n, stride=q), :]
                        dst = (d_prev * r2 + r) * n
                        h_ref[v, dst:dst + n, cols] = rows.astype(BF16)
                        if not last:
                            slab_ref[v % 2, s, dst:dst + n, :] = rows


def _norm(x2d, g):
    m = x2d.shape[0]
    tm = PERM_TILE
    nv = len(ROW_ORDERS)
    assert ROW_ORDERS[0] == 1
    return pl.pallas_call(
        _norm_kernel,
        grid=(m // tm,),
        in_specs=[pl.BlockSpec((tm, D_MODEL), lambda i: (i, 0)),
                  pl.BlockSpec((1, D_MODEL), lambda i: (0, 0))],
        out_specs=pl.BlockSpec((nv, tm, D_MODEL), lambda i: (0, i, 0)),
        out_shape=jax.ShapeDtypeStruct((nv, m, D_MODEL), BF16),
        scratch_shapes=[pltpu.VMEM((2, NORM_CHUNK // LANES, tm, LANES), F32)],
        compiler_params=_params(("arbitrary",)),
        name="rmsnorm_row_orders",
    )(x2d, g)


def _proj_kernel(h_ref, w_first_ref, wq_next_ref, wg_next_ref, bg_ref, wout_ref, wpa_ref, wpb_ref,
                 qkv_ref, gate_ref, wout_bf_ref, wpa_bf_ref, wpb_bf_ref, w_bf_ref, *, n_qkv_steps):
    j = pl.program_id(0)
    i = pl.program_id(1)

    @pl.when((j == 0) & (i == 0))
    def _():
        w_bf_ref[0] = w_first_ref[...].astype(BF16)

    def side_casts(next_ref):
        part = next_ref.shape[0]
        rows = pl.ds(pl.multiple_of(i * part, part), part)
        w_bf_ref[(j + 1) % 2, rows, :] = next_ref[...].astype(BF16)
        wout_bf_ref[...] = wout_ref[...].astype(BF16)
        wpa_bf_ref[...] = wpa_ref[...].astype(BF16)
        wpb_bf_ref[...] = wpb_ref[...].astype(BF16)

    def qkv_step(next_ref):
        y = jnp.dot(h_ref[...], w_bf_ref[j % 2], preferred_element_type=F32)
        qkv_ref[...] = y.astype(BF16)
        side_casts(next_ref)

    @pl.when(j < n_qkv_steps - 1)
    def _():
        qkv_step(wq_next_ref)

    @pl.when(j == n_qkv_steps - 1)
    def _():
        qkv_step(wg_next_ref)

    @pl.when(j >= n_qkv_steps)
    def _():
        z = jnp.dot(h_ref[...], w_bf_ref[j % 2], preferred_element_type=F32) + bg_ref[...]
        gate_ref[...] = (0.5 * jnp.tanh(0.5 * z) + 0.5).astype(BF16)
        side_casts(wg_next_ref)


def _project(h_orders, w_qkv, w_gate, b_gate, w_out, w_proj_a, w_proj_b):
    m = h_orders.shape[1]
    tm, tn = PROJ_TM, PROJ_TN
    n_groups = N_HEAD_GROUPS
    nq = 3 * n_groups
    ng = (2 * D_MODEL) // tn
    n_i = m // tm
    row_order = GROUP_ROW_ORDER

    def lhs_map(j, i):
        v = jnp.int32(0)
        for grp in range(n_groups):
            v = jnp.where(j // 3 == grp, row_order[grp], v)
        return (v, i, 0)

    def w_qkv_col(j):
        jj = jnp.minimum(j, nq - 1)
        return (jj % 3) * n_groups + jj // 3

    gate_col = lambda j, i: (0, jnp.maximum(j - nq, 0))
    wq_next_map = lambda j, i: (i, w_qkv_col(j + 1))
    wg_next_map = lambda j, i: (i, jnp.clip(j + 1 - nq, 0, ng - 1))

    def qkv_out_map(j, i):
        jj = jnp.minimum(j, nq - 1)
        return (jj // 3, jnp.where(j < nq, i, n_i - 1), jj % 3)

    gate_out_map = lambda j, i: (jnp.where(j >= nq, i, 0), jnp.maximum(j - nq, 0))

    def side_rows(w):
        n_blocks = w.shape[0] // SIDE_CAST_ROWS
        return pl.BlockSpec((SIDE_CAST_ROWS, w.shape[1]),
                            lambda j, i: (jnp.minimum(j * n_i + i, n_blocks - 1), 0))

    side = [w_out, w_proj_a, w_proj_b]
    assert all(w.shape[0] // SIDE_CAST_ROWS <= (nq + ng) * n_i for w in side)
    return pl.pallas_call(
        functools.partial(_proj_kernel, n_qkv_steps=nq),
        grid=(nq + ng, n_i),
        in_specs=[
            pl.BlockSpec((None, tm, D_MODEL), lhs_map),
            pl.BlockSpec((D_MODEL, tn), lambda j, i: (0, 0), pipeline_mode=pl.Buffered(1)),
            pl.BlockSpec((D_MODEL // n_i, tn), wq_next_map),
            pl.BlockSpec((D_MODEL // n_i, tn), wg_next_map),
            pl.BlockSpec((1, tn), gate_col),
        ] + [side_rows(w) for w in side],
        out_specs=[
            pl.BlockSpec((None, tm, tn), qkv_out_map),
            pl.BlockSpec((tm, tn), gate_out_map),
        ] + [side_rows(w) for w in side],
        out_shape=[
            jax.ShapeDtypeStruct((n_groups, m, 3 * GROUP_WIDTH), BF16),
            jax.ShapeDtypeStruct((m, 2 * D_MODEL), BF16),
        ] + [jax.ShapeDtypeStruct(w.shape, BF16) for w in side],
        scratch_shapes=[pltpu.VMEM((2, D_MODEL, tn), BF16)],
        compiler_params=_params(("arbitrary", "arbitrary")),
        name="proj_qkv_gate",
    )(h_orders, w_qkv, w_qkv, w_gate, b_gate, *side)


def _dilated_block_shape(seq, half_window):
    kw = DIL_QB + 2 * half_window
    return (seq, seq) if seq <= kw else (DIL_QB, kw)


def _dilated_kernel(q_ref, k_ref, v_ref, o_ref, lse_ref, bias_ref, *, seq, half_window, coefs):
    qb, kw = _dilated_block_shape(seq, half_window)
    n_blocks = seq // qb
    n_tiles, n_res = q_ref.shape[:2]

    def rows_loader(ref):
        if n_res == 1:
            flat = ref.at[:, 0].reshape(seq, GROUP_WIDTH)
            return lambda rr, start, size, cols: flat[pl.ds(start, size), cols]
        assert n_blocks == 1
        return lambda rr, start, size, cols: jnp.concatenate(
            [ref[t, rr, :, cols] for t in range(n_tiles)], axis=0)

    load_q, load_k, load_v = rows_loader(q_ref), rows_loader(k_ref), rows_loader(v_ref)
    lane = lax.broadcasted_iota(jnp.int32, (qb, HEAD_DIM), 1)
    heads = range(HEADS_PER_GROUP)
    head_cols = [slice(h * HEAD_DIM, (h + 1) * HEAD_DIM) for h in heads]

    @pl.when((pl.program_id(0) == 0) & (pl.program_id(1) == 0))
    def _():
        qrow = lax.broadcasted_iota(jnp.int32, (qb, kw), 0)
        kcol = lax.broadcasted_iota(jnp.int32, (qb, kw), 1)
        for c in range(bias_ref.shape[0]):
            dist = jnp.abs(kcol - qrow - c * half_window)
            dist_f = dist.astype(F32)
            for h in heads:
                bias_ref[c, h] = jnp.where(dist <= half_window, (-coefs[h] * LOG2E) * dist_f, NEG)

    def block(qi, carry):
        q0 = pl.multiple_of(qi * qb, qb)
        ks = pl.multiple_of(jnp.clip(q0 - half_window, 0, seq - kw), half_window)
        window_case = (q0 - ks) // half_window
        pairs = [(rr, h) for rr in range(n_res) for h in heads]
        scores = [lax.dot_general(load_q(rr, q0, qb, head_cols[h]), load_k(rr, ks, kw, head_cols[h]),
                                  (((1,), (1,)), ((), ())), preferred_element_type=F32)
                  for rr, h in pairs]
        probs, dens = [], []
        lse_all = [jnp.zeros((qb, HEAD_DIM), F32) for _ in range(n_res)]
        for (rr, h), s in zip(pairs, scores):
            t = s * (SCALE * LOG2E) + bias_ref[window_case, h]
            m = jnp.max(t, axis=-1, keepdims=True)
            p = jnp.exp2(t - m)
            den = jnp.sum(p, axis=-1, keepdims=True)
            probs.append(p.astype(BF16))
            dens.append(den)
            lse_all[rr] = jnp.where(lane == h, m * LN2 + jnp.log(den), lse_all[rr])
        for (rr, h), p, den in zip(pairs, probs, dens):
            o = jnp.dot(p, load_v(rr, ks, kw, head_cols[h]), preferred_element_type=F32) / den
            o_ref[rr, pl.ds(q0, qb), head_cols[h]] = o.astype(BF16)
        for rr in range(n_res):
            lse_ref[rr, pl.ds(q0, qb), :] = lse_all[rr]
        return carry

    lax.fori_loop(0, n_blocks, block, 0, unroll=min(DIL_UNROLL, n_blocks))


def _dilated_group(qkv, batch, seq_total, group, window, dilation, slopes):
    d = dilation
    seq = seq_total // d
    half_window = window // (2 * d)
    qb, kw = _dilated_block_shape(seq, half_window)
    coefs = tuple(float(slopes[group * HEADS_PER_GROUP + h]) * d for h in range(HEADS_PER_GROUP))
    tiles = seq_total // PERM_TILE
    rows = PERM_TILE // d
    view = qkv.reshape(qkv.shape[0], batch, tiles, d, rows, qkv.shape[-1])
    n_res = min(d, max(1, DIL_ROWS_PER_STEP // seq))
    part = lambda which: pl.BlockSpec((None, None, tiles, n_res, rows, GROUP_WIDTH),
                                      lambda b, r: (group, b, 0, r, 0, which))
    return pl.pallas_call(
        functools.partial(_dilated_kernel, seq=seq, half_window=half_window, coefs=coefs),
        grid=(batch, d // n_res),
        in_specs=[part(0), part(1), part(2)],
        out_specs=[
            pl.BlockSpec((None, n_res, seq, GROUP_WIDTH), lambda b, r: (b, r, 0, 0)),
            pl.BlockSpec((None, n_res, seq, HEAD_DIM), lambda b, r: (b, r, 0, 0)),
        ],
        out_shape=[
            jax.ShapeDtypeStruct((batch, d, seq, GROUP_WIDTH), BF16),
            jax.ShapeDtypeStruct((batch, d, seq, HEAD_DIM), F32),
        ],
        scratch_shapes=[pltpu.VMEM((N_WINDOW_CASES, HEADS_PER_GROUP, qb, kw), F32)],
        compiler_params=_params(("arbitrary", "arbitrary")),
        name=f"dilated_attention_d{d}",
    )(view, view, view)


N_BIAS_VARIANTS = NA_ROWS


def _expand_na_bias(rpb_ref, out_ref, h, kh):
    n_off = 2 * NA_ROWS - 1
    width = (n_off + 1) * GRID_W
    qc = lax.broadcasted_iota(jnp.int32, (GRID_W, width), 0)
    kc = lax.broadcasted_iota(jnp.int32, (GRID_W, width), 1) % GRID_W
    col_idx = jnp.clip(kc - qc, -(NA_COLS - 1), NA_COLS - 1) + (NA_COLS - 1)
    off_row = lax.broadcasted_iota(jnp.int32, (1, width), 1) // GRID_W
    table = jnp.zeros((GRID_W, width), F32)
    for j in range(2 * NA_COLS - 1):
        row_vals = jnp.zeros((1, width), F32)
        for a in range(n_off):
            row_vals = jnp.where(off_row == a, rpb_ref[h, a, j], row_vals)
        table = jnp.where(col_idx == j, row_vals, table)
    cs = jnp.clip(qc - NA_COLS // 2, 0, GRID_W - NA_COLS)
    table = jnp.where((kc >= cs) & (kc < cs + NA_COLS), table * LOG2E, NEG)
    for var in range(N_BIAS_VARIANTS):
        out_ref[var, h] = table[:, var * GRID_W:(var + kh) * GRID_W]


def _na_kernel(rpb_ref, q_ref, k_ref, v_ref, wup_ref, o_ref, wup_bf_ref, bias_ref, *, rows, kh):
    rb = pl.program_id(1)
    nkeys = kh * GRID_W

    @pl.when((pl.program_id(0) == 0) & (rb == 0))
    def _():
        for h in range(N_HEADS_B):
            _expand_na_bias(rpb_ref, bias_ref, h, kh)

    def one_row(rl, carry):
        r = rb * NA_ROWS_PER_STEP + rl
        rs = jnp.clip(r - kh // 2, 0, rows - kh)
        var = rs - r + (NA_ROWS - 1)
        q0 = pl.multiple_of(rl * GRID_W, GRID_W)
        k0 = pl.multiple_of(rs * GRID_W, GRID_W)
        heads = range(N_HEADS_B)
        head_cols = [slice(h * HEAD_DIM, (h + 1) * HEAD_DIM) for h in heads]
        scores = [lax.dot_general(q_ref[pl.ds(q0, GRID_W), c], k_ref[pl.ds(k0, nkeys), c],
                                  (((1,), (1,)), ((), ())), preferred_element_type=F32)
                  for c in head_cols]
        probs, dens = [], []
        for h in heads:
            t = scores[h] * (SCALE * LOG2E) + bias_ref[var, h]
            m = jnp.max(t, axis=-1, keepdims=True)
            p = jnp.exp2(t - m)
            dens.append(jnp.sum(p, axis=-1, keepdims=True))
            probs.append(p.astype(BF16))
        for h in heads:
            o = jnp.dot(probs[h], v_ref[pl.ds(k0, nkeys), head_cols[h]],
                        preferred_element_type=F32) / dens[h]
            o_ref[pl.ds(q0, GRID_W), head_cols[h]] = o.astype(BF16)
        return carry

    lax.fori_loop(0, NA_ROWS_PER_STEP, one_row, 0, unroll=NA_UNROLL)
    wup_bf_ref[...] = wup_ref[...].astype(BF16)


def _neighbourhood(qkv, rpb, batch, seq_total, w_up):
    group = N_GROUPS_A
    assert GROUP_DILATIONS[group] == 1
    rows = seq_total // GRID_W
    kh = min(NA_ROWS, rows)
    assert kh == NA_ROWS and rows % NA_ROWS_PER_STEP == 0
    tq = NA_ROWS_PER_STEP * GRID_W
    steps_per_batch = rows // NA_ROWS_PER_STEP
    view = qkv.reshape(qkv.shape[0], batch, seq_total, qkv.shape[-1])
    wup_rows = pl.BlockSpec((w_up.shape[0] // (batch * steps_per_batch), w_up.shape[1]),
                            lambda b, i: (b * steps_per_batch + i, 0))
    o, w_up_bf = pl.pallas_call(
        functools.partial(_na_kernel, rows=rows, kh=kh),
        grid=(batch, steps_per_batch),
        in_specs=[
            pl.BlockSpec(memory_space=pltpu.SMEM),
            pl.BlockSpec((None, None, tq, GROUP_WIDTH), lambda b, i: (group, b, i, 0)),
            pl.BlockSpec((None, None, seq_total, GROUP_WIDTH), lambda b, i: (group, b, 0, 1)),
            pl.BlockSpec((None, None, seq_total, GROUP_WIDTH), lambda b, i: (group, b, 0, 2)),
            wup_rows,
        ],
        out_specs=[pl.BlockSpec((None, tq, GROUP_WIDTH), lambda b, i: (b, i, 0)), wup_rows],
        out_shape=[jax.ShapeDtypeStruct((batch, seq_total, GROUP_WIDTH), BF16),
                   jax.ShapeDtypeStruct(w_up.shape, BF16)],
        scratch_shapes=[pltpu.VMEM((N_BIAS_VARIANTS, N_HEADS_B, GRID_W, kh * GRID_W), F32)],
        compiler_params=_params(("arbitrary", "arbitrary")),
        name="neighbourhood_attention",
    )(rpb, view, view, view, w_up)
    return o.reshape(batch * seq_total, GROUP_WIDTH), w_up_bf


def _to_token_order_matrix(tm, d):
    t = lax.broadcasted_iota(jnp.int32, (tm, tm), 0)
    c = lax.broadcasted_iota(jnp.int32, (tm, tm), 1)
    return (c == (t % d) * (tm // d) + t // d).astype(BF16)


def _merge_kernel(o0_ref, o1_ref, o2_ref, l0_ref, l1_ref, l2_ref, ob_ref, gate_ref, x_ref,
                  wpa_ref, wpb_ref, wout_ref, gn_ref, wdn_ref,
                  x2_ref, h2_ref, wdn_bf_ref, lse_tok_ref, *, dilations):
    tm = x_ref.shape[0]
    o_tok, lse_tok = [], []
    for g, (o_ref, l_ref, d) in enumerate(zip((o0_ref, o1_ref, o2_ref),
                                              (l0_ref, l1_ref, l2_ref), dilations)):
        o = o_ref[...].reshape(tm, GROUP_WIDTH)
        if d == 1:
            o_tok.append(o.astype(F32))
            lse_tok.append(l_ref[...].reshape(tm, HEAD_DIM))
        else:
            o_tok.append(jnp.dot(_to_token_order_matrix(tm, d), o, preferred_element_type=F32))
            for r in range(d):
                lse_tok_ref[g, pl.ds(r, tm // d, stride=d), :] = l_ref[r]
            lse_tok.append(lse_tok_ref[g])
    l0, l1, l2 = lse_tok
    mx = jnp.maximum(jnp.maximum(l0, l1), l2)
    e0, e1, e2 = jnp.exp(l0 - mx), jnp.exp(l1 - mx), jnp.exp(l2 - mx)
    tot = e0 + e1 + e2
    a0, a1, a2 = e0 / tot, e1 / tot, e2 / tot
    parts = []
    for h in range(HEADS_PER_GROUP):
        cols = slice(h * HEAD_DIM, (h + 1) * HEAD_DIM)
        y = (a0[:, h:h + 1] * o_tok[0][:, cols] + a1[:, h:h + 1] * o_tok[1][:, cols]
             + a2[:, h:h + 1] * o_tok[2][:, cols])
        parts.append(y.astype(BF16))
    ya = jnp.concatenate(parts, axis=1)
    half = tm // MERGE_ROW_CHUNKS
    x2s = []
    for r0 in range(0, tm, half):
        rows = slice(r0, r0 + half)
        ta = jnp.dot(ya[rows], wpa_ref[...], preferred_element_type=F32)
        tb = jnp.dot(ob_ref[rows, :], wpb_ref[...], preferred_element_type=F32)
        merged = (gate_ref[rows, :D_MODEL].astype(F32) * ta
                  + gate_ref[rows, D_MODEL:].astype(F32) * tb)
        x2 = x_ref[rows, :] + jnp.dot(merged.astype(BF16), wout_ref[...],
                                      preferred_element_type=F32)
        x2_ref[rows, :] = x2
        x2s.append(x2)
        if r0 == 0:
            wdn_bf_ref[...] = wdn_ref[...].astype(BF16)
    for r0, x2 in zip(range(0, tm, half), x2s):
        h2_ref[r0:r0 + half, :] = _rms(x2, gn_ref[...]).astype(BF16)


def _merge(o_groups, lse_groups, ob, gates, x2d, seq, wpa, wpb, wout, gn, w_down):
    m = x2d.shape[0]
    tm = MERGE_TM
    n_steps = m // tm
    tiles_per_batch = seq // tm
    dilations = tuple(o.shape[1] for o in o_groups)
    row = lambda w: pl.BlockSpec((tm, w), lambda i: (i, 0))
    wdn_rows = pl.BlockSpec((w_down.shape[0] // n_steps, w_down.shape[1]), lambda i: (i, 0))
    grouped = lambda d, w: pl.BlockSpec(
        (None, d, tm // d, w), lambda i: (i // tiles_per_batch, 0, i % tiles_per_batch, 0))
    const = lambda a, b: pl.BlockSpec((a, b), lambda i: (0, 0), pipeline_mode=pl.Buffered(1))
    return pl.pallas_call(
        functools.partial(_merge_kernel, dilations=dilations),
        grid=(n_steps,),
        in_specs=[grouped(d, GROUP_WIDTH) for d in dilations]
        + [grouped(d, HEAD_DIM) for d in dilations]
        + [row(GROUP_WIDTH), row(2 * D_MODEL), row(D_MODEL),
           const(GROUP_WIDTH, D_MODEL), const(GROUP_WIDTH, D_MODEL), const(D_MODEL, D_MODEL),
           const(1, D_MODEL), wdn_rows],
        out_specs=[row(D_MODEL), row(D_MODEL), wdn_rows],
        out_shape=[jax.ShapeDtypeStruct((m, D_MODEL), F32), jax.ShapeDtypeStruct((m, D_MODEL), BF16),
                   jax.ShapeDtypeStruct(w_down.shape, BF16)],
        scratch_shapes=[pltpu.VMEM((len(dilations), tm, HEAD_DIM), F32)],
        compiler_params=_params(("arbitrary",)),
        name="merge_out_proj",
    )(*o_groups, *lse_groups, ob, gates, x2d, wpa, wpb, wout, gn, w_down)


def _mlp_kernel(h2_ref, wup_ref, wdn_ref, x2_hbm, gf_ref, out_ref, x2_sem, *, n_f):
    i = pl.program_id(0)
    f = pl.program_id(1)
    tm = out_ref.shape[0]

    def residual_copy():
        return pltpu.make_async_copy(x2_hbm.at[pl.ds(i * tm, tm), :], out_ref, x2_sem)

    @pl.when(f == 0)
    def _():
        residual_copy().start()

    hid = jnp.dot(h2_ref[...], wup_ref[...], preferred_element_type=F32)
    hid = jnp.square(jnp.maximum(hid, 0.0)).astype(BF16)

    @pl.when(f == 0)
    def _():
        residual_copy().wait()

    out_ref[...] += jnp.dot(hid, wdn_ref[...], preferred_element_type=F32)

    @pl.when(f == n_f - 1)
    def _():
        out_ref[...] = _rms(out_ref[...], gf_ref[...])


def _mlp(h2, wup, wdn, x2, gf):
    m = h2.shape[0]
    tm, tf = MLP_TM, MLP_TF
    n_f = D_FF // tf
    return pl.pallas_call(
        functools.partial(_mlp_kernel, n_f=n_f),
        grid=(m // tm, n_f),
        in_specs=[
            pl.BlockSpec((tm, D_MODEL), lambda i, f: (i, 0)),
            pl.BlockSpec((D_MODEL, tf), lambda i, f: (0, f)),
            pl.BlockSpec((tf, D_MODEL), lambda i, f: (f, 0)),
            pl.BlockSpec(memory_space=pl.ANY),
            pl.BlockSpec((1, D_MODEL), lambda i, f: (0, 0)),
        ],
        out_specs=pl.BlockSpec((tm, D_MODEL), lambda i, f: (i, 0)),
        out_shape=jax.ShapeDtypeStruct((m, D_MODEL), F32),
        scratch_shapes=[pltpu.SemaphoreType.DMA],
        compiler_params=_params(("arbitrary", "arbitrary")),
        name="mlp_residual_norm",
    )(h2, wup, wdn, x2, gf)


def _layer(x2d, batch, seq, norm_mix, w_qkv, w_gate, b_gate, rpb, w_proj_a, w_proj_b, w_out,
           norm_mlp, w_up, w_down):
    slopes = 2.0 ** (-8.0 * np.arange(1, N_HEADS_A + 1) / N_HEADS_A)
    row = lambda v: v.reshape(1, -1)
    h_orders = _norm(x2d, row(norm_mix))
    qkv, gates, w_out_bf, w_pa_bf, w_pb_bf = _project(
        h_orders, w_qkv, w_gate, row(b_gate), w_out, w_proj_a, w_proj_b)
    o_groups, lse_groups = [], []
    for g, (window, d) in enumerate(DILATION_PATTERNS):
        o, lse = _dilated_group(qkv, batch, seq, g, window, d, slopes)
        o_groups.append(o)
        lse_groups.append(lse)
    ob, w_up_bf = _neighbourhood(qkv, rpb, batch, seq, w_up)
    x2, h2, w_down_bf = _merge(o_groups, lse_groups, ob, gates, x2d, seq, w_pa_bf, w_pb_bf,
                               w_out_bf, row(norm_mlp), w_down)
    return x2, h2, w_up_bf, w_down_bf


def kernel(x, norm_mix, w_qkv, w_gate, b_gate, rpb, w_proj_a, w_proj_b, w_out, norm_mlp, w_up,
           w_down, norm_final):
    batch, seq, _ = x.shape
    depth = norm_mix.shape[0]
    assert depth == 1 and seq % PERM_TILE == 0
    x2d = x.reshape(batch * seq, D_MODEL)
    x2, h2, w_up_bf, w_down_bf = _layer(
        x2d, batch, seq, norm_mix[0], w_qkv[0], w_gate[0], b_gate[0], rpb[0],
        w_proj_a[0], w_proj_b[0], w_out[0], norm_mlp[0], w_up[0], w_down[0])
    out = _mlp(h2, w_up_bf, w_down_bf, x2, norm_final.reshape(1, -1))
    return out.reshape(batch, seq, D_MODEL)
```

```python
import functools

import jax
import jax.numpy as jnp
import numpy as np
from jax import lax
from jax.experimental import pallas as pl
from jax.experimental.pallas import tpu as pltpu

D_MODEL = 2048
HEAD_DIM = 128
N_HEADS = D_MODEL // HEAD_DIM
N_HEADS_B = N_HEADS // 4
N_HEADS_A = N_HEADS - N_HEADS_B
DILATION_PATTERNS = ((128, 1), (512, 4), (2048, 16))
N_GROUPS_A = len(DILATION_PATTERNS)
HEADS_PER_GROUP = N_HEADS_A // N_GROUPS_A
GROUP_WIDTH = HEADS_PER_GROUP * HEAD_DIM
N_HEAD_GROUPS = N_HEADS // HEADS_PER_GROUP
QKV_WIDTH = 3 * D_MODEL
GRID_W = 64
NA_ROWS = 8
NA_COLS = 16
D_FF = 4 * D_MODEL
EPS = 1e-6
NEG = -1e30
SCALE = HEAD_DIM ** -0.5
LOG2E = float(np.log2(np.e))
LN2 = float(np.log(2.0))
LANES = 128

F32 = jnp.float32
BF16 = jnp.bfloat16

VMEM_LIMIT_BYTES = 56 * 1024 * 1024

PERM_TILE = 1024
NORM_CHUNK = 512
PROJ_TM, PROJ_TN = 2048, GROUP_WIDTH
SIDE_CAST_ROWS = 32
MERGE_TM = 256
MERGE_ROW_CHUNKS = 2
MLP_TM, MLP_TF = 1024, 1024
DIL_QB = 128
DIL_UNROLL = 4
DIL_ROWS_PER_STEP = 1024
N_WINDOW_CASES = 3
NA_ROWS_PER_STEP = 8
NA_UNROLL = 4

GROUP_DILATIONS = tuple(d for _, d in DILATION_PATTERNS) + (1,)
ROW_ORDERS = tuple(sorted(set(GROUP_DILATIONS)))
GROUP_ROW_ORDER = tuple(ROW_ORDERS.index(d) for d in GROUP_DILATIONS)


def _params(sem):
    return pltpu.CompilerParams(dimension_semantics=sem, vmem_limit_bytes=VMEM_LIMIT_BYTES)


def _rms(x, g):
    ms = jnp.mean(x * x, axis=-1, keepdims=True)
    return (x * lax.rsqrt(ms + EPS)) * g


def _norm_kernel(x_ref, g_ref, h_ref, slab_ref):
    tm = x_ref.shape[0]
    x = x_ref[...]
    inv = lax.rsqrt(jnp.mean(x * x, axis=-1, keepdims=True) + EPS)
    n_slabs = NORM_CHUNK // LANES
    for c0 in range(0, D_MODEL, NORM_CHUNK):
        hc = (x_ref[:, c0:c0 + NORM_CHUNK] * inv) * g_ref[:, c0:c0 + NORM_CHUNK]
        h_ref[0, :, c0:c0 + NORM_CHUNK] = hc.astype(BF16)
        for s in range(n_slabs):
            slab_ref[0, s] = hc[:, s * LANES:(s + 1) * LANES]
        for v in range(1, len(ROW_ORDERS)):
            d_prev, d = ROW_ORDERS[v - 1], ROW_ORDERS[v]
            q = d // d_prev
            n_prev, n = tm // d_prev, tm // d
            last = v == len(ROW_ORDERS) - 1
            for s in range(n_slabs):
                cols = slice(c0 + s * LANES, c0 + (s + 1) * LANES)
                for r in range(d_prev):
                    for r2 in range(q):
                        rows = slab_ref[(v - 1) % 2, s, pl.ds(r * n_prev + r2, n, stride=q), :]
                        dst = (d_prev * r2 + r) * n
                        h_ref[v, dst:dst + n, cols] = rows.astype(BF16)
                        if not last:
                            slab_ref[v % 2, s, dst:dst + n, :] = rows


def _norm(x2d, g):
    m = x2d.shape[0]
    tm = PERM_TILE
    nv = len(ROW_ORDERS)
    assert ROW_ORDERS[0] == 1
    return pl.pallas_call(
        _norm_kernel,
        grid=(m // tm,),
        in_specs=[pl.BlockSpec((tm, D_MODEL), lambda i: (i, 0)),
                  pl.BlockSpec((1, D_MODEL), lambda i: (0, 0))],
        out_specs=pl.BlockSpec((nv, tm, D_MODEL), lambda i: (0, i, 0)),
        out_shape=jax.ShapeDtypeStruct((nv, m, D_MODEL), BF16),
        scratch_shapes=[pltpu.VMEM((2, NORM_CHUNK // LANES, tm, LANES), F32)],
        compiler_params=_params(("arbitrary",)),
        name="rmsnorm_row_orders",
    )(x2d, g)


def _proj_kernel(h_ref, w_first_ref, wq_next_ref, wg_next_ref, bg_ref, wout_ref, wpa_ref, wpb_ref,
                 qkv_ref, gate_ref, wout_bf_ref, wpa_bf_ref, wpb_bf_ref, w_bf_ref, *, n_qkv_steps):
    j = pl.program_id(0)
    i = pl.program_id(1)

    @pl.when((j == 0) & (i == 0))
    def _():
        w_bf_ref[0] = w_first_ref[...].astype(BF16)

    def side_casts(next_ref):
        part = next_ref.shape[0]
        rows = pl.ds(pl.multiple_of(i * part, part), part)
        w_bf_ref[(j + 1) % 2, rows, :] = next_ref[...].astype(BF16)
        wout_bf_ref[...] = wout_ref[...].astype(BF16)
        wpa_bf_ref[...] = wpa_ref[...].astype(BF16)
        wpb_bf_ref[...] = wpb_ref[...].astype(BF16)

    def qkv_step(next_ref):
        y = jnp.dot(h_ref[...], w_bf_ref[j % 2], preferred_element_type=F32)
        qkv_ref[...] = y.astype(BF16)
        side_casts(next_ref)

    @pl.when(j < n_qkv_steps - 1)
    def _():
        qkv_step(wq_next_ref)

    @pl.when(j == n_qkv_steps - 1)
    def _():
        qkv_step(wg_next_ref)

    @pl.when(j >= n_qkv_steps)
    def _():
        z = jnp.dot(h_ref[...], w_bf_ref[j % 2], preferred_element_type=F32) + bg_ref[...]
        gate_ref[...] = (0.5 * jnp.tanh(0.5 * z) + 0.5).astype(BF16)
        side_casts(wg_next_ref)


def _project(h_orders, w_qkv, w_gate, b_gate, w_out, w_proj_a, w_proj_b):
    m = h_orders.shape[1]
    tm, tn = PROJ_TM, PROJ_TN
    n_groups = N_HEAD_GROUPS
    nq = 3 * n_groups
    ng = (2 * D_MODEL) // tn
    n_i = m // tm
    row_order = GROUP_ROW_ORDER

    def lhs_map(j, i):
        v = jnp.int32(0)
        for grp in range(n_groups):
            v = jnp.where(j // 3 == grp, row_order[grp], v)
        return (v, i, 0)

    def w_qkv_col(j):
        jj = jnp.minimum(j, nq - 1)
        return (jj % 3) * n_groups + jj // 3

    gate_col = lambda j, i: (0, jnp.maximum(j - nq, 0))
    wq_next_map = lambda j, i: (jnp.where(j + 1 < nq, i, n_i - 1), w_qkv_col(j + 1))
    wg_next_map = lambda j, i: (jnp.where(j + 1 >= nq, i, 0), jnp.clip(j + 1 - nq, 0, ng - 1))

    def qkv_out_map(j, i):
        jj = jnp.minimum(j, nq - 1)
        return (jj // 3, jnp.where(j < nq, i, n_i - 1), jj % 3)

    gate_out_map = lambda j, i: (jnp.where(j >= nq, i, 0), jnp.maximum(j - nq, 0))

    def side_rows(w):
        n_blocks = w.shape[0] // SIDE_CAST_ROWS
        return pl.BlockSpec((SIDE_CAST_ROWS, w.shape[1]),
                            lambda j, i: (jnp.minimum(j * n_i + i, n_blocks - 1), 0))

    side = [w_out, w_proj_a, w_proj_b]
    assert all(w.shape[0] // SIDE_CAST_ROWS <= (nq + ng) * n_i for w in side)
    return pl.pallas_call(
        functools.partial(_proj_kernel, n_qkv_steps=nq),
        grid=(nq + ng, n_i),
        in_specs=[
            pl.BlockSpec((None, tm, D_MODEL), lhs_map),
            pl.BlockSpec((D_MODEL, tn), lambda j, i: (0, 0), pipeline_mode=pl.Buffered(1)),
            pl.BlockSpec((D_MODEL // n_i, tn), wq_next_map),
            pl.BlockSpec((D_MODEL // n_i, tn), wg_next_map),
            pl.BlockSpec((1, tn), gate_col),
        ] + [side_rows(w) for w in side],
        out_specs=[
            pl.BlockSpec((None, tm, tn), qkv_out_map),
            pl.BlockSpec((tm, tn), gate_out_map),
        ] + [side_rows(w) for w in side],
        out_shape=[
            jax.ShapeDtypeStruct((n_groups, m, 3 * GROUP_WIDTH), BF16),
            jax.ShapeDtypeStruct((m, 2 * D_MODEL), BF16),
        ] + [jax.ShapeDtypeStruct(w.shape, BF16) for w in side],
        scratch_shapes=[pltpu.VMEM((2, D_MODEL, tn), BF16)],
        compiler_params=_params(("arbitrary", "arbitrary")),
        name="proj_qkv_gate",
    )(h_orders, w_qkv, w_qkv, w_gate, b_gate, *side)


def _dilated_block_shape(seq, half_window):
    kw = DIL_QB + 2 * half_window
    return (seq, seq) if seq <= kw else (DIL_QB, kw)


def _dilated_kernel(q_ref, k_ref, v_ref, o_ref, lse_ref, bias_ref, *, seq, half_window, coefs):
    qb, kw = _dilated_block_shape(seq, half_window)
    n_blocks = seq // qb
    n_tiles, n_res = q_ref.shape[:2]

    def rows_loader(ref):
        if n_res == 1:
            flat = ref.at[:, 0].reshape(seq, GROUP_WIDTH)
            return lambda rr, start, size, cols: flat[pl.ds(start, size), cols]
        assert n_blocks == 1
        return lambda rr, start, size, cols: jnp.concatenate(
            [ref[t, rr, :, cols] for t in range(n_tiles)], axis=0)

    load_q, load_k, load_v = rows_loader(q_ref), rows_loader(k_ref), rows_loader(v_ref)
    lane = lax.broadcasted_iota(jnp.int32, (qb, HEAD_DIM), 1)
    heads = range(HEADS_PER_GROUP)
    head_cols = [slice(h * HEAD_DIM, (h + 1) * HEAD_DIM) for h in heads]

    @pl.when((pl.program_id(0) == 0) & (pl.program_id(1) == 0))
    def _():
        qrow = lax.broadcasted_iota(jnp.int32, (qb, kw), 0)
        kcol = lax.broadcasted_iota(jnp.int32, (qb, kw), 1)
        for c in range(bias_ref.shape[0]):
            dist = jnp.abs(kcol - qrow - c * half_window)
            dist_f = dist.astype(F32)
            for h in heads:
                bias_ref[c, h] = jnp.where(dist <= half_window, (-coefs[h] * LOG2E) * dist_f, NEG)

    def block(qi, carry):
        q0 = pl.multiple_of(qi * qb, qb)
        ks = pl.multiple_of(jnp.clip(q0 - half_window, 0, seq - kw), half_window)
        window_case = (q0 - ks) // half_window
        pairs = [(rr, h) for rr in range(n_res) for h in heads]
        scores = [lax.dot_general(load_q(rr, q0, qb, head_cols[h]), load_k(rr, ks, kw, head_cols[h]),
                                  (((1,), (1,)), ((), ())), preferred_element_type=F32)
                  for rr, h in pairs]
        probs, dens = [], []
        lse_all = [jnp.zeros((qb, HEAD_DIM), F32) for _ in range(n_res)]
        for (rr, h), s in zip(pairs, scores):
            t = s * (SCALE * LOG2E) + bias_ref[window_case, h]
            m = jnp.max(t, axis=-1, keepdims=True)
            p = jnp.exp2(t - m)
            den = jnp.sum(p, axis=-1, keepdims=True)
            probs.append(p.astype(BF16))
            dens.append(den)
            lse_all[rr] = jnp.where(lane == h, m * LN2 + jnp.log(den), lse_all[rr])
        for (rr, h), p, den in zip(pairs, probs, dens):
            o = jnp.dot(p, load_v(rr, ks, kw, head_cols[h]), preferred_element_type=F32) / den
            o_ref[rr, pl.ds(q0, qb), head_cols[h]] = o.astype(BF16)
        for rr in range(n_res):
            lse_ref[rr, pl.ds(q0, qb), :] = lse_all[rr]
        return carry

    lax.fori_loop(0, n_blocks, block, 0, unroll=min(DIL_UNROLL, n_blocks))


def _dilated_group(qkv, batch, seq_total, group, window, dilation, slopes):
    d = dilation
    seq = seq_total // d
    half_window = window // (2 * d)
    qb, kw = _dilated_block_shape(seq, half_window)
    coefs = tuple(float(slopes[group * HEADS_PER_GROUP + h]) * d for h in range(HEADS_PER_GROUP))
    tiles = seq_total // PERM_TILE
    rows = PERM_TILE // d
    view = qkv.reshape(qkv.shape[0], batch, tiles, d, rows, qkv.shape[-1])
    n_res = min(d, max(1, DIL_ROWS_PER_STEP // seq))
    part = lambda which: pl.BlockSpec((None, None, tiles, n_res, rows, GROUP_WIDTH),
                                      lambda b, r: (group, b, 0, r, 0, which))
    return pl.pallas_call(
        functools.partial(_dilated_kernel, seq=seq, half_window=half_window, coefs=coefs),
        grid=(batch, d // n_res),
        in_specs=[part(0), part(1), part(2)],
        out_specs=[
            pl.BlockSpec((None, n_res, seq, GROUP_WIDTH), lambda b, r: (b, r, 0, 0)),
            pl.BlockSpec((None, n_res, seq, HEAD_DIM), lambda b, r: (b, r, 0, 0)),
        ],
        out_shape=[
            jax.ShapeDtypeStruct((batch, d, seq, GROUP_WIDTH), BF16),
            jax.ShapeDtypeStruct((batch, d, seq, HEAD_DIM), F32),
        ],
        scratch_shapes=[pltpu.VMEM((N_WINDOW_CASES, HEADS_PER_GROUP, qb, kw), F32)],
        compiler_params=_params(("arbitrary", "arbitrary")),
        name=f"dilated_attention_d{d}",
    )(view, view, view)


N_BIAS_VARIANTS = NA_ROWS


def _expand_na_bias(rpb_ref, out_ref, h, kh):
    n_off = 2 * NA_ROWS - 1
    width = (n_off + 1) * GRID_W
    qc = lax.broadcasted_iota(jnp.int32, (GRID_W, width), 0)
    kc = lax.broadcasted_iota(jnp.int32, (GRID_W, width), 1) % GRID_W
    col_idx = jnp.clip(kc - qc, -(NA_COLS - 1), NA_COLS - 1) + (NA_COLS - 1)
    off_row = lax.broadcasted_iota(jnp.int32, (1, width), 1) // GRID_W
    table = jnp.zeros((GRID_W, width), F32)
    for j in range(2 * NA_COLS - 1):
        row_vals = jnp.zeros((1, width), F32)
        for a in range(n_off):
            row_vals = jnp.where(off_row == a, rpb_ref[h, a, j], row_vals)
        table = jnp.where(col_idx == j, row_vals, table)
    cs = jnp.clip(qc - NA_COLS // 2, 0, GRID_W - NA_COLS)
    table = jnp.where((kc >= cs) & (kc < cs + NA_COLS), table * LOG2E, NEG)
    for var in range(N_BIAS_VARIANTS):
        out_ref[var, h] = table[:, var * GRID_W:(var + kh) * GRID_W]


def _na_kernel(rpb_ref, q_ref, k_ref, v_ref, wup_ref, o_ref, wup_bf_ref, bias_ref, *, rows, kh):
    rb = pl.program_id(1)
    nkeys = kh * GRID_W

    @pl.when((pl.program_id(0) == 0) & (rb == 0))
    def _():
        for h in range(N_HEADS_B):
            _expand_na_bias(rpb_ref, bias_ref, h, kh)

    def one_row(rl, carry):
        r = rb * NA_ROWS_PER_STEP + rl
        rs = jnp.clip(r - kh // 2, 0, rows - kh)
        var = rs - r + (NA_ROWS - 1)
        q0 = pl.multiple_of(rl * GRID_W, GRID_W)
        k0 = pl.multiple_of(rs * GRID_W, GRID_W)
        heads = range(N_HEADS_B)
        head_cols = [slice(h * HEAD_DIM, (h + 1) * HEAD_DIM) for h in heads]
        scores = [lax.dot_general(q_ref[pl.ds(q0, GRID_W), c], k_ref[pl.ds(k0, nkeys), c],
                                  (((1,), (1,)), ((), ())), preferred_element_type=F32)
                  for c in head_cols]
        probs, dens = [], []
        for h in heads:
            t = scores[h] * (SCALE * LOG2E) + bias_ref[var, h]
            m = jnp.max(t, axis=-1, keepdims=True)
            p = jnp.exp2(t - m)
            dens.append(jnp.sum(p, axis=-1, keepdims=True))
            probs.append(p.astype(BF16))
        for h in heads:
            o = jnp.dot(probs[h], v_ref[pl.ds(k0, nkeys), head_cols[h]],
                        preferred_element_type=F32) / dens[h]
            o_ref[pl.ds(q0, GRID_W), head_cols[h]] = o.astype(BF16)
        return carry

    lax.fori_loop(0, NA_ROWS_PER_STEP, one_row, 0, unroll=NA_UNROLL)
    wup_bf_ref[...] = wup_ref[...].astype(BF16)


def _neighbourhood(qkv, rpb, batch, seq_total, w_up):
    group = N_GROUPS_A
    assert GROUP_DILATIONS[group] == 1
    rows = seq_total // GRID_W
    kh = min(NA_ROWS, rows)
    assert kh == NA_ROWS and rows % NA_ROWS_PER_STEP == 0
    tq = NA_ROWS_PER_STEP * GRID_W
    steps_per_batch = rows // NA_ROWS_PER_STEP
    view = qkv.reshape(qkv.shape[0], batch, seq_total, qkv.shape[-1])
    wup_rows = pl.BlockSpec((w_up.shape[0] // (batch * steps_per_batch), w_up.shape[1]),
                            lambda b, i: (b * steps_per_batch + i, 0))
    o, w_up_bf = pl.pallas_call(
        functools.partial(_na_kernel, rows=rows, kh=kh),
        grid=(batch, steps_per_batch),
        in_specs=[
            pl.BlockSpec(memory_space=pltpu.SMEM),
            pl.BlockSpec((None, None, tq, GROUP_WIDTH), lambda b, i: (group, b, i, 0)),
            pl.BlockSpec((None, None, seq_total, GROUP_WIDTH), lambda b, i: (group, b, 0, 1)),
            pl.BlockSpec((None, None, seq_total, GROUP_WIDTH), lambda b, i: (group, b, 0, 2)),
            wup_rows,
        ],
        out_specs=[pl.BlockSpec((None, tq, GROUP_WIDTH), lambda b, i: (b, i, 0)), wup_rows],
        out_shape=[jax.ShapeDtypeStruct((batch, seq_total, GROUP_WIDTH), BF16),
                   jax.ShapeDtypeStruct(w_up.shape, BF16)],
        scratch_shapes=[pltpu.VMEM((N_BIAS_VARIANTS, N_HEADS_B, GRID_W, kh * GRID_W), F32)],
        compiler_params=_params(("arbitrary", "arbitrary")),
        name="neighbourhood_attention",
    )(rpb, view, view, view, w_up)
    return o.reshape(batch * seq_total, GROUP_WIDTH), w_up_bf


def _to_token_order_matrix(tm, d):
    t = lax.broadcasted_iota(jnp.int32, (tm, tm), 0)
    c = lax.broadcasted_iota(jnp.int32, (tm, tm), 1)
    return (c == (t % d) * (tm // d) + t // d).astype(BF16)


def _merge_kernel(o0_ref, o1_ref, o2_ref, l0_ref, l1_ref, l2_ref, ob_ref, gate_ref, x_ref,
                  wpa_ref, wpb_ref, wout_ref, gn_ref, wdn_ref,
                  x2_ref, h2_ref, wdn_bf_ref, lse_tok_ref, *, dilations):
    tm = x_ref.shape[0]
    o_tok, lse_tok = [], []
    for g, (o_ref, l_ref, d) in enumerate(zip((o0_ref, o1_ref, o2_ref),
                                              (l0_ref, l1_ref, l2_ref), dilations)):
        o = o_ref[...].reshape(tm, GROUP_WIDTH)
        if d == 1:
            o_tok.append(o.astype(F32))
            lse_tok.append(l_ref[...].reshape(tm, HEAD_DIM))
        else:
            o_tok.append(jnp.dot(_to_token_order_matrix(tm, d), o, preferred_element_type=F32))
            for r in range(d):
                lse_tok_ref[g, pl.ds(r, tm // d, stride=d), :] = l_ref[r]
            lse_tok.append(lse_tok_ref[g])
    l0, l1, l2 = lse_tok
    mx = jnp.maximum(jnp.maximum(l0, l1), l2)
    e0, e1, e2 = jnp.exp(l0 - mx), jnp.exp(l1 - mx), jnp.exp(l2 - mx)
    tot = e0 + e1 + e2
    a0, a1, a2 = e0 / tot, e1 / tot, e2 / tot
    parts = []
    for h in range(HEADS_PER_GROUP):
        cols = slice(h * HEAD_DIM, (h + 1) * HEAD_DIM)
        y = (a0[:, h:h + 1] * o_tok[0][:, cols] + a1[:, h:h + 1] * o_tok[1][:, cols]
             + a2[:, h:h + 1] * o_tok[2][:, cols])
        parts.append(y.astype(BF16))
    ya = jnp.concatenate(parts, axis=1)
    half = tm // MERGE_ROW_CHUNKS
    x2s = []
    for r0 in range(0, tm, half):
        rows = slice(r0, r0 + half)
        ta = jnp.dot(ya[rows], wpa_ref[...], preferred_element_type=F32)
        tb = jnp.dot(ob_ref[rows, :], wpb_ref[...], preferred_element_type=F32)
        merged = (gate_ref[rows, :D_MODEL].astype(F32) * ta
                  + gate_ref[rows, D_MODEL:].astype(F32) * tb)
        x2 = x_ref[rows, :] + jnp.dot(merged.astype(BF16), wout_ref[...],
                                      preferred_element_type=F32)
        x2_ref[rows, :] = x2
        x2s.append(x2)
        if r0 == 0:
            wdn_bf_ref[...] = wdn_ref[...].astype(BF16)
    for r0, x2 in zip(range(0, tm, half), x2s):
        h2_ref[r0:r0 + half, :] = _rms(x2, gn_ref[...]).astype(BF16)


def _merge(o_groups, lse_groups, ob, gates, x2d, seq, wpa, wpb, wout, gn, w_down):
    m = x2d.shape[0]
    tm = MERGE_TM
    n_steps = m // tm
    tiles_per_batch = seq // tm
    dilations = tuple(o.shape[1] for o in o_groups)
    row = lambda w: pl.BlockSpec((tm, w), lambda i: (i, 0))
    wdn_rows = pl.BlockSpec((w_down.shape[0] // n_steps, w_down.shape[1]), lambda i: (i, 0))
    grouped = lambda d, w: pl.BlockSpec(
        (None, d, tm // d, w), lambda i: (i // tiles_per_batch, 0, i % tiles_per_batch, 0))
    const = lambda a, b: pl.BlockSpec((a, b), lambda i: (0, 0), pipeline_mode=pl.Buffered(1))
    return pl.pallas_call(
        functools.partial(_merge_kernel, dilations=dilations),
        grid=(n_steps,),
        in_specs=[grouped(d, GROUP_WIDTH) for d in dilations]
        + [grouped(d, HEAD_DIM) for d in dilations]
        + [row(GROUP_WIDTH), row(2 * D_MODEL), row(D_MODEL),
           const(GROUP_WIDTH, D_MODEL), const(GROUP_WIDTH, D_MODEL), const(D_MODEL, D_MODEL),
           const(1, D_MODEL), wdn_rows],
        out_specs=[row(D_MODEL), row(D_MODEL), wdn_rows],
        out_shape=[jax.ShapeDtypeStruct((m, D_MODEL), F32), jax.ShapeDtypeStruct((m, D_MODEL), BF16),
                   jax.ShapeDtypeStruct(w_down.shape, BF16)],
        scratch_shapes=[pltpu.VMEM((len(dilations), tm, HEAD_DIM), F32)],
        compiler_params=_params(("arbitrary",)),
        name="merge_out_proj",
    )(*o_groups, *lse_groups, ob, gates, x2d, wpa, wpb, wout, gn, w_down)


def _mlp_kernel(h2_ref, wup_ref, wdn_ref, x2_hbm, gf_ref, out_ref, x2_sem, *, n_f):
    i = pl.program_id(0)
    f = pl.program_id(1)
    tm = out_ref.shape[0]

    def residual_copy():
        return pltpu.make_async_copy(x2_hbm.at[pl.ds(i * tm, tm), :], out_ref, x2_sem)

    @pl.when(f == 0)
    def _():
        residual_copy().start()

    hid = jnp.dot(h2_ref[...], wup_ref[...], preferred_element_type=F32)
    hid = jnp.square(jnp.maximum(hid, 0.0)).astype(BF16)

    @pl.when(f == 0)
    def _():
        residual_copy().wait()

    out_ref[...] += jnp.dot(hid, wdn_ref[...], preferred_element_type=F32)

    @pl.when(f == n_f - 1)
    def _():
        out_ref[...] = _rms(out_ref[...], gf_ref[...])


def _mlp(h2, wup, wdn, x2, gf):
    m = h2.shape[0]
    tm, tf = MLP_TM, MLP_TF
    n_f = D_FF // tf
    return pl.pallas_call(
        functools.partial(_mlp_kernel, n_f=n_f),
        grid=(m // tm, n_f),
        in_specs=[
            pl.BlockSpec((tm, D_MODEL), lambda i, f: (i, 0)),
            pl.BlockSpec((D_MODEL, tf), lambda i, f: (0, f)),
            pl.BlockSpec((tf, D_MODEL), lambda i, f: (f, 0)),
            pl.BlockSpec(memory_space=pl.ANY),
            pl.BlockSpec((1, D_MODEL), lambda i, f: (0, 0)),
        ],
        out_specs=pl.BlockSpec((tm, D_MODEL), lambda i, f: (i, 0)),
        out_shape=jax.ShapeDtypeStruct((m, D_MODEL), F32),
        scratch_shapes=[pltpu.SemaphoreType.DMA],
        compiler_params=_params(("arbitrary", "arbitrary")),
        name="mlp_residual_norm",
    )(h2, wup, wdn, x2, gf)


def _layer(x2d, batch, seq, norm_mix, w_qkv, w_gate, b_gate, rpb, w_proj_a, w_proj_b, w_out,
           norm_mlp, w_up, w_down):
    slopes = 2.0 ** (-8.0 * np.arange(1, N_HEADS_A + 1) / N_HEADS_A)
    row = lambda v: v.reshape(1, -1)
    h_orders = _norm(x2d, row(norm_mix))
    qkv, gates, w_out_bf, w_pa_bf, w_pb_bf = _project(
        h_orders, w_qkv, w_gate, row(b_gate), w_out, w_proj_a, w_proj_b)
    o_groups, lse_groups = [], []
    for g, (window, d) in enumerate(DILATION_PATTERNS):
        o, lse = _dilated_group(qkv, batch, seq, g, window, d, slopes)
        o_groups.append(o)
        lse_groups.append(lse)
    ob, w_up_bf = _neighbourhood(qkv, rpb, batch, seq, w_up)
    x2, h2, w_down_bf = _merge(o_groups, lse_groups, ob, gates, x2d, seq, w_pa_bf, w_pb_bf,
                               w_out_bf, row(norm_mlp), w_down)
    return x2, h2, w_up_bf, w_down_bf


def kernel(x, norm_mix, w_qkv, w_gate, b_gate, rpb, w_proj_a, w_proj_b, w_out, norm_mlp, w_up,
           w_down, norm_final):
    batch, seq, _ = x.shape
    depth = norm_mix.shape[0]
    assert depth == 1 and seq % PERM_TILE == 0
    x2d = x.reshape(batch * seq, D_MODEL)
    x2, h2, w_up_bf, w_down_bf = _layer(
        x2d, batch, seq, norm_mix[0], w_qkv[0], w_gate[0], b_gate[0], rpb[0],
        w_proj_a[0], w_proj_b[0], w_out[0], norm_mlp[0], w_up[0], w_down[0])
    out = _mlp(h2, w_up_bf, w_down_bf, x2, norm_final.reshape(1, -1))
    return out.reshape(batch, seq, D_MODEL)
```

```python
import functools

import jax
import jax.numpy as jnp
import numpy as np
from jax import lax
from jax.experimental import pallas as pl
from jax.experimental.pallas import tpu as pltpu

D_MODEL = 2048
HEAD_DIM = 128
N_HEADS = D_MODEL // HEAD_DIM
N_HEADS_B = N_HEADS // 4
N_HEADS_A = N_HEADS - N_HEADS_B
DILATION_PATTERNS = ((128, 1), (512, 4), (2048, 16))
N_GROUPS_A = len(DILATION_PATTERNS)
HEADS_PER_GROUP = N_HEADS_A // N_GROUPS_A
GROUP_WIDTH = HEADS_PER_GROUP * HEAD_DIM
N_HEAD_GROUPS = N_HEADS // HEADS_PER_GROUP
QKV_WIDTH = 3 * D_MODEL
GRID_W = 64
NA_ROWS = 8
NA_COLS = 16
D_FF = 4 * D_MODEL
EPS = 1e-6
NEG = -1e30
SCALE = HEAD_DIM ** -0.5
LOG2E = float(np.log2(np.e))
LN2 = float(np.log(2.0))
LANES = 128

F32 = jnp.float32
BF16 = jnp.bfloat16

VMEM_LIMIT_BYTES = 56 * 1024 * 1024

PERM_TILE = 1024
NORM_CHUNK = 512
PROJ_TM, PROJ_TN = 2048, GROUP_WIDTH
SIDE_CAST_ROWS = 32
MERGE_TM = 256
MERGE_ROW_CHUNKS = 2
MLP_TM, MLP_TF = 1024, 1024
MLP_FINAL_ROW_CHUNKS = 4
DIL_QB = 128
DIL_UNROLL = 4
DIL_ROWS_PER_STEP = 1024
N_WINDOW_CASES = 3
NA_ROWS_PER_STEP = 8
NA_UNROLL = 4

GROUP_DILATIONS = tuple(d for _, d in DILATION_PATTERNS) + (1,)
ROW_ORDERS = tuple(sorted(set(GROUP_DILATIONS)))
GROUP_ROW_ORDER = tuple(ROW_ORDERS.index(d) for d in GROUP_DILATIONS)


def _params(sem):
    return pltpu.CompilerParams(dimension_semantics=sem, vmem_limit_bytes=VMEM_LIMIT_BYTES)


def _rms(x, g):
    ms = jnp.mean(x * x, axis=-1, keepdims=True)
    return (x * lax.rsqrt(ms + EPS)) * g


def _norm_kernel(x_ref, g_ref, h_ref, slab_ref):
    tm = x_ref.shape[0]
    x = x_ref[...]
    inv = lax.rsqrt(jnp.mean(x * x, axis=-1, keepdims=True) + EPS)
    n_slabs = NORM_CHUNK // LANES
    for c0 in range(0, D_MODEL, NORM_CHUNK):
        hc = (x_ref[:, c0:c0 + NORM_CHUNK] * inv) * g_ref[:, c0:c0 + NORM_CHUNK]
        h_ref[0, :, c0:c0 + NORM_CHUNK] = hc.astype(BF16)
        for s in range(n_slabs):
            slab_ref[0, s] = hc[:, s * LANES:(s + 1) * LANES]
        for v in range(1, len(ROW_ORDERS)):
            d_prev, d = ROW_ORDERS[v - 1], ROW_ORDERS[v]
            q = d // d_prev
            n_prev, n = tm // d_prev, tm // d
            last = v == len(ROW_ORDERS) - 1
            for s in range(n_slabs):
                cols = slice(c0 + s * LANES, c0 + (s + 1) * LANES)
                for r in range(d_prev):
                    for r2 in range(q):
                        rows = slab_ref[(v - 1) % 2, s, pl.ds(r * n_prev + r2, n, stride=q), :]
                        dst = (d_prev * r2 + r) * n
                        h_ref[v, dst:dst + n, cols] = rows.astype(BF16)
                        if not last:
                            slab_ref[v % 2, s, dst:dst + n, :] = rows


def _norm(x2d, g):
    m = x2d.shape[0]
    tm = PERM_TILE
    nv = len(ROW_ORDERS)
    assert ROW_ORDERS[0] == 1
    return pl.pallas_call(
        _norm_kernel,
        grid=(m // tm,),
        in_specs=[pl.BlockSpec((tm, D_MODEL), lambda i: (i, 0)),
                  pl.BlockSpec((1, D_MODEL), lambda i: (0, 0))],
        out_specs=pl.BlockSpec((nv, tm, D_MODEL), lambda i: (0, i, 0)),
        out_shape=jax.ShapeDtypeStruct((nv, m, D_MODEL), BF16),
        scratch_shapes=[pltpu.VMEM((2, NORM_CHUNK // LANES, tm, LANES), F32)],
        compiler_params=_params(("arbitrary",)),
        name="rmsnorm_row_orders",
    )(x2d, g)


def _proj_kernel(h_ref, w_first_ref, wq_next_ref, wg_next_ref, bg_ref, wout_ref, wpa_ref, wpb_ref,
                 qkv_ref, gate_ref, wout_bf_ref, wpa_bf_ref, wpb_bf_ref, w_bf_ref, *, n_qkv_steps):
    j = pl.program_id(0)
    i = pl.program_id(1)

    @pl.when((j == 0) & (i == 0))
    def _():
        w_bf_ref[0] = w_first_ref[...].astype(BF16)

    def side_casts(next_ref):
        part = next_ref.shape[0]
        rows = pl.ds(pl.multiple_of(i * part, part), part)
        w_bf_ref[(j + 1) % 2, rows, :] = next_ref[...].astype(BF16)
        wout_bf_ref[...] = wout_ref[...].astype(BF16)
        wpa_bf_ref[...] = wpa_ref[...].astype(BF16)
        wpb_bf_ref[...] = wpb_ref[...].astype(BF16)

    def qkv_step(next_ref):
        y = jnp.dot(h_ref[...], w_bf_ref[j % 2], preferred_element_type=F32)
        qkv_ref[...] = y.astype(BF16)
        side_casts(next_ref)

    @pl.when(j < n_qkv_steps - 1)
    def _():
        qkv_step(wq_next_ref)

    @pl.when(j == n_qkv_steps - 1)
    def _():
        qkv_step(wg_next_ref)

    @pl.when(j >= n_qkv_steps)
    def _():
        z = jnp.dot(h_ref[...], w_bf_ref[j % 2], preferred_element_type=F32) + bg_ref[...]
        gate_ref[...] = (0.5 * jnp.tanh(0.5 * z) + 0.5).astype(BF16)
        side_casts(wg_next_ref)


def _project(h_orders, w_qkv, w_gate, b_gate, w_out, w_proj_a, w_proj_b):
    m = h_orders.shape[1]
    tm, tn = PROJ_TM, PROJ_TN
    n_groups = N_HEAD_GROUPS
    nq = 3 * n_groups
    ng = (2 * D_MODEL) // tn
    n_i = m // tm
    row_order = GROUP_ROW_ORDER

    def lhs_map(j, i):
        v = jnp.int32(0)
        for grp in range(n_groups):
            v = jnp.where(j // 3 == grp, row_order[grp], v)
        return (v, i, 0)

    def w_qkv_col(j):
        jj = jnp.minimum(j, nq - 1)
        return (jj % 3) * n_groups + jj // 3

    gate_col = lambda j, i: (0, jnp.maximum(j - nq, 0))
    wq_next_map = lambda j, i: (i, w_qkv_col(j + 1))
    wg_next_map = lambda j, i: (i, jnp.clip(j + 1 - nq, 0, ng - 1))

    def qkv_out_map(j, i):
        jj = jnp.minimum(j, nq - 1)
        return (jj // 3, jnp.where(j < nq, i, n_i - 1), jj % 3)

    gate_out_map = lambda j, i: (jnp.where(j >= nq, i, 0), jnp.maximum(j - nq, 0))

    def side_rows(w):
        n_blocks = w.shape[0] // SIDE_CAST_ROWS
        return pl.BlockSpec((SIDE_CAST_ROWS, w.shape[1]),
                            lambda j, i: (jnp.minimum(j * n_i + i, n_blocks - 1), 0))

    side = [w_out, w_proj_a, w_proj_b]
    assert all(w.shape[0] // SIDE_CAST_ROWS <= (nq + ng) * n_i for w in side)
    return pl.pallas_call(
        functools.partial(_proj_kernel, n_qkv_steps=nq),
        grid=(nq + ng, n_i),
        in_specs=[
            pl.BlockSpec((None, tm, D_MODEL), lhs_map),
            pl.BlockSpec((D_MODEL, tn), lambda j, i: (0, 0), pipeline_mode=pl.Buffered(1)),
            pl.BlockSpec((D_MODEL // n_i, tn), wq_next_map),
            pl.BlockSpec((D_MODEL // n_i, tn), wg_next_map),
            pl.BlockSpec((1, tn), gate_col),
        ] + [side_rows(w) for w in side],
        out_specs=[
            pl.BlockSpec((None, tm, tn), qkv_out_map),
            pl.BlockSpec((tm, tn), gate_out_map),
        ] + [side_rows(w) for w in side],
        out_shape=[
            jax.ShapeDtypeStruct((n_groups, m, 3 * GROUP_WIDTH), BF16),
            jax.ShapeDtypeStruct((m, 2 * D_MODEL), BF16),
        ] + [jax.ShapeDtypeStruct(w.shape, BF16) for w in side],
        scratch_shapes=[pltpu.VMEM((2, D_MODEL, tn), BF16)],
        compiler_params=_params(("arbitrary", "arbitrary")),
        name="proj_qkv_gate",
    )(h_orders, w_qkv, w_qkv, w_gate, b_gate, *side)


def _dilated_block_shape(seq, half_window):
    kw = DIL_QB + 2 * half_window
    return (seq, seq) if seq <= kw else (DIL_QB, kw)


def _dilated_kernel(q_ref, k_ref, v_ref, o_ref, lse_ref, bias_ref, *, seq, half_window, coefs):
    qb, kw = _dilated_block_shape(seq, half_window)
    n_blocks = seq // qb
    n_tiles, n_res = q_ref.shape[:2]

    def rows_loader(ref):
        if n_res == 1:
            flat = ref.at[:, 0].reshape(seq, GROUP_WIDTH)
            return lambda rr, start, size, cols: flat[pl.ds(start, size), cols]
        assert n_blocks == 1
        return lambda rr, start, size, cols: jnp.concatenate(
            [ref[t, rr, :, cols] for t in range(n_tiles)], axis=0)

    load_q, load_k, load_v = rows_loader(q_ref), rows_loader(k_ref), rows_loader(v_ref)
    lane = lax.broadcasted_iota(jnp.int32, (qb, HEAD_DIM), 1)
    heads = range(HEADS_PER_GROUP)
    head_cols = [slice(h * HEAD_DIM, (h + 1) * HEAD_DIM) for h in heads]

    @pl.when((pl.program_id(0) == 0) & (pl.program_id(1) == 0))
    def _():
        qrow = lax.broadcasted_iota(jnp.int32, (qb, kw), 0)
        kcol = lax.broadcasted_iota(jnp.int32, (qb, kw), 1)
        for c in range(bias_ref.shape[0]):
            dist = jnp.abs(kcol - qrow - c * half_window)
            dist_f = dist.astype(F32)
            for h in heads:
                bias_ref[c, h] = jnp.where(dist <= half_window, (-coefs[h] * LOG2E) * dist_f, NEG)

    def block(qi, carry):
        q0 = pl.multiple_of(qi * qb, qb)
        ks = pl.multiple_of(jnp.clip(q0 - half_window, 0, seq - kw), half_window)
        window_case = (q0 - ks) // half_window
        pairs = [(rr, h) for rr in range(n_res) for h in heads]
        scores = [lax.dot_general(load_q(rr, q0, qb, head_cols[h]), load_k(rr, ks, kw, head_cols[h]),
                                  (((1,), (1,)), ((), ())), preferred_element_type=F32)
                  for rr, h in pairs]
        probs, dens = [], []
        lse_all = [jnp.zeros((qb, HEAD_DIM), F32) for _ in range(n_res)]
        for (rr, h), s in zip(pairs, scores):
            t = s * (SCALE * LOG2E) + bias_ref[window_case, h]
            m = jnp.max(t, axis=-1, keepdims=True)
            p = jnp.exp2(t - m)
            den = jnp.sum(p, axis=-1, keepdims=True)
            probs.append(p.astype(BF16))
            dens.append(den)
            lse_all[rr] = jnp.where(lane == h, m * LN2 + jnp.log(den), lse_all[rr])
        for (rr, h), p, den in zip(pairs, probs, dens):
            o = jnp.dot(p, load_v(rr, ks, kw, head_cols[h]), preferred_element_type=F32) / den
            o_ref[rr, pl.ds(q0, qb), head_cols[h]] = o.astype(BF16)
        for rr in range(n_res):
            lse_ref[rr, pl.ds(q0, qb), :] = lse_all[rr]
        return carry

    lax.fori_loop(0, n_blocks, block, 0, unroll=min(DIL_UNROLL, n_blocks))


def _dilated_group(qkv, batch, seq_total, group, window, dilation, slopes):
    d = dilation
    seq = seq_total // d
    half_window = window // (2 * d)
    qb, kw = _dilated_block_shape(seq, half_window)
    coefs = tuple(float(slopes[group * HEADS_PER_GROUP + h]) * d for h in range(HEADS_PER_GROUP))
    tiles = seq_total // PERM_TILE
    rows = PERM_TILE // d
    view = qkv.reshape(qkv.shape[0], batch, tiles, d, rows, qkv.shape[-1])
    n_res = min(d, max(1, DIL_ROWS_PER_STEP // seq))
    part = lambda which: pl.BlockSpec((None, None, tiles, n_res, rows, GROUP_WIDTH),
                                      lambda b, r: (group, b, 0, r, 0, which))
    return pl.pallas_call(
        functools.partial(_dilated_kernel, seq=seq, half_window=half_window, coefs=coefs),
        grid=(batch, d // n_res),
        in_specs=[part(0), part(1), part(2)],
        out_specs=[
            pl.BlockSpec((None, n_res, seq, GROUP_WIDTH), lambda b, r: (b, r, 0, 0)),
            pl.BlockSpec((None, n_res, seq, HEAD_DIM), lambda b, r: (b, r, 0, 0)),
        ],
        out_shape=[
            jax.ShapeDtypeStruct((batch, d, seq, GROUP_WIDTH), BF16),
            jax.ShapeDtypeStruct((batch, d, seq, HEAD_DIM), F32),
        ],
        scratch_shapes=[pltpu.VMEM((N_WINDOW_CASES, HEADS_PER_GROUP, qb, kw), F32)],
        compiler_params=_params(("arbitrary", "arbitrary")),
        name=f"dilated_attention_d{d}",
    )(view, view, view)


N_BIAS_VARIANTS = NA_ROWS


def _expand_na_bias(rpb_ref, out_ref, h, kh):
    n_off = 2 * NA_ROWS - 1
    width = (n_off + 1) * GRID_W
    qc = lax.broadcasted_iota(jnp.int32, (GRID_W, width), 0)
    kc = lax.broadcasted_iota(jnp.int32, (GRID_W, width), 1) % GRID_W
    col_idx = jnp.clip(kc - qc, -(NA_COLS - 1), NA_COLS - 1) + (NA_COLS - 1)
    off_row = lax.broadcasted_iota(jnp.int32, (1, width), 1) // GRID_W
    table = jnp.zeros((GRID_W, width), F32)
    for j in range(2 * NA_COLS - 1):
        row_vals = jnp.zeros((1, width), F32)
        for a in range(n_off):
            row_vals = jnp.where(off_row == a, rpb_ref[h, a, j], row_vals)
        table = jnp.where(col_idx == j, row_vals, table)
    cs = jnp.clip(qc - NA_COLS // 2, 0, GRID_W - NA_COLS)
    table = jnp.where((kc >= cs) & (kc < cs + NA_COLS), table * LOG2E, NEG)
    for var in range(N_BIAS_VARIANTS):
        out_ref[var, h] = table[:, var * GRID_W:(var + kh) * GRID_W]


def _na_kernel(rpb_ref, q_ref, k_ref, v_ref, wup_ref, o_ref, wup_bf_ref, bias_ref, *, rows, kh):
    rb = pl.program_id(1)
    nkeys = kh * GRID_W

    @pl.when((pl.program_id(0) == 0) & (rb == 0))
    def _():
        for h in range(N_HEADS_B):
            _expand_na_bias(rpb_ref, bias_ref, h, kh)

    def one_row(rl, carry):
        r = rb * NA_ROWS_PER_STEP + rl
        rs = jnp.clip(r - kh // 2, 0, rows - kh)
        var = rs - r + (NA_ROWS - 1)
        q0 = pl.multiple_of(rl * GRID_W, GRID_W)
        k0 = pl.multiple_of(rs * GRID_W, GRID_W)
        heads = range(N_HEADS_B)
        head_cols = [slice(h * HEAD_DIM, (h + 1) * HEAD_DIM) for h in heads]
        scores = [lax.dot_general(q_ref[pl.ds(q0, GRID_W), c], k_ref[pl.ds(k0, nkeys), c],
                                  (((1,), (1,)), ((), ())), preferred_element_type=F32)
                  for c in head_cols]
        probs, dens = [], []
        for h in heads:
            t = scores[h] * (SCALE * LOG2E) + bias_ref[var, h]
            m = jnp.max(t, axis=-1, keepdims=True)
            p = jnp.exp2(t - m)
            dens.append(jnp.sum(p, axis=-1, keepdims=True))
            probs.append(p.astype(BF16))
        for h in heads:
            o = jnp.dot(probs[h], v_ref[pl.ds(k0, nkeys), head_cols[h]],
                        preferred_element_type=F32) / dens[h]
            o_ref[pl.ds(q0, GRID_W), head_cols[h]] = o.astype(BF16)
        return carry

    lax.fori_loop(0, NA_ROWS_PER_STEP, one_row, 0, unroll=NA_UNROLL)
    wup_bf_ref[...] = wup_ref[...].astype(BF16)


def _neighbourhood(qkv, rpb, batch, seq_total, w_up):
    group = N_GROUPS_A
    assert GROUP_DILATIONS[group] == 1
    rows = seq_total // GRID_W
    kh = min(NA_ROWS, rows)
    assert kh == NA_ROWS and rows % NA_ROWS_PER_STEP == 0
    tq = NA_ROWS_PER_STEP * GRID_W
    steps_per_batch = rows // NA_ROWS_PER_STEP
    view = qkv.reshape(qkv.shape[0], batch, seq_total, qkv.shape[-1])
    wup_rows = pl.BlockSpec((w_up.shape[0] // (batch * steps_per_batch), w_up.shape[1]),
                            lambda b, i: (b * steps_per_batch + i, 0))
    o, w_up_bf = pl.pallas_call(
        functools.partial(_na_kernel, rows=rows, kh=kh),
        grid=(batch, steps_per_batch),
        in_specs=[
            pl.BlockSpec(memory_space=pltpu.SMEM),
            pl.BlockSpec((None, None, tq, GROUP_WIDTH), lambda b, i: (group, b, i, 0)),
            pl.BlockSpec((None, None, seq_total, GROUP_WIDTH), lambda b, i: (group, b, 0, 1)),
            pl.BlockSpec((None, None, seq_total, GROUP_WIDTH), lambda b, i: (group, b, 0, 2)),
            wup_rows,
        ],
        out_specs=[pl.BlockSpec((None, tq, GROUP_WIDTH), lambda b, i: (b, i, 0)), wup_rows],
        out_shape=[jax.ShapeDtypeStruct((batch, seq_total, GROUP_WIDTH), BF16),
                   jax.ShapeDtypeStruct(w_up.shape, BF16)],
        scratch_shapes=[pltpu.VMEM((N_BIAS_VARIANTS, N_HEADS_B, GRID_W, kh * GRID_W), F32)],
        compiler_params=_params(("arbitrary", "arbitrary")),
        name="neighbourhood_attention",
    )(rpb, view, view, view, w_up)
    return o.reshape(batch * seq_total, GROUP_WIDTH), w_up_bf


def _to_token_order_matrix(tm, d):
    t = lax.broadcasted_iota(jnp.int32, (tm, tm), 0)
    c = lax.broadcasted_iota(jnp.int32, (tm, tm), 1)
    return (c == (t % d) * (tm // d) + t // d).astype(BF16)


def _merge_kernel(o0_ref, o1_ref, o2_ref, l0_ref, l1_ref, l2_ref, ob_ref, gate_ref, x_ref,
                  wpa_ref, wpb_ref, wout_ref, gn_ref, wdn_ref,
                  x2_ref, h2_ref, wdn_bf_ref, lse_tok_ref, *, dilations):
    tm = x_ref.shape[0]
    o_tok, lse_tok = [], []
    for g, (o_ref, l_ref, d) in enumerate(zip((o0_ref, o1_ref, o2_ref),
                                              (l0_ref, l1_ref, l2_ref), dilations)):
        o = o_ref[...].reshape(tm, GROUP_WIDTH)
        if d == 1:
            o_tok.append(o.astype(F32))
            lse_tok.append(l_ref[...].reshape(tm, HEAD_DIM))
        else:
            o_tok.append(jnp.dot(_to_token_order_matrix(tm, d), o, preferred_element_type=F32))
            for r in range(d):
                lse_tok_ref[g, pl.ds(r, tm // d, stride=d), :] = l_ref[r]
            lse_tok.append(lse_tok_ref[g])
    l0, l1, l2 = lse_tok
    mx = jnp.maximum(jnp.maximum(l0, l1), l2)
    e0, e1, e2 = jnp.exp(l0 - mx), jnp.exp(l1 - mx), jnp.exp(l2 - mx)
    tot = e0 + e1 + e2
    a0, a1, a2 = e0 / tot, e1 / tot, e2 / tot
    parts = []
    for h in range(HEADS_PER_GROUP):
        cols = slice(h * HEAD_DIM, (h + 1) * HEAD_DIM)
        y = (a0[:, h:h + 1] * o_tok[0][:, cols] + a1[:, h:h + 1] * o_tok[1][:, cols]
             + a2[:, h:h + 1] * o_tok[2][:, cols])
        parts.append(y.astype(BF16))
    ya = jnp.concatenate(parts, axis=1)
    half = tm // MERGE_ROW_CHUNKS
    x2s = []
    for r0 in range(0, tm, half):
        rows = slice(r0, r0 + half)
        ta = jnp.dot(ya[rows], wpa_ref[...], preferred_element_type=F32)
        tb = jnp.dot(ob_ref[rows, :], wpb_ref[...], preferred_element_type=F32)
        merged = (gate_ref[rows, :D_MODEL].astype(F32) * ta
                  + gate_ref[rows, D_MODEL:].astype(F32) * tb)
        x2 = x_ref[rows, :] + jnp.dot(merged.astype(BF16), wout_ref[...],
                                      preferred_element_type=F32)
        x2_ref[rows, :] = x2
        x2s.append(x2)
        if r0 == 0:
            wdn_bf_ref[...] = wdn_ref[...].astype(BF16)
    for r0, x2 in zip(range(0, tm, half), x2s):
        h2_ref[r0:r0 + half, :] = _rms(x2, gn_ref[...]).astype(BF16)


def _merge(o_groups, lse_groups, ob, gates, x2d, seq, wpa, wpb, wout, gn, w_down):
    m = x2d.shape[0]
    tm = MERGE_TM
    n_steps = m // tm
    tiles_per_batch = seq // tm
    dilations = tuple(o.shape[1] for o in o_groups)
    row = lambda w: pl.BlockSpec((tm, w), lambda i: (i, 0))
    wdn_rows = pl.BlockSpec((w_down.shape[0] // n_steps, w_down.shape[1]), lambda i: (i, 0))
    grouped = lambda d, w: pl.BlockSpec(
        (None, d, tm // d, w), lambda i: (i // tiles_per_batch, 0, i % tiles_per_batch, 0))
    const = lambda a, b: pl.BlockSpec((a, b), lambda i: (0, 0), pipeline_mode=pl.Buffered(1))
    return pl.pallas_call(
        functools.partial(_merge_kernel, dilations=dilations),
        grid=(n_steps,),
        in_specs=[grouped(d, GROUP_WIDTH) for d in dilations]
        + [grouped(d, HEAD_DIM) for d in dilations]
        + [row(GROUP_WIDTH), row(2 * D_MODEL), row(D_MODEL),
           const(GROUP_WIDTH, D_MODEL), const(GROUP_WIDTH, D_MODEL), const(D_MODEL, D_MODEL),
           const(1, D_MODEL), wdn_rows],
        out_specs=[row(D_MODEL), row(D_MODEL), wdn_rows],
        out_shape=[jax.ShapeDtypeStruct((m, D_MODEL), F32), jax.ShapeDtypeStruct((m, D_MODEL), BF16),
                   jax.ShapeDtypeStruct(w_down.shape, BF16)],
        scratch_shapes=[pltpu.VMEM((len(dilations), tm, HEAD_DIM), F32)],
        compiler_params=_params(("arbitrary",)),
        name="merge_out_proj",
    )(*o_groups, *lse_groups, ob, gates, x2d, wpa, wpb, wout, gn, w_down)


def _mlp_kernel(h2_ref, wup_ref, wdn_ref, x2_hbm, gf_ref, out_ref, x2_sem, *, n_f):
    i = pl.program_id(0)
    f = pl.program_id(1)
    tm = out_ref.shape[0]

    def residual_copy():
        return pltpu.make_async_copy(x2_hbm.at[pl.ds(i * tm, tm), :], out_ref, x2_sem)

    @pl.when(f == 0)
    def _():
        residual_copy().start()

    hid = jnp.dot(h2_ref[...], wup_ref[...], preferred_element_type=F32)
    hid = jnp.square(jnp.maximum(hid, 0.0)).astype(BF16)

    @pl.when(f == 0)
    def _():
        residual_copy().wait()

    @pl.when(f < n_f - 1)
    def _():
        out_ref[...] += jnp.dot(hid, wdn_ref[...], preferred_element_type=F32)

    @pl.when(f == n_f - 1)
    def _():
        chunk = tm // MLP_FINAL_ROW_CHUNKS
        for r0 in range(0, tm, chunk):
            rows = slice(r0, r0 + chunk)
            y = out_ref[rows, :] + jnp.dot(hid[rows], wdn_ref[...], preferred_element_type=F32)
            out_ref[rows, :] = _rms(y, gf_ref[...])


def _mlp(h2, wup, wdn, x2, gf):
    m = h2.shape[0]
    tm, tf = MLP_TM, MLP_TF
    n_f = D_FF // tf
    return pl.pallas_call(
        functools.partial(_mlp_kernel, n_f=n_f),
        grid=(m // tm, n_f),
        in_specs=[
            pl.BlockSpec((tm, D_MODEL), lambda i, f: (i, 0)),
            pl.BlockSpec((D_MODEL, tf), lambda i, f: (0, f)),
            pl.BlockSpec((tf, D_MODEL), lambda i, f: (f, 0)),
            pl.BlockSpec(memory_space=pl.ANY),
            pl.BlockSpec((1, D_MODEL), lambda i, f: (0, 0)),
        ],
        out_specs=pl.BlockSpec((tm, D_MODEL), lambda i, f: (i, 0)),
        out_shape=jax.ShapeDtypeStruct((m, D_MODEL), F32),
        scratch_shapes=[pltpu.SemaphoreType.DMA],
        compiler_params=_params(("arbitrary", "arbitrary")),
        name="mlp_residual_norm",
    )(h2, wup, wdn, x2, gf)


def _layer(x2d, batch, seq, norm_mix, w_qkv, w_gate, b_gate, rpb, w_proj_a, w_proj_b, w_out,
           norm_mlp, w_up, w_down):
    slopes = 2.0 ** (-8.0 * np.arange(1, N_HEADS_A + 1) / N_HEADS_A)
    row = lambda v: v.reshape(1, -1)
    h_orders = _norm(x2d, row(norm_mix))
    qkv, gates, w_out_bf, w_pa_bf, w_pb_bf = _project(
        h_orders, w_qkv, w_gate, row(b_gate), w_out, w_proj_a, w_proj_b)
    o_groups, lse_groups = [], []
    for g, (window, d) in enumerate(DILATION_PATTERNS):
        o, lse = _dilated_group(qkv, batch, seq, g, window, d, slopes)
        o_groups.append(o)
        lse_groups.append(lse)
    ob, w_up_bf = _neighbourhood(qkv, rpb, batch, seq, w_up)
    x2, h2, w_down_bf = _merge(o_groups, lse_groups, ob, gates, x2d, seq, w_pa_bf, w_pb_bf,
                               w_out_bf, row(norm_mlp), w_down)
    return x2, h2, w_up_bf, w_down_bf


def kernel(x, norm_mix, w_qkv, w_gate, b_gate, rpb, w_proj_a, w_proj_b, w_out, norm_mlp, w_up,
           w_down, norm_final):
    batch, seq, _ = x.shape
    depth = norm_mix.shape[0]
    assert depth == 1 and seq % PERM_TILE == 0
    x2d = x.reshape(batch * seq, D_MODEL)
    x2, h2, w_up_bf, w_down_bf = _layer(
        x2d, batch, seq, norm_mix[0], w_qkv[0], w_gate[0], b_gate[0], rpb[0],
        w_proj_a[0], w_proj_b[0], w_out[0], norm_mlp[0], w_up[0], w_down[0])
    out = _mlp(h2, w_up_bf, w_down_bf, x2, norm_final.reshape(1, -1))
    return out.reshape(batch, seq, D_MODEL)
```

```python
import functools

import jax
import jax.numpy as jnp
import numpy as np
from jax import lax
from jax.experimental import pallas as pl
from jax.experimental.pallas import tpu as pltpu

D_MODEL = 2048
HEAD_DIM = 128
N_HEADS = D_MODEL // HEAD_DIM
N_HEADS_B = N_HEADS // 4
N_HEADS_A = N_HEADS - N_HEADS_B
DILATION_PATTERNS = ((128, 1), (512, 4), (2048, 16))
N_GROUPS_A = len(DILATION_PATTERNS)
HEADS_PER_GROUP = N_HEADS_A // N_GROUPS_A
GROUP_WIDTH = HEADS_PER_GROUP * HEAD_DIM
N_HEAD_GROUPS = N_HEADS // HEADS_PER_GROUP
GRID_W = 64
NA_ROWS = 8
NA_COLS = 16
D_FF = 4 * D_MODEL
EPS = 1e-6
NEG = -1e30
SCALE = HEAD_DIM ** -0.5
LOG2E = float(np.log2(np.e))
LN2 = float(np.log(2.0))
LANES = 128

F32 = jnp.float32
BF16 = jnp.bfloat16

VMEM_LIMIT_BYTES = 56 * 1024 * 1024

PERM_TILE = 1024
NORM_CHUNK = 512
PROJ_TM, PROJ_TN = 2048, GROUP_WIDTH
SIDE_CAST_ROWS = 32
MERGE_TM = 256
MERGE_ROW_CHUNKS = 2
MLP_TM, MLP_TF = 1024, 1024
DIL_QB = 128
DIL_UNROLL = 8
DIL_ROWS_PER_STEP = 1024
N_WINDOW_CASES = 3
NA_ROWS_PER_STEP = 8
NA_UNROLL = 4

GROUP_DILATIONS = tuple(d for _, d in DILATION_PATTERNS) + (1,)
ROW_ORDERS = tuple(sorted(set(GROUP_DILATIONS)))
GROUP_ROW_ORDER = tuple(ROW_ORDERS.index(d) for d in GROUP_DILATIONS)


def _params(sem):
    return pltpu.CompilerParams(dimension_semantics=sem, vmem_limit_bytes=VMEM_LIMIT_BYTES)


def _rms(x, g):
    ms = jnp.mean(x * x, axis=-1, keepdims=True)
    return (x * lax.rsqrt(ms + EPS)) * g


def _norm_kernel(x_ref, g_ref, h_ref, slab_ref):
    tm = x_ref.shape[0]
    x = x_ref[...]
    inv = lax.rsqrt(jnp.mean(x * x, axis=-1, keepdims=True) + EPS)
    n_slabs = NORM_CHUNK // LANES
    for c0 in range(0, D_MODEL, NORM_CHUNK):
        hc = (x_ref[:, c0:c0 + NORM_CHUNK] * inv) * g_ref[:, c0:c0 + NORM_CHUNK]
        h_ref[0, :, c0:c0 + NORM_CHUNK] = hc.astype(BF16)
        for s in range(n_slabs):
            slab_ref[0, s] = hc[:, s * LANES:(s + 1) * LANES]
        for v in range(1, len(ROW_ORDERS)):
            d_prev, d = ROW_ORDERS[v - 1], ROW_ORDERS[v]
            q = d // d_prev
            n_prev, n = tm // d_prev, tm // d
            last = v == len(ROW_ORDERS) - 1
            for s in range(n_slabs):
                cols = slice(c0 + s * LANES, c0 + (s + 1) * LANES)
                for r in range(d_prev):
                    for r2 in range(q):
                        rows = slab_ref[(v - 1) % 2, s, pl.ds(r * n_prev + r2, n, stride=q), :]
                        dst = (d_prev * r2 + r) * n
                        h_ref[v, dst:dst + n, cols] = rows.astype(BF16)
                        if not last:
                            slab_ref[v % 2, s, dst:dst + n, :] = rows


def _norm(x2d, g):
    m = x2d.shape[0]
    tm = PERM_TILE
    nv = len(ROW_ORDERS)
    assert ROW_ORDERS[0] == 1
    return pl.pallas_call(
        _norm_kernel,
        grid=(m // tm,),
        in_specs=[pl.BlockSpec((tm, D_MODEL), lambda i: (i, 0)),
                  pl.BlockSpec((1, D_MODEL), lambda i: (0, 0))],
        out_specs=pl.BlockSpec((nv, tm, D_MODEL), lambda i: (0, i, 0)),
        out_shape=jax.ShapeDtypeStruct((nv, m, D_MODEL), BF16),
        scratch_shapes=[pltpu.VMEM((2, NORM_CHUNK // LANES, tm, LANES), F32)],
        compiler_params=_params(("arbitrary",)),
        name="rmsnorm_row_orders",
    )(x2d, g)


def _proj_kernel(h_ref, w_first_ref, wq_next_ref, wg_next_ref, bg_ref, wout_ref, wpa_ref, wpb_ref,
                 qkv_ref, gate_ref, wout_bf_ref, wpa_bf_ref, wpb_bf_ref, w_bf_ref, *, n_qkv_steps):
    j = pl.program_id(0)
    i = pl.program_id(1)

    @pl.when((j == 0) & (i == 0))
    def _():
        w_bf_ref[0] = w_first_ref[...].astype(BF16)

    def side_casts(next_ref):
        part = next_ref.shape[0]
        rows = pl.ds(pl.multiple_of(i * part, part), part)
        w_bf_ref[(j + 1) % 2, rows, :] = next_ref[...].astype(BF16)
        wout_bf_ref[...] = wout_ref[...].astype(BF16)
        wpa_bf_ref[...] = wpa_ref[...].astype(BF16)
        wpb_bf_ref[...] = wpb_ref[...].astype(BF16)

    def qkv_step(next_ref):
        y = jnp.dot(h_ref[...], w_bf_ref[j % 2], preferred_element_type=F32)
        qkv_ref[...] = y.astype(BF16)
        side_casts(next_ref)

    @pl.when(j < n_qkv_steps - 1)
    def _():
        qkv_step(wq_next_ref)

    @pl.when(j == n_qkv_steps - 1)
    def _():
        qkv_step(wg_next_ref)

    @pl.when(j >= n_qkv_steps)
    def _():
        z = jnp.dot(h_ref[...], w_bf_ref[j % 2], preferred_element_type=F32) + bg_ref[...]
        gate_ref[...] = (0.5 * jnp.tanh(0.5 * z) + 0.5).astype(BF16)
        side_casts(wg_next_ref)


def _project(h_orders, w_qkv, w_gate, b_gate, w_out, w_proj_a, w_proj_b):
    m = h_orders.shape[1]
    tm, tn = PROJ_TM, PROJ_TN
    n_groups = N_HEAD_GROUPS
    nq = 3 * n_groups
    ng = (2 * D_MODEL) // tn
    n_i = m // tm
    row_order = GROUP_ROW_ORDER

    def lhs_map(j, i):
        v = jnp.int32(0)
        for grp in range(n_groups):
            v = jnp.where(j // 3 == grp, row_order[grp], v)
        return (v, i, 0)

    def w_qkv_col(j):
        jj = jnp.minimum(j, nq - 1)
        return (jj % 3) * n_groups + jj // 3

    gate_col = lambda j, i: (0, jnp.maximum(j - nq, 0))
    wq_next_map = lambda j, i: (jnp.where(j + 1 < nq, i, n_i - 1), w_qkv_col(j + 1))
    wg_next_map = lambda j, i: (jnp.where(j + 1 >= nq, i, 0), jnp.clip(j + 1 - nq, 0, ng - 1))

    def qkv_out_map(j, i):
        jj = jnp.minimum(j, nq - 1)
        return (jj // 3, jnp.where(j < nq, i, n_i - 1), jj % 3)

    gate_out_map = lambda j, i: (jnp.where(j >= nq, i, 0), jnp.maximum(j - nq, 0))

    def side_rows(w):
        n_blocks = w.shape[0] // SIDE_CAST_ROWS
        return pl.BlockSpec((SIDE_CAST_ROWS, w.shape[1]),
                            lambda j, i: (jnp.minimum(j * n_i + i, n_blocks - 1), 0))

    side = [w_out, w_proj_a, w_proj_b]
    assert all(w.shape[0] // SIDE_CAST_ROWS <= (nq + ng) * n_i for w in side)
    return pl.pallas_call(
        functools.partial(_proj_kernel, n_qkv_steps=nq),
        grid=(nq + ng, n_i),
        in_specs=[
            pl.BlockSpec((None, tm, D_MODEL), lhs_map),
            pl.BlockSpec((D_MODEL, tn), lambda j, i: (0, 0), pipeline_mode=pl.Buffered(1)),
            pl.BlockSpec((D_MODEL // n_i, tn), wq_next_map),
            pl.BlockSpec((D_MODEL // n_i, tn), wg_next_map),
            pl.BlockSpec((1, tn), gate_col),
        ] + [side_rows(w) for w in side],
        out_specs=[
            pl.BlockSpec((None, tm, tn), qkv_out_map),
            pl.BlockSpec((tm, tn), gate_out_map),
        ] + [side_rows(w) for w in side],
        out_shape=[
            jax.ShapeDtypeStruct((n_groups, m, 3 * GROUP_WIDTH), BF16),
            jax.ShapeDtypeStruct((m, 2 * D_MODEL), BF16),
        ] + [jax.ShapeDtypeStruct(w.shape, BF16) for w in side],
        scratch_shapes=[pltpu.VMEM((2, D_MODEL, tn), BF16)],
        compiler_params=_params(("arbitrary", "arbitrary")),
        name="proj_qkv_gate",
    )(h_orders, w_qkv, w_qkv, w_gate, b_gate, *side)


def _dilated_block_shape(seq, half_window):
    kw = DIL_QB + 2 * half_window
    return (seq, seq) if seq <= kw else (DIL_QB, kw)


def _dilated_kernel(q_ref, k_ref, v_ref, o_ref, lse_ref, bias_ref, *, seq, half_window, coefs):
    qb, kw = _dilated_block_shape(seq, half_window)
    n_blocks = seq // qb
    n_tiles, n_res = q_ref.shape[:2]

    def rows_loader(ref):
        if n_res == 1:
            flat = ref.at[:, 0].reshape(seq, GROUP_WIDTH)
            return lambda rr, start, size, cols: flat[pl.ds(start, size), cols]
        assert n_blocks == 1
        return lambda rr, start, size, cols: jnp.concatenate(
            [ref[t, rr, :, cols] for t in range(n_tiles)], axis=0)

    load_q, load_k, load_v = rows_loader(q_ref), rows_loader(k_ref), rows_loader(v_ref)
    lane = lax.broadcasted_iota(jnp.int32, (qb, HEAD_DIM), 1)
    heads = range(HEADS_PER_GROUP)
    head_cols = [slice(h * HEAD_DIM, (h + 1) * HEAD_DIM) for h in heads]

    @pl.when((pl.program_id(0) == 0) & (pl.program_id(1) == 0))
    def _():
        qrow = lax.broadcasted_iota(jnp.int32, (qb, kw), 0)
        kcol = lax.broadcasted_iota(jnp.int32, (qb, kw), 1)
        for c in range(bias_ref.shape[0]):
            dist = jnp.abs(kcol - qrow - c * half_window)
            dist_f = dist.astype(F32)
            for h in heads:
                bias_ref[c, h] = jnp.where(dist <= half_window, (-coefs[h] * LOG2E) * dist_f, NEG)

    def block(qi, carry):
        q0 = pl.multiple_of(qi * qb, qb)
        ks = pl.multiple_of(jnp.clip(q0 - half_window, 0, seq - kw), half_window)
        window_case = (q0 - ks) // half_window
        pairs = [(rr, h) for rr in range(n_res) for h in heads]
        scores = [lax.dot_general(load_q(rr, q0, qb, head_cols[h]), load_k(rr, ks, kw, head_cols[h]),
                                  (((1,), (1,)), ((), ())), preferred_element_type=F32)
                  for rr, h in pairs]
        probs, dens = [], []
        lse_all = [jnp.zeros((qb, HEAD_DIM), F32) for _ in range(n_res)]
        for (rr, h), s in zip(pairs, scores):
            t = s * (SCALE * LOG2E) + bias_ref[window_case, h]
            m = jnp.max(t, axis=-1, keepdims=True)
            p = jnp.exp2(t - m)
            den = jnp.sum(p, axis=-1, keepdims=True)
            probs.append(p.astype(BF16))
            dens.append(den)
            lse_all[rr] = jnp.where(lane == h, m * LN2 + jnp.log(den), lse_all[rr])
        for (rr, h), p, den in zip(pairs, probs, dens):
            o = jnp.dot(p, load_v(rr, ks, kw, head_cols[h]), preferred_element_type=F32) / den
            o_ref[rr, pl.ds(q0, qb), head_cols[h]] = o.astype(BF16)
        for rr in range(n_res):
            lse_ref[rr, pl.ds(q0, qb), :] = lse_all[rr]
        return carry

    lax.fori_loop(0, n_blocks, block, 0, unroll=min(DIL_UNROLL, n_blocks))


def _dilated_group(qkv, batch, seq_total, group, window, dilation, slopes):
    d = dilation
    seq = seq_total // d
    half_window = window // (2 * d)
    qb, kw = _dilated_block_shape(seq, half_window)
    coefs = tuple(float(slopes[group * HEADS_PER_GROUP + h]) * d for h in range(HEADS_PER_GROUP))
    tiles = seq_total // PERM_TILE
    rows = PERM_TILE // d
    view = qkv.reshape(qkv.shape[0], batch, tiles, d, rows, qkv.shape[-1])
    n_res = min(d, max(1, DIL_ROWS_PER_STEP // seq))
    part = lambda which: pl.BlockSpec((None, None, tiles, n_res, rows, GROUP_WIDTH),
                                      lambda b, r: (group, b, 0, r, 0, which))
    return pl.pallas_call(
        functools.partial(_dilated_kernel, seq=seq, half_window=half_window, coefs=coefs),
        grid=(batch, d // n_res),
        in_specs=[part(0), part(1), part(2)],
        out_specs=[
            pl.BlockSpec((None, n_res, seq, GROUP_WIDTH), lambda b, r: (b, r, 0, 0)),
            pl.BlockSpec((None, n_res, seq, HEAD_DIM), lambda b, r: (b, r, 0, 0)),
        ],
        out_shape=[
            jax.ShapeDtypeStruct((batch, d, seq, GROUP_WIDTH), BF16),
            jax.ShapeDtypeStruct((batch, d, seq, HEAD_DIM), F32),
        ],
        scratch_shapes=[pltpu.VMEM((N_WINDOW_CASES, HEADS_PER_GROUP, qb, kw), F32)],
        compiler_params=_params(("arbitrary", "arbitrary")),
        name=f"dilated_attention_d{d}",
    )(view, view, view)


N_BIAS_VARIANTS = NA_ROWS


def _expand_na_bias(rpb_ref, out_ref, h, kh):
    n_off = 2 * NA_ROWS - 1
    width = (n_off + 1) * GRID_W
    qc = lax.broadcasted_iota(jnp.int32, (GRID_W, width), 0)
    kc = lax.broadcasted_iota(jnp.int32, (GRID_W, width), 1) % GRID_W
    col_idx = jnp.clip(kc - qc, -(NA_COLS - 1), NA_COLS - 1) + (NA_COLS - 1)
    off_row = lax.broadcasted_iota(jnp.int32, (1, width), 1) // GRID_W
    table = jnp.zeros((GRID_W, width), F32)
    for j in range(2 * NA_COLS - 1):
        row_vals = jnp.zeros((1, width), F32)
        for a in range(n_off):
            row_vals = jnp.where(off_row == a, rpb_ref[h, a, j], row_vals)
        table = jnp.where(col_idx == j, row_vals, table)
    cs = jnp.clip(qc - NA_COLS // 2, 0, GRID_W - NA_COLS)
    table = jnp.where((kc >= cs) & (kc < cs + NA_COLS), table * LOG2E, NEG)
    for var in range(N_BIAS_VARIANTS):
        out_ref[var, h] = table[:, var * GRID_W:(var + kh) * GRID_W]


def _na_kernel(rpb_ref, q_ref, k_ref, v_ref, wup_ref, o_ref, wup_bf_ref, bias_ref, *, rows, kh):
    rb = pl.program_id(1)
    nkeys = kh * GRID_W

    @pl.when((pl.program_id(0) == 0) & (rb == 0))
    def _():
        for h in range(N_HEADS_B):
            _expand_na_bias(rpb_ref, bias_ref, h, kh)

    def one_row(rl, carry):
        r = rb * NA_ROWS_PER_STEP + rl
        rs = jnp.clip(r - kh // 2, 0, rows - kh)
        var = rs - r + (NA_ROWS - 1)
        q0 = pl.multiple_of(rl * GRID_W, GRID_W)
        k0 = pl.multiple_of(rs * GRID_W, GRID_W)
        heads = range(N_HEADS_B)
        head_cols = [slice(h * HEAD_DIM, (h + 1) * HEAD_DIM) for h in heads]
        scores = [lax.dot_general(q_ref[pl.ds(q0, GRID_W), c], k_ref[pl.ds(k0, nkeys), c],
                                  (((1,), (1,)), ((), ())), preferred_element_type=F32)
                  for c in head_cols]
        probs, dens = [], []
        for h in heads:
            t = scores[h] * (SCALE * LOG2E) + bias_ref[var, h]
            m = jnp.max(t, axis=-1, keepdims=True)
            p = jnp.exp2(t - m)
            dens.append(jnp.sum(p, axis=-1, keepdims=True))
            probs.append(p.astype(BF16))
        for h in heads:
            o = jnp.dot(probs[h], v_ref[pl.ds(k0, nkeys), head_cols[h]],
                        preferred_element_type=F32) / dens[h]
            o_ref[pl.ds(q0, GRID_W), head_cols[h]] = o.astype(BF16)
        return carry

    lax.fori_loop(0, NA_ROWS_PER_STEP, one_row, 0, unroll=NA_UNROLL)
    wup_bf_ref[...] = wup_ref[...].astype(BF16)


def _neighbourhood(qkv, rpb, batch, seq_total, w_up):
    group = N_GROUPS_A
    assert GROUP_DILATIONS[group] == 1
    rows = seq_total // GRID_W
    kh = min(NA_ROWS, rows)
    assert kh == NA_ROWS and rows % NA_ROWS_PER_STEP == 0
    tq = NA_ROWS_PER_STEP * GRID_W
    steps_per_batch = rows // NA_ROWS_PER_STEP
    view = qkv.reshape(qkv.shape[0], batch, seq_total, qkv.shape[-1])
    wup_rows = pl.BlockSpec((w_up.shape[0] // (batch * steps_per_batch), w_up.shape[1]),
                            lambda b, i: (b * steps_per_batch + i, 0))
    o, w_up_bf = pl.pallas_call(
        functools.partial(_na_kernel, rows=rows, kh=kh),
        grid=(batch, steps_per_batch),
        in_specs=[
            pl.BlockSpec(memory_space=pltpu.SMEM),
            pl.BlockSpec((None, None, tq, GROUP_WIDTH), lambda b, i: (group, b, i, 0)),
            pl.BlockSpec((None, None, seq_total, GROUP_WIDTH), lambda b, i: (group, b, 0, 1)),
            pl.BlockSpec((None, None, seq_total, GROUP_WIDTH), lambda b, i: (group, b, 0, 2)),
            wup_rows,
        ],
        out_specs=[pl.BlockSpec((None, tq, GROUP_WIDTH), lambda b, i: (b, i, 0)), wup_rows],
        out_shape=[jax.ShapeDtypeStruct((batch, seq_total, GROUP_WIDTH), BF16),
                   jax.ShapeDtypeStruct(w_up.shape, BF16)],
        scratch_shapes=[pltpu.VMEM((N_BIAS_VARIANTS, N_HEADS_B, GRID_W, kh * GRID_W), F32)],
        compiler_params=_params(("arbitrary", "arbitrary")),
        name="neighbourhood_attention",
    )(rpb, view, view, view, w_up)
    return o.reshape(batch * seq_total, GROUP_WIDTH), w_up_bf


def _to_token_order_matrix(tm, d):
    t = lax.broadcasted_iota(jnp.int32, (tm, tm), 0)
    c = lax.broadcasted_iota(jnp.int32, (tm, tm), 1)
    return (c == (t % d) * (tm // d) + t // d).astype(BF16)


def _merge_kernel(o0_ref, o1_ref, o2_ref, l0_ref, l1_ref, l2_ref, ob_ref, gate_ref, x_ref,
                  wpa_ref, wpb_ref, wout_ref, gn_ref, wdn_ref,
                  x2_ref, h2_ref, wdn_bf_ref, lse_tok_ref, *, dilations):
    tm = x_ref.shape[0]
    o_tok, lse_tok = [], []
    for g, (o_ref, l_ref, d) in enumerate(zip((o0_ref, o1_ref, o2_ref),
                                              (l0_ref, l1_ref, l2_ref), dilations)):
        o = o_ref[...].reshape(tm, GROUP_WIDTH)
        if d == 1:
            o_tok.append(o.astype(F32))
            lse_tok.append(l_ref[...].reshape(tm, HEAD_DIM))
        else:
            o_tok.append(jnp.dot(_to_token_order_matrix(tm, d), o, preferred_element_type=F32))
            for r in range(d):
                lse_tok_ref[g, pl.ds(r, tm // d, stride=d), :] = l_ref[r]
            lse_tok.append(lse_tok_ref[g])
    l0, l1, l2 = lse_tok
    mx = jnp.maximum(jnp.maximum(l0, l1), l2)
    e0, e1, e2 = jnp.exp(l0 - mx), jnp.exp(l1 - mx), jnp.exp(l2 - mx)
    tot = e0 + e1 + e2
    a0, a1, a2 = e0 / tot, e1 / tot, e2 / tot
    parts = []
    for h in range(HEADS_PER_GROUP):
        cols = slice(h * HEAD_DIM, (h + 1) * HEAD_DIM)
        y = (a0[:, h:h + 1] * o_tok[0][:, cols] + a1[:, h:h + 1] * o_tok[1][:, cols]
             + a2[:, h:h + 1] * o_tok[2][:, cols])
        parts.append(y.astype(BF16))
    ya = jnp.concatenate(parts, axis=1)
    half = tm // MERGE_ROW_CHUNKS
    x2s = []
    for r0 in range(0, tm, half):
        rows = slice(r0, r0 + half)
        ta = jnp.dot(ya[rows], wpa_ref[...], preferred_element_type=F32)
        tb = jnp.dot(ob_ref[rows, :], wpb_ref[...], preferred_element_type=F32)
        merged = (gate_ref[rows, :D_MODEL].astype(F32) * ta
                  + gate_ref[rows, D_MODEL:].astype(F32) * tb)
        x2 = x_ref[rows, :] + jnp.dot(merged.astype(BF16), wout_ref[...],
                                      preferred_element_type=F32)
        x2_ref[rows, :] = x2
        x2s.append(x2)
        if r0 == 0:
            wdn_bf_ref[...] = wdn_ref[...].astype(BF16)
    for r0, x2 in zip(range(0, tm, half), x2s):
        h2_ref[r0:r0 + half, :] = _rms(x2, gn_ref[...]).astype(BF16)


def _merge(o_groups, lse_groups, ob, gates, x2d, seq, wpa, wpb, wout, gn, w_down):
    m = x2d.shape[0]
    tm = MERGE_TM
    n_steps = m // tm
    tiles_per_batch = seq // tm
    dilations = tuple(o.shape[1] for o in o_groups)
    row = lambda w: pl.BlockSpec((tm, w), lambda i: (i, 0))
    wdn_rows = pl.BlockSpec((w_down.shape[0] // n_steps, w_down.shape[1]), lambda i: (i, 0))
    grouped = lambda d, w: pl.BlockSpec(
        (None, d, tm // d, w), lambda i: (i // tiles_per_batch, 0, i % tiles_per_batch, 0))
    const = lambda a, b: pl.BlockSpec((a, b), lambda i: (0, 0), pipeline_mode=pl.Buffered(1))
    return pl.pallas_call(
        functools.partial(_merge_kernel, dilations=dilations),
        grid=(n_steps,),
        in_specs=[grouped(d, GROUP_WIDTH) for d in dilations]
        + [grouped(d, HEAD_DIM) for d in dilations]
        + [row(GROUP_WIDTH), row(2 * D_MODEL), row(D_MODEL),
           const(GROUP_WIDTH, D_MODEL), const(GROUP_WIDTH, D_MODEL), const(D_MODEL, D_MODEL),
           const(1, D_MODEL), wdn_rows],
        out_specs=[row(D_MODEL), row(D_MODEL), wdn_rows],
        out_shape=[jax.ShapeDtypeStruct((m, D_MODEL), F32), jax.ShapeDtypeStruct((m, D_MODEL), BF16),
                   jax.ShapeDtypeStruct(w_down.shape, BF16)],
        scratch_shapes=[pltpu.VMEM((len(dilations), tm, HEAD_DIM), F32)],
        compiler_params=_params(("arbitrary",)),
        name="merge_out_proj",
    )(*o_groups, *lse_groups, ob, gates, x2d, wpa, wpb, wout, gn, w_down)


def _mlp_kernel(h2_ref, wup_ref, wdn_ref, x2_hbm, gf_ref, out_ref, x2_sem, *, n_f):
    i = pl.program_id(0)
    f = pl.program_id(1)
    tm = out_ref.shape[0]

    def residual_copy():
        return pltpu.make_async_copy(x2_hbm.at[pl.ds(i * tm, tm), :], out_ref, x2_sem)

    @pl.when(f == 0)
    def _():
        residual_copy().start()

    hid = jnp.dot(h2_ref[...], wup_ref[...], preferred_element_type=F32)
    hid = jnp.square(jnp.maximum(hid, 0.0)).astype(BF16)

    @pl.when(f == 0)
    def _():
        residual_copy().wait()

    out_ref[...] += jnp.dot(hid, wdn_ref[...], preferred_element_type=F32)

    @pl.when(f == n_f - 1)
    def _():
        out_ref[...] = _rms(out_ref[...], gf_ref[...])


def _mlp(h2, wup, wdn, x2, gf):
    m = h2.shape[0]
    tm, tf = MLP_TM, MLP_TF
    n_f = D_FF // tf
    return pl.pallas_call(
        functools.partial(_mlp_kernel, n_f=n_f),
        grid=(m // tm, n_f),
        in_specs=[
            pl.BlockSpec((tm, D_MODEL), lambda i, f: (i, 0)),
            pl.BlockSpec((D_MODEL, tf), lambda i, f: (0, f)),
            pl.BlockSpec((tf, D_MODEL), lambda i, f: (f, 0)),
            pl.BlockSpec(memory_space=pl.ANY),
            pl.BlockSpec((1, D_MODEL), lambda i, f: (0, 0)),
        ],
        out_specs=pl.BlockSpec((tm, D_MODEL), lambda i, f: (i, 0)),
        out_shape=jax.ShapeDtypeStruct((m, D_MODEL), F32),
        scratch_shapes=[pltpu.SemaphoreType.DMA],
        compiler_params=_params(("arbitrary", "arbitrary")),
        name="mlp_residual_norm",
    )(h2, wup, wdn, x2, gf)


def _layer(x2d, batch, seq, norm_mix, w_qkv, w_gate, b_gate, rpb, w_proj_a, w_proj_b, w_out,
           norm_mlp, w_up, w_down):
    slopes = 2.0 ** (-8.0 * np.arange(1, N_HEADS_A + 1) / N_HEADS_A)
    row = lambda v: v.reshape(1, -1)
    h_orders = _norm(x2d, row(norm_mix))
    qkv, gates, w_out_bf, w_pa_bf, w_pb_bf = _project(
        h_orders, w_qkv, w_gate, row(b_gate), w_out, w_proj_a, w_proj_b)
    o_groups, lse_groups = [], []
    for g, (window, d) in enumerate(DILATION_PATTERNS):
        o, lse = _dilated_group(qkv, batch, seq, g, window, d, slopes)
        o_groups.append(o)
        lse_groups.append(lse)
    ob, w_up_bf = _neighbourhood(qkv, rpb, batch, seq, w_up)
    x2, h2, w_down_bf = _merge(o_groups, lse_groups, ob, gates, x2d, seq, w_pa_bf, w_pb_bf,
                               w_out_bf, row(norm_mlp), w_down)
    return x2, h2, w_up_bf, w_down_bf


def kernel(x, norm_mix, w_qkv, w_gate, b_gate, rpb, w_proj_a, w_proj_b, w_out, norm_mlp, w_up,
           w_down, norm_final):
    batch, seq, _ = x.shape
    depth = norm_mix.shape[0]
    assert depth == 1 and seq % PERM_TILE == 0
    x2d = x.reshape(batch * seq, D_MODEL)
    x2, h2, w_up_bf, w_down_bf = _layer(
        x2d, batch, seq, norm_mix[0], w_qkv[0], w_gate[0], b_gate[0], rpb[0],
        w_proj_a[0], w_proj_b[0], w_out[0], norm_mlp[0], w_up[0], w_down[0])
    out = _mlp(h2, w_up_bf, w_down_bf, x2, norm_final.reshape(1, -1))
    return out.reshape(batch, seq, D_MODEL)
```

```python
import functools

import jax
import jax.numpy as jnp
import numpy as np
from jax import lax
from jax.experimental import pallas as pl
from jax.experimental.pallas import tpu as pltpu

D_MODEL = 2048
HEAD_DIM = 128
N_HEADS = D_MODEL // HEAD_DIM
N_HEADS_B = N_HEADS // 4
N_HEADS_A = N_HEADS - N_HEADS_B
DILATION_PATTERNS = ((128, 1), (512, 4), (2048, 16))
N_GROUPS_A = len(DILATION_PATTERNS)
HEADS_PER_GROUP = N_HEADS_A // N_GROUPS_A
GROUP_WIDTH = HEADS_PER_GROUP * HEAD_DIM
N_HEAD_GROUPS = N_HEADS // HEADS_PER_GROUP
GRID_W = 64
NA_ROWS = 8
NA_COLS = 16
D_FF = 4 * D_MODEL
EPS = 1e-6
NEG = -1e30
SCALE = HEAD_DIM ** -0.5
LOG2E = float(np.log2(np.e))
LN2 = float(np.log(2.0))
LANES = 128

F32 = jnp.float32
BF16 = jnp.bfloat16

VMEM_LIMIT_BYTES = 56 * 1024 * 1024

PERM_TILE = 1024
NORM_CHUNK = 512
PROJ_TM, PROJ_TN = 2048, GROUP_WIDTH
SIDE_CAST_ROWS = 32
MERGE_TM = 256
MERGE_ROW_CHUNKS = 2
MLP_TM, MLP_TF = 1024, 1024
DIL_QB = 128
DIL_UNROLL = 8
DIL_ROWS_PER_STEP = 1024
N_WINDOW_CASES = 3
NA_ROWS_PER_STEP = 8
NA_UNROLL = 2

GROUP_DILATIONS = tuple(d for _, d in DILATION_PATTERNS) + (1,)
ROW_ORDERS = tuple(sorted(set(GROUP_DILATIONS)))
GROUP_ROW_ORDER = tuple(ROW_ORDERS.index(d) for d in GROUP_DILATIONS)


def _params(sem):
    return pltpu.CompilerParams(dimension_semantics=sem, vmem_limit_bytes=VMEM_LIMIT_BYTES)


def _rms(x, g):
    ms = jnp.mean(x * x, axis=-1, keepdims=True)
    return (x * lax.rsqrt(ms + EPS)) * g


def _norm_kernel(x_ref, g_ref, h_ref, slab_ref):
    tm = x_ref.shape[0]
    x = x_ref[...]
    inv = lax.rsqrt(jnp.mean(x * x, axis=-1, keepdims=True) + EPS)
    n_slabs = NORM_CHUNK // LANES
    for c0 in range(0, D_MODEL, NORM_CHUNK):
        hc = (x_ref[:, c0:c0 + NORM_CHUNK] * inv) * g_ref[:, c0:c0 + NORM_CHUNK]
        h_ref[0, :, c0:c0 + NORM_CHUNK] = hc.astype(BF16)
        for s in range(n_slabs):
            slab_ref[0, s] = hc[:, s * LANES:(s + 1) * LANES]
        for v in range(1, len(ROW_ORDERS)):
            d_prev, d = ROW_ORDERS[v - 1], ROW_ORDERS[v]
            q = d // d_prev
            n_prev, n = tm // d_prev, tm // d
            last = v == len(ROW_ORDERS) - 1
            for s in range(n_slabs):
                cols = slice(c0 + s * LANES, c0 + (s + 1) * LANES)
                for r in range(d_prev):
                    for r2 in range(q):
                        rows = slab_ref[(v - 1) % 2, s, pl.ds(r * n_prev + r2, n, stride=q), :]
                        dst = (d_prev * r2 + r) * n
                        h_ref[v, dst:dst + n, cols] = rows.astype(BF16)
                        if not last:
                            slab_ref[v % 2, s, dst:dst + n, :] = rows


def _norm(x2d, g):
    m = x2d.shape[0]
    tm = PERM_TILE
    nv = len(ROW_ORDERS)
    assert ROW_ORDERS[0] == 1
    return pl.pallas_call(
        _norm_kernel,
        grid=(m // tm,),
        in_specs=[pl.BlockSpec((tm, D_MODEL), lambda i: (i, 0)),
                  pl.BlockSpec((1, D_MODEL), lambda i: (0, 0))],
        out_specs=pl.BlockSpec((nv, tm, D_MODEL), lambda i: (0, i, 0)),
        out_shape=jax.ShapeDtypeStruct((nv, m, D_MODEL), BF16),
        scratch_shapes=[pltpu.VMEM((2, NORM_CHUNK // LANES, tm, LANES), F32)],
        compiler_params=_params(("arbitrary",)),
        name="rmsnorm_row_orders",
    )(x2d, g)


def _proj_kernel(h_ref, w_first_ref, wq_next_ref, wg_next_ref, bg_ref, wout_ref, wpa_ref, wpb_ref,
                 qkv_ref, gate_ref, wout_bf_ref, wpa_bf_ref, wpb_bf_ref, w_bf_ref, *, n_qkv_steps):
    j = pl.program_id(0)
    i = pl.program_id(1)

    @pl.when((j == 0) & (i == 0))
    def _():
        w_bf_ref[0] = w_first_ref[...].astype(BF16)

    def side_casts(next_ref):
        part = next_ref.shape[0]
        rows = pl.ds(pl.multiple_of(i * part, part), part)
        w_bf_ref[(j + 1) % 2, rows, :] = next_ref[...].astype(BF16)
        wout_bf_ref[...] = wout_ref[...].astype(BF16)
        wpa_bf_ref[...] = wpa_ref[...].astype(BF16)
        wpb_bf_ref[...] = wpb_ref[...].astype(BF16)

    def qkv_step(next_ref):
        y = jnp.dot(h_ref[...], w_bf_ref[j % 2], preferred_element_type=F32)
        qkv_ref[...] = y.astype(BF16)
        side_casts(next_ref)

    @pl.when(j < n_qkv_steps - 1)
    def _():
        qkv_step(wq_next_ref)

    @pl.when(j == n_qkv_steps - 1)
    def _():
        qkv_step(wg_next_ref)

    @pl.when(j >= n_qkv_steps)
    def _():
        z = jnp.dot(h_ref[...], w_bf_ref[j % 2], preferred_element_type=F32) + bg_ref[...]
        gate_ref[...] = (0.5 * jnp.tanh(0.5 * z) + 0.5).astype(BF16)
        side_casts(wg_next_ref)


def _project(h_orders, w_qkv, w_gate, b_gate, w_out, w_proj_a, w_proj_b):
    m = h_orders.shape[1]
    tm, tn = PROJ_TM, PROJ_TN
    n_groups = N_HEAD_GROUPS
    nq = 3 * n_groups
    ng = (2 * D_MODEL) // tn
    n_i = m // tm
    row_order = GROUP_ROW_ORDER

    def lhs_map(j, i):
        v = jnp.int32(0)
        for grp in range(n_groups):
            v = jnp.where(j // 3 == grp, row_order[grp], v)
        return (v, i, 0)

    def w_qkv_col(j):
        jj = jnp.minimum(j, nq - 1)
        return (jj % 3) * n_groups + jj // 3

    gate_col = lambda j, i: (0, jnp.maximum(j - nq, 0))
    wq_next_map = lambda j, i: (jnp.where(j + 1 < nq, i, n_i - 1), w_qkv_col(j + 1))
    wg_next_map = lambda j, i: (jnp.where(j + 1 >= nq, i, 0), jnp.clip(j + 1 - nq, 0, ng - 1))

    def qkv_out_map(j, i):
        jj = jnp.minimum(j, nq - 1)
        return (jj // 3, jnp.where(j < nq, i, n_i - 1), jj % 3)

    gate_out_map = lambda j, i: (jnp.where(j >= nq, i, 0), jnp.maximum(j - nq, 0))

    def side_rows(w):
        n_blocks = w.shape[0] // SIDE_CAST_ROWS
        return pl.BlockSpec((SIDE_CAST_ROWS, w.shape[1]),
                            lambda j, i: (jnp.minimum(j * n_i + i, n_blocks - 1), 0))

    side = [w_out, w_proj_a, w_proj_b]
    assert all(w.shape[0] // SIDE_CAST_ROWS <= (nq + ng) * n_i for w in side)
    return pl.pallas_call(
        functools.partial(_proj_kernel, n_qkv_steps=nq),
        grid=(nq + ng, n_i),
        in_specs=[
            pl.BlockSpec((None, tm, D_MODEL), lhs_map),
            pl.BlockSpec((D_MODEL, tn), lambda j, i: (0, 0), pipeline_mode=pl.Buffered(1)),
            pl.BlockSpec((D_MODEL // n_i, tn), wq_next_map),
            pl.BlockSpec((D_MODEL // n_i, tn), wg_next_map),
            pl.BlockSpec((1, tn), gate_col),
        ] + [side_rows(w) for w in side],
        out_specs=[
            pl.BlockSpec((None, tm, tn), qkv_out_map),
            pl.BlockSpec((tm, tn), gate_out_map),
        ] + [side_rows(w) for w in side],
        out_shape=[
            jax.ShapeDtypeStruct((n_groups, m, 3 * GROUP_WIDTH), BF16),
            jax.ShapeDtypeStruct((m, 2 * D_MODEL), BF16),
        ] + [jax.ShapeDtypeStruct(w.shape, BF16) for w in side],
        scratch_shapes=[pltpu.VMEM((2, D_MODEL, tn), BF16)],
        compiler_params=_params(("arbitrary", "arbitrary")),
        name="proj_qkv_gate",
    )(h_orders, w_qkv, w_qkv, w_gate, b_gate, *side)


def _dilated_block_shape(seq, half_window):
    kw = DIL_QB + 2 * half_window
    return (seq, seq) if seq <= kw else (DIL_QB, kw)


def _dilated_kernel(q_ref, k_ref, v_ref, o_ref, lse_ref, bias_ref, *, seq, half_window, coefs):
    qb, kw = _dilated_block_shape(seq, half_window)
    n_blocks = seq // qb
    n_tiles, n_res = q_ref.shape[:2]

    def rows_loader(ref):
        if n_res == 1:
            flat = ref.at[:, 0].reshape(seq, GROUP_WIDTH)
            return lambda rr, start, size, cols: flat[pl.ds(start, size), cols]
        assert n_blocks == 1
        return lambda rr, start, size, cols: jnp.concatenate(
            [ref[t, rr, :, cols] for t in range(n_tiles)], axis=0)

    load_q, load_k, load_v = rows_loader(q_ref), rows_loader(k_ref), rows_loader(v_ref)
    lane = lax.broadcasted_iota(jnp.int32, (qb, HEAD_DIM), 1)
    heads = range(HEADS_PER_GROUP)
    head_cols = [slice(h * HEAD_DIM, (h + 1) * HEAD_DIM) for h in heads]

    @pl.when((pl.program_id(0) == 0) & (pl.program_id(1) == 0))
    def _():
        qrow = lax.broadcasted_iota(jnp.int32, (qb, kw), 0)
        kcol = lax.broadcasted_iota(jnp.int32, (qb, kw), 1)
        for c in range(bias_ref.shape[0]):
            dist = jnp.abs(kcol - qrow - c * half_window)
            dist_f = dist.astype(F32)
            for h in heads:
                bias_ref[c, h] = jnp.where(dist <= half_window, (-coefs[h] * LOG2E) * dist_f, NEG)

    def block(qi, carry):
        q0 = pl.multiple_of(qi * qb, qb)
        ks = pl.multiple_of(jnp.clip(q0 - half_window, 0, seq - kw), half_window)
        window_case = (q0 - ks) // half_window
        pairs = [(rr, h) for rr in range(n_res) for h in heads]
        scores = [lax.dot_general(load_q(rr, q0, qb, head_cols[h]), load_k(rr, ks, kw, head_cols[h]),
                                  (((1,), (1,)), ((), ())), preferred_element_type=F32)
                  for rr, h in pairs]
        probs, dens = [], []
        lse_all = [jnp.zeros((qb, HEAD_DIM), F32) for _ in range(n_res)]
        for (rr, h), s in zip(pairs, scores):
            t = s * (SCALE * LOG2E) + bias_ref[window_case, h]
            m = jnp.max(t, axis=-1, keepdims=True)
            p = jnp.exp2(t - m)
            den = jnp.sum(p, axis=-1, keepdims=True)
            probs.append(p.astype(BF16))
            dens.append(den)
            lse_all[rr] = jnp.where(lane == h, m * LN2 + jnp.log(den), lse_all[rr])
        for (rr, h), p, den in zip(pairs, probs, dens):
            o = jnp.dot(p, load_v(rr, ks, kw, head_cols[h]), preferred_element_type=F32) / den
            o_ref[rr, pl.ds(q0, qb), head_cols[h]] = o.astype(BF16)
        for rr in range(n_res):
            lse_ref[rr, pl.ds(q0, qb), :] = lse_all[rr]
        return carry

    lax.fori_loop(0, n_blocks, block, 0, unroll=min(DIL_UNROLL, n_blocks))


def _dilated_group(qkv, batch, seq_total, group, window, dilation, slopes):
    d = dilation
    seq = seq_total // d
    half_window = window // (2 * d)
    qb, kw = _dilated_block_shape(seq, half_window)
    coefs = tuple(float(slopes[group * HEADS_PER_GROUP + h]) * d for h in range(HEADS_PER_GROUP))
    tiles = seq_total // PERM_TILE
    rows = PERM_TILE // d
    view = qkv.reshape(qkv.shape[0], batch, tiles, d, rows, qkv.shape[-1])
    n_res = min(d, max(1, DIL_ROWS_PER_STEP // seq))
    part = lambda which: pl.BlockSpec((None, None, tiles, n_res, rows, GROUP_WIDTH),
                                      lambda b, r: (group, b, 0, r, 0, which))
    return pl.pallas_call(
        functools.partial(_dilated_kernel, seq=seq, half_window=half_window, coefs=coefs),
        grid=(batch, d // n_res),
        in_specs=[part(0), part(1), part(2)],
        out_specs=[
            pl.BlockSpec((None, n_res, seq, GROUP_WIDTH), lambda b, r: (b, r, 0, 0)),
            pl.BlockSpec((None, n_res, seq, HEAD_DIM), lambda b, r: (b, r, 0, 0)),
        ],
        out_shape=[
            jax.ShapeDtypeStruct((batch, d, seq, GROUP_WIDTH), BF16),
            jax.ShapeDtypeStruct((batch, d, seq, HEAD_DIM), F32),
        ],
        scratch_shapes=[pltpu.VMEM((N_WINDOW_CASES, HEADS_PER_GROUP, qb, kw), F32)],
        compiler_params=_params(("arbitrary", "arbitrary")),
        name=f"dilated_attention_d{d}",
    )(view, view, view)


NA_GROUP_ROWS = NA_ROWS // 2
NA_WINDOW_ROWS = NA_ROWS + NA_GROUP_ROWS
N_BIAS_VARIANTS = 3


def _expand_na_bias(rpb_ref, out_ref, h):
    half = NA_ROWS // 2
    n_off = 2 * NA_ROWS - 1
    pad = half
    n_seg = 3 * NA_ROWS
    width = n_seg * GRID_W
    qc = lax.broadcasted_iota(jnp.int32, (GRID_W, width), 0)
    kc = lax.broadcasted_iota(jnp.int32, (GRID_W, width), 1) % GRID_W
    col_idx = jnp.clip(kc - qc, -(NA_COLS - 1), NA_COLS - 1) + (NA_COLS - 1)
    seg = lax.broadcasted_iota(jnp.int32, (1, width), 1) // GRID_W
    table = jnp.zeros((GRID_W, width), F32)
    for j in range(2 * NA_COLS - 1):
        row_vals = jnp.zeros((1, width), F32)
        for i in range(n_off):
            row_vals = jnp.where(seg == i + pad, rpb_ref[h, i, j], row_vals)
        table = jnp.where(col_idx == j, row_vals, table)
    cs = jnp.clip(qc - NA_COLS // 2, 0, GRID_W - NA_COLS)
    table = jnp.where((kc >= cs) & (kc < cs + NA_COLS), table * LOG2E, NEG)
    nkeys = NA_WINDOW_ROWS * GRID_W
    key_row = lax.broadcasted_iota(jnp.int32, (GRID_W, nkeys), 1) // GRID_W
    for v in range(N_BIAS_VARIANTS):
        for a in range(NA_GROUP_ROWS):
            first = {0: 0, 1: a, 2: NA_WINDOW_ROWS - NA_ROWS}[v]
            seg0 = (NA_ROWS - 1) - half * v - a + pad
            assert 0 <= seg0 and seg0 + NA_WINDOW_ROWS <= n_seg
            slab = table[:, seg0 * GRID_W:(seg0 + NA_WINDOW_ROWS) * GRID_W]
            slab = jnp.where((key_row >= first) & (key_row < first + NA_ROWS), slab, NEG)
            out_ref[v, h, a * GRID_W:(a + 1) * GRID_W, :] = slab


def _na_kernel(rpb_ref, q_ref, k_ref, v_ref, wup_ref, o_ref, wup_bf_ref, bias_ref, *, rows):
    rb = pl.program_id(1)
    nq = NA_GROUP_ROWS * GRID_W
    nkeys = NA_WINDOW_ROWS * GRID_W

    @pl.when((pl.program_id(0) == 0) & (rb == 0))
    def _():
        for h in range(N_HEADS_B):
            _expand_na_bias(rpb_ref, bias_ref, h)

    def one_group(gl, carry):
        r0 = rb * NA_ROWS_PER_STEP + gl * NA_GROUP_ROWS
        ws = jnp.clip(r0 - NA_ROWS // 2, 0, rows - NA_WINDOW_ROWS)
        var = (r0 - ws) // (NA_ROWS // 2)
        q0 = pl.multiple_of(gl * nq, nq)
        k0 = pl.multiple_of(ws * GRID_W, GRID_W)
        heads = range(N_HEADS_B)
        head_cols = [slice(h * HEAD_DIM, (h + 1) * HEAD_DIM) for h in heads]
        scores = [lax.dot_general(q_ref[pl.ds(q0, nq), c], k_ref[pl.ds(k0, nkeys), c],
                                  (((1,), (1,)), ((), ())), preferred_element_type=F32)
                  for c in head_cols]
        probs, dens = [], []
        for h in heads:
            t = scores[h] * (SCALE * LOG2E) + bias_ref[var, h]
            m = jnp.max(t, axis=-1, keepdims=True)
            p = jnp.exp2(t - m)
            dens.append(jnp.sum(p, axis=-1, keepdims=True))
            probs.append(p.astype(BF16))
        for h in heads:
            o = jnp.dot(probs[h], v_ref[pl.ds(k0, nkeys), head_cols[h]],
                        preferred_element_type=F32) / dens[h]
            o_ref[pl.ds(q0, nq), head_cols[h]] = o.astype(BF16)
        return carry

    n_groups = NA_ROWS_PER_STEP // NA_GROUP_ROWS
    lax.fori_loop(0, n_groups, one_group, 0, unroll=min(NA_UNROLL, n_groups))
    wup_bf_ref[...] = wup_ref[...].astype(BF16)


def _neighbourhood(qkv, rpb, batch, seq_total, w_up):
    group = N_GROUPS_A
    assert GROUP_DILATIONS[group] == 1
    rows = seq_total // GRID_W
    assert rows >= NA_WINDOW_ROWS and rows % NA_ROWS_PER_STEP == 0
    assert NA_ROWS_PER_STEP % NA_GROUP_ROWS == 0
    tq = NA_ROWS_PER_STEP * GRID_W
    steps_per_batch = rows // NA_ROWS_PER_STEP
    view = qkv.reshape(qkv.shape[0], batch, seq_total, qkv.shape[-1])
    wup_rows = pl.BlockSpec((w_up.shape[0] // (batch * steps_per_batch), w_up.shape[1]),
                            lambda b, i: (b * steps_per_batch + i, 0))
    o, w_up_bf = pl.pallas_call(
        functools.partial(_na_kernel, rows=rows),
        grid=(batch, steps_per_batch),
        in_specs=[
            pl.BlockSpec(memory_space=pltpu.SMEM),
            pl.BlockSpec((None, None, tq, GROUP_WIDTH), lambda b, i: (group, b, i, 0)),
            pl.BlockSpec((None, None, seq_total, GROUP_WIDTH), lambda b, i: (group, b, 0, 1)),
            pl.BlockSpec((None, None, seq_total, GROUP_WIDTH), lambda b, i: (group, b, 0, 2)),
            wup_rows,
        ],
        out_specs=[pl.BlockSpec((None, tq, GROUP_WIDTH), lambda b, i: (b, i, 0)), wup_rows],
        out_shape=[jax.ShapeDtypeStruct((batch, seq_total, GROUP_WIDTH), BF16),
                   jax.ShapeDtypeStruct(w_up.shape, BF16)],
        scratch_shapes=[pltpu.VMEM((N_BIAS_VARIANTS, N_HEADS_B, NA_GROUP_ROWS * GRID_W,
                                    NA_WINDOW_ROWS * GRID_W), F32)],
        compiler_params=_params(("arbitrary", "arbitrary")),
        name="neighbourhood_attention",
    )(rpb, view, view, view, w_up)
    return o.reshape(batch * seq_total, GROUP_WIDTH), w_up_bf


def _to_token_order_matrix(tm, d):
    t = lax.broadcasted_iota(jnp.int32, (tm, tm), 0)
    c = lax.broadcasted_iota(jnp.int32, (tm, tm), 1)
    return (c == (t % d) * (tm // d) + t // d).astype(BF16)


def _merge_kernel(o0_ref, o1_ref, o2_ref, l0_ref, l1_ref, l2_ref, ob_ref, gate_ref, x_ref,
                  wpa_ref, wpb_ref, wout_ref, gn_ref, wdn_ref,
                  x2_ref, h2_ref, wdn_bf_ref, lse_tok_ref, *, dilations):
    tm = x_ref.shape[0]
    o_tok, lse_tok = [], []
    for g, (o_ref, l_ref, d) in enumerate(zip((o0_ref, o1_ref, o2_ref),
                                              (l0_ref, l1_ref, l2_ref), dilations)):
        o = o_ref[...].reshape(tm, GROUP_WIDTH)
        if d == 1:
            o_tok.append(o.astype(F32))
            lse_tok.append(l_ref[...].reshape(tm, HEAD_DIM))
        else:
            o_tok.append(jnp.dot(_to_token_order_matrix(tm, d), o, preferred_element_type=F32))
            for r in range(d):
                lse_tok_ref[g, pl.ds(r, tm // d, stride=d), :] = l_ref[r]
            lse_tok.append(lse_tok_ref[g])
    l0, l1, l2 = lse_tok
    mx = jnp.maximum(jnp.maximum(l0, l1), l2)
    e0, e1, e2 = jnp.exp(l0 - mx), jnp.exp(l1 - mx), jnp.exp(l2 - mx)
    tot = e0 + e1 + e2
    a0, a1, a2 = e0 / tot, e1 / tot, e2 / tot
    parts = []
    for h in range(HEADS_PER_GROUP):
        cols = slice(h * HEAD_DIM, (h + 1) * HEAD_DIM)
        y = (a0[:, h:h + 1] * o_tok[0][:, cols] + a1[:, h:h + 1] * o_tok[1][:, cols]
             + a2[:, h:h + 1] * o_tok[2][:, cols])
        parts.append(y.astype(BF16))
    ya = jnp.concatenate(parts, axis=1)
    half = tm // MERGE_ROW_CHUNKS
    x2s = []
    for r0 in range(0, tm, half):
        rows = slice(r0, r0 + half)
        ta = jnp.dot(ya[rows], wpa_ref[...], preferred_element_type=F32)
        tb = jnp.dot(ob_ref[rows, :], wpb_ref[...], preferred_element_type=F32)
        merged = (gate_ref[rows, :D_MODEL].astype(F32) * ta
                  + gate_ref[rows, D_MODEL:].astype(F32) * tb)
        x2 = x_ref[rows, :] + jnp.dot(merged.astype(BF16), wout_ref[...],
                                      preferred_element_type=F32)
        x2_ref[rows, :] = x2
        x2s.append(x2)
        if r0 == 0:
            wdn_bf_ref[...] = wdn_ref[...].astype(BF16)
    for r0, x2 in zip(range(0, tm, half), x2s):
        h2_ref[r0:r0 + half, :] = _rms(x2, gn_ref[...]).astype(BF16)


def _merge(o_groups, lse_groups, ob, gates, x2d, seq, wpa, wpb, wout, gn, w_down):
    m = x2d.shape[0]
    tm = MERGE_TM
    n_steps = m // tm
    tiles_per_batch = seq // tm
    dilations = tuple(o.shape[1] for o in o_groups)
    row = lambda w: pl.BlockSpec((tm, w), lambda i: (i, 0))
    wdn_rows = pl.BlockSpec((w_down.shape[0] // n_steps, w_down.shape[1]), lambda i: (i, 0))
    grouped = lambda d, w: pl.BlockSpec(
        (None, d, tm // d, w), lambda i: (i // tiles_per_batch, 0, i % tiles_per_batch, 0))
    const = lambda a, b: pl.BlockSpec((a, b), lambda i: (0, 0), pipeline_mode=pl.Buffered(1))
    return pl.pallas_call(
        functools.partial(_merge_kernel, dilations=dilations),
        grid=(n_steps,),
        in_specs=[grouped(d, GROUP_WIDTH) for d in dilations]
        + [grouped(d, HEAD_DIM) for d in dilations]
        + [row(GROUP_WIDTH), row(2 * D_MODEL), row(D_MODEL),
           const(GROUP_WIDTH, D_MODEL), const(GROUP_WIDTH, D_MODEL), const(D_MODEL, D_MODEL),
           const(1, D_MODEL), wdn_rows],
        out_specs=[row(D_MODEL), row(D_MODEL), wdn_rows],
        out_shape=[jax.ShapeDtypeStruct((m, D_MODEL), F32), jax.ShapeDtypeStruct((m, D_MODEL), BF16),
                   jax.ShapeDtypeStruct(w_down.shape, BF16)],
        scratch_shapes=[pltpu.VMEM((len(dilations), tm, HEAD_DIM), F32)],
        compiler_params=_params(("arbitrary",)),
        name="merge_out_proj",
    )(*o_groups, *lse_groups, ob, gates, x2d, wpa, wpb, wout, gn, w_down)


def _mlp_kernel(h2_ref, wup_ref, wdn_ref, x2_hbm, gf_ref, out_ref, x2_sem, *, n_f):
    i = pl.program_id(0)
    f = pl.program_id(1)
    tm = out_ref.shape[0]

    def residual_copy():
        return pltpu.make_async_copy(x2_hbm.at[pl.ds(i * tm, tm), :], out_ref, x2_sem)

    @pl.when(f == 0)
    def _():
        residual_copy().start()

    hid = jnp.dot(h2_ref[...], wup_ref[...], preferred_element_type=F32)
    hid = jnp.square(jnp.maximum(hid, 0.0)).astype(BF16)

    @pl.when(f == 0)
    def _():
        residual_copy().wait()

    out_ref[...] += jnp.dot(hid, wdn_ref[...], preferred_element_type=F32)

    @pl.when(f == n_f - 1)
    def _():
        out_ref[...] = _rms(out_ref[...], gf_ref[...])


def _mlp(h2, wup, wdn, x2, gf):
    m = h2.shape[0]
    tm, tf = MLP_TM, MLP_TF
    n_f = D_FF // tf
    return pl.pallas_call(
        functools.partial(_mlp_kernel, n_f=n_f),
        grid=(m // tm, n_f),
        in_specs=[
            pl.BlockSpec((tm, D_MODEL), lambda i, f: (i, 0)),
            pl.BlockSpec((D_MODEL, tf), lambda i, f: (0, f)),
            pl.BlockSpec((tf, D_MODEL), lambda i, f: (f, 0)),
            pl.BlockSpec(memory_space=pl.ANY),
            pl.BlockSpec((1, D_MODEL), lambda i, f: (0, 0)),
        ],
        out_specs=pl.BlockSpec((tm, D_MODEL), lambda i, f: (i, 0)),
        out_shape=jax.ShapeDtypeStruct((m, D_MODEL), F32),
        scratch_shapes=[pltpu.SemaphoreType.DMA],
        compiler_params=_params(("arbitrary", "arbitrary")),
        name="mlp_residual_norm",
    )(h2, wup, wdn, x2, gf)


def _layer(x2d, batch, seq, norm_mix, w_qkv, w_gate, b_gate, rpb, w_proj_a, w_proj_b, w_out,
           norm_mlp, w_up, w_down):
    slopes = 2.0 ** (-8.0 * np.arange(1, N_HEADS_A + 1) / N_HEADS_A)
    row = lambda v: v.reshape(1, -1)
    h_orders = _norm(x2d, row(norm_mix))
    qkv, gates, w_out_bf, w_pa_bf, w_pb_bf = _project(
        h_orders, w_qkv, w_gate, row(b_gate), w_out, w_proj_a, w_proj_b)
    o_groups, lse_groups = [], []
    for g, (window, d) in enumerate(DILATION_PATTERNS):
        o, lse = _dilated_group(qkv, batch, seq, g, window, d, slopes)
        o_groups.append(o)
        lse_groups.append(lse)
    ob, w_up_bf = _neighbourhood(qkv, rpb, batch, seq, w_up)
    x2, h2, w_down_bf = _merge(o_groups, lse_groups, ob, gates, x2d, seq, w_pa_bf, w_pb_bf,
                               w_out_bf, row(norm_mlp), w_down)
    return x2, h2, w_up_bf, w_down_bf


def kernel(x, norm_mix, w_qkv, w_gate, b_gate, rpb, w_proj_a, w_proj_b, w_out, norm_mlp, w_up,
           w_down, norm_final):
    batch, seq, _ = x.shape
    depth = norm_mix.shape[0]
    assert depth == 1 and seq % PERM_TILE == 0
    x2d = x.reshape(batch * seq, D_MODEL)
    x2, h2, w_up_bf, w_down_bf = _layer(
        x2d, batch, seq, norm_mix[0], w_qkv[0], w_gate[0], b_gate[0], rpb[0],
        w_proj_a[0], w_proj_b[0], w_out[0], norm_mlp[0], w_up[0], w_down[0])
    out = _mlp(h2, w_up_bf, w_down_bf, x2, norm_final.reshape(1, -1))
    return out.reshape(batch, seq, D_MODEL)
```

```python
import functools

import jax
import jax.numpy as jnp
import numpy as np
from jax import lax
from jax.experimental import pallas as pl
from jax.experimental.pallas import tpu as pltpu

D_MODEL = 2048
HEAD_DIM = 128
N_HEADS = D_MODEL // HEAD_DIM
N_HEADS_B = N_HEADS // 4
N_HEADS_A = N_HEADS - N_HEADS_B
DILATION_PATTERNS = ((128, 1), (512, 4), (2048, 16))
N_GROUPS_A = len(DILATION_PATTERNS)
HEADS_PER_GROUP = N_HEADS_A // N_GROUPS_A
GROUP_WIDTH = HEADS_PER_GROUP * HEAD_DIM
N_HEAD_GROUPS = N_HEADS // HEADS_PER_GROUP
GRID_W = 64
NA_ROWS = 8
NA_COLS = 16
D_FF = 4 * D_MODEL
EPS = 1e-6
NEG = -1e30
SCALE = HEAD_DIM ** -0.5
LOG2E = float(np.log2(np.e))
LN2 = float(np.log(2.0))
LANES = 128

F32 = jnp.float32
BF16 = jnp.bfloat16

VMEM_LIMIT_BYTES = 56 * 1024 * 1024

PERM_TILE = 1024
NORM_CHUNK = 512
PROJ_TM, PROJ_TN = 2048, GROUP_WIDTH
SIDE_CAST_ROWS = 32
MERGE_TM = 256
MERGE_ROW_CHUNKS = 1
MLP_TM, MLP_TF = 1024, 1024
DIL_QB = 128
DIL_UNROLL = 8
DIL_ROWS_PER_STEP = 1024
N_WINDOW_CASES = 3
NA_ROWS_PER_STEP = 8
NA_UNROLL = 2

GROUP_DILATIONS = tuple(d for _, d in DILATION_PATTERNS) + (1,)
ROW_ORDERS = tuple(sorted(set(GROUP_DILATIONS)))
GROUP_ROW_ORDER = tuple(ROW_ORDERS.index(d) for d in GROUP_DILATIONS)


def _params(sem):
    return pltpu.CompilerParams(dimension_semantics=sem, vmem_limit_bytes=VMEM_LIMIT_BYTES)


def _rms(x, g):
    ms = jnp.mean(x * x, axis=-1, keepdims=True)
    return (x * lax.rsqrt(ms + EPS)) * g


def _norm_kernel(x_ref, g_ref, h_ref, slab_ref):
    tm = x_ref.shape[0]
    x = x_ref[...]
    inv = lax.rsqrt(jnp.mean(x * x, axis=-1, keepdims=True) + EPS)
    n_slabs = NORM_CHUNK // LANES
    for c0 in range(0, D_MODEL, NORM_CHUNK):
        hc = (x_ref[:, c0:c0 + NORM_CHUNK] * inv) * g_ref[:, c0:c0 + NORM_CHUNK]
        h_ref[0, :, c0:c0 + NORM_CHUNK] = hc.astype(BF16)
        for s in range(n_slabs):
            slab_ref[0, s] = hc[:, s * LANES:(s + 1) * LANES]
        for v in range(1, len(ROW_ORDERS)):
            d_prev, d = ROW_ORDERS[v - 1], ROW_ORDERS[v]
            q = d // d_prev
            n_prev, n = tm // d_prev, tm // d
            last = v == len(ROW_ORDERS) - 1
            for s in range(n_slabs):
                cols = slice(c0 + s * LANES, c0 + (s + 1) * LANES)
                for r in range(d_prev):
                    for r2 in range(q):
                        rows = slab_ref[(v - 1) % 2, s, pl.ds(r * n_prev + r2, n, stride=q), :]
                        dst = (d_prev * r2 + r) * n
                        h_ref[v, dst:dst + n, cols] = rows.astype(BF16)
                        if not last:
                            slab_ref[v % 2, s, dst:dst + n, :] = rows


def _norm(x2d, g):
    m = x2d.shape[0]
    tm = PERM_TILE
    nv = len(ROW_ORDERS)
    assert ROW_ORDERS[0] == 1
    return pl.pallas_call(
        _norm_kernel,
        grid=(m // tm,),
        in_specs=[pl.BlockSpec((tm, D_MODEL), lambda i: (i, 0)),
                  pl.BlockSpec((1, D_MODEL), lambda i: (0, 0))],
        out_specs=pl.BlockSpec((nv, tm, D_MODEL), lambda i: (0, i, 0)),
        out_shape=jax.ShapeDtypeStruct((nv, m, D_MODEL), BF16),
        scratch_shapes=[pltpu.VMEM((2, NORM_CHUNK // LANES, tm, LANES), F32)],
        compiler_params=_params(("arbitrary",)),
        name="rmsnorm_row_orders",
    )(x2d, g)


def _proj_kernel(h_ref, w_first_ref, wq_next_ref, wg_next_ref, bg_ref, wout_ref, wpa_ref, wpb_ref,
                 qkv_ref, gate_ref, wout_bf_ref, wpa_bf_ref, wpb_bf_ref, w_bf_ref, *, n_qkv_steps):
    j = pl.program_id(0)
    i = pl.program_id(1)

    @pl.when((j == 0) & (i == 0))
    def _():
        w_bf_ref[0] = w_first_ref[...].astype(BF16)

    def side_casts(next_ref):
        part = next_ref.shape[0]
        rows = pl.ds(pl.multiple_of(i * part, part), part)
        w_bf_ref[(j + 1) % 2, rows, :] = next_ref[...].astype(BF16)
        wout_bf_ref[...] = wout_ref[...].astype(BF16)
        wpa_bf_ref[...] = wpa_ref[...].astype(BF16)
        wpb_bf_ref[...] = wpb_ref[...].astype(BF16)

    def qkv_step(next_ref):
        y = jnp.dot(h_ref[...], w_bf_ref[j % 2], preferred_element_type=F32)
        qkv_ref[...] = y.astype(BF16)
        side_casts(next_ref)

    @pl.when(j < n_qkv_steps - 1)
    def _():
        qkv_step(wq_next_ref)

    @pl.when(j == n_qkv_steps - 1)
    def _():
        qkv_step(wg_next_ref)

    @pl.when(j >= n_qkv_steps)
    def _():
        z = jnp.dot(h_ref[...], w_bf_ref[j % 2], preferred_element_type=F32) + bg_ref[...]
        gate_ref[...] = (0.5 * jnp.tanh(0.5 * z) + 0.5).astype(BF16)
        side_casts(wg_next_ref)


def _project(h_orders, w_qkv, w_gate, b_gate, w_out, w_proj_a, w_proj_b):
    m = h_orders.shape[1]
    tm, tn = PROJ_TM, PROJ_TN
    n_groups = N_HEAD_GROUPS
    nq = 3 * n_groups
    ng = (2 * D_MODEL) // tn
    n_i = m // tm
    row_order = GROUP_ROW_ORDER

    def lhs_map(j, i):
        v = jnp.int32(0)
        for grp in range(n_groups):
            v = jnp.where(j // 3 == grp, row_order[grp], v)
        return (v, i, 0)

    def w_qkv_col(j):
        jj = jnp.minimum(j, nq - 1)
        return (jj % 3) * n_groups + jj // 3

    gate_col = lambda j, i: (0, jnp.maximum(j - nq, 0))
    wq_next_map = lambda j, i: (jnp.where(j + 1 < nq, i, n_i - 1), w_qkv_col(j + 1))
    wg_next_map = lambda j, i: (jnp.where(j + 1 >= nq, i, 0), jnp.clip(j + 1 - nq, 0, ng - 1))

    def qkv_out_map(j, i):
        jj = jnp.minimum(j, nq - 1)
        return (jj // 3, jnp.where(j < nq, i, n_i - 1), jj % 3)

    gate_out_map = lambda j, i: (jnp.where(j >= nq, i, 0), jnp.maximum(j - nq, 0))

    def side_rows(w):
        n_blocks = w.shape[0] // SIDE_CAST_ROWS
        return pl.BlockSpec((SIDE_CAST_ROWS, w.shape[1]),
                            lambda j, i: (jnp.minimum(j * n_i + i, n_blocks - 1), 0))

    side = [w_out, w_proj_a, w_proj_b]
    assert all(w.shape[0] // SIDE_CAST_ROWS <= (nq + ng) * n_i for w in side)
    return pl.pallas_call(
        functools.partial(_proj_kernel, n_qkv_steps=nq),
        grid=(nq + ng, n_i),
        in_specs=[
            pl.BlockSpec((None, tm, D_MODEL), lhs_map),
            pl.BlockSpec((D_MODEL, tn), lambda j, i: (0, 0), pipeline_mode=pl.Buffered(1)),
            pl.BlockSpec((D_MODEL // n_i, tn), wq_next_map),
            pl.BlockSpec((D_MODEL // n_i, tn), wg_next_map),
            pl.BlockSpec((1, tn), gate_col),
        ] + [side_rows(w) for w in side],
        out_specs=[
            pl.BlockSpec((None, tm, tn), qkv_out_map),
            pl.BlockSpec((tm, tn), gate_out_map),
        ] + [side_rows(w) for w in side],
        out_shape=[
            jax.ShapeDtypeStruct((n_groups, m, 3 * GROUP_WIDTH), BF16),
            jax.ShapeDtypeStruct((m, 2 * D_MODEL), BF16),
        ] + [jax.ShapeDtypeStruct(w.shape, BF16) for w in side],
        scratch_shapes=[pltpu.VMEM((2, D_MODEL, tn), BF16)],
        compiler_params=_params(("arbitrary", "arbitrary")),
        name="proj_qkv_gate",
    )(h_orders, w_qkv, w_qkv, w_gate, b_gate, *side)


def _dilated_block_shape(seq, half_window):
    kw = DIL_QB + 2 * half_window
    return (seq, seq) if seq <= kw else (DIL_QB, kw)


def _dilated_kernel(q_ref, k_ref, v_ref, o_ref, lse_ref, bias_ref, *, seq, half_window, coefs):
    qb, kw = _dilated_block_shape(seq, half_window)
    n_blocks = seq // qb
    n_tiles, n_res = q_ref.shape[:2]

    def rows_loader(ref):
        if n_res == 1:
            flat = ref.at[:, 0].reshape(seq, GROUP_WIDTH)
            return lambda rr, start, size, cols: flat[pl.ds(start, size), cols]
        assert n_blocks == 1
        return lambda rr, start, size, cols: jnp.concatenate(
            [ref[t, rr, :, cols] for t in range(n_tiles)], axis=0)

    load_q, load_k, load_v = rows_loader(q_ref), rows_loader(k_ref), rows_loader(v_ref)
    lane = lax.broadcasted_iota(jnp.int32, (qb, HEAD_DIM), 1)
    heads = range(HEADS_PER_GROUP)
    head_cols = [slice(h * HEAD_DIM, (h + 1) * HEAD_DIM) for h in heads]

    @pl.when((pl.program_id(0) == 0) & (pl.program_id(1) == 0))
    def _():
        qrow = lax.broadcasted_iota(jnp.int32, (qb, kw), 0)
        kcol = lax.broadcasted_iota(jnp.int32, (qb, kw), 1)
        for c in range(bias_ref.shape[0]):
            dist = jnp.abs(kcol - qrow - c * half_window)
            dist_f = dist.astype(F32)
            for h in heads:
                bias_ref[c, h] = jnp.where(dist <= half_window, (-coefs[h] * LOG2E) * dist_f, NEG)

    def block(qi, carry):
        q0 = pl.multiple_of(qi * qb, qb)
        ks = pl.multiple_of(jnp.clip(q0 - half_window, 0, seq - kw), half_window)
        window_case = (q0 - ks) // half_window
        pairs = [(rr, h) for rr in range(n_res) for h in heads]
        scores = [lax.dot_general(load_q(rr, q0, qb, head_cols[h]), load_k(rr, ks, kw, head_cols[h]),
                                  (((1,), (1,)), ((), ())), preferred_element_type=F32)
                  for rr, h in pairs]
        probs, dens = [], []
        lse_all = [jnp.zeros((qb, HEAD_DIM), F32) for _ in range(n_res)]
        for (rr, h), s in zip(pairs, scores):
            t = s * (SCALE * LOG2E) + bias_ref[window_case, h]
            m = jnp.max(t, axis=-1, keepdims=True)
            p = jnp.exp2(t - m)
            den = jnp.sum(p, axis=-1, keepdims=True)
            probs.append(p.astype(BF16))
            dens.append(den)
            lse_all[rr] = jnp.where(lane == h, m * LN2 + jnp.log(den), lse_all[rr])
        for (rr, h), p, den in zip(pairs, probs, dens):
            o = jnp.dot(p, load_v(rr, ks, kw, head_cols[h]), preferred_element_type=F32) / den
            o_ref[rr, pl.ds(q0, qb), head_cols[h]] = o.astype(BF16)
        for rr in range(n_res):
            lse_ref[rr, pl.ds(q0, qb), :] = lse_all[rr]
        return carry

    lax.fori_loop(0, n_blocks, block, 0, unroll=min(DIL_UNROLL, n_blocks))


def _dilated_group(qkv, batch, seq_total, group, window, dilation, slopes):
    d = dilation
    seq = seq_total // d
    half_window = window // (2 * d)
    qb, kw = _dilated_block_shape(seq, half_window)
    coefs = tuple(float(slopes[group * HEADS_PER_GROUP + h]) * d for h in range(HEADS_PER_GROUP))
    tiles = seq_total // PERM_TILE
    rows = PERM_TILE // d
    view = qkv.reshape(qkv.shape[0], batch, tiles, d, rows, qkv.shape[-1])
    n_res = min(d, max(1, DIL_ROWS_PER_STEP // seq))
    part = lambda which: pl.BlockSpec((None, None, tiles, n_res, rows, GROUP_WIDTH),
                                      lambda b, r: (group, b, 0, r, 0, which))
    return pl.pallas_call(
        functools.partial(_dilated_kernel, seq=seq, half_window=half_window, coefs=coefs),
        grid=(batch, d // n_res),
        in_specs=[part(0), part(1), part(2)],
        out_specs=[
            pl.BlockSpec((None, n_res, seq, GROUP_WIDTH), lambda b, r: (b, r, 0, 0)),
            pl.BlockSpec((None, n_res, seq, HEAD_DIM), lambda b, r: (b, r, 0, 0)),
        ],
        out_shape=[
            jax.ShapeDtypeStruct((batch, d, seq, GROUP_WIDTH), BF16),
            jax.ShapeDtypeStruct((batch, d, seq, HEAD_DIM), F32),
        ],
        scratch_shapes=[pltpu.VMEM((N_WINDOW_CASES, HEADS_PER_GROUP, qb, kw), F32)],
        compiler_params=_params(("arbitrary", "arbitrary")),
        name=f"dilated_attention_d{d}",
    )(view, view, view)


NA_GROUP_ROWS = NA_ROWS // 2
NA_WINDOW_ROWS = NA_ROWS + NA_GROUP_ROWS
N_BIAS_VARIANTS = 3


def _expand_na_bias(rpb_ref, out_ref, h):
    half = NA_ROWS // 2
    n_off = 2 * NA_ROWS - 1
    pad = half
    n_seg = 3 * NA_ROWS
    width = n_seg * GRID_W
    qc = lax.broadcasted_iota(jnp.int32, (GRID_W, width), 0)
    kc = lax.broadcasted_iota(jnp.int32, (GRID_W, width), 1) % GRID_W
    col_idx = jnp.clip(kc - qc, -(NA_COLS - 1), NA_COLS - 1) + (NA_COLS - 1)
    seg = lax.broadcasted_iota(jnp.int32, (1, width), 1) // GRID_W
    table = jnp.zeros((GRID_W, width), F32)
    for j in range(2 * NA_COLS - 1):
        row_vals = jnp.zeros((1, width), F32)
        for i in range(n_off):
            row_vals = jnp.where(seg == i + pad, rpb_ref[h, i, j], row_vals)
        table = jnp.where(col_idx == j, row_vals, table)
    cs = jnp.clip(qc - NA_COLS // 2, 0, GRID_W - NA_COLS)
    table = jnp.where((kc >= cs) & (kc < cs + NA_COLS), table * LOG2E, NEG)
    nkeys = NA_WINDOW_ROWS * GRID_W
    key_row = lax.broadcasted_iota(jnp.int32, (GRID_W, nkeys), 1) // GRID_W
    for v in range(N_BIAS_VARIANTS):
        for a in range(NA_GROUP_ROWS):
            first = {0: 0, 1: a, 2: NA_WINDOW_ROWS - NA_ROWS}[v]
            seg0 = (NA_ROWS - 1) - half * v - a + pad
            assert 0 <= seg0 and seg0 + NA_WINDOW_ROWS <= n_seg
            slab = table[:, seg0 * GRID_W:(seg0 + NA_WINDOW_ROWS) * GRID_W]
            slab = jnp.where((key_row >= first) & (key_row < first + NA_ROWS), slab, NEG)
            out_ref[v, h, a * GRID_W:(a + 1) * GRID_W, :] = slab


def _na_kernel(rpb_ref, q_ref, k_ref, v_ref, wup_ref, o_ref, wup_bf_ref, bias_ref, *, rows):
    rb = pl.program_id(1)
    nq = NA_GROUP_ROWS * GRID_W
    nkeys = NA_WINDOW_ROWS * GRID_W

    @pl.when((pl.program_id(0) == 0) & (rb == 0))
    def _():
        for h in range(N_HEADS_B):
            _expand_na_bias(rpb_ref, bias_ref, h)

    def one_group(gl, carry):
        r0 = rb * NA_ROWS_PER_STEP + gl * NA_GROUP_ROWS
        ws = jnp.clip(r0 - NA_ROWS // 2, 0, rows - NA_WINDOW_ROWS)
        var = (r0 - ws) // (NA_ROWS // 2)
        q0 = pl.multiple_of(gl * nq, nq)
        k0 = pl.multiple_of(ws * GRID_W, GRID_W)
        heads = range(N_HEADS_B)
        head_cols = [slice(h * HEAD_DIM, (h + 1) * HEAD_DIM) for h in heads]
        scores = [lax.dot_general(q_ref[pl.ds(q0, nq), c], k_ref[pl.ds(k0, nkeys), c],
                                  (((1,), (1,)), ((), ())), preferred_element_type=F32)
                  for c in head_cols]
        probs, dens = [], []
        for h in heads:
            t = scores[h] * (SCALE * LOG2E) + bias_ref[var, h]
            m = jnp.max(t, axis=-1, keepdims=True)
            p = jnp.exp2(t - m)
            dens.append(jnp.sum(p, axis=-1, keepdims=True))
            probs.append(p.astype(BF16))
        for h in heads:
            o = jnp.dot(probs[h], v_ref[pl.ds(k0, nkeys), head_cols[h]],
                        preferred_element_type=F32) / dens[h]
            o_ref[pl.ds(q0, nq), head_cols[h]] = o.astype(BF16)
        return carry

    n_groups = NA_ROWS_PER_STEP // NA_GROUP_ROWS
    lax.fori_loop(0, n_groups, one_group, 0, unroll=min(NA_UNROLL, n_groups))
    wup_bf_ref[...] = wup_ref[...].astype(BF16)


def _neighbourhood(qkv, rpb, batch, seq_total, w_up):
    group = N_GROUPS_A
    assert GROUP_DILATIONS[group] == 1
    rows = seq_total // GRID_W
    assert rows >= NA_WINDOW_ROWS and rows % NA_ROWS_PER_STEP == 0
    assert NA_ROWS_PER_STEP % NA_GROUP_ROWS == 0
    tq = NA_ROWS_PER_STEP * GRID_W
    steps_per_batch = rows // NA_ROWS_PER_STEP
    view = qkv.reshape(qkv.shape[0], batch, seq_total, qkv.shape[-1])
    wup_rows = pl.BlockSpec((w_up.shape[0] // (batch * steps_per_batch), w_up.shape[1]),
                            lambda b, i: (b * steps_per_batch + i, 0))
    o, w_up_bf = pl.pallas_call(
        functools.partial(_na_kernel, rows=rows),
        grid=(batch, steps_per_batch),
        in_specs=[
            pl.BlockSpec(memory_space=pltpu.SMEM),
            pl.BlockSpec((None, None, tq, GROUP_WIDTH), lambda b, i: (group, b, i, 0)),
            pl.BlockSpec((None, None, seq_total, GROUP_WIDTH), lambda b, i: (group, b, 0, 1)),
            pl.BlockSpec((None, None, seq_total, GROUP_WIDTH), lambda b, i: (group, b, 0, 2)),
            wup_rows,
        ],
        out_specs=[pl.BlockSpec((None, tq, GROUP_WIDTH), lambda b, i: (b, i, 0)), wup_rows],
        out_shape=[jax.ShapeDtypeStruct((batch, seq_total, GROUP_WIDTH), BF16),
                   jax.ShapeDtypeStruct(w_up.shape, BF16)],
        scratch_shapes=[pltpu.VMEM((N_BIAS_VARIANTS, N_HEADS_B, NA_GROUP_ROWS * GRID_W,
                                    NA_WINDOW_ROWS * GRID_W), F32)],
        compiler_params=_params(("arbitrary", "arbitrary")),
        name="neighbourhood_attention",
    )(rpb, view, view, view, w_up)
    return o.reshape(batch * seq_total, GROUP_WIDTH), w_up_bf


def _to_token_order_matrix(tm, d):
    t = lax.broadcasted_iota(jnp.int32, (tm, tm), 0)
    c = lax.broadcasted_iota(jnp.int32, (tm, tm), 1)
    return (c == (t % d) * (tm // d) + t // d).astype(BF16)


def _merge_kernel(o0_ref, o1_ref, o2_ref, l0_ref, l1_ref, l2_ref, ob_ref, gate_ref, x_ref,
                  wpa_ref, wpb_ref, wout_ref, gn_ref, wdn_ref,
                  x2_ref, h2_ref, wdn_bf_ref, lse_tok_ref, *, dilations):
    tm = x_ref.shape[0]
    o_tok, lse_tok = [], []
    for g, (o_ref, l_ref, d) in enumerate(zip((o0_ref, o1_ref, o2_ref),
                                              (l0_ref, l1_ref, l2_ref), dilations)):
        o = o_ref[...].reshape(tm, GROUP_WIDTH)
        if d == 1:
            o_tok.append(o.astype(F32))
            lse_tok.append(l_ref[...].reshape(tm, HEAD_DIM))
        else:
            o_tok.append(jnp.dot(_to_token_order_matrix(tm, d), o, preferred_element_type=F32))
            for r in range(d):
                lse_tok_ref[g, pl.ds(r, tm // d, stride=d), :] = l_ref[r]
            lse_tok.append(lse_tok_ref[g])
    l0, l1, l2 = lse_tok
    mx = jnp.maximum(jnp.maximum(l0, l1), l2)
    e0, e1, e2 = jnp.exp(l0 - mx), jnp.exp(l1 - mx), jnp.exp(l2 - mx)
    tot = e0 + e1 + e2
    a0, a1, a2 = e0 / tot, e1 / tot, e2 / tot
    parts = []
    for h in range(HEADS_PER_GROUP):
        cols = slice(h * HEAD_DIM, (h + 1) * HEAD_DIM)
        y = (a0[:, h:h + 1] * o_tok[0][:, cols] + a1[:, h:h + 1] * o_tok[1][:, cols]
             + a2[:, h:h + 1] * o_tok[2][:, cols])
        parts.append(y.astype(BF16))
    ya = jnp.concatenate(parts, axis=1)
    half = tm // MERGE_ROW_CHUNKS
    x2s = []
    for r0 in range(0, tm, half):
        rows = slice(r0, r0 + half)
        ta = jnp.dot(ya[rows], wpa_ref[...], preferred_element_type=F32)
        tb = jnp.dot(ob_ref[rows, :], wpb_ref[...], preferred_element_type=F32)
        merged = (gate_ref[rows, :D_MODEL].astype(F32) * ta
                  + gate_ref[rows, D_MODEL:].astype(F32) * tb)
        x2 = x_ref[rows, :] + jnp.dot(merged.astype(BF16), wout_ref[...],
                                      preferred_element_type=F32)
        x2_ref[rows, :] = x2
        x2s.append(x2)
        if r0 == 0:
            wdn_bf_ref[...] = wdn_ref[...].astype(BF16)
    for r0, x2 in zip(range(0, tm, half), x2s):
        h2_ref[r0:r0 + half, :] = _rms(x2, gn_ref[...]).astype(BF16)


def _merge(o_groups, lse_groups, ob, gates, x2d, seq, wpa, wpb, wout, gn, w_down):
    m = x2d.shape[0]
    tm = MERGE_TM
    n_steps = m // tm
    tiles_per_batch = seq // tm
    dilations = tuple(o.shape[1] for o in o_groups)
    row = lambda w: pl.BlockSpec((tm, w), lambda i: (i, 0))
    wdn_rows = pl.BlockSpec((w_down.shape[0] // n_steps, w_down.shape[1]), lambda i: (i, 0))
    grouped = lambda d, w: pl.BlockSpec(
        (None, d, tm // d, w), lambda i: (i // tiles_per_batch, 0, i % tiles_per_batch, 0))
    const = lambda a, b: pl.BlockSpec((a, b), lambda i: (0, 0), pipeline_mode=pl.Buffered(1))
    return pl.pallas_call(
        functools.partial(_merge_kernel, dilations=dilations),
        grid=(n_steps,),
        in_specs=[grouped(d, GROUP_WIDTH) for d in dilations]
        + [grouped(d, HEAD_DIM) for d in dilations]
        + [row(GROUP_WIDTH), row(2 * D_MODEL), row(D_MODEL),
           const(GROUP_WIDTH, D_MODEL), const(GROUP_WIDTH, D_MODEL), const(D_MODEL, D_MODEL),
           const(1, D_MODEL), wdn_rows],
        out_specs=[row(D_MODEL), row(D_MODEL), wdn_rows],
        out_shape=[jax.ShapeDtypeStruct((m, D_MODEL), F32), jax.ShapeDtypeStruct((m, D_MODEL), BF16),
                   jax.ShapeDtypeStruct(w_down.shape, BF16)],
        scratch_shapes=[pltpu.VMEM((len(dilations), tm, HEAD_DIM), F32)],
        compiler_params=_params(("arbitrary",)),
        name="merge_out_proj",
    )(*o_groups, *lse_groups, ob, gates, x2d, wpa, wpb, wout, gn, w_down)


def _mlp_kernel(h2_ref, wup_ref, wdn_ref, x2_hbm, gf_ref, out_ref, x2_sem, *, n_f):
    i = pl.program_id(0)
    f = pl.program_id(1)
    tm = out_ref.shape[0]

    def residual_copy():
        return pltpu.make_async_copy(x2_hbm.at[pl.ds(i * tm, tm), :], out_ref, x2_sem)

    @pl.when(f == 0)
    def _():
        residual_copy().start()

    hid = jnp.dot(h2_ref[...], wup_ref[...], preferred_element_type=F32)
    hid = jnp.square(jnp.maximum(hid, 0.0)).astype(BF16)

    @pl.when(f == 0)
    def _():
        residual_copy().wait()

    out_ref[...] += jnp.dot(hid, wdn_ref[...], preferred_element_type=F32)

    @pl.when(f == n_f - 1)
    def _():
        out_ref[...] = _rms(out_ref[...], gf_ref[...])


def _mlp(h2, wup, wdn, x2, gf):
    m = h2.shape[0]
    tm, tf = MLP_TM, MLP_TF
    n_f = D_FF // tf
    return pl.pallas_call(
        functools.partial(_mlp_kernel, n_f=n_f),
        grid=(m // tm, n_f),
        in_specs=[
            pl.BlockSpec((tm, D_MODEL), lambda i, f: (i, 0)),
            pl.BlockSpec((D_MODEL, tf), lambda i, f: (0, f)),
            pl.BlockSpec((tf, D_MODEL), lambda i, f: (f, 0)),
            pl.BlockSpec(memory_space=pl.ANY),
            pl.BlockSpec((1, D_MODEL), lambda i, f: (0, 0)),
        ],
        out_specs=pl.BlockSpec((tm, D_MODEL), lambda i, f: (i, 0)),
        out_shape=jax.ShapeDtypeStruct((m, D_MODEL), F32),
        scratch_shapes=[pltpu.SemaphoreType.DMA],
        compiler_params=_params(("arbitrary", "arbitrary")),
        name="mlp_residual_norm",
    )(h2, wup, wdn, x2, gf)


def _layer(x2d, batch, seq, norm_mix, w_qkv, w_gate, b_gate, rpb, w_proj_a, w_proj_b, w_out,
           norm_mlp, w_up, w_down):
    slopes = 2.0 ** (-8.0 * np.arange(1, N_HEADS_A + 1) / N_HEADS_A)
    row = lambda v: v.reshape(1, -1)
    h_orders = _norm(x2d, row(norm_mix))
    qkv, gates, w_out_bf, w_pa_bf, w_pb_bf = _project(
        h_orders, w_qkv, w_gate, row(b_gate), w_out, w_proj_a, w_proj_b)
    o_groups, lse_groups = [], []
    for g, (window, d) in enumerate(DILATION_PATTERNS):
        o, lse = _dilated_group(qkv, batch, seq, g, window, d, slopes)
        o_groups.append(o)
        lse_groups.append(lse)
    ob, w_up_bf = _neighbourhood(qkv, rpb, batch, seq, w_up)
    x2, h2, w_down_bf = _merge(o_groups, lse_groups, ob, gates, x2d, seq, w_pa_bf, w_pb_bf,
                               w_out_bf, row(norm_mlp), w_down)
    return x2, h2, w_up_bf, w_down_bf


def kernel(x, norm_mix, w_qkv, w_gate, b_gate, rpb, w_proj_a, w_proj_b, w_out, norm_mlp, w_up,
           w_down, norm_final):
    batch, seq, _ = x.shape
    depth = norm_mix.shape[0]
    assert depth == 1 and seq % PERM_TILE == 0
    x2d = x.reshape(batch * seq, D_MODEL)
    x2, h2, w_up_bf, w_down_bf = _layer(
        x2d, batch, seq, norm_mix[0], w_qkv[0], w_gate[0], b_gate[0], rpb[0],
        w_proj_a[0], w_proj_b[0], w_out[0], norm_mlp[0], w_up[0], w_down[0])
    out = _mlp(h2, w_up_bf, w_down_bf, x2, norm_final.reshape(1, -1))
    return out.reshape(batch, seq, D_MODEL)
```

```python
import functools

import jax
import jax.numpy as jnp
import numpy as np
from jax import lax
from jax.experimental import pallas as pl
from jax.experimental.pallas import tpu as pltpu

D_MODEL = 2048
HEAD_DIM = 128
N_HEADS = D_MODEL // HEAD_DIM
N_HEADS_B = N_HEADS // 4
N_HEADS_A = N_HEADS - N_HEADS_B
DILATION_PATTERNS = ((128, 1), (512, 4), (2048, 16))
N_GROUPS_A = len(DILATION_PATTERNS)
HEADS_PER_GROUP = N_HEADS_A // N_GROUPS_A
GROUP_WIDTH = HEADS_PER_GROUP * HEAD_DIM
N_HEAD_GROUPS = N_HEADS // HEADS_PER_GROUP
GRID_W = 64
NA_ROWS = 8
NA_COLS = 16
D_FF = 4 * D_MODEL
EPS = 1e-6
NEG = -1e30
SCALE = HEAD_DIM ** -0.5
LOG2E = float(np.log2(np.e))
LN2 = float(np.log(2.0))
LANES = 128

F32 = jnp.float32
BF16 = jnp.bfloat16

VMEM_LIMIT_BYTES = 56 * 1024 * 1024

PERM_TILE = 1024
NORM_CHUNK = 512
PROJ_TM, PROJ_TN = 2048, GROUP_WIDTH
SIDE_CAST_ROWS = 32
MERGE_TM = 256
MERGE_ROW_CHUNKS = 2
MLP_TM, MLP_TF = 1024, 1024
DIL_QB = 128
DIL_UNROLL = 8
DIL_ROWS_PER_STEP = 1024
N_WINDOW_CASES = 3
NA_ROWS_PER_STEP = 8
NA_UNROLL = 2

GROUP_DILATIONS = tuple(d for _, d in DILATION_PATTERNS) + (1,)
ROW_ORDERS = tuple(sorted(set(GROUP_DILATIONS)))
GROUP_ROW_ORDER = tuple(ROW_ORDERS.index(d) for d in GROUP_DILATIONS)


def _params(sem):
    return pltpu.CompilerParams(dimension_semantics=sem, vmem_limit_bytes=VMEM_LIMIT_BYTES)


def _rms(x, g):
    ms = jnp.mean(x * x, axis=-1, keepdims=True)
    return (x * lax.rsqrt(ms + EPS)) * g


def _norm_kernel(x_ref, g_ref, h_ref, slab_ref):
    tm = x_ref.shape[0]
    x = x_ref[...]
    inv = lax.rsqrt(jnp.mean(x * x, axis=-1, keepdims=True) + EPS)
    n_slabs = NORM_CHUNK // LANES
    for c0 in range(0, D_MODEL, NORM_CHUNK):
        hc = (x_ref[:, c0:c0 + NORM_CHUNK] * inv) * g_ref[:, c0:c0 + NORM_CHUNK]
        h_ref[0, :, c0:c0 + NORM_CHUNK] = hc.astype(BF16)
        for s in range(n_slabs):
            slab_ref[0, s] = hc[:, s * LANES:(s + 1) * LANES]
        for v in range(1, len(ROW_ORDERS)):
            d_prev, d = ROW_ORDERS[v - 1], ROW_ORDERS[v]
            q = d // d_prev
            n_prev, n = tm // d_prev, tm // d
            last = v == len(ROW_ORDERS) - 1
            for s in range(n_slabs):
                cols = slice(c0 + s * LANES, c0 + (s + 1) * LANES)
                for r in range(d_prev):
                    for r2 in range(q):
                        rows = slab_ref[(v - 1) % 2, s, pl.ds(r * n_prev + r2, n, stride=q), :]
                        dst = (d_prev * r2 + r) * n
                        h_ref[v, dst:dst + n, cols] = rows.astype(BF16)
                        if not last:
                            slab_ref[v % 2, s, dst:dst + n, :] = rows


def _norm(x2d, g):
    m = x2d.shape[0]
    tm = PERM_TILE
    nv = len(ROW_ORDERS)
    assert ROW_ORDERS[0] == 1
    return pl.pallas_call(
        _norm_kernel,
        grid=(m // tm,),
        in_specs=[pl.BlockSpec((tm, D_MODEL), lambda i: (i, 0)),
                  pl.BlockSpec((1, D_MODEL), lambda i: (0, 0))],
        out_specs=pl.BlockSpec((nv, tm, D_MODEL), lambda i: (0, i, 0)),
        out_shape=jax.ShapeDtypeStruct((nv, m, D_MODEL), BF16),
        scratch_shapes=[pltpu.VMEM((2, NORM_CHUNK // LANES, tm, LANES), F32)],
        compiler_params=_params(("arbitrary",)),
        name="rmsnorm_row_orders",
    )(x2d, g)


def _proj_kernel(h_ref, w_first_ref, wq_next_ref, wg_next_ref, bg_ref, wout_ref, wpa_ref, wpb_ref,
                 qkv_ref, gate_ref, wout_bf_ref, wpa_bf_ref, wpb_bf_ref, w_bf_ref, *, n_qkv_steps):
    j = pl.program_id(0)
    i = pl.program_id(1)

    @pl.when((j == 0) & (i == 0))
    def _():
        w_bf_ref[0] = w_first_ref[...].astype(BF16)

    def side_casts(next_ref):
        part = next_ref.shape[0]
        rows = pl.ds(pl.multiple_of(i * part, part), part)
        w_bf_ref[(j + 1) % 2, rows, :] = next_ref[...].astype(BF16)
        wout_bf_ref[...] = wout_ref[...].astype(BF16)
        wpa_bf_ref[...] = wpa_ref[...].astype(BF16)
        wpb_bf_ref[...] = wpb_ref[...].astype(BF16)

    def qkv_step(next_ref):
        y = jnp.dot(h_ref[...], w_bf_ref[j % 2], preferred_element_type=F32)
        qkv_ref[...] = y.astype(BF16)
        side_casts(next_ref)

    @pl.when(j < n_qkv_steps - 1)
    def _():
        qkv_step(wq_next_ref)

    @pl.when(j == n_qkv_steps - 1)
    def _():
        qkv_step(wg_next_ref)

    @pl.when(j >= n_qkv_steps)
    def _():
        z = jnp.dot(h_ref[...], w_bf_ref[j % 2], preferred_element_type=F32) + bg_ref[...]
        gate_ref[...] = (0.5 * jnp.tanh(0.5 * z) + 0.5).astype(BF16)
        side_casts(wg_next_ref)


def _project(h_orders, w_qkv, w_gate, b_gate, w_out, w_proj_a, w_proj_b):
    m = h_orders.shape[1]
    tm, tn = PROJ_TM, PROJ_TN
    n_groups = N_HEAD_GROUPS
    nq = 3 * n_groups
    ng = (2 * D_MODEL) // tn
    n_i = m // tm
    row_order = GROUP_ROW_ORDER

    def lhs_map(j, i):
        v = jnp.int32(0)
        for grp in range(n_groups):
            v = jnp.where(j // 3 == grp, row_order[grp], v)
        return (v, i, 0)

    def w_qkv_col(j):
        jj = jnp.minimum(j, nq - 1)
        return (jj % 3) * n_groups + jj // 3

    gate_col = lambda j, i: (0, jnp.maximum(j - nq, 0))
    wq_next_map = lambda j, i: (jnp.where(j + 1 < nq, i, n_i - 1), w_qkv_col(j + 1))
    wg_next_map = lambda j, i: (jnp.where(j + 1 >= nq, i, 0), jnp.clip(j + 1 - nq, 0, ng - 1))

    def qkv_out_map(j, i):
        jj = jnp.minimum(j, nq - 1)
        return (jj // 3, jnp.where(j < nq, i, n_i - 1), jj % 3)

    gate_out_map = lambda j, i: (jnp.where(j >= nq, i, 0), jnp.maximum(j - nq, 0))

    def side_rows(w):
        n_blocks = w.shape[0] // SIDE_CAST_ROWS
        return pl.BlockSpec((SIDE_CAST_ROWS, w.shape[1]),
                            lambda j, i: (jnp.minimum(j * n_i + i, n_blocks - 1), 0))

    side = [w_out, w_proj_a, w_proj_b]
    assert all(w.shape[0] // SIDE_CAST_ROWS <= (nq + ng) * n_i for w in side)
    return pl.pallas_call(
        functools.partial(_proj_kernel, n_qkv_steps=nq),
        grid=(nq + ng, n_i),
        in_specs=[
            pl.BlockSpec((None, tm, D_MODEL), lhs_map),
            pl.BlockSpec((D_MODEL, tn), lambda j, i: (0, 0), pipeline_mode=pl.Buffered(1)),
            pl.BlockSpec((D_MODEL // n_i, tn), wq_next_map),
            pl.BlockSpec((D_MODEL // n_i, tn), wg_next_map),
            pl.BlockSpec((1, tn), gate_col),
        ] + [side_rows(w) for w in side],
        out_specs=[
            pl.BlockSpec((None, tm, tn), qkv_out_map),
            pl.BlockSpec((tm, tn), gate_out_map),
        ] + [side_rows(w) for w in side],
        out_shape=[
            jax.ShapeDtypeStruct((n_groups, m, 3 * GROUP_WIDTH), BF16),
            jax.ShapeDtypeStruct((m, 2 * D_MODEL), BF16),
        ] + [jax.ShapeDtypeStruct(w.shape, BF16) for w in side],
        scratch_shapes=[pltpu.VMEM((2, D_MODEL, tn), BF16)],
        compiler_params=_params(("arbitrary", "arbitrary")),
        name="proj_qkv_gate",
    )(h_orders, w_qkv, w_qkv, w_gate, b_gate, *side)


def _dilated_block_shape(seq, half_window):
    kw = DIL_QB + 2 * half_window
    return (seq, seq) if seq <= kw else (DIL_QB, kw)


def _dilated_kernel(q_ref, k_ref, v_ref, o_ref, lse_ref, bias_ref, *, seq, half_window, coefs):
    qb, kw = _dilated_block_shape(seq, half_window)
    n_blocks = seq // qb
    n_tiles, n_res = q_ref.shape[:2]

    def rows_loader(ref):
        if n_res == 1:
            flat = ref.at[:, 0].reshape(seq, GROUP_WIDTH)
            return lambda rr, start, size, cols: flat[pl.ds(start, size), cols]
        assert n_blocks == 1
        return lambda rr, start, size, cols: jnp.concatenate(
            [ref[t, rr, :, cols] for t in range(n_tiles)], axis=0)

    load_q, load_k, load_v = rows_loader(q_ref), rows_loader(k_ref), rows_loader(v_ref)
    lane = lax.broadcasted_iota(jnp.int32, (qb, HEAD_DIM), 1)
    heads = range(HEADS_PER_GROUP)
    head_cols = [slice(h * HEAD_DIM, (h + 1) * HEAD_DIM) for h in heads]

    @pl.when((pl.program_id(0) == 0) & (pl.program_id(1) == 0))
    def _():
        qrow = lax.broadcasted_iota(jnp.int32, (qb, kw), 0)
        kcol = lax.broadcasted_iota(jnp.int32, (qb, kw), 1)
        for c in range(bias_ref.shape[0]):
            dist = jnp.abs(kcol - qrow - c * half_window)
            dist_f = dist.astype(F32)
            for h in heads:
                bias_ref[c, h] = jnp.where(dist <= half_window, (-coefs[h] * LOG2E) * dist_f, NEG)

    def block(qi, carry):
        q0 = pl.multiple_of(qi * qb, qb)
        ks = pl.multiple_of(jnp.clip(q0 - half_window, 0, seq - kw), half_window)
        window_case = (q0 - ks) // half_window
        pairs = [(rr, h) for rr in range(n_res) for h in heads]
        scores = [lax.dot_general(load_q(rr, q0, qb, head_cols[h]), load_k(rr, ks, kw, head_cols[h]),
                                  (((1,), (1,)), ((), ())), preferred_element_type=F32)
                  for rr, h in pairs]
        probs, dens = [], []
        lse_all = [jnp.zeros((qb, HEAD_DIM), F32) for _ in range(n_res)]
        for (rr, h), s in zip(pairs, scores):
            t = s * (SCALE * LOG2E) + bias_ref[window_case, h]
            m = jnp.max(t, axis=-1, keepdims=True)
            p = jnp.exp2(t - m)
            den = jnp.sum(p, axis=-1, keepdims=True)
            probs.append(p.astype(BF16))
            dens.append(den)
            lse_all[rr] = jnp.where(lane == h, m * LN2 + jnp.log(den), lse_all[rr])
        for (rr, h), p, den in zip(pairs, probs, dens):
            o = jnp.dot(p, load_v(rr, ks, kw, head_cols[h]), preferred_element_type=F32) / den
            o_ref[rr, pl.ds(q0, qb), head_cols[h]] = o.astype(BF16)
        for rr in range(n_res):
            lse_ref[rr, pl.ds(q0, qb), :] = lse_all[rr]
        return carry

    lax.fori_loop(0, n_blocks, block, 0, unroll=min(DIL_UNROLL, n_blocks))


def _dilated_group(qkv, batch, seq_total, group, window, dilation, slopes):
    d = dilation
    seq = seq_total // d
    half_window = window // (2 * d)
    qb, kw = _dilated_block_shape(seq, half_window)
    coefs = tuple(float(slopes[group * HEADS_PER_GROUP + h]) * d for h in range(HEADS_PER_GROUP))
    tiles = seq_total // PERM_TILE
    rows = PERM_TILE // d
    view = qkv.reshape(qkv.shape[0], batch, tiles, d, rows, qkv.shape[-1])
    n_res = min(d, max(1, DIL_ROWS_PER_STEP // seq))
    part = lambda which: pl.BlockSpec((None, None, tiles, n_res, rows, GROUP_WIDTH),
                                      lambda b, r: (group, b, 0, r, 0, which))
    return pl.pallas_call(
        functools.partial(_dilated_kernel, seq=seq, half_window=half_window, coefs=coefs),
        grid=(batch, d // n_res),
        in_specs=[part(0), part(1), part(2)],
        out_specs=[
            pl.BlockSpec((None, n_res, seq, GROUP_WIDTH), lambda b, r: (b, r, 0, 0)),
            pl.BlockSpec((None, n_res, seq, HEAD_DIM), lambda b, r: (b, r, 0, 0)),
        ],
        out_shape=[
            jax.ShapeDtypeStruct((batch, d, seq, GROUP_WIDTH), BF16),
            jax.ShapeDtypeStruct((batch, d, seq, HEAD_DIM), F32),
        ],
        scratch_shapes=[pltpu.VMEM((N_WINDOW_CASES, HEADS_PER_GROUP, qb, kw), F32)],
        compiler_params=_params(("arbitrary", "arbitrary")),
        name=f"dilated_attention_d{d}",
    )(view, view, view)


NA_GROUP_ROWS = NA_ROWS // 2
NA_WINDOW_ROWS = NA_ROWS + NA_GROUP_ROWS
N_BIAS_VARIANTS = 3


def _expand_na_bias(rpb_ref, out_ref, h):
    half = NA_ROWS // 2
    n_off = 2 * NA_ROWS - 1
    pad = half
    n_seg = 3 * NA_ROWS
    width = n_seg * GRID_W
    qc = lax.broadcasted_iota(jnp.int32, (GRID_W, width), 0)
    kc = lax.broadcasted_iota(jnp.int32, (GRID_W, width), 1) % GRID_W
    col_idx = jnp.clip(kc - qc, -(NA_COLS - 1), NA_COLS - 1) + (NA_COLS - 1)
    seg = lax.broadcasted_iota(jnp.int32, (1, width), 1) // GRID_W
    table = jnp.zeros((GRID_W, width), F32)
    for j in range(2 * NA_COLS - 1):
        row_vals = jnp.zeros((1, width), F32)
        for i in range(n_off):
            row_vals = jnp.where(seg == i + pad, rpb_ref[h, i, j], row_vals)
        table = jnp.where(col_idx == j, row_vals, table)
    cs = jnp.clip(qc - NA_COLS // 2, 0, GRID_W - NA_COLS)
    table = jnp.where((kc >= cs) & (kc < cs + NA_COLS), table * LOG2E, NEG)
    nkeys = NA_WINDOW_ROWS * GRID_W
    key_row = lax.broadcasted_iota(jnp.int32, (GRID_W, nkeys), 1) // GRID_W
    for v in range(N_BIAS_VARIANTS):
        for a in range(NA_GROUP_ROWS):
            first = {0: 0, 1: a, 2: NA_WINDOW_ROWS - NA_ROWS}[v]
            seg0 = (NA_ROWS - 1) - half * v - a + pad
            assert 0 <= seg0 and seg0 + NA_WINDOW_ROWS <= n_seg
            slab = table[:, seg0 * GRID_W:(seg0 + NA_WINDOW_ROWS) * GRID_W]
            slab = jnp.where((key_row >= first) & (key_row < first + NA_ROWS), slab, NEG)
            out_ref[v, h, a * GRID_W:(a + 1) * GRID_W, :] = slab


def _na_kernel(rpb_ref, q_ref, k_ref, v_ref, wup_ref, o_ref, wup_bf_ref, bias_ref, *, rows):
    rb = pl.program_id(1)
    nq = NA_GROUP_ROWS * GRID_W
    nkeys = NA_WINDOW_ROWS * GRID_W

    @pl.when((pl.program_id(0) == 0) & (rb == 0))
    def _():
        for h in range(N_HEADS_B):
            _expand_na_bias(rpb_ref, bias_ref, h)

    def one_group(gl, carry):
        r0 = rb * NA_ROWS_PER_STEP + gl * NA_GROUP_ROWS
        ws = jnp.clip(r0 - NA_ROWS // 2, 0, rows - NA_WINDOW_ROWS)
        var = (r0 - ws) // (NA_ROWS // 2)
        q0 = pl.multiple_of(gl * nq, nq)
        k0 = pl.multiple_of(ws * GRID_W, GRID_W)
        heads = range(N_HEADS_B)
        head_cols = [slice(h * HEAD_DIM, (h + 1) * HEAD_DIM) for h in heads]
        scores = [lax.dot_general(q_ref[pl.ds(q0, nq), c], k_ref[pl.ds(k0, nkeys), c],
                                  (((1,), (1,)), ((), ())), preferred_element_type=F32)
                  for c in head_cols]
        probs, dens = [], []
        for h in heads:
            t = scores[h] * (SCALE * LOG2E) + bias_ref[var, h]
            m = jnp.max(t, axis=-1, keepdims=True)
            p = jnp.exp2(t - m)
            dens.append(jnp.sum(p, axis=-1, keepdims=True))
            probs.append(p.astype(BF16))
        for h in heads:
            o = jnp.dot(probs[h], v_ref[pl.ds(k0, nkeys), head_cols[h]],
                        preferred_element_type=F32) / dens[h]
            o_ref[pl.ds(q0, nq), head_cols[h]] = o.astype(BF16)
        return carry

    n_groups = NA_ROWS_PER_STEP // NA_GROUP_ROWS
    lax.fori_loop(0, n_groups, one_group, 0, unroll=min(NA_UNROLL, n_groups))
    wup_bf_ref[...] = wup_ref[...].astype(BF16)


def _neighbourhood(qkv, rpb, batch, seq_total, w_up):
    group = N_GROUPS_A
    assert GROUP_DILATIONS[group] == 1
    rows = seq_total // GRID_W
    assert rows >= NA_WINDOW_ROWS and rows % NA_ROWS_PER_STEP == 0
    assert NA_ROWS_PER_STEP % NA_GROUP_ROWS == 0
    tq = NA_ROWS_PER_STEP * GRID_W
    steps_per_batch = rows // NA_ROWS_PER_STEP
    view = qkv.reshape(qkv.shape[0], batch, seq_total, qkv.shape[-1])
    wup_rows = pl.BlockSpec((w_up.shape[0] // (batch * steps_per_batch), w_up.shape[1]),
                            lambda b, i: (b * steps_per_batch + i, 0))
    o, w_up_bf = pl.pallas_call(
        functools.partial(_na_kernel, rows=rows),
        grid=(batch, steps_per_batch),
        in_specs=[
            pl.BlockSpec(memory_space=pltpu.SMEM),
            pl.BlockSpec((None, None, tq, GROUP_WIDTH), lambda b, i: (group, b, i, 0)),
            pl.BlockSpec((None, None, seq_total, GROUP_WIDTH), lambda b, i: (group, b, 0, 1)),
            pl.BlockSpec((None, None, seq_total, GROUP_WIDTH), lambda b, i: (group, b, 0, 2)),
            wup_rows,
        ],
        out_specs=[pl.BlockSpec((None, tq, GROUP_WIDTH), lambda b, i: (b, i, 0)), wup_rows],
        out_shape=[jax.ShapeDtypeStruct((batch, seq_total, GROUP_WIDTH), BF16),
                   jax.ShapeDtypeStruct(w_up.shape, BF16)],
        scratch_shapes=[pltpu.VMEM((N_BIAS_VARIANTS, N_HEADS_B, NA_GROUP_ROWS * GRID_W,
                                    NA_WINDOW_ROWS * GRID_W), F32)],
        compiler_params=_params(("arbitrary", "arbitrary")),
        name="neighbourhood_attention",
    )(rpb, view, view, view, w_up)
    return o.reshape(batch * seq_total, GROUP_WIDTH), w_up_bf


def _to_token_order_matrix(tm, d):
    t = lax.broadcasted_iota(jnp.int32, (tm, tm), 0)
    c = lax.broadcasted_iota(jnp.int32, (tm, tm), 1)
    return (c == (t % d) * (tm // d) + t // d).astype(BF16)


def _merge_kernel(o0_ref, o1_ref, o2_ref, l0_ref, l1_ref, l2_ref, ob_ref, gate_ref, x_ref,
                  wpa_ref, wpb_ref, wout_ref, gn_ref, wdn_ref,
                  x2_ref, h2_ref, wdn_bf_ref, lse_tok_ref, *, dilations):
    tm = x_ref.shape[0]
    o_tok, lse_tok = [], []
    for g, (o_ref, l_ref, d) in enumerate(zip((o0_ref, o1_ref, o2_ref),
                                              (l0_ref, l1_ref, l2_ref), dilations)):
        o = o_ref[...].reshape(tm, GROUP_WIDTH)
        if d == 1:
            o_tok.append(o.astype(F32))
            lse_tok.append(l_ref[...].reshape(tm, HEAD_DIM))
        else:
            o_tok.append(jnp.dot(_to_token_order_matrix(tm, d), o, preferred_element_type=F32))
            for r in range(d):
                lse_tok_ref[g, pl.ds(r, tm // d, stride=d), :] = l_ref[r]
            lse_tok.append(lse_tok_ref[g])
    l0, l1, l2 = lse_tok
    mx = jnp.maximum(jnp.maximum(l0, l1), l2)
    e0, e1, e2 = jnp.exp(l0 - mx), jnp.exp(l1 - mx), jnp.exp(l2 - mx)
    tot = e0 + e1 + e2
    a0, a1, a2 = e0 / tot, e1 / tot, e2 / tot
    parts = []
    for h in range(HEADS_PER_GROUP):
        cols = slice(h * HEAD_DIM, (h + 1) * HEAD_DIM)
        y = (a0[:, h:h + 1] * o_tok[0][:, cols] + a1[:, h:h + 1] * o_tok[1][:, cols]
             + a2[:, h:h + 1] * o_tok[2][:, cols])
        parts.append(y.astype(BF16))
    ya = jnp.concatenate(parts, axis=1)
    half = tm // MERGE_ROW_CHUNKS
    x2s = []
    for r0 in range(0, tm, half):
        rows = slice(r0, r0 + half)
        ta = jnp.dot(ya[rows], wpa_ref[...], preferred_element_type=F32)
        tb = jnp.dot(ob_ref[rows, :], wpb_ref[...], preferred_element_type=F32)
        merged = (gate_ref[rows, :D_MODEL].astype(F32) * ta
                  + gate_ref[rows, D_MODEL:].astype(F32) * tb)
        x2 = x_ref[rows, :] + jnp.dot(merged.astype(BF16), wout_ref[...],
                                      preferred_element_type=F32)
        x2_ref[rows, :] = x2
        x2s.append(x2)
        if r0 == 0:
            wdn_bf_ref[...] = wdn_ref[...].astype(BF16)
    for r0, x2 in zip(range(0, tm, half), x2s):
        h2_ref[r0:r0 + half, :] = _rms(x2, gn_ref[...]).astype(BF16)


def _merge(o_groups, lse_groups, ob, gates, x2d, seq, wpa, wpb, wout, gn, w_down):
    m = x2d.shape[0]
    tm = MERGE_TM
    n_steps = m // tm
    tiles_per_batch = seq // tm
    dilations = tuple(o.shape[1] for o in o_groups)
    row = lambda w: pl.BlockSpec((tm, w), lambda i: (i, 0))
    wdn_rows = pl.BlockSpec((w_down.shape[0] // n_steps, w_down.shape[1]), lambda i: (i, 0))
    grouped = lambda d, w: pl.BlockSpec(
        (None, d, tm // d, w), lambda i: (i // tiles_per_batch, 0, i % tiles_per_batch, 0))
    const = lambda a, b: pl.BlockSpec((a, b), lambda i: (0, 0), pipeline_mode=pl.Buffered(1))
    return pl.pallas_call(
        functools.partial(_merge_kernel, dilations=dilations),
        grid=(n_steps,),
        in_specs=[grouped(d, GROUP_WIDTH) for d in dilations]
        + [grouped(d, HEAD_DIM) for d in dilations]
        + [row(GROUP_WIDTH), row(2 * D_MODEL), row(D_MODEL),
           const(GROUP_WIDTH, D_MODEL), const(GROUP_WIDTH, D_MODEL), const(D_MODEL, D_MODEL),
           const(1, D_MODEL), wdn_rows],
        out_specs=[row(D_MODEL), row(D_MODEL), wdn_rows],
        out_shape=[jax.ShapeDtypeStruct((m, D_MODEL), F32), jax.ShapeDtypeStruct((m, D_MODEL), BF16),
                   jax.ShapeDtypeStruct(w_down.shape, BF16)],
        scratch_shapes=[pltpu.VMEM((len(dilations), tm, HEAD_DIM), F32)],
        compiler_params=_params(("arbitrary",)),
        name="merge_out_proj",
    )(*o_groups, *lse_groups, ob, gates, x2d, wpa, wpb, wout, gn, w_down)


def _mlp_kernel(h2_ref, wup_ref, wdn_ref, x2_hbm, gf_ref, out_ref, x2_sem, *, n_f):
    i = pl.program_id(0)
    f = pl.program_id(1)
    tm = out_ref.shape[0]

    def residual_copy():
        return pltpu.make_async_copy(x2_hbm.at[pl.ds(i * tm, tm), :], out_ref, x2_sem)

    def up_down(first):
        if first:
            residual_copy().start()
        hid = jnp.dot(h2_ref[...], wup_ref[...], preferred_element_type=F32)
        hid = jnp.square(jnp.maximum(hid, 0.0)).astype(BF16)
        if first:
            residual_copy().wait()
        out_ref[...] += jnp.dot(hid, wdn_ref[...], preferred_element_type=F32)

    @pl.when(f == 0)
    def _():
        up_down(first=True)

    @pl.when(f > 0)
    def _():
        up_down(first=False)

    @pl.when(f == n_f - 1)
    def _():
        out_ref[...] = _rms(out_ref[...], gf_ref[...])


def _mlp(h2, wup, wdn, x2, gf):
    m = h2.shape[0]
    tm, tf = MLP_TM, MLP_TF
    n_f = D_FF // tf
    return pl.pallas_call(
        functools.partial(_mlp_kernel, n_f=n_f),
        grid=(m // tm, n_f),
        in_specs=[
            pl.BlockSpec((tm, D_MODEL), lambda i, f: (i, 0)),
            pl.BlockSpec((D_MODEL, tf), lambda i, f: (0, f)),
            pl.BlockSpec((tf, D_MODEL), lambda i, f: (f, 0)),
            pl.BlockSpec(memory_space=pl.ANY),
            pl.BlockSpec((1, D_MODEL), lambda i, f: (0, 0)),
        ],
        out_specs=pl.BlockSpec((tm, D_MODEL), lambda i, f: (i, 0)),
        out_shape=jax.ShapeDtypeStruct((m, D_MODEL), F32),
        scratch_shapes=[pltpu.SemaphoreType.DMA],
        compiler_params=_params(("arbitrary", "arbitrary")),
        name="mlp_residual_norm",
    )(h2, wup, wdn, x2, gf)


def _layer(x2d, batch, seq, norm_mix, w_qkv, w_gate, b_gate, rpb, w_proj_a, w_proj_b, w_out,
           norm_mlp, w_up, w_down):
    slopes = 2.0 ** (-8.0 * np.arange(1, N_HEADS_A + 1) / N_HEADS_A)
    row = lambda v: v.reshape(1, -1)
    h_orders = _norm(x2d, row(norm_mix))
    qkv, gates, w_out_bf, w_pa_bf, w_pb_bf = _project(
        h_orders, w_qkv, w_gate, row(b_gate), w_out, w_proj_a, w_proj_b)
    o_groups, lse_groups = [], []
    for g, (window, d) in enumerate(DILATION_PATTERNS):
        o, lse = _dilated_group(qkv, batch, seq, g, window, d, slopes)
        o_groups.append(o)
        lse_groups.append(lse)
    ob, w_up_bf = _neighbourhood(qkv, rpb, batch, seq, w_up)
    x2, h2, w_down_bf = _merge(o_groups, lse_groups, ob, gates, x2d, seq, w_pa_bf, w_pb_bf,
                               w_out_bf, row(norm_mlp), w_down)
    return x2, h2, w_up_bf, w_down_bf


def kernel(x, norm_mix, w_qkv, w_gate, b_gate, rpb, w_proj_a, w_proj_b, w_out, norm_mlp, w_up,
           w_down, norm_final):
    batch, seq, _ = x.shape
    depth = norm_mix.shape[0]
    assert depth == 1 and seq % PERM_TILE == 0
    x2d = x.reshape(batch * seq, D_MODEL)
    x2, h2, w_up_bf, w_down_bf = _layer(
        x2d, batch, seq, norm_mix[0], w_qkv[0], w_gate[0], b_gate[0], rpb[0],
        w_proj_a[0], w_proj_b[0], w_out[0], norm_mlp[0], w_up[0], w_down[0])
    out = _mlp(h2, w_up_bf, w_down_bf, x2, norm_final.reshape(1, -1))
    return out.reshape(batch, seq, D_MODEL)
```

```python
import functools

import jax
import jax.numpy as jnp
import numpy as np
from jax import lax
from jax.experimental import pallas as pl
from jax.experimental.pallas import tpu as pltpu

D_MODEL = 2048
HEAD_DIM = 128
N_HEADS = D_MODEL // HEAD_DIM
N_HEADS_B = N_HEADS // 4
N_HEADS_A = N_HEADS - N_HEADS_B
DILATION_PATTERNS = ((128, 1), (512, 4), (2048, 16))
N_GROUPS_A = len(DILATION_PATTERNS)
HEADS_PER_GROUP = N_HEADS_A // N_GROUPS_A
GROUP_WIDTH = HEADS_PER_GROUP * HEAD_DIM
N_HEAD_GROUPS = N_HEADS // HEADS_PER_GROUP
GRID_W = 64
NA_ROWS = 8
NA_COLS = 16
D_FF = 4 * D_MODEL
EPS = 1e-6
NEG = -1e30
SCALE = HEAD_DIM ** -0.5
LOG2E = float(np.log2(np.e))
LN2 = float(np.log(2.0))
LANES = 128

F32 = jnp.float32
BF16 = jnp.bfloat16

VMEM_LIMIT_BYTES = 56 * 1024 * 1024

PERM_TILE = 1024
NORM_CHUNK = 512
PROJ_TM, PROJ_TN = 2048, GROUP_WIDTH
SIDE_CAST_ROWS = 32
MERGE_TM = 256
MLP_TM, MLP_TF = 1024, 1024
MLP_FINAL_ROW_CHUNKS = 4
DIL_QB = 128
DIL_UNROLL = 8
DIL_ROWS_PER_STEP = 1024
N_WINDOW_CASES = 3
NA_ROWS_PER_STEP = 8
NA_UNROLL = 2

GROUP_DILATIONS = tuple(d for _, d in DILATION_PATTERNS) + (1,)
ROW_ORDERS = tuple(sorted(set(GROUP_DILATIONS)))
GROUP_ROW_ORDER = tuple(ROW_ORDERS.index(d) for d in GROUP_DILATIONS)


def _params(sem):
    return pltpu.CompilerParams(dimension_semantics=sem, vmem_limit_bytes=VMEM_LIMIT_BYTES)


def _rms(x, g):
    ms = jnp.mean(x * x, axis=-1, keepdims=True)
    return (x * lax.rsqrt(ms + EPS)) * g


def _norm_kernel(x_ref, g_ref, h_ref, slab_ref):
    tm = x_ref.shape[0]
    x = x_ref[...]
    inv = lax.rsqrt(jnp.mean(x * x, axis=-1, keepdims=True) + EPS)
    n_slabs = NORM_CHUNK // LANES
    for c0 in range(0, D_MODEL, NORM_CHUNK):
        hc = (x_ref[:, c0:c0 + NORM_CHUNK] * inv) * g_ref[:, c0:c0 + NORM_CHUNK]
        h_ref[0, :, c0:c0 + NORM_CHUNK] = hc.astype(BF16)
        for s in range(n_slabs):
            slab_ref[0, s] = hc[:, s * LANES:(s + 1) * LANES]
        for v in range(1, len(ROW_ORDERS)):
            d_prev, d = ROW_ORDERS[v - 1], ROW_ORDERS[v]
            q = d // d_prev
            n_prev, n = tm // d_prev, tm // d
            last = v == len(ROW_ORDERS) - 1
            for s in range(n_slabs):
                cols = slice(c0 + s * LANES, c0 + (s + 1) * LANES)
                for r in range(d_prev):
                    for r2 in range(q):
                        rows = slab_ref[(v - 1) % 2, s, pl.ds(r * n_prev + r2, n, stride=q), :]
                        dst = (d_prev * r2 + r) * n
                        h_ref[v, dst:dst + n, cols] = rows.astype(BF16)
                        if not last:
                            slab_ref[v % 2, s, dst:dst + n, :] = rows


def _norm(x2d, g):
    m = x2d.shape[0]
    tm = PERM_TILE
    nv = len(ROW_ORDERS)
    assert ROW_ORDERS[0] == 1
    return pl.pallas_call(
        _norm_kernel,
        grid=(m // tm,),
        in_specs=[pl.BlockSpec((tm, D_MODEL), lambda i: (i, 0)),
                  pl.BlockSpec((1, D_MODEL), lambda i: (0, 0))],
        out_specs=pl.BlockSpec((nv, tm, D_MODEL), lambda i: (0, i, 0)),
        out_shape=jax.ShapeDtypeStruct((nv, m, D_MODEL), BF16),
        scratch_shapes=[pltpu.VMEM((2, NORM_CHUNK // LANES, tm, LANES), F32)],
        compiler_params=_params(("arbitrary",)),
        name="rmsnorm_row_orders",
    )(x2d, g)


def _proj_kernel(h_ref, w_first_ref, wq_next_ref, wg_next_ref, bg_ref, wout_ref, wpa_ref, wpb_ref,
                 qkv_ref, gate_ref, wout_bf_ref, wpa_bf_ref, wpb_bf_ref, w_bf_ref, *, n_qkv_steps):
    j = pl.program_id(0)
    i = pl.program_id(1)

    @pl.when((j == 0) & (i == 0))
    def _():
        w_bf_ref[0] = w_first_ref[...].astype(BF16)

    def side_casts(next_ref):
        part = next_ref.shape[0]
        rows = pl.ds(pl.multiple_of(i * part, part), part)
        w_bf_ref[(j + 1) % 2, rows, :] = next_ref[...].astype(BF16)
        wout_bf_ref[...] = wout_ref[...].astype(BF16)
        wpa_bf_ref[...] = wpa_ref[...].astype(BF16)
        wpb_bf_ref[...] = wpb_ref[...].astype(BF16)

    def qkv_step(next_ref):
        y = jnp.dot(h_ref[...], w_bf_ref[j % 2], preferred_element_type=F32)
        qkv_ref[...] = y.astype(BF16)
        side_casts(next_ref)

    @pl.when(j < n_qkv_steps - 1)
    def _():
        qkv_step(wq_next_ref)

    @pl.when(j == n_qkv_steps - 1)
    def _():
        qkv_step(wg_next_ref)

    @pl.when(j >= n_qkv_steps)
    def _():
        z = jnp.dot(h_ref[...], w_bf_ref[j % 2], preferred_element_type=F32) + bg_ref[...]
        gate_ref[...] = 0.5 * jnp.tanh((0.5 * z).astype(BF16)) + 0.5
        side_casts(wg_next_ref)


def _project(h_orders, w_qkv, w_gate, b_gate, w_out, w_proj_a, w_proj_b):
    m = h_orders.shape[1]
    tm, tn = PROJ_TM, PROJ_TN
    n_groups = N_HEAD_GROUPS
    nq = 3 * n_groups
    ng = (2 * D_MODEL) // tn
    n_i = m // tm
    row_order = GROUP_ROW_ORDER

    def lhs_map(j, i):
        v = jnp.int32(0)
        for grp in range(n_groups):
            v = jnp.where(j // 3 == grp, row_order[grp], v)
        return (v, i, 0)

    def w_qkv_col(j):
        jj = jnp.minimum(j, nq - 1)
        return (jj % 3) * n_groups + jj // 3

    gate_col = lambda j, i: (0, jnp.maximum(j - nq, 0))
    wq_next_map = lambda j, i: (jnp.where(j + 1 < nq, i, n_i - 1), w_qkv_col(j + 1))
    wg_next_map = lambda j, i: (jnp.where(j + 1 >= nq, i, 0), jnp.clip(j + 1 - nq, 0, ng - 1))

    def qkv_out_map(j, i):
        jj = jnp.minimum(j, nq - 1)
        return (jj // 3, jnp.where(j < nq, i, n_i - 1), jj % 3)

    gate_out_map = lambda j, i: (jnp.where(j >= nq, i, 0), jnp.maximum(j - nq, 0))

    def side_rows(w):
        n_blocks = w.shape[0] // SIDE_CAST_ROWS
        return pl.BlockSpec((SIDE_CAST_ROWS, w.shape[1]),
                            lambda j, i: (jnp.minimum(j * n_i + i, n_blocks - 1), 0))

    side = [w_out, w_proj_a, w_proj_b]
    assert all(w.shape[0] // SIDE_CAST_ROWS <= (nq + ng) * n_i for w in side)
    return pl.pallas_call(
        functools.partial(_proj_kernel, n_qkv_steps=nq),
        grid=(nq + ng, n_i),
        in_specs=[
            pl.BlockSpec((None, tm, D_MODEL), lhs_map),
            pl.BlockSpec((D_MODEL, tn), lambda j, i: (0, 0), pipeline_mode=pl.Buffered(1)),
            pl.BlockSpec((D_MODEL // n_i, tn), wq_next_map),
            pl.BlockSpec((D_MODEL // n_i, tn), wg_next_map),
            pl.BlockSpec((1, tn), gate_col),
        ] + [side_rows(w) for w in side],
        out_specs=[
            pl.BlockSpec((None, tm, tn), qkv_out_map),
            pl.BlockSpec((tm, tn), gate_out_map),
        ] + [side_rows(w) for w in side],
        out_shape=[
            jax.ShapeDtypeStruct((n_groups, m, 3 * GROUP_WIDTH), BF16),
            jax.ShapeDtypeStruct((m, 2 * D_MODEL), BF16),
        ] + [jax.ShapeDtypeStruct(w.shape, BF16) for w in side],
        scratch_shapes=[pltpu.VMEM((2, D_MODEL, tn), BF16)],
        compiler_params=_params(("arbitrary", "arbitrary")),
        name="proj_qkv_gate",
    )(h_orders, w_qkv, w_qkv, w_gate, b_gate, *side)


def _dilated_block_shape(seq, half_window):
    kw = DIL_QB + 2 * half_window
    return (seq, seq) if seq <= kw else (DIL_QB, kw)


def _dilated_kernel(*refs, jobs):
    n = len(jobs)
    for idx, (seq, half_window, coefs) in enumerate(jobs):
        _dilated_job(*refs[3 * idx:3 * idx + 3], *refs[3 * n + 2 * idx:3 * n + 2 * idx + 2],
                     refs[5 * n + idx], seq=seq, half_window=half_window, coefs=coefs)


def _dilated_job(q_ref, k_ref, v_ref, o_ref, lse_ref, bias_ref, *, seq, half_window, coefs):
    qb, kw = _dilated_block_shape(seq, half_window)
    n_blocks = seq // qb
    n_tiles, n_res = q_ref.shape[:2]

    def rows_loader(ref):
        if n_res == 1:
            flat = ref.at[:, 0].reshape(seq, GROUP_WIDTH)
            return lambda rr, start, size, cols: flat[pl.ds(start, size), cols]
        assert n_blocks == 1
        return lambda rr, start, size, cols: jnp.concatenate(
            [ref[t, rr, :, cols] for t in range(n_tiles)], axis=0)

    load_q, load_k, load_v = rows_loader(q_ref), rows_loader(k_ref), rows_loader(v_ref)
    lane = lax.broadcasted_iota(jnp.int32, (qb, HEAD_DIM), 1)
    heads = range(HEADS_PER_GROUP)
    head_cols = [slice(h * HEAD_DIM, (h + 1) * HEAD_DIM) for h in heads]

    @pl.when((pl.program_id(0) == 0) & (pl.program_id(1) == 0))
    def _():
        qrow = lax.broadcasted_iota(jnp.int32, (qb, kw), 0)
        kcol = lax.broadcasted_iota(jnp.int32, (qb, kw), 1)
        for c in range(bias_ref.shape[0]):
            dist = jnp.abs(kcol - qrow - c * half_window)
            dist_f = dist.astype(F32)
            for h in heads:
                bias_ref[c, h] = jnp.where(dist <= half_window, (-coefs[h] * LOG2E) * dist_f, NEG)

    def block(qi, carry):
        q0 = pl.multiple_of(qi * qb, qb)
        ks = pl.multiple_of(jnp.clip(q0 - half_window, 0, seq - kw), half_window)
        window_case = (q0 - ks) // half_window
        pairs = [(rr, h) for rr in range(n_res) for h in heads]
        scores = [lax.dot_general(load_q(rr, q0, qb, head_cols[h]), load_k(rr, ks, kw, head_cols[h]),
                                  (((1,), (1,)), ((), ())), preferred_element_type=F32)
                  for rr, h in pairs]
        probs, dens = [], []
        lse_all = [jnp.zeros((qb, HEAD_DIM), F32) for _ in range(n_res)]
        for (rr, h), s in zip(pairs, scores):
            t = s * (SCALE * LOG2E) + bias_ref[window_case, h]
            m = jnp.max(t, axis=-1, keepdims=True)
            p = jnp.exp2(t - m)
            den = jnp.sum(p, axis=-1, keepdims=True)
            probs.append(p.astype(BF16))
            dens.append(den)
            lse_all[rr] = jnp.where(lane == h, m * LN2 + jnp.log(den), lse_all[rr])
        for (rr, h), p, den in zip(pairs, probs, dens):
            o = jnp.dot(p, load_v(rr, ks, kw, head_cols[h]), preferred_element_type=F32) / den
            o_ref[rr, pl.ds(q0, qb), head_cols[h]] = o.astype(BF16)
        for rr in range(n_res):
            lse_ref[rr, pl.ds(q0, qb), :] = lse_all[rr]
        return carry

    lax.fori_loop(0, n_blocks, block, 0, unroll=min(DIL_UNROLL, n_blocks))


def _dilated_steps(seq_total, window, dilation):
    seq = seq_total // dilation
    qb, _ = _dilated_block_shape(seq, window // (2 * dilation))
    n_res = min(dilation, max(1, DIL_ROWS_PER_STEP // seq)) if qb == seq else 1
    return n_res, dilation // n_res


def _dilated_groups(qkv, batch, seq_total, groups, slopes):
    tiles = seq_total // PERM_TILE
    jobs, operands, in_specs, out_specs, out_shapes, scratch = [], [], [], [], [], []
    steps = {_dilated_steps(seq_total, w, d)[1] for _, w, d in groups}
    assert len(steps) == 1
    for group, window, d in groups:
        seq = seq_total // d
        half_window = window // (2 * d)
        qb, kw = _dilated_block_shape(seq, half_window)
        n_res, _ = _dilated_steps(seq_total, window, d)
        coefs = tuple(float(slopes[group * HEADS_PER_GROUP + h]) * d
                      for h in range(HEADS_PER_GROUP))
        jobs.append((seq, half_window, coefs))
        rows = PERM_TILE // d
        view = qkv.reshape(qkv.shape[0], batch, tiles, d, rows, qkv.shape[-1])
        for which in range(3):
            operands.append(view)
            in_specs.append(pl.BlockSpec(
                (None, None, tiles, n_res, rows, GROUP_WIDTH),
                lambda b, r, group=group, which=which: (group, b, 0, r, 0, which)))
        out_specs += [pl.BlockSpec((None, n_res, seq, GROUP_WIDTH), lambda b, r: (b, r, 0, 0)),
                      pl.BlockSpec((None, n_res, seq, HEAD_DIM), lambda b, r: (b, r, 0, 0))]
        out_shapes += [jax.ShapeDtypeStruct((batch, d, seq, GROUP_WIDTH), BF16),
                       jax.ShapeDtypeStruct((batch, d, seq, HEAD_DIM), F32)]
        scratch.append(pltpu.VMEM((N_WINDOW_CASES, HEADS_PER_GROUP, qb, kw), F32))
    outs = pl.pallas_call(
        functools.partial(_dilated_kernel, jobs=tuple(jobs)),
        grid=(batch, steps.pop()),
        in_specs=in_specs,
        out_specs=out_specs,
        out_shape=out_shapes,
        scratch_shapes=scratch,
        compiler_params=_params(("arbitrary", "arbitrary")),
        name="dilated_attention_d" + "_d".join(str(d) for _, _, d in groups),
    )(*operands)
    return [(outs[2 * i], outs[2 * i + 1]) for i in range(len(groups))]


NA_GROUP_ROWS = NA_ROWS // 2
NA_WINDOW_ROWS = NA_ROWS + NA_GROUP_ROWS
N_BIAS_VARIANTS = 3


def _expand_na_bias(rpb_ref, out_ref, h):
    half = NA_ROWS // 2
    n_off = 2 * NA_ROWS - 1
    pad = half
    n_seg = 3 * NA_ROWS
    width = n_seg * GRID_W
    qc = lax.broadcasted_iota(jnp.int32, (GRID_W, width), 0)
    kc = lax.broadcasted_iota(jnp.int32, (GRID_W, width), 1) % GRID_W
    col_idx = jnp.clip(kc - qc, -(NA_COLS - 1), NA_COLS - 1) + (NA_COLS - 1)
    seg = lax.broadcasted_iota(jnp.int32, (1, width), 1) // GRID_W
    table = jnp.zeros((GRID_W, width), F32)
    for j in range(2 * NA_COLS - 1):
        row_vals = jnp.zeros((1, width), F32)
        for i in range(n_off):
            row_vals = jnp.where(seg == i + pad, rpb_ref[h, i, j], row_vals)
        table = jnp.where(col_idx == j, row_vals, table)
    cs = jnp.clip(qc - NA_COLS // 2, 0, GRID_W - NA_COLS)
    table = jnp.where((kc >= cs) & (kc < cs + NA_COLS), table * LOG2E, NEG)
    nkeys = NA_WINDOW_ROWS * GRID_W
    key_row = lax.broadcasted_iota(jnp.int32, (GRID_W, nkeys), 1) // GRID_W
    for v in range(N_BIAS_VARIANTS):
        for a in range(NA_GROUP_ROWS):
            first = {0: 0, 1: a, 2: NA_WINDOW_ROWS - NA_ROWS}[v]
            seg0 = (NA_ROWS - 1) - half * v - a + pad
            assert 0 <= seg0 and seg0 + NA_WINDOW_ROWS <= n_seg
            slab = table[:, seg0 * GRID_W:(seg0 + NA_WINDOW_ROWS) * GRID_W]
            slab = jnp.where((key_row >= first) & (key_row < first + NA_ROWS), slab, NEG)
            out_ref[v, h, a * GRID_W:(a + 1) * GRID_W, :] = slab


def _na_kernel(rpb_ref, q_ref, k_ref, v_ref, wup_ref, o_ref, wup_bf_ref, bias_ref, *, rows):
    rb = pl.program_id(1)
    nq = NA_GROUP_ROWS * GRID_W
    nkeys = NA_WINDOW_ROWS * GRID_W

    @pl.when((pl.program_id(0) == 0) & (rb == 0))
    def _():
        for h in range(N_HEADS_B):
            _expand_na_bias(rpb_ref, bias_ref, h)

    def one_group(gl, carry):
        r0 = rb * NA_ROWS_PER_STEP + gl * NA_GROUP_ROWS
        ws = jnp.clip(r0 - NA_ROWS // 2, 0, rows - NA_WINDOW_ROWS)
        var = (r0 - ws) // (NA_ROWS // 2)
        q0 = pl.multiple_of(gl * nq, nq)
        k0 = pl.multiple_of(ws * GRID_W, GRID_W)
        heads = range(N_HEADS_B)
        head_cols = [slice(h * HEAD_DIM, (h + 1) * HEAD_DIM) for h in heads]
        scores = [lax.dot_general(q_ref[pl.ds(q0, nq), c], k_ref[pl.ds(k0, nkeys), c],
                                  (((1,), (1,)), ((), ())), preferred_element_type=F32)
                  for c in head_cols]
        probs, dens = [], []
        for h in heads:
            t = scores[h] * (SCALE * LOG2E) + bias_ref[var, h]
            m = jnp.max(t, axis=-1, keepdims=True)
            p = jnp.exp2(t - m)
            dens.append(jnp.sum(p, axis=-1, keepdims=True))
            probs.append(p.astype(BF16))
        for h in heads:
            o = jnp.dot(probs[h], v_ref[pl.ds(k0, nkeys), head_cols[h]],
                        preferred_element_type=F32) / dens[h]
            o_ref[pl.ds(q0, nq), head_cols[h]] = o.astype(BF16)
        return carry

    n_groups = NA_ROWS_PER_STEP // NA_GROUP_ROWS
    lax.fori_loop(0, n_groups, one_group, 0, unroll=min(NA_UNROLL, n_groups))
    wup_bf_ref[...] = wup_ref[...].astype(BF16)


def _neighbourhood(qkv, rpb, batch, seq_total, w_up):
    group = N_GROUPS_A
    assert GROUP_DILATIONS[group] == 1
    rows = seq_total // GRID_W
    assert rows >= NA_WINDOW_ROWS and rows % NA_ROWS_PER_STEP == 0
    assert NA_ROWS_PER_STEP % NA_GROUP_ROWS == 0
    tq = NA_ROWS_PER_STEP * GRID_W
    steps_per_batch = rows // NA_ROWS_PER_STEP
    view = qkv.reshape(qkv.shape[0], batch, seq_total, qkv.shape[-1])
    wup_rows = pl.BlockSpec((w_up.shape[0] // (batch * steps_per_batch), w_up.shape[1]),
                            lambda b, i: (b * steps_per_batch + i, 0))
    o, w_up_bf = pl.pallas_call(
        functools.partial(_na_kernel, rows=rows),
        grid=(batch, steps_per_batch),
        in_specs=[
            pl.BlockSpec(memory_space=pltpu.SMEM),
            pl.BlockSpec((None, None, tq, GROUP_WIDTH), lambda b, i: (group, b, i, 0)),
            pl.BlockSpec((None, None, seq_total, GROUP_WIDTH), lambda b, i: (group, b, 0, 1)),
            pl.BlockSpec((None, None, seq_total, GROUP_WIDTH), lambda b, i: (group, b, 0, 2)),
            wup_rows,
        ],
        out_specs=[pl.BlockSpec((None, tq, GROUP_WIDTH), lambda b, i: (b, i, 0)), wup_rows],
        out_shape=[jax.ShapeDtypeStruct((batch, seq_total, GROUP_WIDTH), BF16),
                   jax.ShapeDtypeStruct(w_up.shape, BF16)],
        scratch_shapes=[pltpu.VMEM((N_BIAS_VARIANTS, N_HEADS_B, NA_GROUP_ROWS * GRID_W,
                                    NA_WINDOW_ROWS * GRID_W), F32)],
        compiler_params=_params(("arbitrary", "arbitrary")),
        name="neighbourhood_attention",
    )(rpb, view, view, view, w_up)
    return o.reshape(batch * seq_total, GROUP_WIDTH), w_up_bf


def _to_token_order_matrix(tm, d):
    t = lax.broadcasted_iota(jnp.int32, (tm, tm), 0)
    c = lax.broadcasted_iota(jnp.int32, (tm, tm), 1)
    return (c == (t % d) * (tm // d) + t // d).astype(BF16)


def _merge_kernel(o0_ref, o1_ref, o2_ref, l0_ref, l1_ref, l2_ref, ob_ref, gate_ref, x_ref,
                  wpa_ref, wpb_ref, wout_ref, gn_ref, wdn_ref,
                  x2_ref, h2_ref, wdn_bf_ref, lse_tok_ref, *, dilations):
    tm = x_ref.shape[0]
    tb = jnp.dot(ob_ref[...], wpb_ref[...], preferred_element_type=F32)
    o_tok, lse_tok = [], []
    for g, (o_ref, l_ref, d) in enumerate(zip((o0_ref, o1_ref, o2_ref),
                                              (l0_ref, l1_ref, l2_ref), dilations)):
        o = o_ref[...].reshape(tm, GROUP_WIDTH)
        if d == 1:
            o_tok.append(o.astype(F32))
            lse_tok.append(l_ref[...].reshape(tm, HEAD_DIM))
        else:
            o_tok.append(jnp.dot(_to_token_order_matrix(tm, d), o, preferred_element_type=F32))
            for r in range(d):
                lse_tok_ref[g, pl.ds(r, tm // d, stride=d), :] = l_ref[r]
            lse_tok.append(lse_tok_ref[g])
    l0, l1, l2 = lse_tok
    mx = jnp.maximum(jnp.maximum(l0, l1), l2)
    e0, e1, e2 = jnp.exp(l0 - mx), jnp.exp(l1 - mx), jnp.exp(l2 - mx)
    tot = e0 + e1 + e2
    a0, a1, a2 = e0 / tot, e1 / tot, e2 / tot
    parts = []
    for h in range(HEADS_PER_GROUP):
        cols = slice(h * HEAD_DIM, (h + 1) * HEAD_DIM)
        y = (a0[:, h:h + 1] * o_tok[0][:, cols] + a1[:, h:h + 1] * o_tok[1][:, cols]
             + a2[:, h:h + 1] * o_tok[2][:, cols])
        parts.append(y.astype(BF16))
    ya = jnp.concatenate(parts, axis=1)
    ta = jnp.dot(ya, wpa_ref[...], preferred_element_type=F32)
    merged = gate_ref[:, :D_MODEL].astype(F32) * ta + gate_ref[:, D_MODEL:].astype(F32) * tb
    x2 = x_ref[...] + jnp.dot(merged.astype(BF16), wout_ref[...], preferred_element_type=F32)
    x2_ref[...] = x2
    wdn_bf_ref[...] = wdn_ref[...].astype(BF16)
    h2_ref[...] = _rms(x2, gn_ref[...]).astype(BF16)


def _merge(o_groups, lse_groups, ob, gates, x2d, seq, wpa, wpb, wout, gn, w_down):
    m = x2d.shape[0]
    tm = MERGE_TM
    n_steps = m // tm
    tiles_per_batch = seq // tm
    dilations = tuple(o.shape[1] for o in o_groups)
    row = lambda w: pl.BlockSpec((tm, w), lambda i: (i, 0))
    wdn_rows = pl.BlockSpec((w_down.shape[0] // n_steps, w_down.shape[1]), lambda i: (i, 0))
    grouped = lambda d, w: pl.BlockSpec(
        (None, d, tm // d, w), lambda i: (i // tiles_per_batch, 0, i % tiles_per_batch, 0))
    const = lambda a, b: pl.BlockSpec((a, b), lambda i: (0, 0), pipeline_mode=pl.Buffered(1))
    return pl.pallas_call(
        functools.partial(_merge_kernel, dilations=dilations),
        grid=(n_steps,),
        in_specs=[grouped(d, GROUP_WIDTH) for d in dilations]
        + [grouped(d, HEAD_DIM) for d in dilations]
        + [row(GROUP_WIDTH), row(2 * D_MODEL), row(D_MODEL),
           const(GROUP_WIDTH, D_MODEL), const(GROUP_WIDTH, D_MODEL), const(D_MODEL, D_MODEL),
           const(1, D_MODEL), wdn_rows],
        out_specs=[row(D_MODEL), row(D_MODEL), wdn_rows],
        out_shape=[jax.ShapeDtypeStruct((m, D_MODEL), F32), jax.ShapeDtypeStruct((m, D_MODEL), BF16),
                   jax.ShapeDtypeStruct(w_down.shape, BF16)],
        scratch_shapes=[pltpu.VMEM((len(dilations), tm, HEAD_DIM), F32)],
        compiler_params=_params(("arbitrary",)),
        name="merge_out_proj",
    )(*o_groups, *lse_groups, ob, gates, x2d, wpa, wpb, wout, gn, w_down)


def _mlp_kernel(h2_ref, wup_ref, wdn_ref, x2_hbm, gf_ref, out_ref, x2_sem, *, n_f):
    i = pl.program_id(0)
    f = pl.program_id(1)
    tm = out_ref.shape[0]

    def residual_copy():
        return pltpu.make_async_copy(x2_hbm.at[pl.ds(i * tm, tm), :], out_ref, x2_sem)

    def up_down(first=False, last=False):
        if first:
            residual_copy().start()
        hid = jnp.dot(h2_ref[...], wup_ref[...], preferred_element_type=F32)
        hid = jnp.square(jnp.maximum(hid, 0.0)).astype(BF16)
        if first:
            residual_copy().wait()
        if not last:
            out_ref[...] += jnp.dot(hid, wdn_ref[...], preferred_element_type=F32)
            return
        chunk = tm // MLP_FINAL_ROW_CHUNKS
        for r0 in range(0, tm, chunk):
            rows = slice(r0, r0 + chunk)
            y = out_ref[rows, :] + jnp.dot(hid[rows], wdn_ref[...], preferred_element_type=F32)
            out_ref[rows, :] = _rms(y, gf_ref[...])

    assert n_f >= 2

    @pl.when(f == 0)
    def _():
        up_down(first=True)

    @pl.when((f > 0) & (f < n_f - 1))
    def _():
        up_down()

    @pl.when(f == n_f - 1)
    def _():
        up_down(last=True)


def _mlp(h2, wup, wdn, x2, gf):
    m = h2.shape[0]
    tm, tf = MLP_TM, MLP_TF
    n_f = D_FF // tf
    return pl.pallas_call(
        functools.partial(_mlp_kernel, n_f=n_f),
        grid=(m // tm, n_f),
        in_specs=[
            pl.BlockSpec((tm, D_MODEL), lambda i, f: (i, 0)),
            pl.BlockSpec((D_MODEL, tf), lambda i, f: (0, f)),
            pl.BlockSpec((tf, D_MODEL), lambda i, f: (f, 0)),
            pl.BlockSpec(memory_space=pl.ANY),
            pl.BlockSpec((1, D_MODEL), lambda i, f: (0, 0)),
        ],
        out_specs=pl.BlockSpec((tm, D_MODEL), lambda i, f: (i, 0)),
        out_shape=jax.ShapeDtypeStruct((m, D_MODEL), F32),
        scratch_shapes=[pltpu.SemaphoreType.DMA],
        compiler_params=_params(("arbitrary", "arbitrary")),
        name="mlp_residual_norm",
    )(h2, wup, wdn, x2, gf)


def _layer(x2d, batch, seq, norm_mix, w_qkv, w_gate, b_gate, rpb, w_proj_a, w_proj_b, w_out,
           norm_mlp, w_up, w_down):
    slopes = 2.0 ** (-8.0 * np.arange(1, N_HEADS_A + 1) / N_HEADS_A)
    row = lambda v: v.reshape(1, -1)
    h_orders = _norm(x2d, row(norm_mix))
    qkv, gates, w_out_bf, w_pa_bf, w_pb_bf = _project(
        h_orders, w_qkv, w_gate, row(b_gate), w_out, w_proj_a, w_proj_b)
    by_steps = {}
    for g, (window, d) in enumerate(DILATION_PATTERNS):
        by_steps.setdefault(_dilated_steps(seq, window, d)[1], []).append((g, window, d))
    attn = {}
    for groups in by_steps.values():
        for (g, _, _), o_lse in zip(groups, _dilated_groups(qkv, batch, seq, groups, slopes)):
            attn[g] = o_lse
    o_groups = [attn[g][0] for g in range(N_GROUPS_A)]
    lse_groups = [attn[g][1] for g in range(N_GROUPS_A)]
    ob, w_up_bf = _neighbourhood(qkv, rpb, batch, seq, w_up)
    x2, h2, w_down_bf = _merge(o_groups, lse_groups, ob, gates, x2d, seq, w_pa_bf, w_pb_bf,
                               w_out_bf, row(norm_mlp), w_down)
    return x2, h2, w_up_bf, w_down_bf


def kernel(x, norm_mix, w_qkv, w_gate, b_gate, rpb, w_proj_a, w_proj_b, w_out, norm_mlp, w_up,
           w_down, norm_final):
    batch, seq, _ = x.shape
    depth = norm_mix.shape[0]
    assert depth == 1 and seq % PERM_TILE == 0
    x2d = x.reshape(batch * seq, D_MODEL)
    x2, h2, w_up_bf, w_down_bf = _layer(
        x2d, batch, seq, norm_mix[0], w_qkv[0], w_gate[0], b_gate[0], rpb[0],
        w_proj_a[0], w_proj_b[0], w_out[0], norm_mlp[0], w_up[0], w_down[0])
    out = _mlp(h2, w_up_bf, w_down_bf, x2, norm_final.reshape(1, -1))
    return out.reshape(batch, seq, D_MODEL)
```

```python
import functools

import jax
import jax.numpy as jnp
import numpy as np
from jax import lax
from jax.experimental import pallas as pl
from jax.experimental.pallas import tpu as pltpu

D_MODEL = 2048
HEAD_DIM = 128
N_HEADS = D_MODEL // HEAD_DIM
N_HEADS_B = N_HEADS // 4
N_HEADS_A = N_HEADS - N_HEADS_B
DILATION_PATTERNS = ((128, 1), (512, 4), (2048, 16))
N_GROUPS_A = len(DILATION_PATTERNS)
HEADS_PER_GROUP = N_HEADS_A // N_GROUPS_A
GROUP_WIDTH = HEADS_PER_GROUP * HEAD_DIM
N_HEAD_GROUPS = N_HEADS // HEADS_PER_GROUP
GRID_W = 64
NA_ROWS = 8
NA_COLS = 16
D_FF = 4 * D_MODEL
EPS = 1e-6
NEG = -1e30
SCALE = HEAD_DIM ** -0.5
LOG2E = float(np.log2(np.e))
LN2 = float(np.log(2.0))
LANES = 128

F32 = jnp.float32
BF16 = jnp.bfloat16

VMEM_LIMIT_BYTES = 56 * 1024 * 1024

PERM_TILE = 1024
NORM_CHUNK = 512
PROJ_TM, PROJ_TN = 2048, GROUP_WIDTH
SIDE_CAST_ROWS = 32
MERGE_TM = 256
MLP_TM, MLP_TF = 1024, 1024
MLP_FINAL_ROW_CHUNKS = 4
DIL_QB = 128
DIL_UNROLL = 8
DIL_ROWS_PER_STEP = 1024
N_WINDOW_CASES = 3
NA_ROWS_PER_STEP = 8
NA_UNROLL = 2

GROUP_DILATIONS = tuple(d for _, d in DILATION_PATTERNS) + (1,)
ROW_ORDERS = tuple(sorted(set(GROUP_DILATIONS)))
GROUP_ROW_ORDER = tuple(ROW_ORDERS.index(d) for d in GROUP_DILATIONS)


def _params(sem):
    return pltpu.CompilerParams(dimension_semantics=sem, vmem_limit_bytes=VMEM_LIMIT_BYTES)


def _rms(x, g):
    ms = jnp.mean(x * x, axis=-1, keepdims=True)
    return (x * lax.rsqrt(ms + EPS)) * g


def _norm_kernel(x_ref, g_ref, h_ref, slab_ref):
    tm = x_ref.shape[0]
    x = x_ref[...]
    inv = lax.rsqrt(jnp.mean(x * x, axis=-1, keepdims=True) + EPS)
    n_slabs = NORM_CHUNK // LANES
    for c0 in range(0, D_MODEL, NORM_CHUNK):
        hc = (x_ref[:, c0:c0 + NORM_CHUNK] * inv) * g_ref[:, c0:c0 + NORM_CHUNK]
        h_ref[0, :, c0:c0 + NORM_CHUNK] = hc.astype(BF16)
        for s in range(n_slabs):
            slab_ref[0, s] = hc[:, s * LANES:(s + 1) * LANES]
        for v in range(1, len(ROW_ORDERS)):
            d_prev, d = ROW_ORDERS[v - 1], ROW_ORDERS[v]
            q = d // d_prev
            n_prev, n = tm // d_prev, tm // d
            last = v == len(ROW_ORDERS) - 1
            for s in range(n_slabs):
                cols = slice(c0 + s * LANES, c0 + (s + 1) * LANES)
                for r in range(d_prev):
                    for r2 in range(q):
                        rows = slab_ref[(v - 1) % 2, s, pl.ds(r * n_prev + r2, n, stride=q), :]
                        dst = (d_prev * r2 + r) * n
                        h_ref[v, dst:dst + n, cols] = rows.astype(BF16)
                        if not last:
                            slab_ref[v % 2, s, dst:dst + n, :] = rows


def _norm(x2d, g):
    m = x2d.shape[0]
    tm = PERM_TILE
    nv = len(ROW_ORDERS)
    assert ROW_ORDERS[0] == 1
    return pl.pallas_call(
        _norm_kernel,
        grid=(m // tm,),
        in_specs=[pl.BlockSpec((tm, D_MODEL), lambda i: (i, 0)),
                  pl.BlockSpec((1, D_MODEL), lambda i: (0, 0))],
        out_specs=pl.BlockSpec((nv, tm, D_MODEL), lambda i: (0, i, 0)),
        out_shape=jax.ShapeDtypeStruct((nv, m, D_MODEL), BF16),
        scratch_shapes=[pltpu.VMEM((2, NORM_CHUNK // LANES, tm, LANES), F32)],
        compiler_params=_params(("arbitrary",)),
        name="rmsnorm_row_orders",
    )(x2d, g)


def _proj_kernel(h_ref, w_first_ref, wq_next_ref, wg_next_ref, bg_ref, wout_ref, wpa_ref, wpb_ref,
                 qkv_ref, gate_ref, wout_bf_ref, wpa_bf_ref, wpb_bf_ref, w_bf_ref, *, n_qkv_steps):
    j = pl.program_id(0)
    i = pl.program_id(1)

    @pl.when((j == 0) & (i == 0))
    def _():
        w_bf_ref[0] = w_first_ref[...].astype(BF16)

    def side_casts(next_ref):
        part = next_ref.shape[0]
        rows = pl.ds(pl.multiple_of(i * part, part), part)
        w_bf_ref[(j + 1) % 2, rows, :] = next_ref[...].astype(BF16)
        wout_bf_ref[...] = wout_ref[...].astype(BF16)
        wpa_bf_ref[...] = wpa_ref[...].astype(BF16)
        wpb_bf_ref[...] = wpb_ref[...].astype(BF16)

    def qkv_step(next_ref):
        y = jnp.dot(h_ref[...], w_bf_ref[j % 2], preferred_element_type=F32)
        qkv_ref[...] = y.astype(BF16)
        side_casts(next_ref)

    @pl.when(j < n_qkv_steps - 1)
    def _():
        qkv_step(wq_next_ref)

    @pl.when(j == n_qkv_steps - 1)
    def _():
        qkv_step(wg_next_ref)

    @pl.when(j >= n_qkv_steps)
    def _():
        z = jnp.dot(h_ref[...], w_bf_ref[j % 2], preferred_element_type=F32) + bg_ref[...]
        gate_ref[...] = 0.5 * jnp.tanh((0.5 * z).astype(BF16)) + 0.5
        side_casts(wg_next_ref)


def _project(h_orders, w_qkv, w_gate, b_gate, w_out, w_proj_a, w_proj_b):
    m = h_orders.shape[1]
    tm, tn = PROJ_TM, PROJ_TN
    n_groups = N_HEAD_GROUPS
    nq = 3 * n_groups
    ng = (2 * D_MODEL) // tn
    n_i = m // tm
    row_order = GROUP_ROW_ORDER

    def lhs_map(j, i):
        v = jnp.int32(0)
        for grp in range(n_groups):
            v = jnp.where(j // 3 == grp, row_order[grp], v)
        return (v, i, 0)

    def w_qkv_col(j):
        jj = jnp.minimum(j, nq - 1)
        return (jj % 3) * n_groups + jj // 3

    gate_col = lambda j, i: (0, jnp.maximum(j - nq, 0))
    wq_next_map = lambda j, i: (jnp.where(j + 1 < nq, i, n_i - 1), w_qkv_col(j + 1))
    wg_next_map = lambda j, i: (jnp.where(j + 1 >= nq, i, 0), jnp.clip(j + 1 - nq, 0, ng - 1))

    def qkv_out_map(j, i):
        jj = jnp.minimum(j, nq - 1)
        return (jj // 3, jnp.where(j < nq, i, n_i - 1), jj % 3)

    gate_out_map = lambda j, i: (jnp.where(j >= nq, i, 0), jnp.maximum(j - nq, 0))

    def side_rows(w):
        n_blocks = w.shape[0] // SIDE_CAST_ROWS
        return pl.BlockSpec((SIDE_CAST_ROWS, w.shape[1]),
                            lambda j, i: (jnp.minimum(j * n_i + i, n_blocks - 1), 0))

    side = [w_out, w_proj_a, w_proj_b]
    assert all(w.shape[0] // SIDE_CAST_ROWS <= (nq + ng) * n_i for w in side)
    return pl.pallas_call(
        functools.partial(_proj_kernel, n_qkv_steps=nq),
        grid=(nq + ng, n_i),
        in_specs=[
            pl.BlockSpec((None, tm, D_MODEL), lhs_map),
            pl.BlockSpec((D_MODEL, tn), lambda j, i: (0, 0), pipeline_mode=pl.Buffered(1)),
            pl.BlockSpec((D_MODEL // n_i, tn), wq_next_map),
            pl.BlockSpec((D_MODEL // n_i, tn), wg_next_map),
            pl.BlockSpec((1, tn), gate_col),
        ] + [side_rows(w) for w in side],
        out_specs=[
            pl.BlockSpec((None, tm, tn), qkv_out_map),
            pl.BlockSpec((tm, tn), gate_out_map),
        ] + [side_rows(w) for w in side],
        out_shape=[
            jax.ShapeDtypeStruct((n_groups, m, 3 * GROUP_WIDTH), BF16),
            jax.ShapeDtypeStruct((m, 2 * D_MODEL), BF16),
        ] + [jax.ShapeDtypeStruct(w.shape, BF16) for w in side],
        scratch_shapes=[pltpu.VMEM((2, D_MODEL, tn), BF16)],
        compiler_params=_params(("arbitrary", "arbitrary")),
        name="proj_qkv_gate",
    )(h_orders, w_qkv, w_qkv, w_gate, b_gate, *side)


def _dilated_block_shape(seq, half_window):
    kw = DIL_QB + 2 * half_window
    return (seq, seq) if seq <= kw else (DIL_QB, kw)


def _dilated_kernel(*refs, jobs):
    n = len(jobs)
    for idx, (seq, half_window, coefs) in enumerate(jobs):
        _dilated_job(*refs[3 * idx:3 * idx + 3], *refs[3 * n + 2 * idx:3 * n + 2 * idx + 2],
                     refs[5 * n + idx], seq=seq, half_window=half_window, coefs=coefs)


def _dilated_job(q_ref, k_ref, v_ref, o_ref, lse_ref, bias_ref, *, seq, half_window, coefs):
    qb, kw = _dilated_block_shape(seq, half_window)
    n_res = q_ref.shape[1]
    q_rows = q_ref.shape[0] * q_ref.shape[2]
    q_base = 0 if q_rows == seq else (pl.program_id(1) % (seq // q_rows)) * q_rows
    n_blocks = q_rows // qb

    def rows_loader(ref):
        n_tiles, _, tile_rows, _ = ref.shape
        if n_res == 1:
            flat = ref.at[:, 0].reshape(n_tiles * tile_rows, GROUP_WIDTH)
            return lambda rr, start, size, cols: flat[pl.ds(start, size), cols]
        assert n_blocks == 1
        return lambda rr, start, size, cols: jnp.concatenate(
            [ref[t, rr, :, cols] for t in range(n_tiles)], axis=0)

    load_q, load_k, load_v = rows_loader(q_ref), rows_loader(k_ref), rows_loader(v_ref)
    lane = lax.broadcasted_iota(jnp.int32, (qb, HEAD_DIM), 1)
    heads = range(HEADS_PER_GROUP)
    head_cols = [slice(h * HEAD_DIM, (h + 1) * HEAD_DIM) for h in heads]

    @pl.when((pl.program_id(0) == 0) & (pl.program_id(1) == 0))
    def _():
        qrow = lax.broadcasted_iota(jnp.int32, (qb, kw), 0)
        kcol = lax.broadcasted_iota(jnp.int32, (qb, kw), 1)
        for c in range(bias_ref.shape[0]):
            dist = jnp.abs(kcol - qrow - c * half_window)
            dist_f = dist.astype(F32)
            for h in heads:
                bias_ref[c, h] = jnp.where(dist <= half_window, (-coefs[h] * LOG2E) * dist_f, NEG)

    def block(qi, carry):
        q0 = pl.multiple_of(qi * qb, qb)
        qg = q_base + q0
        ks = pl.multiple_of(jnp.clip(qg - half_window, 0, seq - kw), half_window)
        window_case = (qg - ks) // half_window
        pairs = [(rr, h) for rr in range(n_res) for h in heads]
        scores = [lax.dot_general(load_q(rr, q0, qb, head_cols[h]), load_k(rr, ks, kw, head_cols[h]),
                                  (((1,), (1,)), ((), ())), preferred_element_type=F32)
                  for rr, h in pairs]
        probs, dens = [], []
        lse_all = [jnp.zeros((qb, HEAD_DIM), F32) for _ in range(n_res)]
        for (rr, h), s in zip(pairs, scores):
            t = s * (SCALE * LOG2E) + bias_ref[window_case, h]
            m = jnp.max(t, axis=-1, keepdims=True)
            p = jnp.exp2(t - m)
            den = jnp.sum(p, axis=-1, keepdims=True)
            probs.append(p.astype(BF16))
            dens.append(den)
            lse_all[rr] = jnp.where(lane == h, m * LN2 + jnp.log(den), lse_all[rr])
        for (rr, h), p, den in zip(pairs, probs, dens):
            o = jnp.dot(p, load_v(rr, ks, kw, head_cols[h]), preferred_element_type=F32) / den
            o_ref[rr, pl.ds(q0, qb), head_cols[h]] = o.astype(BF16)
        for rr in range(n_res):
            lse_ref[rr, pl.ds(q0, qb), :] = lse_all[rr]
        return carry

    lax.fori_loop(0, n_blocks, block, 0, unroll=min(DIL_UNROLL, n_blocks))


def _dilated_steps(seq_total, window, dilation):
    seq = seq_total // dilation
    qb, _ = _dilated_block_shape(seq, window // (2 * dilation))
    n_res = min(dilation, max(1, DIL_ROWS_PER_STEP // seq)) if qb == seq else 1
    return n_res, dilation // n_res


def _dilated_groups(qkv, batch, seq_total, groups, slopes):
    tiles = seq_total // PERM_TILE
    jobs, operands, in_specs, out_specs, out_shapes, scratch = [], [], [], [], [], []
    n_steps = max(_dilated_steps(seq_total, w, d)[1] for _, w, d in groups)
    for group, window, d in groups:
        seq = seq_total // d
        half_window = window // (2 * d)
        qb, kw = _dilated_block_shape(seq, half_window)
        n_res, res_steps = _dilated_steps(seq_total, window, d)
        q_split = n_steps // res_steps
        assert res_steps * q_split == n_steps and tiles % q_split == 0
        q_tiles = tiles // q_split
        coefs = tuple(float(slopes[group * HEADS_PER_GROUP + h]) * d
                      for h in range(HEADS_PER_GROUP))
        jobs.append((seq, half_window, coefs))
        rows = PERM_TILE // d
        view = qkv.reshape(qkv.shape[0], batch, tiles, d, rows, qkv.shape[-1])
        q_map = lambda b, r, g=group, s=q_split: (g, b, r % s, r // s, 0, 0)
        kv_map = lambda which: (lambda b, r, g=group, s=q_split: (g, b, 0, r // s, 0, which))
        operands += [view, view, view]
        in_specs += [
            pl.BlockSpec((None, None, q_tiles, n_res, rows, GROUP_WIDTH), q_map),
            pl.BlockSpec((None, None, tiles, n_res, rows, GROUP_WIDTH), kv_map(1)),
            pl.BlockSpec((None, None, tiles, n_res, rows, GROUP_WIDTH), kv_map(2)),
        ]
        out_map = lambda b, r, s=q_split: (b, r // s, r % s, 0)
        out_specs += [pl.BlockSpec((None, n_res, seq // q_split, GROUP_WIDTH), out_map),
                      pl.BlockSpec((None, n_res, seq // q_split, HEAD_DIM), out_map)]
        out_shapes += [jax.ShapeDtypeStruct((batch, d, seq, GROUP_WIDTH), BF16),
                       jax.ShapeDtypeStruct((batch, d, seq, HEAD_DIM), F32)]
        scratch.append(pltpu.VMEM((N_WINDOW_CASES, HEADS_PER_GROUP, qb, kw), F32))
    outs = pl.pallas_call(
        functools.partial(_dilated_kernel, jobs=tuple(jobs)),
        grid=(batch, n_steps),
        in_specs=in_specs,
        out_specs=out_specs,
        out_shape=out_shapes,
        scratch_shapes=scratch,
        compiler_params=_params(("arbitrary", "arbitrary")),
        name="dilated_attention_d" + "_d".join(str(d) for _, _, d in groups),
    )(*operands)
    return [(outs[2 * i], outs[2 * i + 1]) for i in range(len(groups))]


NA_GROUP_ROWS = NA_ROWS // 2
NA_WINDOW_ROWS = NA_ROWS + NA_GROUP_ROWS
N_BIAS_VARIANTS = 3


def _expand_na_bias(rpb_ref, out_ref, h):
    half = NA_ROWS // 2
    n_off = 2 * NA_ROWS - 1
    pad = half
    n_seg = 3 * NA_ROWS
    width = n_seg * GRID_W
    qc = lax.broadcasted_iota(jnp.int32, (GRID_W, width), 0)
    kc = lax.broadcasted_iota(jnp.int32, (GRID_W, width), 1) % GRID_W
    col_idx = jnp.clip(kc - qc, -(NA_COLS - 1), NA_COLS - 1) + (NA_COLS - 1)
    seg = lax.broadcasted_iota(jnp.int32, (1, width), 1) // GRID_W
    table = jnp.zeros((GRID_W, width), F32)
    for j in range(2 * NA_COLS - 1):
        row_vals = jnp.zeros((1, width), F32)
        for i in range(n_off):
            row_vals = jnp.where(seg == i + pad, rpb_ref[h, i, j], row_vals)
        table = jnp.where(col_idx == j, row_vals, table)
    cs = jnp.clip(qc - NA_COLS // 2, 0, GRID_W - NA_COLS)
    table = jnp.where((kc >= cs) & (kc < cs + NA_COLS), table * LOG2E, NEG)
    nkeys = NA_WINDOW_ROWS * GRID_W
    key_row = lax.broadcasted_iota(jnp.int32, (GRID_W, nkeys), 1) // GRID_W
    for v in range(N_BIAS_VARIANTS):
        for a in range(NA_GROUP_ROWS):
            first = {0: 0, 1: a, 2: NA_WINDOW_ROWS - NA_ROWS}[v]
            seg0 = (NA_ROWS - 1) - half * v - a + pad
            assert 0 <= seg0 and seg0 + NA_WINDOW_ROWS <= n_seg
            slab = table[:, seg0 * GRID_W:(seg0 + NA_WINDOW_ROWS) * GRID_W]
            slab = jnp.where((key_row >= first) & (key_row < first + NA_ROWS), slab, NEG)
            out_ref[v, h, a * GRID_W:(a + 1) * GRID_W, :] = slab


def _na_kernel(rpb_ref, q_ref, k_ref, v_ref, wup_ref, o_ref, wup_bf_ref, bias_ref, *, rows):
    rb = pl.program_id(1)
    nq = NA_GROUP_ROWS * GRID_W
    nkeys = NA_WINDOW_ROWS * GRID_W

    @pl.when((pl.program_id(0) == 0) & (rb == 0))
    def _():
        for h in range(N_HEADS_B):
            _expand_na_bias(rpb_ref, bias_ref, h)

    def one_group(gl, carry):
        r0 = rb * NA_ROWS_PER_STEP + gl * NA_GROUP_ROWS
        ws = jnp.clip(r0 - NA_ROWS // 2, 0, rows - NA_WINDOW_ROWS)
        var = (r0 - ws) // (NA_ROWS // 2)
        q0 = pl.multiple_of(gl * nq, nq)
        k0 = pl.multiple_of(ws * GRID_W, GRID_W)
        heads = range(N_HEADS_B)
        head_cols = [slice(h * HEAD_DIM, (h + 1) * HEAD_DIM) for h in heads]
        scores = [lax.dot_general(q_ref[pl.ds(q0, nq), c], k_ref[pl.ds(k0, nkeys), c],
                                  (((1,), (1,)), ((), ())), preferred_element_type=F32)
                  for c in head_cols]
        probs, dens = [], []
        for h in heads:
            t = scores[h] * (SCALE * LOG2E) + bias_ref[var, h]
            m = jnp.max(t, axis=-1, keepdims=True)
            p = jnp.exp2(t - m)
            dens.append(jnp.sum(p, axis=-1, keepdims=True))
            probs.append(p.astype(BF16))
        for h in heads:
            o = jnp.dot(probs[h], v_ref[pl.ds(k0, nkeys), head_cols[h]],
                        preferred_element_type=F32) / dens[h]
            o_ref[pl.ds(q0, nq), head_cols[h]] = o.astype(BF16)
        return carry

    n_groups = NA_ROWS_PER_STEP // NA_GROUP_ROWS
    lax.fori_loop(0, n_groups, one_group, 0, unroll=min(NA_UNROLL, n_groups))
    wup_bf_ref[...] = wup_ref[...].astype(BF16)


def _neighbourhood(qkv, rpb, batch, seq_total, w_up):
    group = N_GROUPS_A
    assert GROUP_DILATIONS[group] == 1
    rows = seq_total // GRID_W
    assert rows >= NA_WINDOW_ROWS and rows % NA_ROWS_PER_STEP == 0
    assert NA_ROWS_PER_STEP % NA_GROUP_ROWS == 0
    tq = NA_ROWS_PER_STEP * GRID_W
    steps_per_batch = rows // NA_ROWS_PER_STEP
    view = qkv.reshape(qkv.shape[0], batch, seq_total, qkv.shape[-1])
    wup_rows = pl.BlockSpec((w_up.shape[0] // (batch * steps_per_batch), w_up.shape[1]),
                            lambda b, i: (b * steps_per_batch + i, 0))
    o, w_up_bf = pl.pallas_call(
        functools.partial(_na_kernel, rows=rows),
        grid=(batch, steps_per_batch),
        in_specs=[
            pl.BlockSpec(memory_space=pltpu.SMEM),
            pl.BlockSpec((None, None, tq, GROUP_WIDTH), lambda b, i: (group, b, i, 0)),
            pl.BlockSpec((None, None, seq_total, GROUP_WIDTH), lambda b, i: (group, b, 0, 1)),
            pl.BlockSpec((None, None, seq_total, GROUP_WIDTH), lambda b, i: (group, b, 0, 2)),
            wup_rows,
        ],
        out_specs=[pl.BlockSpec((None, tq, GROUP_WIDTH), lambda b, i: (b, i, 0)), wup_rows],
        out_shape=[jax.ShapeDtypeStruct((batch, seq_total, GROUP_WIDTH), BF16),
                   jax.ShapeDtypeStruct(w_up.shape, BF16)],
        scratch_shapes=[pltpu.VMEM((N_BIAS_VARIANTS, N_HEADS_B, NA_GROUP_ROWS * GRID_W,
                                    NA_WINDOW_ROWS * GRID_W), F32)],
        compiler_params=_params(("arbitrary", "arbitrary")),
        name="neighbourhood_attention",
    )(rpb, view, view, view, w_up)
    return o.reshape(batch * seq_total, GROUP_WIDTH), w_up_bf


def _to_token_order_matrix(tm, d):
    t = lax.broadcasted_iota(jnp.int32, (tm, tm), 0)
    c = lax.broadcasted_iota(jnp.int32, (tm, tm), 1)
    return (c == (t % d) * (tm // d) + t // d).astype(BF16)


def _merge_kernel(o0_ref, o1_ref, o2_ref, l0_ref, l1_ref, l2_ref, ob_ref, gate_ref, x_ref,
                  wpa_ref, wpb_ref, wout_ref, gn_ref, wdn_ref,
                  x2_ref, h2_ref, wdn_bf_ref, lse_tok_ref, *, dilations):
    tm = x_ref.shape[0]
    tb = jnp.dot(ob_ref[...], wpb_ref[...], preferred_element_type=F32)
    o_tok, lse_tok = [], []
    for g, (o_ref, l_ref, d) in enumerate(zip((o0_ref, o1_ref, o2_ref),
                                              (l0_ref, l1_ref, l2_ref), dilations)):
        o = o_ref[...].reshape(tm, GROUP_WIDTH)
        if d == 1:
            o_tok.append(o.astype(F32))
            lse_tok.append(l_ref[...].reshape(tm, HEAD_DIM))
        else:
            o_tok.append(jnp.dot(_to_token_order_matrix(tm, d), o, preferred_element_type=F32))
            for r in range(d):
                lse_tok_ref[g, pl.ds(r, tm // d, stride=d), :] = l_ref[r]
            lse_tok.append(lse_tok_ref[g])
    l0, l1, l2 = lse_tok
    mx = jnp.maximum(jnp.maximum(l0, l1), l2)
    e0, e1, e2 = jnp.exp(l0 - mx), jnp.exp(l1 - mx), jnp.exp(l2 - mx)
    tot = e0 + e1 + e2
    a0, a1, a2 = e0 / tot, e1 / tot, e2 / tot
    parts = []
    for h in range(HEADS_PER_GROUP):
        cols = slice(h * HEAD_DIM, (h + 1) * HEAD_DIM)
        y = (a0[:, h:h + 1] * o_tok[0][:, cols] + a1[:, h:h + 1] * o_tok[1][:, cols]
             + a2[:, h:h + 1] * o_tok[2][:, cols])
        parts.append(y.astype(BF16))
    ya = jnp.concatenate(parts, axis=1)
    ta = jnp.dot(ya, wpa_ref[...], preferred_element_type=F32)
    merged = gate_ref[:, :D_MODEL].astype(F32) * ta + gate_ref[:, D_MODEL:].astype(F32) * tb
    x2 = x_ref[...] + jnp.dot(merged.astype(BF16), wout_ref[...], preferred_element_type=F32)
    x2_ref[...] = x2
    wdn_bf_ref[...] = wdn_ref[...].astype(BF16)
    h2_ref[...] = _rms(x2, gn_ref[...]).astype(BF16)


def _merge(o_groups, lse_groups, ob, gates, x2d, seq, wpa, wpb, wout, gn, w_down):
    m = x2d.shape[0]
    tm = MERGE_TM
    n_steps = m // tm
    tiles_per_batch = seq // tm
    dilations = tuple(o.shape[1] for o in o_groups)
    row = lambda w: pl.BlockSpec((tm, w), lambda i: (i, 0))
    wdn_rows = pl.BlockSpec((w_down.shape[0] // n_steps, w_down.shape[1]), lambda i: (i, 0))
    grouped = lambda d, w: pl.BlockSpec(
        (None, d, tm // d, w), lambda i: (i // tiles_per_batch, 0, i % tiles_per_batch, 0))
    const = lambda a, b: pl.BlockSpec((a, b), lambda i: (0, 0), pipeline_mode=pl.Buffered(1))
    return pl.pallas_call(
        functools.partial(_merge_kernel, dilations=dilations),
        grid=(n_steps,),
        in_specs=[grouped(d, GROUP_WIDTH) for d in dilations]
        + [grouped(d, HEAD_DIM) for d in dilations]
        + [row(GROUP_WIDTH), row(2 * D_MODEL), row(D_MODEL),
           const(GROUP_WIDTH, D_MODEL), const(GROUP_WIDTH, D_MODEL), const(D_MODEL, D_MODEL),
           const(1, D_MODEL), wdn_rows],
        out_specs=[row(D_MODEL), row(D_MODEL), wdn_rows],
        out_shape=[jax.ShapeDtypeStruct((m, D_MODEL), F32), jax.ShapeDtypeStruct((m, D_MODEL), BF16),
                   jax.ShapeDtypeStruct(w_down.shape, BF16)],
        scratch_shapes=[pltpu.VMEM((len(dilations), tm, HEAD_DIM), F32)],
        compiler_params=_params(("arbitrary",)),
        name="merge_out_proj",
    )(*o_groups, *lse_groups, ob, gates, x2d, wpa, wpb, wout, gn, w_down)


def _mlp_kernel(h2_ref, wup_ref, wdn_ref, x2_hbm, gf_ref, out_ref, x2_sem, *, n_f):
    i = pl.program_id(0)
    f = pl.program_id(1)
    tm = out_ref.shape[0]

    def residual_copy():
        return pltpu.make_async_copy(x2_hbm.at[pl.ds(i * tm, tm), :], out_ref, x2_sem)

    def up_down(first=False, last=False):
        if first:
            residual_copy().start()
        hid = jnp.dot(h2_ref[...], wup_ref[...], preferred_element_type=F32)
        hid = jnp.square(jnp.maximum(hid, 0.0)).astype(BF16)
        if first:
            residual_copy().wait()
        if not last:
            out_ref[...] += jnp.dot(hid, wdn_ref[...], preferred_element_type=F32)
            return
        chunk = tm // MLP_FINAL_ROW_CHUNKS
        for r0 in range(0, tm, chunk):
            rows = slice(r0, r0 + chunk)
            y = out_ref[rows, :] + jnp.dot(hid[rows], wdn_ref[...], preferred_element_type=F32)
            out_ref[rows, :] = _rms(y, gf_ref[...])

    assert n_f >= 2

    @pl.when(f == 0)
    def _():
        up_down(first=True)

    @pl.when((f > 0) & (f < n_f - 1))
    def _():
        up_down()

    @pl.when(f == n_f - 1)
    def _():
        up_down(last=True)


def _mlp(h2, wup, wdn, x2, gf):
    m = h2.shape[0]
    tm, tf = MLP_TM, MLP_TF
    n_f = D_FF // tf
    return pl.pallas_call(
        functools.partial(_mlp_kernel, n_f=n_f),
        grid=(m // tm, n_f),
        in_specs=[
            pl.BlockSpec((tm, D_MODEL), lambda i, f: (i, 0)),
            pl.BlockSpec((D_MODEL, tf), lambda i, f: (0, f)),
            pl.BlockSpec((tf, D_MODEL), lambda i, f: (f, 0)),
            pl.BlockSpec(memory_space=pl.ANY),
            pl.BlockSpec((1, D_MODEL), lambda i, f: (0, 0)),
        ],
        out_specs=pl.BlockSpec((tm, D_MODEL), lambda i, f: (i, 0)),
        out_shape=jax.ShapeDtypeStruct((m, D_MODEL), F32),
        scratch_shapes=[pltpu.SemaphoreType.DMA],
        compiler_params=_params(("arbitrary", "arbitrary")),
        name="mlp_residual_norm",
    )(h2, wup, wdn, x2, gf)


def _layer(x2d, batch, seq, norm_mix, w_qkv, w_gate, b_gate, rpb, w_proj_a, w_proj_b, w_out,
           norm_mlp, w_up, w_down):
    slopes = 2.0 ** (-8.0 * np.arange(1, N_HEADS_A + 1) / N_HEADS_A)
    row = lambda v: v.reshape(1, -1)
    h_orders = _norm(x2d, row(norm_mix))
    qkv, gates, w_out_bf, w_pa_bf, w_pb_bf = _project(
        h_orders, w_qkv, w_gate, row(b_gate), w_out, w_proj_a, w_proj_b)
    groups = [(g, window, d) for g, (window, d) in enumerate(DILATION_PATTERNS)]
    o_groups, lse_groups = zip(*_dilated_groups(qkv, batch, seq, groups, slopes))
    ob, w_up_bf = _neighbourhood(qkv, rpb, batch, seq, w_up)
    x2, h2, w_down_bf = _merge(o_groups, lse_groups, ob, gates, x2d, seq, w_pa_bf, w_pb_bf,
                               w_out_bf, row(norm_mlp), w_down)
    return x2, h2, w_up_bf, w_down_bf


def kernel(x, norm_mix, w_qkv, w_gate, b_gate, rpb, w_proj_a, w_proj_b, w_out, norm_mlp, w_up,
           w_down, norm_final):
    batch, seq, _ = x.shape
    depth = norm_mix.shape[0]
    assert depth == 1 and seq % PERM_TILE == 0
    x2d = x.reshape(batch * seq, D_MODEL)
    x2, h2, w_up_bf, w_down_bf = _layer(
        x2d, batch, seq, norm_mix[0], w_qkv[0], w_gate[0], b_gate[0], rpb[0],
        w_proj_a[0], w_proj_b[0], w_out[0], norm_mlp[0], w_up[0], w_down[0])
    out = _mlp(h2, w_up_bf, w_down_bf, x2, norm_final.reshape(1, -1))
    return out.reshape(batch, seq, D_MODEL)
```

```python
import functools

import jax
import jax.numpy as jnp
import numpy as np
from jax import lax
from jax.experimental import pallas as pl
from jax.experimental.pallas import tpu as pltpu

D_MODEL = 2048
HEAD_DIM = 128
N_HEADS = D_MODEL // HEAD_DIM
N_HEADS_B = N_HEADS // 4
N_HEADS_A = N_HEADS - N_HEADS_B
DILATION_PATTERNS = ((128, 1), (512, 4), (2048, 16))
N_GROUPS_A = len(DILATION_PATTERNS)
HEADS_PER_GROUP = N_HEADS_A // N_GROUPS_A
GROUP_WIDTH = HEADS_PER_GROUP * HEAD_DIM
N_HEAD_GROUPS = N_HEADS // HEADS_PER_GROUP
GRID_W = 64
NA_ROWS = 8
NA_COLS = 16
D_FF = 4 * D_MODEL
EPS = 1e-6
NEG = -1e30
SCALE = HEAD_DIM ** -0.5
LOG2E = float(np.log2(np.e))
LN2 = float(np.log(2.0))
LANES = 128

F32 = jnp.float32
BF16 = jnp.bfloat16

VMEM_LIMIT_BYTES = 56 * 1024 * 1024

PERM_TILE = 1024
NORM_CHUNK = 512
NORM_X_SLOTS = 3
PROJ_TM, PROJ_TN = 2048, GROUP_WIDTH
SIDE_CAST_ROWS = 32
MERGE_TM = 256
MLP_TM, MLP_TF = 1024, 1024
MLP_FINAL_ROW_CHUNKS = 4
DIL_QB = 128
DIL_UNROLL = 8
DIL_ROWS_PER_STEP = 1024
N_WINDOW_CASES = 3
NA_ROWS_PER_STEP = 8
NA_UNROLL = 2

GROUP_DILATIONS = tuple(d for _, d in DILATION_PATTERNS) + (1,)
ROW_ORDERS = tuple(sorted(set(GROUP_DILATIONS)))
GROUP_ROW_ORDER = tuple(ROW_ORDERS.index(d) for d in GROUP_DILATIONS)


def _params(sem):
    return pltpu.CompilerParams(dimension_semantics=sem, vmem_limit_bytes=VMEM_LIMIT_BYTES)


def _rms(x, g):
    ms = jnp.mean(x * x, axis=-1, keepdims=True)
    return (x * lax.rsqrt(ms + EPS)) * g


def _norm_kernel(x_hbm, g_ref, h_ref, slab_ref, xbuf_ref, x_sems, *, n_tiles):
    i = pl.program_id(0)
    tm = xbuf_ref.shape[1]
    ahead = NORM_X_SLOTS - 1

    def x_copy(tile):
        slot = tile % NORM_X_SLOTS
        return pltpu.make_async_copy(x_hbm.at[pl.ds(tile * tm, tm), :], xbuf_ref.at[slot],
                                     x_sems.at[slot])

    @pl.when(i == 0)
    def _():
        for t in range(min(ahead, n_tiles)):
            x_copy(t).start()

    @pl.when(i + ahead < n_tiles)
    def _():
        x_copy(i + ahead).start()

    x_copy(i).wait()
    x_ref = xbuf_ref.at[i % NORM_X_SLOTS]
    x = x_ref[...]
    inv = lax.rsqrt(jnp.mean(x * x, axis=-1, keepdims=True) + EPS)
    n_slabs = NORM_CHUNK // LANES
    for c0 in range(0, D_MODEL, NORM_CHUNK):
        hc = (x_ref[:, c0:c0 + NORM_CHUNK] * inv) * g_ref[:, c0:c0 + NORM_CHUNK]
        h_ref[0, :, c0:c0 + NORM_CHUNK] = hc.astype(BF16)
        for s in range(n_slabs):
            slab_ref[0, s] = hc[:, s * LANES:(s + 1) * LANES]
        for v in range(1, len(ROW_ORDERS)):
            d_prev, d = ROW_ORDERS[v - 1], ROW_ORDERS[v]
            q = d // d_prev
            n_prev, n = tm // d_prev, tm // d
            last = v == len(ROW_ORDERS) - 1
            for s in range(n_slabs):
                cols = slice(c0 + s * LANES, c0 + (s + 1) * LANES)
                for r in range(d_prev):
                    for r2 in range(q):
                        rows = slab_ref[(v - 1) % 2, s, pl.ds(r * n_prev + r2, n, stride=q), :]
                        dst = (d_prev * r2 + r) * n
                        h_ref[v, dst:dst + n, cols] = rows.astype(BF16)
                        if not last:
                            slab_ref[v % 2, s, dst:dst + n, :] = rows


def _norm(x2d, g):
    m = x2d.shape[0]
    tm = PERM_TILE
    nv = len(ROW_ORDERS)
    assert ROW_ORDERS[0] == 1
    return pl.pallas_call(
        functools.partial(_norm_kernel, n_tiles=m // tm),
        grid=(m // tm,),
        in_specs=[pl.BlockSpec(memory_space=pl.ANY),
                  pl.BlockSpec((1, D_MODEL), lambda i: (0, 0))],
        out_specs=pl.BlockSpec((nv, tm, D_MODEL), lambda i: (0, i, 0)),
        out_shape=jax.ShapeDtypeStruct((nv, m, D_MODEL), BF16),
        scratch_shapes=[pltpu.VMEM((2, NORM_CHUNK // LANES, tm, LANES), F32),
                        pltpu.VMEM((NORM_X_SLOTS, tm, D_MODEL), F32),
                        pltpu.SemaphoreType.DMA((NORM_X_SLOTS,))],
        compiler_params=_params(("arbitrary",)),
        name="rmsnorm_row_orders",
    )(x2d, g)


def _proj_kernel(h_ref, w_first_ref, wq_next_ref, wg_next_ref, bg_ref, wout_ref, wpa_ref, wpb_ref,
                 qkv_ref, gate_ref, wout_bf_ref, wpa_bf_ref, wpb_bf_ref, w_bf_ref, *, n_qkv_steps):
    j = pl.program_id(0)
    i = pl.program_id(1)

    @pl.when((j == 0) & (i == 0))
    def _():
        w_bf_ref[0] = w_first_ref[...].astype(BF16)

    def side_casts(next_ref):
        part = next_ref.shape[0]
        rows = pl.ds(pl.multiple_of(i * part, part), part)
        w_bf_ref[(j + 1) % 2, rows, :] = next_ref[...].astype(BF16)
        wout_bf_ref[...] = wout_ref[...].astype(BF16)
        wpa_bf_ref[...] = wpa_ref[...].astype(BF16)
        wpb_bf_ref[...] = wpb_ref[...].astype(BF16)

    def qkv_step(next_ref):
        y = jnp.dot(h_ref[...], w_bf_ref[j % 2], preferred_element_type=F32)
        qkv_ref[...] = y.astype(BF16)
        side_casts(next_ref)

    @pl.when(j < n_qkv_steps - 1)
    def _():
        qkv_step(wq_next_ref)

    @pl.when(j == n_qkv_steps - 1)
    def _():
        qkv_step(wg_next_ref)

    @pl.when(j >= n_qkv_steps)
    def _():
        z = jnp.dot(h_ref[...], w_bf_ref[j % 2], preferred_element_type=F32) + bg_ref[...]
        gate_ref[...] = 0.5 * jnp.tanh((0.5 * z).astype(BF16)) + 0.5
        side_casts(wg_next_ref)


def _project(h_orders, w_qkv, w_gate, b_gate, w_out, w_proj_a, w_proj_b):
    m = h_orders.shape[1]
    tm, tn = PROJ_TM, PROJ_TN
    n_groups = N_HEAD_GROUPS
    nq = 3 * n_groups
    ng = (2 * D_MODEL) // tn
    n_i = m // tm
    row_order = GROUP_ROW_ORDER

    def lhs_map(j, i):
        v = jnp.int32(0)
        for grp in range(n_groups):
            v = jnp.where(j // 3 == grp, row_order[grp], v)
        return (v, i, 0)

    def w_qkv_col(j):
        jj = jnp.minimum(j, nq - 1)
        return (jj % 3) * n_groups + jj // 3

    gate_col = lambda j, i: (0, jnp.maximum(j - nq, 0))
    wq_next_map = lambda j, i: (jnp.where(j + 1 < nq, i, n_i - 1), w_qkv_col(j + 1))
    wg_next_map = lambda j, i: (jnp.where(j + 1 >= nq, i, 0), jnp.clip(j + 1 - nq, 0, ng - 1))

    def qkv_out_map(j, i):
        jj = jnp.minimum(j, nq - 1)
        return (jj // 3, jnp.where(j < nq, i, n_i - 1), jj % 3)

    gate_out_map = lambda j, i: (jnp.where(j >= nq, i, 0), jnp.maximum(j - nq, 0))

    def side_rows(w):
        n_blocks = w.shape[0] // SIDE_CAST_ROWS
        return pl.BlockSpec((SIDE_CAST_ROWS, w.shape[1]),
                            lambda j, i: (jnp.minimum(j * n_i + i, n_blocks - 1), 0))

    side = [w_out, w_proj_a, w_proj_b]
    assert all(w.shape[0] // SIDE_CAST_ROWS <= (nq + ng) * n_i for w in side)
    return pl.pallas_call(
        functools.partial(_proj_kernel, n_qkv_steps=nq),
        grid=(nq + ng, n_i),
        in_specs=[
            pl.BlockSpec((None, tm, D_MODEL), lhs_map),
            pl.BlockSpec((D_MODEL, tn), lambda j, i: (0, 0), pipeline_mode=pl.Buffered(1)),
            pl.BlockSpec((D_MODEL // n_i, tn), wq_next_map),
            pl.BlockSpec((D_MODEL // n_i, tn), wg_next_map),
            pl.BlockSpec((1, tn), gate_col),
        ] + [side_rows(w) for w in side],
        out_specs=[
            pl.BlockSpec((None, tm, tn), qkv_out_map),
            pl.BlockSpec((tm, tn), gate_out_map),
        ] + [side_rows(w) for w in side],
        out_shape=[
            jax.ShapeDtypeStruct((n_groups, m, 3 * GROUP_WIDTH), BF16),
            jax.ShapeDtypeStruct((m, 2 * D_MODEL), BF16),
        ] + [jax.ShapeDtypeStruct(w.shape, BF16) for w in side],
        scratch_shapes=[pltpu.VMEM((2, D_MODEL, tn), BF16)],
        compiler_params=_params(("arbitrary", "arbitrary")),
        name="proj_qkv_gate",
    )(h_orders, w_qkv, w_qkv, w_gate, b_gate, *side)


def _dilated_block_shape(seq, half_window):
    kw = DIL_QB + 2 * half_window
    return (seq, seq) if seq <= kw else (DIL_QB, kw)


def _dilated_kernel(*refs, jobs):
    n = len(jobs)
    for idx, (seq, half_window, coefs) in enumerate(jobs):
        _dilated_job(*refs[3 * idx:3 * idx + 3], *refs[3 * n + 2 * idx:3 * n + 2 * idx + 2],
                     refs[5 * n + idx], seq=seq, half_window=half_window, coefs=coefs)


def _dilated_job(q_ref, k_ref, v_ref, o_ref, lse_ref, bias_ref, *, seq, half_window, coefs):
    qb, kw = _dilated_block_shape(seq, half_window)
    n_res = q_ref.shape[1]
    q_rows = q_ref.shape[0] * q_ref.shape[2]
    q_base = 0 if q_rows == seq else (pl.program_id(1) % (seq // q_rows)) * q_rows
    n_blocks = q_rows // qb

    def rows_loader(ref):
        n_tiles, _, tile_rows, _ = ref.shape
        if n_res == 1:
            flat = ref.at[:, 0].reshape(n_tiles * tile_rows, GROUP_WIDTH)
            return lambda rr, start, size, cols: flat[pl.ds(start, size), cols]
        assert n_blocks == 1
        return lambda rr, start, size, cols: jnp.concatenate(
            [ref[t, rr, :, cols] for t in range(n_tiles)], axis=0)

    load_q, load_k, load_v = rows_loader(q_ref), rows_loader(k_ref), rows_loader(v_ref)
    lane = lax.broadcasted_iota(jnp.int32, (qb, HEAD_DIM), 1)
    heads = range(HEADS_PER_GROUP)
    head_cols = [slice(h * HEAD_DIM, (h + 1) * HEAD_DIM) for h in heads]

    @pl.when((pl.program_id(0) == 0) & (pl.program_id(1) == 0))
    def _():
        qrow = lax.broadcasted_iota(jnp.int32, (qb, kw), 0)
        kcol = lax.broadcasted_iota(jnp.int32, (qb, kw), 1)
        for c in range(bias_ref.shape[0]):
            dist = jnp.abs(kcol - qrow - c * half_window)
            dist_f = dist.astype(F32)
            for h in heads:
                bias_ref[c, h] = jnp.where(dist <= half_window, (-coefs[h] * LOG2E) * dist_f, NEG)

    def block(qi, carry):
        q0 = pl.multiple_of(qi * qb, qb)
        qg = q_base + q0
        ks = pl.multiple_of(jnp.clip(qg - half_window, 0, seq - kw), half_window)
        window_case = (qg - ks) // half_window
        pairs = [(rr, h) for rr in range(n_res) for h in heads]
        scores = [lax.dot_general(load_q(rr, q0, qb, head_cols[h]), load_k(rr, ks, kw, head_cols[h]),
                                  (((1,), (1,)), ((), ())), preferred_element_type=F32)
                  for rr, h in pairs]
        probs, dens = [], []
        lse_all = [jnp.zeros((qb, HEAD_DIM), F32) for _ in range(n_res)]
        for (rr, h), s in zip(pairs, scores):
            t = s * (SCALE * LOG2E) + bias_ref[window_case, h]
            m = jnp.max(t, axis=-1, keepdims=True)
            p = jnp.exp2(t - m)
            den = jnp.sum(p, axis=-1, keepdims=True)
            probs.append(p.astype(BF16))
            dens.append(den)
            lse_all[rr] = jnp.where(lane == h, m * LN2 + jnp.log(den), lse_all[rr])
        for (rr, h), p, den in zip(pairs, probs, dens):
            o = jnp.dot(p, load_v(rr, ks, kw, head_cols[h]), preferred_element_type=F32) / den
            o_ref[rr, pl.ds(q0, qb), head_cols[h]] = o.astype(BF16)
        for rr in range(n_res):
            lse_ref[rr, pl.ds(q0, qb), :] = lse_all[rr]
        return carry

    lax.fori_loop(0, n_blocks, block, 0, unroll=min(DIL_UNROLL, n_blocks))


def _dilated_steps(seq_total, window, dilation):
    seq = seq_total // dilation
    qb, _ = _dilated_block_shape(seq, window // (2 * dilation))
    n_res = min(dilation, max(1, DIL_ROWS_PER_STEP // seq)) if qb == seq else 1
    return n_res, dilation // n_res


def _dilated_groups(qkv, batch, seq_total, groups, slopes):
    tiles = seq_total // PERM_TILE
    jobs, operands, in_specs, out_specs, out_shapes, scratch = [], [], [], [], [], []
    n_steps = max(_dilated_steps(seq_total, w, d)[1] for _, w, d in groups)
    for group, window, d in groups:
        seq = seq_total // d
        half_window = window // (2 * d)
        qb, kw = _dilated_block_shape(seq, half_window)
        n_res, res_steps = _dilated_steps(seq_total, window, d)
        q_split = n_steps // res_steps
        assert res_steps * q_split == n_steps and tiles % q_split == 0
        q_tiles = tiles // q_split
        coefs = tuple(float(slopes[group * HEADS_PER_GROUP + h]) * d
                      for h in range(HEADS_PER_GROUP))
        jobs.append((seq, half_window, coefs))
        rows = PERM_TILE // d
        view = qkv.reshape(qkv.shape[0], batch, tiles, d, rows, qkv.shape[-1])
        q_map = lambda b, r, g=group, s=q_split: (g, b, r % s, r // s, 0, 0)
        kv_map = lambda which: (lambda b, r, g=group, s=q_split: (g, b, 0, r // s, 0, which))
        operands += [view, view, view]
        in_specs += [
            pl.BlockSpec((None, None, q_tiles, n_res, rows, GROUP_WIDTH), q_map),
            pl.BlockSpec((None, None, tiles, n_res, rows, GROUP_WIDTH), kv_map(1)),
            pl.BlockSpec((None, None, tiles, n_res, rows, GROUP_WIDTH), kv_map(2)),
        ]
        out_map = lambda b, r, s=q_split: (b, r // s, r % s, 0)
        out_specs += [pl.BlockSpec((None, n_res, seq // q_split, GROUP_WIDTH), out_map),
                      pl.BlockSpec((None, n_res, seq // q_split, HEAD_DIM), out_map)]
        out_shapes += [jax.ShapeDtypeStruct((batch, d, seq, GROUP_WIDTH), BF16),
                       jax.ShapeDtypeStruct((batch, d, seq, HEAD_DIM), F32)]
        scratch.append(pltpu.VMEM((N_WINDOW_CASES, HEADS_PER_GROUP, qb, kw), F32))
    outs = pl.pallas_call(
        functools.partial(_dilated_kernel, jobs=tuple(jobs)),
        grid=(batch, n_steps),
        in_specs=in_specs,
        out_specs=out_specs,
        out_shape=out_shapes,
        scratch_shapes=scratch,
        compiler_params=_params(("arbitrary", "arbitrary")),
        name="dilated_attention_d" + "_d".join(str(d) for _, _, d in groups),
    )(*operands)
    return [(outs[2 * i], outs[2 * i + 1]) for i in range(len(groups))]


NA_GROUP_ROWS = NA_ROWS // 2
NA_WINDOW_ROWS = NA_ROWS + NA_GROUP_ROWS
N_BIAS_VARIANTS = 3


def _expand_na_bias(rpb_ref, out_ref, h):
    half = NA_ROWS // 2
    n_off = 2 * NA_ROWS - 1
    pad = half
    n_seg = 3 * NA_ROWS
    width = n_seg * GRID_W
    qc = lax.broadcasted_iota(jnp.int32, (GRID_W, width), 0)
    kc = lax.broadcasted_iota(jnp.int32, (GRID_W, width), 1) % GRID_W
    col_idx = jnp.clip(kc - qc, -(NA_COLS - 1), NA_COLS - 1) + (NA_COLS - 1)
    seg = lax.broadcasted_iota(jnp.int32, (1, width), 1) // GRID_W
    table = jnp.zeros((GRID_W, width), F32)
    for j in range(2 * NA_COLS - 1):
        row_vals = jnp.zeros((1, width), F32)
        for i in range(n_off):
            row_vals = jnp.where(seg == i + pad, rpb_ref[h, i, j], row_vals)
        table = jnp.where(col_idx == j, row_vals, table)
    cs = jnp.clip(qc - NA_COLS // 2, 0, GRID_W - NA_COLS)
    table = jnp.where((kc >= cs) & (kc < cs + NA_COLS), table * LOG2E, NEG)
    nkeys = NA_WINDOW_ROWS * GRID_W
    key_row = lax.broadcasted_iota(jnp.int32, (GRID_W, nkeys), 1) // GRID_W
    for v in range(N_BIAS_VARIANTS):
        for a in range(NA_GROUP_ROWS):
            first = {0: 0, 1: a, 2: NA_WINDOW_ROWS - NA_ROWS}[v]
            seg0 = (NA_ROWS - 1) - half * v - a + pad
            assert 0 <= seg0 and seg0 + NA_WINDOW_ROWS <= n_seg
            slab = table[:, seg0 * GRID_W:(seg0 + NA_WINDOW_ROWS) * GRID_W]
            slab = jnp.where((key_row >= first) & (key_row < first + NA_ROWS), slab, NEG)
            out_ref[v, h, a * GRID_W:(a + 1) * GRID_W, :] = slab


def _na_kernel(rpb_ref, q_ref, k_ref, v_ref, wup_ref, o_ref, wup_bf_ref, bias_ref, *, rows):
    rb = pl.program_id(1)
    nq = NA_GROUP_ROWS * GRID_W
    nkeys = NA_WINDOW_ROWS * GRID_W

    @pl.when((pl.program_id(0) == 0) & (rb == 0))
    def _():
        for h in range(N_HEADS_B):
            _expand_na_bias(rpb_ref, bias_ref, h)

    def one_group(gl, carry):
        r0 = rb * NA_ROWS_PER_STEP + gl * NA_GROUP_ROWS
        ws = jnp.clip(r0 - NA_ROWS // 2, 0, rows - NA_WINDOW_ROWS)
        var = (r0 - ws) // (NA_ROWS // 2)
        q0 = pl.multiple_of(gl * nq, nq)
        k0 = pl.multiple_of(ws * GRID_W, GRID_W)
        heads = range(N_HEADS_B)
        head_cols = [slice(h * HEAD_DIM, (h + 1) * HEAD_DIM) for h in heads]
        scores = [lax.dot_general(q_ref[pl.ds(q0, nq), c], k_ref[pl.ds(k0, nkeys), c],
                                  (((1,), (1,)), ((), ())), preferred_element_type=F32)
                  for c in head_cols]
        probs, dens = [], []
        for h in heads:
            t = scores[h] * (SCALE * LOG2E) + bias_ref[var, h]
            m = jnp.max(t, axis=-1, keepdims=True)
            p = jnp.exp2(t - m)
            dens.append(jnp.sum(p, axis=-1, keepdims=True))
            probs.append(p.astype(BF16))
        for h in heads:
            o = jnp.dot(probs[h], v_ref[pl.ds(k0, nkeys), head_cols[h]],
                        preferred_element_type=F32) / dens[h]
            o_ref[pl.ds(q0, nq), head_cols[h]] = o.astype(BF16)
        return carry

    n_groups = NA_ROWS_PER_STEP // NA_GROUP_ROWS
    lax.fori_loop(0, n_groups, one_group, 0, unroll=min(NA_UNROLL, n_groups))
    wup_bf_ref[...] = wup_ref[...].astype(BF16)


def _neighbourhood(qkv, rpb, batch, seq_total, w_up):
    group = N_GROUPS_A
    assert GROUP_DILATIONS[group] == 1
    rows = seq_total // GRID_W
    assert rows >= NA_WINDOW_ROWS and rows % NA_ROWS_PER_STEP == 0
    assert NA_ROWS_PER_STEP % NA_GROUP_ROWS == 0
    tq = NA_ROWS_PER_STEP * GRID_W
    steps_per_batch = rows // NA_ROWS_PER_STEP
    view = qkv.reshape(qkv.shape[0], batch, seq_total, qkv.shape[-1])
    wup_rows = pl.BlockSpec((w_up.shape[0] // (batch * steps_per_batch), w_up.shape[1]),
                            lambda b, i: (b * steps_per_batch + i, 0))
    o, w_up_bf = pl.pallas_call(
        functools.partial(_na_kernel, rows=rows),
        grid=(batch, steps_per_batch),
        in_specs=[
            pl.BlockSpec(memory_space=pltpu.SMEM),
            pl.BlockSpec((None, None, tq, GROUP_WIDTH), lambda b, i: (group, b, i, 0)),
            pl.BlockSpec((None, None, seq_total, GROUP_WIDTH), lambda b, i: (group, b, 0, 1)),
            pl.BlockSpec((None, None, seq_total, GROUP_WIDTH), lambda b, i: (group, b, 0, 2)),
            wup_rows,
        ],
        out_specs=[pl.BlockSpec((None, tq, GROUP_WIDTH), lambda b, i: (b, i, 0)), wup_rows],
        out_shape=[jax.ShapeDtypeStruct((batch, seq_total, GROUP_WIDTH), BF16),
                   jax.ShapeDtypeStruct(w_up.shape, BF16)],
        scratch_shapes=[pltpu.VMEM((N_BIAS_VARIANTS, N_HEADS_B, NA_GROUP_ROWS * GRID_W,
                                    NA_WINDOW_ROWS * GRID_W), F32)],
        compiler_params=_params(("arbitrary", "arbitrary")),
        name="neighbourhood_attention",
    )(rpb, view, view, view, w_up)
    return o.reshape(batch * seq_total, GROUP_WIDTH), w_up_bf


def _to_token_order_matrix(tm, d):
    t = lax.broadcasted_iota(jnp.int32, (tm, tm), 0)
    c = lax.broadcasted_iota(jnp.int32, (tm, tm), 1)
    return (c == (t % d) * (tm // d) + t // d).astype(BF16)


def _merge_kernel(o0_ref, o1_ref, o2_ref, l0_ref, l1_ref, l2_ref, ob_ref, gate_ref, x_ref,
                  wpa_ref, wpb_ref, wout_ref, gn_ref, wdn_ref,
                  x2_ref, h2_ref, wdn_bf_ref, lse_tok_ref, *, dilations):
    tm = x_ref.shape[0]
    tb = jnp.dot(ob_ref[...], wpb_ref[...], preferred_element_type=F32)
    o_tok, lse_tok = [], []
    for g, (o_ref, l_ref, d) in enumerate(zip((o0_ref, o1_ref, o2_ref),
                                              (l0_ref, l1_ref, l2_ref), dilations)):
        o = o_ref[...].reshape(tm, GROUP_WIDTH)
        if d == 1:
            o_tok.append(o.astype(F32))
            lse_tok.append(l_ref[...].reshape(tm, HEAD_DIM))
        else:
            o_tok.append(jnp.dot(_to_token_order_matrix(tm, d), o, preferred_element_type=F32))
            for r in range(d):
                lse_tok_ref[g, pl.ds(r, tm // d, stride=d), :] = l_ref[r]
            lse_tok.append(lse_tok_ref[g])
    l0, l1, l2 = lse_tok
    mx = jnp.maximum(jnp.maximum(l0, l1), l2)
    e0, e1, e2 = jnp.exp(l0 - mx), jnp.exp(l1 - mx), jnp.exp(l2 - mx)
    tot = e0 + e1 + e2
    a0, a1, a2 = e0 / tot, e1 / tot, e2 / tot
    parts = []
    for h in range(HEADS_PER_GROUP):
        cols = slice(h * HEAD_DIM, (h + 1) * HEAD_DIM)
        y = (a0[:, h:h + 1] * o_tok[0][:, cols] + a1[:, h:h + 1] * o_tok[1][:, cols]
             + a2[:, h:h + 1] * o_tok[2][:, cols])
        parts.append(y.astype(BF16))
    ya = jnp.concatenate(parts, axis=1)
    ta = jnp.dot(ya, wpa_ref[...], preferred_element_type=F32)
    merged = gate_ref[:, :D_MODEL].astype(F32) * ta + gate_ref[:, D_MODEL:].astype(F32) * tb
    x2 = x_ref[...] + jnp.dot(merged.astype(BF16), wout_ref[...], preferred_element_type=F32)
    x2_ref[...] = x2
    wdn_bf_ref[...] = wdn_ref[...].astype(BF16)
    h2_ref[...] = _rms(x2, gn_ref[...]).astype(BF16)


def _merge(o_groups, lse_groups, ob, gates, x2d, seq, wpa, wpb, wout, gn, w_down):
    m = x2d.shape[0]
    tm = MERGE_TM
    n_steps = m // tm
    tiles_per_batch = seq // tm
    dilations = tuple(o.shape[1] for o in o_groups)
    row = lambda w: pl.BlockSpec((tm, w), lambda i: (i, 0))
    wdn_rows = pl.BlockSpec((w_down.shape[0] // n_steps, w_down.shape[1]), lambda i: (i, 0))
    grouped = lambda d, w: pl.BlockSpec(
        (None, d, tm // d, w), lambda i: (i // tiles_per_batch, 0, i % tiles_per_batch, 0))
    const = lambda a, b: pl.BlockSpec((a, b), lambda i: (0, 0), pipeline_mode=pl.Buffered(1))
    return pl.pallas_call(
        functools.partial(_merge_kernel, dilations=dilations),
        grid=(n_steps,),
        in_specs=[grouped(d, GROUP_WIDTH) for d in dilations]
        + [grouped(d, HEAD_DIM) for d in dilations]
        + [row(GROUP_WIDTH), row(2 * D_MODEL), row(D_MODEL),
           const(GROUP_WIDTH, D_MODEL), const(GROUP_WIDTH, D_MODEL), const(D_MODEL, D_MODEL),
           const(1, D_MODEL), wdn_rows],
        out_specs=[row(D_MODEL), row(D_MODEL), wdn_rows],
        out_shape=[jax.ShapeDtypeStruct((m, D_MODEL), F32), jax.ShapeDtypeStruct((m, D_MODEL), BF16),
                   jax.ShapeDtypeStruct(w_down.shape, BF16)],
        scratch_shapes=[pltpu.VMEM((len(dilations), tm, HEAD_DIM), F32)],
        compiler_params=_params(("arbitrary",)),
        name="merge_out_proj",
    )(*o_groups, *lse_groups, ob, gates, x2d, wpa, wpb, wout, gn, w_down)


def _mlp_kernel(h2_ref, wup_ref, wdn_ref, x2_hbm, gf_ref, out_ref, x2_sem, *, n_f):
    i = pl.program_id(0)
    f = pl.program_id(1)
    tm = out_ref.shape[0]

    def residual_copy():
        return pltpu.make_async_copy(x2_hbm.at[pl.ds(i * tm, tm), :], out_ref, x2_sem)

    def up_down(first=False, last=False):
        if first:
            residual_copy().start()
        hid = jnp.dot(h2_ref[...], wup_ref[...], preferred_element_type=F32)
        hid = jnp.square(jnp.maximum(hid, 0.0)).astype(BF16)
        if first:
            residual_copy().wait()
        if not last:
            out_ref[...] += jnp.dot(hid, wdn_ref[...], preferred_element_type=F32)
            return
        chunk = tm // MLP_FINAL_ROW_CHUNKS
        for r0 in range(0, tm, chunk):
            rows = slice(r0, r0 + chunk)
            y = out_ref[rows, :] + jnp.dot(hid[rows], wdn_ref[...], preferred_element_type=F32)
            out_ref[rows, :] = _rms(y, gf_ref[...])

    assert n_f >= 2

    @pl.when(f == 0)
    def _():
        up_down(first=True)

    @pl.when((f > 0) & (f < n_f - 1))
    def _():
        up_down()

    @pl.when(f == n_f - 1)
    def _():
        up_down(last=True)


def _mlp(h2, wup, wdn, x2, gf):
    m = h2.shape[0]
    tm, tf = MLP_TM, MLP_TF
    n_f = D_FF // tf
    return pl.pallas_call(
        functools.partial(_mlp_kernel, n_f=n_f),
        grid=(m // tm, n_f),
        in_specs=[
            pl.BlockSpec((tm, D_MODEL), lambda i, f: (i, 0)),
            pl.BlockSpec((D_MODEL, tf), lambda i, f: (0, f)),
            pl.BlockSpec((tf, D_MODEL), lambda i, f: (f, 0)),
            pl.BlockSpec(memory_space=pl.ANY),
            pl.BlockSpec((1, D_MODEL), lambda i, f: (0, 0)),
        ],
        out_specs=pl.BlockSpec((tm, D_MODEL), lambda i, f: (i, 0)),
        out_shape=jax.ShapeDtypeStruct((m, D_MODEL), F32),
        scratch_shapes=[pltpu.SemaphoreType.DMA],
        compiler_params=_params(("arbitrary", "arbitrary")),
        name="mlp_residual_norm",
    )(h2, wup, wdn, x2, gf)


def _layer(x2d, batch, seq, norm_mix, w_qkv, w_gate, b_gate, rpb, w_proj_a, w_proj_b, w_out,
           norm_mlp, w_up, w_down):
    slopes = 2.0 ** (-8.0 * np.arange(1, N_HEADS_A + 1) / N_HEADS_A)
    row = lambda v: v.reshape(1, -1)
    h_orders = _norm(x2d, row(norm_mix))
    qkv, gates, w_out_bf, w_pa_bf, w_pb_bf = _project(
        h_orders, w_qkv, w_gate, row(b_gate), w_out, w_proj_a, w_proj_b)
    groups = [(g, window, d) for g, (window, d) in enumerate(DILATION_PATTERNS)]
    o_groups, lse_groups = zip(*_dilated_groups(qkv, batch, seq, groups, slopes))
    ob, w_up_bf = _neighbourhood(qkv, rpb, batch, seq, w_up)
    x2, h2, w_down_bf = _merge(o_groups, lse_groups, ob, gates, x2d, seq, w_pa_bf, w_pb_bf,
                               w_out_bf, row(norm_mlp), w_down)
    return x2, h2, w_up_bf, w_down_bf


def kernel(x, norm_mix, w_qkv, w_gate, b_gate, rpb, w_proj_a, w_proj_b, w_out, norm_mlp, w_up,
           w_down, norm_final):
    batch, seq, _ = x.shape
    depth = norm_mix.shape[0]
    assert depth == 1 and seq % PERM_TILE == 0
    x2d = x.reshape(batch * seq, D_MODEL)
    x2, h2, w_up_bf, w_down_bf = _layer(
        x2d, batch, seq, norm_mix[0], w_qkv[0], w_gate[0], b_gate[0], rpb[0],
        w_proj_a[0], w_proj_b[0], w_out[0], norm_mlp[0], w_up[0], w_down[0])
    out = _mlp(h2, w_up_bf, w_down_bf, x2, norm_final.reshape(1, -1))
    return out.reshape(batch, seq, D_MODEL)
```

```python
import functools

import jax
import jax.numpy as jnp
import numpy as np
from jax import lax
from jax.experimental import pallas as pl
from jax.experimental.pallas import tpu as pltpu

D_MODEL = 2048
HEAD_DIM = 128
N_HEADS = D_MODEL // HEAD_DIM
N_HEADS_B = N_HEADS // 4
N_HEADS_A = N_HEADS - N_HEADS_B
DILATION_PATTERNS = ((128, 1), (512, 4), (2048, 16))
N_GROUPS_A = len(DILATION_PATTERNS)
HEADS_PER_GROUP = N_HEADS_A // N_GROUPS_A
GROUP_WIDTH = HEADS_PER_GROUP * HEAD_DIM
N_HEAD_GROUPS = N_HEADS // HEADS_PER_GROUP
GRID_W = 64
NA_ROWS = 8
NA_COLS = 16
D_FF = 4 * D_MODEL
EPS = 1e-6
NEG = -1e30
SCALE = HEAD_DIM ** -0.5
LOG2E = float(np.log2(np.e))
LN2 = float(np.log(2.0))
LANES = 128

F32 = jnp.float32
BF16 = jnp.bfloat16

VMEM_LIMIT_BYTES = 56 * 1024 * 1024

PERM_TILE = 1024
NORM_CHUNK = 512
NORM_X_SLOTS = 3
PROJ_TM, PROJ_TN = 2048, GROUP_WIDTH
MERGE_TM = 256
MLP_TM, MLP_TF = 1024, 1024
MLP_FINAL_ROW_CHUNKS = 4
DIL_QB = 128
DIL_UNROLL = 8
DIL_ROWS_PER_STEP = 1024
N_WINDOW_CASES = 3
NA_ROWS_PER_STEP = 8
NA_UNROLL = 2

GROUP_DILATIONS = tuple(d for _, d in DILATION_PATTERNS) + (1,)
ROW_ORDERS = tuple(sorted(set(GROUP_DILATIONS)))
GROUP_ROW_ORDER = tuple(ROW_ORDERS.index(d) for d in GROUP_DILATIONS)


def _params(sem):
    return pltpu.CompilerParams(dimension_semantics=sem, vmem_limit_bytes=VMEM_LIMIT_BYTES)


def _rms(x, g):
    ms = jnp.mean(x * x, axis=-1, keepdims=True)
    return (x * lax.rsqrt(ms + EPS)) * g


def _norm_kernel(x_hbm, g_ref, h_ref, slab_ref, xbuf_ref, x_sems, *, n_tiles):
    i = pl.program_id(0)
    tm = xbuf_ref.shape[1]
    ahead = NORM_X_SLOTS - 1

    def x_copy(tile):
        slot = tile % NORM_X_SLOTS
        return pltpu.make_async_copy(x_hbm.at[pl.ds(tile * tm, tm), :], xbuf_ref.at[slot],
                                     x_sems.at[slot])

    @pl.when(i == 0)
    def _():
        for t in range(min(ahead, n_tiles)):
            x_copy(t).start()

    @pl.when(i + ahead < n_tiles)
    def _():
        x_copy(i + ahead).start()

    x_copy(i).wait()
    x_ref = xbuf_ref.at[i % NORM_X_SLOTS]
    x = x_ref[...]
    inv = lax.rsqrt(jnp.mean(x * x, axis=-1, keepdims=True) + EPS)
    n_slabs = NORM_CHUNK // LANES
    for c0 in range(0, D_MODEL, NORM_CHUNK):
        hc = (x_ref[:, c0:c0 + NORM_CHUNK] * inv) * g_ref[:, c0:c0 + NORM_CHUNK]
        h_ref[0, :, c0:c0 + NORM_CHUNK] = hc.astype(BF16)
        for s in range(n_slabs):
            slab_ref[0, s] = hc[:, s * LANES:(s + 1) * LANES]
        for v in range(1, len(ROW_ORDERS)):
            d_prev, d = ROW_ORDERS[v - 1], ROW_ORDERS[v]
            q = d // d_prev
            n_prev, n = tm // d_prev, tm // d
            last = v == len(ROW_ORDERS) - 1
            for s in range(n_slabs):
                cols = slice(c0 + s * LANES, c0 + (s + 1) * LANES)
                for r in range(d_prev):
                    for r2 in range(q):
                        rows = slab_ref[(v - 1) % 2, s, pl.ds(r * n_prev + r2, n, stride=q), :]
                        dst = (d_prev * r2 + r) * n
                        h_ref[v, dst:dst + n, cols] = rows.astype(BF16)
                        if not last:
                            slab_ref[v % 2, s, dst:dst + n, :] = rows


def _norm(x2d, g):
    m = x2d.shape[0]
    tm = PERM_TILE
    nv = len(ROW_ORDERS)
    assert ROW_ORDERS[0] == 1
    return pl.pallas_call(
        functools.partial(_norm_kernel, n_tiles=m // tm),
        grid=(m // tm,),
        in_specs=[pl.BlockSpec(memory_space=pl.ANY),
                  pl.BlockSpec((1, D_MODEL), lambda i: (0, 0))],
        out_specs=pl.BlockSpec((nv, tm, D_MODEL), lambda i: (0, i, 0)),
        out_shape=jax.ShapeDtypeStruct((nv, m, D_MODEL), BF16),
        scratch_shapes=[pltpu.VMEM((2, NORM_CHUNK // LANES, tm, LANES), F32),
                        pltpu.VMEM((NORM_X_SLOTS, tm, D_MODEL), F32),
                        pltpu.SemaphoreType.DMA((NORM_X_SLOTS,))],
        compiler_params=_params(("arbitrary",)),
        name="rmsnorm_row_orders",
    )(x2d, g)


def _proj_kernel(h_ref, w_first_ref, wq_next_ref, wg_next_ref, bg_ref,
                 qkv_ref, gate_ref, w_bf_ref, *, n_qkv_steps):
    j = pl.program_id(0)
    i = pl.program_id(1)

    @pl.when((j == 0) & (i == 0))
    def _():
        w_bf_ref[0] = w_first_ref[...].astype(BF16)

    def side_casts(next_ref):
        part = next_ref.shape[0]
        rows = pl.ds(pl.multiple_of(i * part, part), part)
        w_bf_ref[(j + 1) % 2, rows, :] = next_ref[...].astype(BF16)

    def qkv_step(next_ref):
        y = jnp.dot(h_ref[...], w_bf_ref[j % 2], preferred_element_type=F32)
        qkv_ref[...] = y.astype(BF16)
        side_casts(next_ref)

    @pl.when(j < n_qkv_steps - 1)
    def _():
        qkv_step(wq_next_ref)

    @pl.when(j == n_qkv_steps - 1)
    def _():
        qkv_step(wg_next_ref)

    @pl.when(j >= n_qkv_steps)
    def _():
        z = jnp.dot(h_ref[...], w_bf_ref[j % 2], preferred_element_type=F32) + bg_ref[...]
        gate_ref[...] = 0.5 * jnp.tanh((0.5 * z).astype(BF16)) + 0.5
        side_casts(wg_next_ref)


def _project(h_orders, w_qkv, w_gate, b_gate):
    m = h_orders.shape[1]
    tm, tn = PROJ_TM, PROJ_TN
    n_groups = N_HEAD_GROUPS
    nq = 3 * n_groups
    ng = (2 * D_MODEL) // tn
    n_i = m // tm
    row_order = GROUP_ROW_ORDER

    def lhs_map(j, i):
        v = jnp.int32(0)
        for grp in range(n_groups):
            v = jnp.where(j // 3 == grp, row_order[grp], v)
        return (v, i, 0)

    def w_qkv_col(j):
        jj = jnp.minimum(j, nq - 1)
        return (jj % 3) * n_groups + jj // 3

    gate_col = lambda j, i: (0, jnp.maximum(j - nq, 0))
    wq_next_map = lambda j, i: (jnp.where(j + 1 < nq, i, n_i - 1), w_qkv_col(j + 1))
    wg_next_map = lambda j, i: (jnp.where(j + 1 >= nq, i, 0), jnp.clip(j + 1 - nq, 0, ng - 1))

    def qkv_out_map(j, i):
        jj = jnp.minimum(j, nq - 1)
        return (jj // 3, jnp.where(j < nq, i, n_i - 1), jj % 3)

    gate_out_map = lambda j, i: (jnp.where(j >= nq, i, 0), jnp.maximum(j - nq, 0))

    return pl.pallas_call(
        functools.partial(_proj_kernel, n_qkv_steps=nq),
        grid=(nq + ng, n_i),
        in_specs=[
            pl.BlockSpec((None, tm, D_MODEL), lhs_map),
            pl.BlockSpec((D_MODEL, tn), lambda j, i: (0, 0), pipeline_mode=pl.Buffered(1)),
            pl.BlockSpec((D_MODEL // n_i, tn), wq_next_map),
            pl.BlockSpec((D_MODEL // n_i, tn), wg_next_map),
            pl.BlockSpec((1, tn), gate_col),
        ],
        out_specs=[
            pl.BlockSpec((None, tm, tn), qkv_out_map),
            pl.BlockSpec((tm, tn), gate_out_map),
        ],
        out_shape=[
            jax.ShapeDtypeStruct((n_groups, m, 3 * GROUP_WIDTH), BF16),
            jax.ShapeDtypeStruct((m, 2 * D_MODEL), BF16),
        ],
        scratch_shapes=[pltpu.VMEM((2, D_MODEL, tn), BF16)],
        compiler_params=_params(("arbitrary", "arbitrary")),
        name="proj_qkv_gate",
    )(h_orders, w_qkv, w_qkv, w_gate, b_gate)


def _dilated_block_shape(seq, half_window):
    kw = DIL_QB + 2 * half_window
    return (seq, seq) if seq <= kw else (DIL_QB, kw)


def _dilated_kernel(*refs, jobs):
    n = len(jobs)
    for idx, (seq, half_window, coefs) in enumerate(jobs):
        _dilated_job(*refs[3 * idx:3 * idx + 3], *refs[3 * n + 2 * idx:3 * n + 2 * idx + 2],
                     refs[5 * n + idx], seq=seq, half_window=half_window, coefs=coefs)


def _dilated_job(q_ref, k_ref, v_ref, o_ref, lse_ref, bias_ref, *, seq, half_window, coefs):
    qb, kw = _dilated_block_shape(seq, half_window)
    n_res = q_ref.shape[1]
    q_rows = q_ref.shape[0] * q_ref.shape[2]
    q_base = 0 if q_rows == seq else (pl.program_id(1) % (seq // q_rows)) * q_rows
    n_blocks = q_rows // qb

    def rows_loader(ref):
        n_tiles, _, tile_rows, _ = ref.shape
        if n_res == 1:
            flat = ref.at[:, 0].reshape(n_tiles * tile_rows, GROUP_WIDTH)
            return lambda rr, start, size, cols: flat[pl.ds(start, size), cols]
        assert n_blocks == 1
        return lambda rr, start, size, cols: jnp.concatenate(
            [ref[t, rr, :, cols] for t in range(n_tiles)], axis=0)

    load_q, load_k, load_v = rows_loader(q_ref), rows_loader(k_ref), rows_loader(v_ref)
    lane = lax.broadcasted_iota(jnp.int32, (qb, HEAD_DIM), 1)
    heads = range(HEADS_PER_GROUP)
    head_cols = [slice(h * HEAD_DIM, (h + 1) * HEAD_DIM) for h in heads]

    @pl.when((pl.program_id(0) == 0) & (pl.program_id(1) == 0))
    def _():
        qrow = lax.broadcasted_iota(jnp.int32, (qb, kw), 0)
        kcol = lax.broadcasted_iota(jnp.int32, (qb, kw), 1)
        for c in range(bias_ref.shape[0]):
            dist = jnp.abs(kcol - qrow - c * half_window)
            dist_f = dist.astype(F32)
            for h in heads:
                bias_ref[c, h] = jnp.where(dist <= half_window, (-coefs[h] * LOG2E) * dist_f, NEG)

    def block(qi, carry):
        q0 = pl.multiple_of(qi * qb, qb)
        qg = q_base + q0
        ks = pl.multiple_of(jnp.clip(qg - half_window, 0, seq - kw), half_window)
        window_case = (qg - ks) // half_window
        pairs = [(rr, h) for rr in range(n_res) for h in heads]
        scores = [lax.dot_general(load_q(rr, q0, qb, head_cols[h]), load_k(rr, ks, kw, head_cols[h]),
                                  (((1,), (1,)), ((), ())), preferred_element_type=F32)
                  for rr, h in pairs]
        probs, dens = [], []
        lse_all = [jnp.zeros((qb, HEAD_DIM), F32) for _ in range(n_res)]
        for (rr, h), s in zip(pairs, scores):
            t = s * (SCALE * LOG2E) + bias_ref[window_case, h]
            m = jnp.max(t, axis=-1, keepdims=True)
            p = jnp.exp2(t - m)
            den = jnp.sum(p, axis=-1, keepdims=True)
            probs.append(p.astype(BF16))
            dens.append(den)
            lse_all[rr] = jnp.where(lane == h, m * LN2 + jnp.log(den), lse_all[rr])
        for (rr, h), p, den in zip(pairs, probs, dens):
            o = jnp.dot(p, load_v(rr, ks, kw, head_cols[h]), preferred_element_type=F32) / den
            o_ref[rr, pl.ds(q0, qb), head_cols[h]] = o.astype(BF16)
        for rr in range(n_res):
            lse_ref[rr, pl.ds(q0, qb), :] = lse_all[rr]
        return carry

    lax.fori_loop(0, n_blocks, block, 0, unroll=min(DIL_UNROLL, n_blocks))


def _dilated_steps(seq_total, window, dilation):
    seq = seq_total // dilation
    qb, _ = _dilated_block_shape(seq, window // (2 * dilation))
    n_res = min(dilation, max(1, DIL_ROWS_PER_STEP // seq)) if qb == seq else 1
    return n_res, dilation // n_res


def _dilated_groups(qkv, batch, seq_total, groups, slopes):
    tiles = seq_total // PERM_TILE
    jobs, operands, in_specs, out_specs, out_shapes, scratch = [], [], [], [], [], []
    n_steps = max(_dilated_steps(seq_total, w, d)[1] for _, w, d in groups)
    for group, window, d in groups:
        seq = seq_total // d
        half_window = window // (2 * d)
        qb, kw = _dilated_block_shape(seq, half_window)
        n_res, res_steps = _dilated_steps(seq_total, window, d)
        q_split = n_steps // res_steps
        assert res_steps * q_split == n_steps and tiles % q_split == 0
        q_tiles = tiles // q_split
        coefs = tuple(float(slopes[group * HEADS_PER_GROUP + h]) * d
                      for h in range(HEADS_PER_GROUP))
        jobs.append((seq, half_window, coefs))
        rows = PERM_TILE // d
        view = qkv.reshape(qkv.shape[0], batch, tiles, d, rows, qkv.shape[-1])
        q_map = lambda b, r, g=group, s=q_split: (g, b, r % s, r // s, 0, 0)
        kv_map = lambda which: (lambda b, r, g=group, s=q_split: (g, b, 0, r // s, 0, which))
        operands += [view, view, view]
        in_specs += [
            pl.BlockSpec((None, None, q_tiles, n_res, rows, GROUP_WIDTH), q_map),
            pl.BlockSpec((None, None, tiles, n_res, rows, GROUP_WIDTH), kv_map(1)),
            pl.BlockSpec((None, None, tiles, n_res, rows, GROUP_WIDTH), kv_map(2)),
        ]
        out_map = lambda b, r, s=q_split: (b, r // s, r % s, 0)
        out_specs += [pl.BlockSpec((None, n_res, seq // q_split, GROUP_WIDTH), out_map),
                      pl.BlockSpec((None, n_res, seq // q_split, HEAD_DIM), out_map)]
        out_shapes += [jax.ShapeDtypeStruct((batch, d, seq, GROUP_WIDTH), BF16),
                       jax.ShapeDtypeStruct((batch, d, seq, HEAD_DIM), F32)]
        scratch.append(pltpu.VMEM((N_WINDOW_CASES, HEADS_PER_GROUP, qb, kw), F32))
    outs = pl.pallas_call(
        functools.partial(_dilated_kernel, jobs=tuple(jobs)),
        grid=(batch, n_steps),
        in_specs=in_specs,
        out_specs=out_specs,
        out_shape=out_shapes,
        scratch_shapes=scratch,
        compiler_params=_params(("arbitrary", "arbitrary")),
        name="dilated_attention_d" + "_d".join(str(d) for _, _, d in groups),
    )(*operands)
    return [(outs[2 * i], outs[2 * i + 1]) for i in range(len(groups))]


NA_GROUP_ROWS = NA_ROWS // 2
NA_WINDOW_ROWS = NA_ROWS + NA_GROUP_ROWS
N_BIAS_VARIANTS = 3


def _expand_na_bias(rpb_ref, out_ref, h):
    half = NA_ROWS // 2
    n_off = 2 * NA_ROWS - 1
    pad = half
    n_seg = 3 * NA_ROWS
    width = n_seg * GRID_W
    qc = lax.broadcasted_iota(jnp.int32, (GRID_W, width), 0)
    kc = lax.broadcasted_iota(jnp.int32, (GRID_W, width), 1) % GRID_W
    col_idx = jnp.clip(kc - qc, -(NA_COLS - 1), NA_COLS - 1) + (NA_COLS - 1)
    seg = lax.broadcasted_iota(jnp.int32, (1, width), 1) // GRID_W
    table = jnp.zeros((GRID_W, width), F32)
    for j in range(2 * NA_COLS - 1):
        row_vals = jnp.zeros((1, width), F32)
        for i in range(n_off):
            row_vals = jnp.where(seg == i + pad, rpb_ref[h, i, j], row_vals)
        table = jnp.where(col_idx == j, row_vals, table)
    cs = jnp.clip(qc - NA_COLS // 2, 0, GRID_W - NA_COLS)
    table = jnp.where((kc >= cs) & (kc < cs + NA_COLS), table * LOG2E, NEG)
    nkeys = NA_WINDOW_ROWS * GRID_W
    key_row = lax.broadcasted_iota(jnp.int32, (GRID_W, nkeys), 1) // GRID_W
    for v in range(N_BIAS_VARIANTS):
        for a in range(NA_GROUP_ROWS):
            first = {0: 0, 1: a, 2: NA_WINDOW_ROWS - NA_ROWS}[v]
            seg0 = (NA_ROWS - 1) - half * v - a + pad
            assert 0 <= seg0 and seg0 + NA_WINDOW_ROWS <= n_seg
            slab = table[:, seg0 * GRID_W:(seg0 + NA_WINDOW_ROWS) * GRID_W]
            slab = jnp.where((key_row >= first) & (key_row < first + NA_ROWS), slab, NEG)
            out_ref[v, h, a * GRID_W:(a + 1) * GRID_W, :] = slab


def _na_kernel(rpb_ref, q_ref, k_ref, v_ref, *refs, rows):
    n_side = (len(refs) - 2) // 2
    side_in, o_ref, side_out, bias_ref = (refs[:n_side], refs[n_side],
                                          refs[n_side + 1:2 * n_side + 1], refs[-1])
    rb = pl.program_id(1)
    nq = NA_GROUP_ROWS * GRID_W
    nkeys = NA_WINDOW_ROWS * GRID_W

    @pl.when((pl.program_id(0) == 0) & (rb == 0))
    def _():
        for h in range(N_HEADS_B):
            _expand_na_bias(rpb_ref, bias_ref, h)

    def one_group(gl, carry):
        r0 = rb * NA_ROWS_PER_STEP + gl * NA_GROUP_ROWS
        ws = jnp.clip(r0 - NA_ROWS // 2, 0, rows - NA_WINDOW_ROWS)
        var = (r0 - ws) // (NA_ROWS // 2)
        q0 = pl.multiple_of(gl * nq, nq)
        k0 = pl.multiple_of(ws * GRID_W, GRID_W)
        heads = range(N_HEADS_B)
        head_cols = [slice(h * HEAD_DIM, (h + 1) * HEAD_DIM) for h in heads]
        scores = [lax.dot_general(q_ref[pl.ds(q0, nq), c], k_ref[pl.ds(k0, nkeys), c],
                                  (((1,), (1,)), ((), ())), preferred_element_type=F32)
                  for c in head_cols]
        probs, dens = [], []
        for h in heads:
            t = scores[h] * (SCALE * LOG2E) + bias_ref[var, h]
            m = jnp.max(t, axis=-1, keepdims=True)
            p = jnp.exp2(t - m)
            dens.append(jnp.sum(p, axis=-1, keepdims=True))
            probs.append(p.astype(BF16))
        for h in heads:
            o = jnp.dot(probs[h], v_ref[pl.ds(k0, nkeys), head_cols[h]],
                        preferred_element_type=F32) / dens[h]
            o_ref[pl.ds(q0, nq), head_cols[h]] = o.astype(BF16)
        return carry

    n_groups = NA_ROWS_PER_STEP // NA_GROUP_ROWS
    lax.fori_loop(0, n_groups, one_group, 0, unroll=min(NA_UNROLL, n_groups))
    for w_ref, w_bf_ref in zip(side_in, side_out):
        w_bf_ref[...] = w_ref[...].astype(BF16)


def _neighbourhood(qkv, rpb, batch, seq_total, side_weights):
    group = N_GROUPS_A
    assert GROUP_DILATIONS[group] == 1
    rows = seq_total // GRID_W
    assert rows >= NA_WINDOW_ROWS and rows % NA_ROWS_PER_STEP == 0
    assert NA_ROWS_PER_STEP % NA_GROUP_ROWS == 0
    tq = NA_ROWS_PER_STEP * GRID_W
    steps_per_batch = rows // NA_ROWS_PER_STEP
    view = qkv.reshape(qkv.shape[0], batch, seq_total, qkv.shape[-1])
    n_steps = batch * steps_per_batch
    assert all(w.shape[0] % (16 * n_steps) == 0 for w in side_weights)
    side_rows = [pl.BlockSpec((w.shape[0] // n_steps, w.shape[1]),
                              lambda b, i: (b * steps_per_batch + i, 0)) for w in side_weights]
    o, *side_bf = pl.pallas_call(
        functools.partial(_na_kernel, rows=rows),
        grid=(batch, steps_per_batch),
        in_specs=[
            pl.BlockSpec(memory_space=pltpu.SMEM),
            pl.BlockSpec((None, None, tq, GROUP_WIDTH), lambda b, i: (group, b, i, 0)),
            pl.BlockSpec((None, None, seq_total, GROUP_WIDTH), lambda b, i: (group, b, 0, 1)),
            pl.BlockSpec((None, None, seq_total, GROUP_WIDTH), lambda b, i: (group, b, 0, 2)),
        ] + side_rows,
        out_specs=[pl.BlockSpec((None, tq, GROUP_WIDTH), lambda b, i: (b, i, 0))] + side_rows,
        out_shape=[jax.ShapeDtypeStruct((batch, seq_total, GROUP_WIDTH), BF16)]
        + [jax.ShapeDtypeStruct(w.shape, BF16) for w in side_weights],
        scratch_shapes=[pltpu.VMEM((N_BIAS_VARIANTS, N_HEADS_B, NA_GROUP_ROWS * GRID_W,
                                    NA_WINDOW_ROWS * GRID_W), F32)],
        compiler_params=_params(("arbitrary", "arbitrary")),
        name="neighbourhood_attention",
    )(rpb, view, view, view, *side_weights)
    return o.reshape(batch * seq_total, GROUP_WIDTH), side_bf


def _to_token_order_matrix(tm, d):
    t = lax.broadcasted_iota(jnp.int32, (tm, tm), 0)
    c = lax.broadcasted_iota(jnp.int32, (tm, tm), 1)
    return (c == (t % d) * (tm // d) + t // d).astype(BF16)


def _merge_kernel(o0_ref, o1_ref, o2_ref, l0_ref, l1_ref, l2_ref, ob_ref, gate_ref, x_ref,
                  wpa_ref, wpb_ref, wout_ref, gn_ref, wdn_ref,
                  x2_ref, h2_ref, wdn_bf_ref, lse_tok_ref, *, dilations):
    tm = x_ref.shape[0]
    tb = jnp.dot(ob_ref[...], wpb_ref[...], preferred_element_type=F32)
    o_tok, lse_tok = [], []
    for g, (o_ref, l_ref, d) in enumerate(zip((o0_ref, o1_ref, o2_ref),
                                              (l0_ref, l1_ref, l2_ref), dilations)):
        o = o_ref[...].reshape(tm, GROUP_WIDTH)
        if d == 1:
            o_tok.append(o.astype(F32))
            lse_tok.append(l_ref[...].reshape(tm, HEAD_DIM))
        else:
            o_tok.append(jnp.dot(_to_token_order_matrix(tm, d), o, preferred_element_type=F32))
            for r in range(d):
                lse_tok_ref[g, pl.ds(r, tm // d, stride=d), :] = l_ref[r]
            lse_tok.append(lse_tok_ref[g])
    l0, l1, l2 = lse_tok
    mx = jnp.maximum(jnp.maximum(l0, l1), l2)
    e0, e1, e2 = jnp.exp(l0 - mx), jnp.exp(l1 - mx), jnp.exp(l2 - mx)
    tot = e0 + e1 + e2
    a0, a1, a2 = e0 / tot, e1 / tot, e2 / tot
    parts = []
    for h in range(HEADS_PER_GROUP):
        cols = slice(h * HEAD_DIM, (h + 1) * HEAD_DIM)
        y = (a0[:, h:h + 1] * o_tok[0][:, cols] + a1[:, h:h + 1] * o_tok[1][:, cols]
             + a2[:, h:h + 1] * o_tok[2][:, cols])
        parts.append(y.astype(BF16))
    ya = jnp.concatenate(parts, axis=1)
    ta = jnp.dot(ya, wpa_ref[...], preferred_element_type=F32)
    merged = gate_ref[:, :D_MODEL].astype(F32) * ta + gate_ref[:, D_MODEL:].astype(F32) * tb
    x2 = x_ref[...] + jnp.dot(merged.astype(BF16), wout_ref[...], preferred_element_type=F32)
    x2_ref[...] = x2
    wdn_bf_ref[...] = wdn_ref[...].astype(BF16)
    h2_ref[...] = _rms(x2, gn_ref[...]).astype(BF16)


def _merge(o_groups, lse_groups, ob, gates, x2d, seq, wpa, wpb, wout, gn, w_down):
    m = x2d.shape[0]
    tm = MERGE_TM
    n_steps = m // tm
    tiles_per_batch = seq // tm
    dilations = tuple(o.shape[1] for o in o_groups)
    row = lambda w: pl.BlockSpec((tm, w), lambda i: (i, 0))
    wdn_rows = pl.BlockSpec((w_down.shape[0] // n_steps, w_down.shape[1]), lambda i: (i, 0))
    grouped = lambda d, w: pl.BlockSpec(
        (None, d, tm // d, w), lambda i: (i // tiles_per_batch, 0, i % tiles_per_batch, 0))
    const = lambda a, b: pl.BlockSpec((a, b), lambda i: (0, 0), pipeline_mode=pl.Buffered(1))
    return pl.pallas_call(
        functools.partial(_merge_kernel, dilations=dilations),
        grid=(n_steps,),
        in_specs=[grouped(d, GROUP_WIDTH) for d in dilations]
        + [grouped(d, HEAD_DIM) for d in dilations]
        + [row(GROUP_WIDTH), row(2 * D_MODEL), row(D_MODEL),
           const(GROUP_WIDTH, D_MODEL), const(GROUP_WIDTH, D_MODEL), const(D_MODEL, D_MODEL),
           const(1, D_MODEL), wdn_rows],
        out_specs=[row(D_MODEL), row(D_MODEL), wdn_rows],
        out_shape=[jax.ShapeDtypeStruct((m, D_MODEL), F32), jax.ShapeDtypeStruct((m, D_MODEL), BF16),
                   jax.ShapeDtypeStruct(w_down.shape, BF16)],
        scratch_shapes=[pltpu.VMEM((len(dilations), tm, HEAD_DIM), F32)],
        compiler_params=_params(("arbitrary",)),
        name="merge_out_proj",
    )(*o_groups, *lse_groups, ob, gates, x2d, wpa, wpb, wout, gn, w_down)


def _mlp_kernel(h2_ref, wup_ref, wdn_ref, x2_hbm, gf_ref, out_ref, x2_sem, *, n_f):
    i = pl.program_id(0)
    f = pl.program_id(1)
    tm = out_ref.shape[0]

    def residual_copy():
        return pltpu.make_async_copy(x2_hbm.at[pl.ds(i * tm, tm), :], out_ref, x2_sem)

    def up_down(first=False, last=False):
        if first:
            residual_copy().start()
        hid = jnp.dot(h2_ref[...], wup_ref[...], preferred_element_type=F32)
        hid = jnp.square(jnp.maximum(hid, 0.0)).astype(BF16)
        if first:
            residual_copy().wait()
        if not last:
            out_ref[...] += jnp.dot(hid, wdn_ref[...], preferred_element_type=F32)
            return
        chunk = tm // MLP_FINAL_ROW_CHUNKS
        for r0 in range(0, tm, chunk):
            rows = slice(r0, r0 + chunk)
            y = out_ref[rows, :] + jnp.dot(hid[rows], wdn_ref[...], preferred_element_type=F32)
            out_ref[rows, :] = _rms(y, gf_ref[...])

    assert n_f >= 2

    @pl.when(f == 0)
    def _():
        up_down(first=True)

    @pl.when((f > 0) & (f < n_f - 1))
    def _():
        up_down()

    @pl.when(f == n_f - 1)
    def _():
        up_down(last=True)


def _mlp(h2, wup, wdn, x2, gf):
    m = h2.shape[0]
    tm, tf = MLP_TM, MLP_TF
    n_f = D_FF // tf
    return pl.pallas_call(
        functools.partial(_mlp_kernel, n_f=n_f),
        grid=(m // tm, n_f),
        in_specs=[
            pl.BlockSpec((tm, D_MODEL), lambda i, f: (i, 0)),
            pl.BlockSpec((D_MODEL, tf), lambda i, f: (0, f)),
            pl.BlockSpec((tf, D_MODEL), lambda i, f: (f, 0)),
            pl.BlockSpec(memory_space=pl.ANY),
            pl.BlockSpec((1, D_MODEL), lambda i, f: (0, 0)),
        ],
        out_specs=pl.BlockSpec((tm, D_MODEL), lambda i, f: (i, 0)),
        out_shape=jax.ShapeDtypeStruct((m, D_MODEL), F32),
        scratch_shapes=[pltpu.SemaphoreType.DMA],
        compiler_params=_params(("arbitrary", "arbitrary")),
        name="mlp_residual_norm",
    )(h2, wup, wdn, x2, gf)


def _layer(x2d, batch, seq, norm_mix, w_qkv, w_gate, b_gate, rpb, w_proj_a, w_proj_b, w_out,
           norm_mlp, w_up, w_down):
    slopes = 2.0 ** (-8.0 * np.arange(1, N_HEADS_A + 1) / N_HEADS_A)
    row = lambda v: v.reshape(1, -1)
    h_orders = _norm(x2d, row(norm_mix))
    qkv, gates = _project(h_orders, w_qkv, w_gate, row(b_gate))
    groups = [(g, window, d) for g, (window, d) in enumerate(DILATION_PATTERNS)]
    o_groups, lse_groups = zip(*_dilated_groups(qkv, batch, seq, groups, slopes))
    ob, (w_up_bf, w_out_bf, w_pa_bf, w_pb_bf) = _neighbourhood(
        qkv, rpb, batch, seq, [w_up, w_out, w_proj_a, w_proj_b])
    x2, h2, w_down_bf = _merge(o_groups, lse_groups, ob, gates, x2d, seq, w_pa_bf, w_pb_bf,
                               w_out_bf, row(norm_mlp), w_down)
    return x2, h2, w_up_bf, w_down_bf


def kernel(x, norm_mix, w_qkv, w_gate, b_gate, rpb, w_proj_a, w_proj_b, w_out, norm_mlp, w_up,
           w_down, norm_final):
    batch, seq, _ = x.shape
    depth = norm_mix.shape[0]
    assert depth == 1 and seq % PERM_TILE == 0
    x2d = x.reshape(batch * seq, D_MODEL)
    x2, h2, w_up_bf, w_down_bf = _layer(
        x2d, batch, seq, norm_mix[0], w_qkv[0], w_gate[0], b_gate[0], rpb[0],
        w_proj_a[0], w_proj_b[0], w_out[0], norm_mlp[0], w_up[0], w_down[0])
    out = _mlp(h2, w_up_bf, w_down_bf, x2, norm_final.reshape(1, -1))
    return out.reshape(batch, seq, D_MODEL)
```

```python
import functools

import jax
import jax.numpy as jnp
import numpy as np
from jax import lax
from jax.experimental import pallas as pl
from jax.experimental.pallas import tpu as pltpu

D_MODEL = 2048
HEAD_DIM = 128
N_HEADS = D_MODEL // HEAD_DIM
N_HEADS_B = N_HEADS // 4
N_HEADS_A = N_HEADS - N_HEADS_B
DILATION_PATTERNS = ((128, 1), (512, 4), (2048, 16))
N_GROUPS_A = len(DILATION_PATTERNS)
HEADS_PER_GROUP = N_HEADS_A // N_GROUPS_A
GROUP_WIDTH = HEADS_PER_GROUP * HEAD_DIM
N_HEAD_GROUPS = N_HEADS // HEADS_PER_GROUP
GRID_W = 64
NA_ROWS = 8
NA_COLS = 16
D_FF = 4 * D_MODEL
EPS = 1e-6
NEG = -1e30
SCALE = HEAD_DIM ** -0.5
LOG2E = float(np.log2(np.e))
LN2 = float(np.log(2.0))
LANES = 128

F32 = jnp.float32
BF16 = jnp.bfloat16

VMEM_LIMIT_BYTES = 56 * 1024 * 1024

PERM_TILE = 1024
NORM_CHUNK = 512
NORM_X_SLOTS = 3
RING_DMA_PRIORITY = 1
PROJ_TM, PROJ_TN = 2048, GROUP_WIDTH
SIDE_CAST_ROWS = 32
MERGE_TM = 256
MLP_TM, MLP_TF = 1024, 1024
MLP_FINAL_ROW_CHUNKS = 4
DIL_QB = 128
DIL_UNROLL = 8
DIL_ROWS_PER_STEP = 1024
N_WINDOW_CASES = 3
NA_ROWS_PER_STEP = 8
NA_UNROLL = 2

GROUP_DILATIONS = tuple(d for _, d in DILATION_PATTERNS) + (1,)
ROW_ORDERS = tuple(sorted(set(GROUP_DILATIONS)))
GROUP_ROW_ORDER = tuple(ROW_ORDERS.index(d) for d in GROUP_DILATIONS)


def _params(sem):
    return pltpu.CompilerParams(dimension_semantics=sem, vmem_limit_bytes=VMEM_LIMIT_BYTES)


def _rms(x, g):
    ms = jnp.mean(x * x, axis=-1, keepdims=True)
    return (x * lax.rsqrt(ms + EPS)) * g


def _norm_kernel(x_hbm, g_ref, h_ref, slab_ref, xbuf_ref, x_sems, *, n_tiles):
    i = pl.program_id(0)
    tm = xbuf_ref.shape[1]
    ahead = NORM_X_SLOTS - 1

    def x_copy(tile):
        slot = tile % NORM_X_SLOTS
        return pltpu.make_async_copy(x_hbm.at[pl.ds(tile * tm, tm), :], xbuf_ref.at[slot],
                                     x_sems.at[slot])

    @pl.when(i == 0)
    def _():
        for t in range(min(ahead, n_tiles)):
            x_copy(t).start(priority=RING_DMA_PRIORITY)

    @pl.when(i + ahead < n_tiles)
    def _():
        x_copy(i + ahead).start(priority=RING_DMA_PRIORITY)

    x_copy(i).wait()
    x_ref = xbuf_ref.at[i % NORM_X_SLOTS]
    x = x_ref[...]
    inv = lax.rsqrt(jnp.mean(x * x, axis=-1, keepdims=True) + EPS)
    n_slabs = NORM_CHUNK // LANES
    for c0 in range(0, D_MODEL, NORM_CHUNK):
        hc = (x_ref[:, c0:c0 + NORM_CHUNK] * inv) * g_ref[:, c0:c0 + NORM_CHUNK]
        h_ref[0, :, c0:c0 + NORM_CHUNK] = hc.astype(BF16)
        for s in range(n_slabs):
            slab_ref[0, s] = hc[:, s * LANES:(s + 1) * LANES]
        for v in range(1, len(ROW_ORDERS)):
            d_prev, d = ROW_ORDERS[v - 1], ROW_ORDERS[v]
            q = d // d_prev
            n_prev, n = tm // d_prev, tm // d
            last = v == len(ROW_ORDERS) - 1
            for s in range(n_slabs):
                cols = slice(c0 + s * LANES, c0 + (s + 1) * LANES)
                for r in range(d_prev):
                    for r2 in range(q):
                        rows = slab_ref[(v - 1) % 2, s, pl.ds(r * n_prev + r2, n, stride=q), :]
                        dst = (d_prev * r2 + r) * n
                        h_ref[v, dst:dst + n, cols] = rows.astype(BF16)
                        if not last:
                            slab_ref[v % 2, s, dst:dst + n, :] = rows


def _norm(x2d, g):
    m = x2d.shape[0]
    tm = PERM_TILE
    nv = len(ROW_ORDERS)
    assert ROW_ORDERS[0] == 1
    return pl.pallas_call(
        functools.partial(_norm_kernel, n_tiles=m // tm),
        grid=(m // tm,),
        in_specs=[pl.BlockSpec(memory_space=pl.ANY),
                  pl.BlockSpec((1, D_MODEL), lambda i: (0, 0))],
        out_specs=pl.BlockSpec((nv, tm, D_MODEL), lambda i: (0, i, 0)),
        out_shape=jax.ShapeDtypeStruct((nv, m, D_MODEL), BF16),
        scratch_shapes=[pltpu.VMEM((2, NORM_CHUNK // LANES, tm, LANES), F32),
                        pltpu.VMEM((NORM_X_SLOTS, tm, D_MODEL), F32),
                        pltpu.SemaphoreType.DMA((NORM_X_SLOTS,))],
        compiler_params=_params(("arbitrary",)),
        name="rmsnorm_row_orders",
    )(x2d, g)


def _proj_kernel(h_ref, w_first_ref, wq_next_ref, wg_next_ref, bg_ref, wout_ref, wpa_ref, wpb_ref,
                 qkv_ref, gate_ref, wout_bf_ref, wpa_bf_ref, wpb_bf_ref, w_bf_ref, *, n_qkv_steps):
    j = pl.program_id(0)
    i = pl.program_id(1)

    @pl.when((j == 0) & (i == 0))
    def _():
        w_bf_ref[0] = w_first_ref[...].astype(BF16)

    def side_casts(next_ref):
        part = next_ref.shape[0]
        rows = pl.ds(pl.multiple_of(i * part, part), part)
        w_bf_ref[(j + 1) % 2, rows, :] = next_ref[...].astype(BF16)
        wout_bf_ref[...] = wout_ref[...].astype(BF16)
        wpa_bf_ref[...] = wpa_ref[...].astype(BF16)
        wpb_bf_ref[...] = wpb_ref[...].astype(BF16)

    def qkv_step(next_ref):
        y = jnp.dot(h_ref[...], w_bf_ref[j % 2], preferred_element_type=F32)
        qkv_ref[...] = y.astype(BF16)
        side_casts(next_ref)

    @pl.when(j < n_qkv_steps - 1)
    def _():
        qkv_step(wq_next_ref)

    @pl.when(j == n_qkv_steps - 1)
    def _():
        qkv_step(wg_next_ref)

    @pl.when(j >= n_qkv_steps)
    def _():
        z = jnp.dot(h_ref[...], w_bf_ref[j % 2], preferred_element_type=F32) + bg_ref[...]
        gate_ref[...] = 0.5 * jnp.tanh((0.5 * z).astype(BF16)) + 0.5
        side_casts(wg_next_ref)


def _project(h_orders, w_qkv, w_gate, b_gate, w_out, w_proj_a, w_proj_b):
    m = h_orders.shape[1]
    tm, tn = PROJ_TM, PROJ_TN
    n_groups = N_HEAD_GROUPS
    nq = 3 * n_groups
    ng = (2 * D_MODEL) // tn
    n_i = m // tm
    row_order = GROUP_ROW_ORDER

    def lhs_map(j, i):
        v = jnp.int32(0)
        for grp in range(n_groups):
            v = jnp.where(j // 3 == grp, row_order[grp], v)
        return (v, i, 0)

    def w_qkv_col(j):
        jj = jnp.minimum(j, nq - 1)
        return (jj % 3) * n_groups + jj // 3

    gate_col = lambda j, i: (0, jnp.maximum(j - nq, 0))
    wq_next_map = lambda j, i: (jnp.where(j + 1 < nq, i, n_i - 1), w_qkv_col(j + 1))
    wg_next_map = lambda j, i: (jnp.where(j + 1 >= nq, i, 0), jnp.clip(j + 1 - nq, 0, ng - 1))

    def qkv_out_map(j, i):
        jj = jnp.minimum(j, nq - 1)
        return (jj // 3, jnp.where(j < nq, i, n_i - 1), jj % 3)

    gate_out_map = lambda j, i: (jnp.where(j >= nq, i, 0), jnp.maximum(j - nq, 0))

    def side_rows(w):
        n_blocks = w.shape[0] // SIDE_CAST_ROWS
        return pl.BlockSpec((SIDE_CAST_ROWS, w.shape[1]),
                            lambda j, i: (jnp.minimum(j * n_i + i, n_blocks - 1), 0))

    side = [w_out, w_proj_a, w_proj_b]
    assert all(w.shape[0] // SIDE_CAST_ROWS <= (nq + ng) * n_i for w in side)
    return pl.pallas_call(
        functools.partial(_proj_kernel, n_qkv_steps=nq),
        grid=(nq + ng, n_i),
        in_specs=[
            pl.BlockSpec((None, tm, D_MODEL), lhs_map),
            pl.BlockSpec((D_MODEL, tn), lambda j, i: (0, 0), pipeline_mode=pl.Buffered(1)),
            pl.BlockSpec((D_MODEL // n_i, tn), wq_next_map),
            pl.BlockSpec((D_MODEL // n_i, tn), wg_next_map),
            pl.BlockSpec((1, tn), gate_col),
        ] + [side_rows(w) for w in side],
        out_specs=[
            pl.BlockSpec((None, tm, tn), qkv_out_map),
            pl.BlockSpec((tm, tn), gate_out_map),
        ] + [side_rows(w) for w in side],
        out_shape=[
            jax.ShapeDtypeStruct((n_groups, m, 3 * GROUP_WIDTH), BF16),
            jax.ShapeDtypeStruct((m, 2 * D_MODEL), BF16),
        ] + [jax.ShapeDtypeStruct(w.shape, BF16) for w in side],
        scratch_shapes=[pltpu.VMEM((2, D_MODEL, tn), BF16)],
        compiler_params=_params(("arbitrary", "arbitrary")),
        name="proj_qkv_gate",
    )(h_orders, w_qkv, w_qkv, w_gate, b_gate, *side)


def _dilated_block_shape(seq, half_window):
    kw = DIL_QB + 2 * half_window
    return (seq, seq) if seq <= kw else (DIL_QB, kw)


def _dilated_kernel(*refs, jobs):
    n = len(jobs)
    for idx, (seq, half_window, coefs) in enumerate(jobs):
        _dilated_job(*refs[3 * idx:3 * idx + 3], *refs[3 * n + 2 * idx:3 * n + 2 * idx + 2],
                     refs[5 * n + idx], seq=seq, half_window=half_window, coefs=coefs)


def _dilated_job(q_ref, k_ref, v_ref, o_ref, lse_ref, bias_ref, *, seq, half_window, coefs):
    qb, kw = _dilated_block_shape(seq, half_window)
    n_res = q_ref.shape[1]
    q_rows = q_ref.shape[0] * q_ref.shape[2]
    q_base = 0 if q_rows == seq else (pl.program_id(1) % (seq // q_rows)) * q_rows
    n_blocks = q_rows // qb

    def rows_loader(ref):
        n_tiles, _, tile_rows, _ = ref.shape
        if n_res == 1:
            flat = ref.at[:, 0].reshape(n_tiles * tile_rows, GROUP_WIDTH)
            return lambda rr, start, size, cols: flat[pl.ds(start, size), cols]
        assert n_blocks == 1
        return lambda rr, start, size, cols: jnp.concatenate(
            [ref[t, rr, :, cols] for t in range(n_tiles)], axis=0)

    load_q, load_k, load_v = rows_loader(q_ref), rows_loader(k_ref), rows_loader(v_ref)
    lane = lax.broadcasted_iota(jnp.int32, (qb, HEAD_DIM), 1)
    heads = range(HEADS_PER_GROUP)
    head_cols = [slice(h * HEAD_DIM, (h + 1) * HEAD_DIM) for h in heads]

    @pl.when((pl.program_id(0) == 0) & (pl.program_id(1) == 0))
    def _():
        qrow = lax.broadcasted_iota(jnp.int32, (qb, kw), 0)
        kcol = lax.broadcasted_iota(jnp.int32, (qb, kw), 1)
        for c in range(bias_ref.shape[0]):
            dist = jnp.abs(kcol - qrow - c * half_window)
            dist_f = dist.astype(F32)
            for h in heads:
                bias_ref[c, h] = jnp.where(dist <= half_window, (-coefs[h] * LOG2E) * dist_f, NEG)

    def block(qi, carry):
        q0 = pl.multiple_of(qi * qb, qb)
        qg = q_base + q0
        ks = pl.multiple_of(jnp.clip(qg - half_window, 0, seq - kw), half_window)
        window_case = (qg - ks) // half_window
        pairs = [(rr, h) for rr in range(n_res) for h in heads]
        scores = [lax.dot_general(load_q(rr, q0, qb, head_cols[h]), load_k(rr, ks, kw, head_cols[h]),
                                  (((1,), (1,)), ((), ())), preferred_element_type=F32)
                  for rr, h in pairs]
        probs, dens = [], []
        lse_all = [jnp.zeros((qb, HEAD_DIM), F32) for _ in range(n_res)]
        for (rr, h), s in zip(pairs, scores):
            t = s * (SCALE * LOG2E) + bias_ref[window_case, h]
            m = jnp.max(t, axis=-1, keepdims=True)
            p = jnp.exp2(t - m)
            den = jnp.sum(p, axis=-1, keepdims=True)
            probs.append(p.astype(BF16))
            dens.append(den)
            lse_all[rr] = jnp.where(lane == h, m * LN2 + jnp.log(den), lse_all[rr])
        for (rr, h), p, den in zip(pairs, probs, dens):
            o = jnp.dot(p, load_v(rr, ks, kw, head_cols[h]), preferred_element_type=F32) / den
            o_ref[rr, pl.ds(q0, qb), head_cols[h]] = o.astype(BF16)
        for rr in range(n_res):
            lse_ref[rr, pl.ds(q0, qb), :] = lse_all[rr]
        return carry

    lax.fori_loop(0, n_blocks, block, 0, unroll=min(DIL_UNROLL, n_blocks))


def _dilated_steps(seq_total, window, dilation):
    seq = seq_total // dilation
    qb, _ = _dilated_block_shape(seq, window // (2 * dilation))
    n_res = min(dilation, max(1, DIL_ROWS_PER_STEP // seq)) if qb == seq else 1
    return n_res, dilation // n_res


def _dilated_groups(qkv, batch, seq_total, groups, slopes):
    tiles = seq_total // PERM_TILE
    jobs, operands, in_specs, out_specs, out_shapes, scratch = [], [], [], [], [], []
    n_steps = max(_dilated_steps(seq_total, w, d)[1] for _, w, d in groups)
    for group, window, d in groups:
        seq = seq_total // d
        half_window = window // (2 * d)
        qb, kw = _dilated_block_shape(seq, half_window)
        n_res, res_steps = _dilated_steps(seq_total, window, d)
        q_split = n_steps // res_steps
        assert res_steps * q_split == n_steps and tiles % q_split == 0
        q_tiles = tiles // q_split
        coefs = tuple(float(slopes[group * HEADS_PER_GROUP + h]) * d
                      for h in range(HEADS_PER_GROUP))
        jobs.append((seq, half_window, coefs))
        rows = PERM_TILE // d
        view = qkv.reshape(qkv.shape[0], batch, tiles, d, rows, qkv.shape[-1])
        q_map = lambda b, r, g=group, s=q_split: (g, b, r % s, r // s, 0, 0)
        kv_map = lambda which: (lambda b, r, g=group, s=q_split: (g, b, 0, r // s, 0, which))
        operands += [view, view, view]
        in_specs += [
            pl.BlockSpec((None, None, q_tiles, n_res, rows, GROUP_WIDTH), q_map),
            pl.BlockSpec((None, None, tiles, n_res, rows, GROUP_WIDTH), kv_map(1)),
            pl.BlockSpec((None, None, tiles, n_res, rows, GROUP_WIDTH), kv_map(2)),
        ]
        out_map = lambda b, r, s=q_split: (b, r // s, r % s, 0)
        out_specs += [pl.BlockSpec((None, n_res, seq // q_split, GROUP_WIDTH), out_map),
                      pl.BlockSpec((None, n_res, seq // q_split, HEAD_DIM), out_map)]
        out_shapes += [jax.ShapeDtypeStruct((batch, d, seq, GROUP_WIDTH), BF16),
                       jax.ShapeDtypeStruct((batch, d, seq, HEAD_DIM), F32)]
        scratch.append(pltpu.VMEM((N_WINDOW_CASES, HEADS_PER_GROUP, qb, kw), F32))
    outs = pl.pallas_call(
        functools.partial(_dilated_kernel, jobs=tuple(jobs)),
        grid=(batch, n_steps),
        in_specs=in_specs,
        out_specs=out_specs,
        out_shape=out_shapes,
        scratch_shapes=scratch,
        compiler_params=_params(("arbitrary", "arbitrary")),
        name="dilated_attention_d" + "_d".join(str(d) for _, _, d in groups),
    )(*operands)
    return [(outs[2 * i], outs[2 * i + 1]) for i in range(len(groups))]


NA_GROUP_ROWS = NA_ROWS // 2
NA_WINDOW_ROWS = NA_ROWS + NA_GROUP_ROWS
N_BIAS_VARIANTS = 3


def _expand_na_bias(rpb_ref, out_ref, h):
    half = NA_ROWS // 2
    n_off = 2 * NA_ROWS - 1
    pad = half
    n_seg = 3 * NA_ROWS
    width = n_seg * GRID_W
    qc = lax.broadcasted_iota(jnp.int32, (GRID_W, width), 0)
    kc = lax.broadcasted_iota(jnp.int32, (GRID_W, width), 1) % GRID_W
    col_idx = jnp.clip(kc - qc, -(NA_COLS - 1), NA_COLS - 1) + (NA_COLS - 1)
    seg = lax.broadcasted_iota(jnp.int32, (1, width), 1) // GRID_W
    table = jnp.zeros((GRID_W, width), F32)
    for j in range(2 * NA_COLS - 1):
        row_vals = jnp.zeros((1, width), F32)
        for i in range(n_off):
            row_vals = jnp.where(seg == i + pad, rpb_ref[h, i, j], row_vals)
        table = jnp.where(col_idx == j, row_vals, table)
    cs = jnp.clip(qc - NA_COLS // 2, 0, GRID_W - NA_COLS)
    table = jnp.where((kc >= cs) & (kc < cs + NA_COLS), table * LOG2E, NEG)
    nkeys = NA_WINDOW_ROWS * GRID_W
    key_row = lax.broadcasted_iota(jnp.int32, (GRID_W, nkeys), 1) // GRID_W
    for v in range(N_BIAS_VARIANTS):
        for a in range(NA_GROUP_ROWS):
            first = {0: 0, 1: a, 2: NA_WINDOW_ROWS - NA_ROWS}[v]
            seg0 = (NA_ROWS - 1) - half * v - a + pad
            assert 0 <= seg0 and seg0 + NA_WINDOW_ROWS <= n_seg
            slab = table[:, seg0 * GRID_W:(seg0 + NA_WINDOW_ROWS) * GRID_W]
            slab = jnp.where((key_row >= first) & (key_row < first + NA_ROWS), slab, NEG)
            out_ref[v, h, a * GRID_W:(a + 1) * GRID_W, :] = slab


def _na_kernel(rpb_ref, q_ref, k_ref, v_ref, wup_ref, o_ref, wup_bf_ref, bias_ref, *, rows):
    rb = pl.program_id(1)
    nq = NA_GROUP_ROWS * GRID_W
    nkeys = NA_WINDOW_ROWS * GRID_W

    @pl.when((pl.program_id(0) == 0) & (rb == 0))
    def _():
        for h in range(N_HEADS_B):
            _expand_na_bias(rpb_ref, bias_ref, h)

    def one_group(gl, carry):
        r0 = rb * NA_ROWS_PER_STEP + gl * NA_GROUP_ROWS
        ws = jnp.clip(r0 - NA_ROWS // 2, 0, rows - NA_WINDOW_ROWS)
        var = (r0 - ws) // (NA_ROWS // 2)
        q0 = pl.multiple_of(gl * nq, nq)
        k0 = pl.multiple_of(ws * GRID_W, GRID_W)
        heads = range(N_HEADS_B)
        head_cols = [slice(h * HEAD_DIM, (h + 1) * HEAD_DIM) for h in heads]
        scores = [lax.dot_general(q_ref[pl.ds(q0, nq), c], k_ref[pl.ds(k0, nkeys), c],
                                  (((1,), (1,)), ((), ())), preferred_element_type=F32)
                  for c in head_cols]
        probs, dens = [], []
        for h in heads:
            t = scores[h] * (SCALE * LOG2E) + bias_ref[var, h]
            m = jnp.max(t, axis=-1, keepdims=True)
            p = jnp.exp2(t - m)
            dens.append(jnp.sum(p, axis=-1, keepdims=True))
            probs.append(p.astype(BF16))
        for h in heads:
            o = jnp.dot(probs[h], v_ref[pl.ds(k0, nkeys), head_cols[h]],
                        preferred_element_type=F32) / dens[h]
            o_ref[pl.ds(q0, nq), head_cols[h]] = o.astype(BF16)
        return carry

    n_groups = NA_ROWS_PER_STEP // NA_GROUP_ROWS
    lax.fori_loop(0, n_groups, one_group, 0, unroll=min(NA_UNROLL, n_groups))
    wup_bf_ref[...] = wup_ref[...].astype(BF16)


def _neighbourhood(qkv, rpb, batch, seq_total, w_up):
    group = N_GROUPS_A
    assert GROUP_DILATIONS[group] == 1
    rows = seq_total // GRID_W
    assert rows >= NA_WINDOW_ROWS and rows % NA_ROWS_PER_STEP == 0
    assert NA_ROWS_PER_STEP % NA_GROUP_ROWS == 0
    tq = NA_ROWS_PER_STEP * GRID_W
    steps_per_batch = rows // NA_ROWS_PER_STEP
    view = qkv.reshape(qkv.shape[0], batch, seq_total, qkv.shape[-1])
    wup_rows = pl.BlockSpec((w_up.shape[0] // (batch * steps_per_batch), w_up.shape[1]),
                            lambda b, i: (b * steps_per_batch + i, 0))
    o, w_up_bf = pl.pallas_call(
        functools.partial(_na_kernel, rows=rows),
        grid=(batch, steps_per_batch),
        in_specs=[
            pl.BlockSpec(memory_space=pltpu.SMEM),
            pl.BlockSpec((None, None, tq, GROUP_WIDTH), lambda b, i: (group, b, i, 0)),
            pl.BlockSpec((None, None, seq_total, GROUP_WIDTH), lambda b, i: (group, b, 0, 1)),
            pl.BlockSpec((None, None, seq_total, GROUP_WIDTH), lambda b, i: (group, b, 0, 2)),
            wup_rows,
        ],
        out_specs=[pl.BlockSpec((None, tq, GROUP_WIDTH), lambda b, i: (b, i, 0)), wup_rows],
        out_shape=[jax.ShapeDtypeStruct((batch, seq_total, GROUP_WIDTH), BF16),
                   jax.ShapeDtypeStruct(w_up.shape, BF16)],
        scratch_shapes=[pltpu.VMEM((N_BIAS_VARIANTS, N_HEADS_B, NA_GROUP_ROWS * GRID_W,
                                    NA_WINDOW_ROWS * GRID_W), F32)],
        compiler_params=_params(("arbitrary", "arbitrary")),
        name="neighbourhood_attention",
    )(rpb, view, view, view, w_up)
    return o.reshape(batch * seq_total, GROUP_WIDTH), w_up_bf


def _to_token_order_matrix(tm, d):
    t = lax.broadcasted_iota(jnp.int32, (tm, tm), 0)
    c = lax.broadcasted_iota(jnp.int32, (tm, tm), 1)
    return (c == (t % d) * (tm // d) + t // d).astype(BF16)


def _merge_kernel(o0_ref, o1_ref, o2_ref, l0_ref, l1_ref, l2_ref, ob_ref, gate_ref, x_ref,
                  wpa_ref, wpb_ref, wout_ref, gn_ref, wdn_ref,
                  x2_ref, h2_ref, wdn_bf_ref, lse_tok_ref, *, dilations):
    tm = x_ref.shape[0]
    tb = jnp.dot(ob_ref[...], wpb_ref[...], preferred_element_type=F32)
    o_tok, lse_tok = [], []
    for g, (o_ref, l_ref, d) in enumerate(zip((o0_ref, o1_ref, o2_ref),
                                              (l0_ref, l1_ref, l2_ref), dilations)):
        o = o_ref[...].reshape(tm, GROUP_WIDTH)
        if d == 1:
            o_tok.append(o.astype(F32))
            lse_tok.append(l_ref[...].reshape(tm, HEAD_DIM))
        else:
            o_tok.append(jnp.dot(_to_token_order_matrix(tm, d), o, preferred_element_type=F32))
            for r in range(d):
                lse_tok_ref[g, pl.ds(r, tm // d, stride=d), :] = l_ref[r]
            lse_tok.append(lse_tok_ref[g])
    l0, l1, l2 = lse_tok
    mx = jnp.maximum(jnp.maximum(l0, l1), l2)
    e0, e1, e2 = jnp.exp(l0 - mx), jnp.exp(l1 - mx), jnp.exp(l2 - mx)
    tot = e0 + e1 + e2
    a0, a1, a2 = e0 / tot, e1 / tot, e2 / tot
    parts = []
    for h in range(HEADS_PER_GROUP):
        cols = slice(h * HEAD_DIM, (h + 1) * HEAD_DIM)
        y = (a0[:, h:h + 1] * o_tok[0][:, cols] + a1[:, h:h + 1] * o_tok[1][:, cols]
             + a2[:, h:h + 1] * o_tok[2][:, cols])
        parts.append(y.astype(BF16))
    ya = jnp.concatenate(parts, axis=1)
    ta = jnp.dot(ya, wpa_ref[...], preferred_element_type=F32)
    merged = gate_ref[:, :D_MODEL].astype(F32) * ta + gate_ref[:, D_MODEL:].astype(F32) * tb
    x2 = x_ref[...] + jnp.dot(merged.astype(BF16), wout_ref[...], preferred_element_type=F32)
    x2_ref[...] = x2
    wdn_bf_ref[...] = wdn_ref[...].astype(BF16)
    h2_ref[...] = _rms(x2, gn_ref[...]).astype(BF16)


def _merge(o_groups, lse_groups, ob, gates, x2d, seq, wpa, wpb, wout, gn, w_down):
    m = x2d.shape[0]
    tm = MERGE_TM
    n_steps = m // tm
    tiles_per_batch = seq // tm
    dilations = tuple(o.shape[1] for o in o_groups)
    row = lambda w: pl.BlockSpec((tm, w), lambda i: (i, 0))
    wdn_rows = pl.BlockSpec((w_down.shape[0] // n_steps, w_down.shape[1]), lambda i: (i, 0))
    grouped = lambda d, w: pl.BlockSpec(
        (None, d, tm // d, w), lambda i: (i // tiles_per_batch, 0, i % tiles_per_batch, 0))
    const = lambda a, b: pl.BlockSpec((a, b), lambda i: (0, 0), pipeline_mode=pl.Buffered(1))
    return pl.pallas_call(
        functools.partial(_merge_kernel, dilations=dilations),
        grid=(n_steps,),
        in_specs=[grouped(d, GROUP_WIDTH) for d in dilations]
        + [grouped(d, HEAD_DIM) for d in dilations]
        + [row(GROUP_WIDTH), row(2 * D_MODEL), row(D_MODEL),
           const(GROUP_WIDTH, D_MODEL), const(GROUP_WIDTH, D_MODEL), const(D_MODEL, D_MODEL),
           const(1, D_MODEL), wdn_rows],
        out_specs=[row(D_MODEL), row(D_MODEL), wdn_rows],
        out_shape=[jax.ShapeDtypeStruct((m, D_MODEL), F32), jax.ShapeDtypeStruct((m, D_MODEL), BF16),
                   jax.ShapeDtypeStruct(w_down.shape, BF16)],
        scratch_shapes=[pltpu.VMEM((len(dilations), tm, HEAD_DIM), F32)],
        compiler_params=_params(("arbitrary",)),
        name="merge_out_proj",
    )(*o_groups, *lse_groups, ob, gates, x2d, wpa, wpb, wout, gn, w_down)


def _mlp_kernel(h2_ref, wup_ref, wdn_ref, x2_hbm, gf_ref, out_ref, x2_sem, *, n_f):
    i = pl.program_id(0)
    f = pl.program_id(1)
    tm = out_ref.shape[0]

    def residual_copy():
        return pltpu.make_async_copy(x2_hbm.at[pl.ds(i * tm, tm), :], out_ref, x2_sem)

    def up_down(first=False, last=False):
        if first:
            residual_copy().start(priority=RING_DMA_PRIORITY)
        hid = jnp.dot(h2_ref[...], wup_ref[...], preferred_element_type=F32)
        hid = jnp.square(jnp.maximum(hid, 0.0)).astype(BF16)
        if first:
            residual_copy().wait()
        if not last:
            out_ref[...] += jnp.dot(hid, wdn_ref[...], preferred_element_type=F32)
            return
        chunk = tm // MLP_FINAL_ROW_CHUNKS
        for r0 in range(0, tm, chunk):
            rows = slice(r0, r0 + chunk)
            y = out_ref[rows, :] + jnp.dot(hid[rows], wdn_ref[...], preferred_element_type=F32)
            out_ref[rows, :] = _rms(y, gf_ref[...])

    assert n_f >= 2

    @pl.when(f == 0)
    def _():
        up_down(first=True)

    @pl.when((f > 0) & (f < n_f - 1))
    def _():
        up_down()

    @pl.when(f == n_f - 1)
    def _():
        up_down(last=True)


def _mlp(h2, wup, wdn, x2, gf):
    m = h2.shape[0]
    tm, tf = MLP_TM, MLP_TF
    n_f = D_FF // tf
    return pl.pallas_call(
        functools.partial(_mlp_kernel, n_f=n_f),
        grid=(m // tm, n_f),
        in_specs=[
            pl.BlockSpec((tm, D_MODEL), lambda i, f: (i, 0)),
            pl.BlockSpec((D_MODEL, tf), lambda i, f: (0, f)),
            pl.BlockSpec((tf, D_MODEL), lambda i, f: (f, 0)),
            pl.BlockSpec(memory_space=pl.ANY),
            pl.BlockSpec((1, D_MODEL), lambda i, f: (0, 0)),
        ],
        out_specs=pl.BlockSpec((tm, D_MODEL), lambda i, f: (i, 0)),
        out_shape=jax.ShapeDtypeStruct((m, D_MODEL), F32),
        scratch_shapes=[pltpu.SemaphoreType.DMA],
        compiler_params=_params(("arbitrary", "arbitrary")),
        name="mlp_residual_norm",
    )(h2, wup, wdn, x2, gf)


def _layer(x2d, batch, seq, norm_mix, w_qkv, w_gate, b_gate, rpb, w_proj_a, w_proj_b, w_out,
           norm_mlp, w_up, w_down):
    slopes = 2.0 ** (-8.0 * np.arange(1, N_HEADS_A + 1) / N_HEADS_A)
    row = lambda v: v.reshape(1, -1)
    h_orders = _norm(x2d, row(norm_mix))
    qkv, gates, w_out_bf, w_pa_bf, w_pb_bf = _project(
        h_orders, w_qkv, w_gate, row(b_gate), w_out, w_proj_a, w_proj_b)
    groups = [(g, window, d) for g, (window, d) in enumerate(DILATION_PATTERNS)]
    o_groups, lse_groups = zip(*_dilated_groups(qkv, batch, seq, groups, slopes))
    ob, w_up_bf = _neighbourhood(qkv, rpb, batch, seq, w_up)
    x2, h2, w_down_bf = _merge(o_groups, lse_groups, ob, gates, x2d, seq, w_pa_bf, w_pb_bf,
                               w_out_bf, row(norm_mlp), w_down)
    return x2, h2, w_up_bf, w_down_bf


def kernel(x, norm_mix, w_qkv, w_gate, b_gate, rpb, w_proj_a, w_proj_b, w_out, norm_mlp, w_up,
           w_down, norm_final):
    batch, seq, _ = x.shape
    depth = norm_mix.shape[0]
    assert depth == 1 and seq % PERM_TILE == 0
    x2d = x.reshape(batch * seq, D_MODEL)
    x2, h2, w_up_bf, w_down_bf = _layer(
        x2d, batch, seq, norm_mix[0], w_qkv[0], w_gate[0], b_gate[0], rpb[0],
        w_proj_a[0], w_proj_b[0], w_out[0], norm_mlp[0], w_up[0], w_down[0])
    out = _mlp(h2, w_up_bf, w_down_bf, x2, norm_final.reshape(1, -1))
    return out.reshape(batch, seq, D_MODEL)
```
